```python
import jax, jax.numpy as jnp
from jax import lax
import numpy as np

D_MODEL = 1024
BATCH = 32
SEQ = 256
DEPTH = 1
DEC_BATCH = 4
DEC_SEQ = 2048
PAST_LEN = 512

GRID_W = 64
N_DIR = 2
H_A = 4
DK_A = 64
DV_A = 128
GLA_RANK = 16
GLA_NORMALIZER = 16.0
H_B = 4
DK_B = 128
DV_B = 128
CONV_K = 5
CHUNK = 64
QA = H_A * DK_A
VA = H_A * DV_A
QB = H_B * DK_B
VB = H_B * DV_B
D_IN = 2 * QA + 2 * VA + N_DIR * GLA_RANK + 2 * QB + 2 * VB + 2 * N_DIR * H_B
D_MIX = VA + VB
N_EXPERTS = 256
TOP_K = 8
N_GROUP = 8
TOPK_GROUP = 4
D_EXPERT = 256
D_SHARED = 256
ROUTED_SCALE = 2.5
EXPERT_BLOCK = 128
EPS = 1e-6

kernel_name = "hybrid_gla_gdn_moe_diffusion_step"


def rmsnorm(x, w):
    xf = x.astype(jnp.float32)
    y = xf * lax.rsqrt(jnp.mean(xf * xf, axis=-1, keepdims=True) + EPS)
    return (y * w.astype(jnp.float32)).astype(x.dtype)


def l2norm(x):
    xf = x.astype(jnp.float32)
    return xf * lax.rsqrt(jnp.sum(xf * xf, axis=-1, keepdims=True) + EPS)


def heads(x, n_heads):
    b, l, _ = x.shape
    return x.reshape(b, l, n_heads, -1).transpose(0, 2, 1, 3)


def to_chunks(x):
    return x.reshape(x.shape[:2] + (x.shape[2] // CHUNK, CHUNK) + x.shape[3:])


def flip(x):
    return jnp.flip(x, axis=2)


def short_conv(x, w, n_rows, row_len):
    b, l, ch = x.shape
    xr = x.reshape(b * n_rows, row_len, ch)
    y = lax.conv_general_dilated(xr, w[:, None, :].astype(x.dtype), (1,), [(CONV_K // 2, CONV_K // 2)],
                                 dimension_numbers=('NWC', 'WIO', 'NWC'), feature_group_count=ch)
    return y.reshape(b, l, ch)


def gla_chunked(q, k, v, log_a, s0):
    q, k, v, log_a = to_chunks(q), to_chunks(k), to_chunks(v), to_chunks(log_a)
    b = jnp.cumsum(log_a, axis=3)
    b_last = b[:, :, :, -1:, :]
    ref = b[:, :, :, CHUNK // 2:CHUNK // 2 + 1, :]
    mask = jnp.tril(jnp.ones((CHUNK, CHUNK), bool))
    att = jnp.einsum('bhntd,bhnsd->bhnts', q * jnp.exp(b - ref), k * jnp.exp(ref - b))
    att = jnp.where(mask, att, 0.0)
    o_intra = jnp.einsum('bhnts,bhnse->bhnte', att, v)
    u = jnp.einsum('bhnsd,bhnse->nbhde', k * jnp.exp(b_last - b), v)
    decay = jnp.moveaxis(jnp.exp(b_last[:, :, :, 0, :]), 2, 0)

    def step(s, inp):
        dec, un = inp
        return dec[..., None] * s + un, s

    s_fin, s_starts = lax.scan(step, s0, (decay, u))
    o_inter = jnp.einsum('bhntd,nbhde->bhnte', q * jnp.exp(b), s_starts)
    o = o_intra + o_inter
    return o.reshape(o.shape[:2] + (-1, o.shape[-1])), s_fin


def gdn_chunked(q, k, v, g, beta, s0):
    dv = v.shape[-1]
    q, k, v, g, beta = to_chunks(q), to_chunks(k), to_chunks(v), to_chunks(g), to_chunks(beta)
    gc = jnp.cumsum(g, axis=-1)
    incl = jnp.tril(jnp.ones((CHUNK, CHUNK), bool))
    strict = jnp.tril(jnp.ones((CHUNK, CHUNK), bool), k=-1)
    decay = jnp.exp(jnp.where(incl, gc[..., :, None] - gc[..., None, :], -jnp.inf))
    kb = k * beta[..., None]
    a_low = jnp.where(strict, jnp.einsum('bhntd,bhnsd->bhnts', kb, k) * decay, 0.0)
    t_mat = a_low + jnp.eye(CHUNK, dtype=jnp.float32)
    rhs = jnp.concatenate([v * beta[..., None], kb * jnp.exp(gc)[..., None]], axis=-1)
    sol = lax.linalg.triangular_solve(t_mat, rhs, left_side=True, lower=True, unit_diagonal=True)
    value, k_cum = sol[..., :dv], sol[..., dv:]
    att_qk = jnp.einsum('bhntd,bhnsd->bhnts', q, k) * decay
    q_dec = q * jnp.exp(gc)[..., None]
    k_dec = k * jnp.exp(gc[..., -1:] - gc)[..., None]
    g_last = jnp.exp(gc[..., -1])
    xs = tuple(jnp.moveaxis(a, 2, 0) for a in (value, k_cum, q_dec, att_qk, k_dec, g_last))

    def step(s, inp):
        val, kc, qd, aqk, kd, gl = inp
        v_new = val - jnp.einsum('bhtd,bhde->bhte', kc, s)
        o = jnp.einsum('bhtd,bhde->bhte', qd, s) + jnp.einsum('bhts,bhse->bhte', aqk, v_new)
        s = s * gl[..., None, None] + jnp.einsum('bhsd,bhse->bhde', kd, v_new)
        return s, o

    s_fin, o = lax.scan(step, s0, xs)
    o = jnp.moveaxis(o, 0, 2)
    return o.reshape(o.shape[:2] + (-1, dv)), s_fin


def token_mixer(h, s_gla0, s_gdn0, n_rows, row_len, w_in, conv_w, gla_lr_w, gla_lr_b,
                gdn_a_log, gdn_dt_bias, gla_norm_w, gdn_norm_w, w_o):
    f32 = jnp.float32
    b, l, _ = h.shape
    proj = h @ w_in
    widths = (QA, QA, VA, VA, N_DIR * GLA_RANK, 3 * QB, VB, N_DIR * H_B, N_DIR * H_B)
    offs = np.cumsum(widths)[:-1].tolist()
    q_a, k_a, v_a, g_a, lr, qkv_b, z_b, a_b, beta_b = jnp.split(proj, offs, axis=-1)
    s_gla0 = s_gla0.astype(f32)
    s_gdn0 = s_gdn0.astype(f32)

    q_a = heads(q_a, H_A).astype(f32) * DK_A ** -0.5
    k_a = heads(k_a, H_A).astype(f32)
    v_a = heads(v_a, H_A).astype(f32)
    lr = lr.reshape(b, l, N_DIR, GLA_RANK).astype(f32)
    log_a = jax.nn.log_sigmoid(jnp.einsum('bldr,drk->dblk', lr, gla_lr_w.astype(f32))
                               + gla_lr_b.astype(f32)[:, None, None, :]) / GLA_NORMALIZER
    o_f, sa_f = gla_chunked(q_a, k_a, v_a, heads(log_a[0], H_A), s_gla0[:, 0])
    o_b, sa_b = gla_chunked(flip(q_a), flip(k_a), flip(v_a), flip(heads(log_a[1], H_A)), s_gla0[:, 1])
    o_gla = o_f + flip(o_b)
    s_gla = jnp.stack([sa_f, sa_b], axis=1)

    qkv_b = jax.nn.silu(short_conv(qkv_b, conv_w, n_rows, row_len))
    q_b, k_b, v_b = jnp.split(qkv_b, 3, axis=-1)
    q_b = l2norm(heads(q_b, H_B)) * DK_B ** -0.5
    k_b = l2norm(heads(k_b, H_B))
    v_b = heads(v_b, H_B).astype(f32)
    a_b = a_b.reshape(b, l, N_DIR, H_B).astype(f32)
    g = (-jnp.exp(gdn_a_log.astype(f32)) * jax.nn.softplus(a_b + gdn_dt_bias.astype(f32))).transpose(2, 0, 3, 1)
    beta = jax.nn.sigmoid(beta_b.reshape(b, l, N_DIR, H_B).astype(f32)).transpose(2, 0, 3, 1)
    p_f, sb_f = gdn_chunked(q_b, k_b, v_b, g[0], beta[0], s_gdn0[:, 0])
    p_b, sb_b = gdn_chunked(flip(q_b), flip(k_b), flip(v_b), flip(g[1]), flip(beta[1]), s_gdn0[:, 1])
    o_gdn = p_f + flip(p_b)
    s_gdn = jnp.stack([sb_f, sb_b], axis=1)

    gla_out = rmsnorm(o_gla.transpose(0, 2, 1, 3), gla_norm_w).reshape(b, l, VA) * jax.nn.silu(g_a.astype(f32))
    gdn_out = rmsnorm(o_gdn.transpose(0, 2, 1, 3), gdn_norm_w).reshape(b, l, VB) * jax.nn.silu(z_b.astype(f32))
    y = jnp.concatenate([gla_out, gdn_out], axis=-1).astype(h.dtype) @ w_o
    return y, s_gla, s_gdn


def swiglu(x, wg, wu, wd):
    return (jax.nn.silu(x @ wg) * (x @ wu)) @ wd


def routed_experts(x, idx, wts, w_gate, w_up, w_down):
    n, d = x.shape
    m = n * TOP_K
    e_flat = idx.reshape(-1)
    tok_flat = jnp.repeat(jnp.arange(n, dtype=jnp.int32), TOP_K)
    w_flat = wts.reshape(-1)
    order = jnp.argsort(e_flat)
    e_sorted = e_flat[order]
    counts = jnp.bincount(e_flat, length=N_EXPERTS)
    padded = (counts + EXPERT_BLOCK - 1) // EXPERT_BLOCK * EXPERT_BLOCK
    pad_end = jnp.cumsum(padded)
    pad_start = pad_end - padded
    start = jnp.cumsum(counts) - counts
    dest = pad_start[e_sorted] + (jnp.arange(m) - start[e_sorted])
    n_blocks = -(-(m + N_EXPERTS * (EXPERT_BLOCK - 1)) // EXPERT_BLOCK)
    p = n_blocks * EXPERT_BLOCK
    row_tok = jnp.full((p,), n, jnp.int32).at[dest].set(tok_flat[order])
    row_w = jnp.zeros((p,), x.dtype).at[dest].set(w_flat[order].astype(x.dtype))
    block_exp = jnp.minimum(jnp.searchsorted(pad_end, jnp.arange(n_blocks) * EXPERT_BLOCK, side='right'),
                            N_EXPERTS - 1)
    x_pad = jnp.concatenate([x, jnp.zeros((1, d), x.dtype)], axis=0)

    def body(acc, inp):
        rows, rw, e = inp
        xb = x_pad[rows]
        yb = swiglu(xb, w_gate[e], w_up[e], w_down[e]) * rw[:, None]
        return acc.at[rows].add(yb), None

    acc, _ = lax.scan(body, jnp.zeros((n + 1, d), x.dtype),
                      (row_tok.reshape(n_blocks, EXPERT_BLOCK), row_w.reshape(n_blocks, EXPERT_BLOCK), block_exp))
    return acc[:n]


def moe(h, router_w, router_bias, exp_w_gate, exp_w_up, exp_w_down, sh_w_gate, sh_w_up, sh_w_down):
    n = h.shape[0]
    scores = jax.nn.sigmoid((h @ router_w).astype(jnp.float32))
    biased = scores + router_bias.astype(jnp.float32)
    grp = jnp.sum(lax.top_k(biased.reshape(n, N_GROUP, N_EXPERTS // N_GROUP), 2)[0], axis=-1)
    _, gidx = lax.top_k(grp, TOPK_GROUP)
    gmask = jnp.any(jax.nn.one_hot(gidx, N_GROUP, dtype=jnp.int32) > 0, axis=1)
    masked = jnp.where(jnp.repeat(gmask, N_EXPERTS // N_GROUP, axis=1), biased, -jnp.inf)
    _, idx = lax.top_k(masked, TOP_K)
    w = jnp.take_along_axis(scores, idx, axis=1)
    w = w / jnp.sum(w, axis=-1, keepdims=True) * ROUTED_SCALE
    routed = routed_experts(h, idx, w, exp_w_gate, exp_w_up, exp_w_down)
    return routed + swiglu(h, sh_w_gate, sh_w_up, sh_w_down)


def modulate(x, shift, scale):
    return x * (1 + scale) + shift


def setup_inputs(seed: int = 0) -> dict:
    key = jax.random.key(seed)
    ks = jax.random.split(key, 32)
    f32 = jnp.float32

    def nrm(k, shape, scale):
        return jax.random.normal(k, shape, f32) * scale

    dt = jnp.exp(jax.random.uniform(ks[14], (DEPTH, N_DIR, H_B), f32, np.log(1e-3), np.log(1e-1)))
    return {
        "x_prompt": nrm(ks[0], (BATCH, SEQ, D_MODEL), 1.0),
        "x_sample": nrm(ks[1], (DEC_BATCH, DEC_SEQ, D_MODEL), 1.0),
        "state_gla": nrm(ks[2], (DEC_BATCH, DEPTH, N_DIR, H_A, DK_A, DV_A), 0.1),
        "state_gdn": nrm(ks[3], (DEC_BATCH, DEPTH, N_DIR, H_B, DK_B, DV_B), 0.1),
        "c": nrm(ks[4], (DEC_BATCH, D_MODEL), 1.0),
        "c_ctx": nrm(ks[5], (D_MODEL,), 1.0),
        "w_ada": nrm(ks[6], (DEPTH, D_MODEL, 6 * D_MODEL), 0.5 * D_MODEL ** -0.5),
        "b_ada": nrm(ks[7], (DEPTH, 6 * D_MODEL), 0.02),
        "norm1_w": 1.0 + nrm(ks[8], (DEPTH, D_MODEL), 0.01),
        "w_in": nrm(ks[9], (DEPTH, D_MODEL, D_IN), D_MODEL ** -0.5),
        "conv_w": nrm(ks[10], (DEPTH, CONV_K, 3 * QB), CONV_K ** -0.5),
        "gla_lr_w": nrm(ks[11], (DEPTH, N_DIR, GLA_RANK, QA), GLA_RANK ** -0.5),
        "gla_lr_b": nrm(ks[12], (DEPTH, N_DIR, QA), 0.1),
        "gdn_a_log": jnp.log(jax.random.uniform(ks[13], (DEPTH, N_DIR, H_B), f32, 1.0, 16.0)),
        "gdn_dt_bias": dt + jnp.log(-jnp.expm1(-dt)),
        "gla_norm_w": 1.0 + nrm(ks[15], (DEPTH, DV_A), 0.01),
        "gdn_norm_w": 1.0 + nrm(ks[16], (DEPTH, DV_B), 0.01),
        "w_o": nrm(ks[17], (DEPTH, D_MIX, D_MODEL), D_MIX ** -0.5),
        "norm2_w": 1.0 + nrm(ks[18], (DEPTH, D_MODEL), 0.01),
        "router_w": nrm(ks[19], (DEPTH, D_MODEL, N_EXPERTS), D_MODEL ** -0.5),
        "router_bias": nrm(ks[20], (DEPTH, N_EXPERTS), 0.01),
        "exp_w_gate": nrm(ks[21], (DEPTH, N_EXPERTS, D_MODEL, D_EXPERT), D_MODEL ** -0.5),
        "exp_w_up": nrm(ks[22], (DEPTH, N_EXPERTS, D_MODEL, D_EXPERT), D_MODEL ** -0.5),
        "exp_w_down": nrm(ks[23], (DEPTH, N_EXPERTS, D_EXPERT, D_MODEL), D_EXPERT ** -0.5),
        "sh_w_gate": nrm(ks[24], (DEPTH, D_MODEL, D_SHARED), D_MODEL ** -0.5),
        "sh_w_up": nrm(ks[25], (DEPTH, D_MODEL, D_SHARED), D_MODEL ** -0.5),
        "sh_w_down": nrm(ks[26], (DEPTH, D_SHARED, D_MODEL), D_SHARED ** -0.5),
        "final_norm_w": 1.0 + nrm(ks[27], (D_MODEL,), 0.01),
    }


def reference(x_prompt, x_sample, state_gla, state_gdn, c, c_ctx, w_ada, b_ada, norm1_w, w_in, conv_w,
              gla_lr_w, gla_lr_b, gdn_a_log, gdn_dt_bias, gla_norm_w, gdn_norm_w, w_o, norm2_w,
              router_w, router_bias, exp_w_gate, exp_w_up, exp_w_down, sh_w_gate, sh_w_up, sh_w_down,
              final_norm_w):
    bp, lp, d = x_prompt.shape
    bs, ls, _ = x_sample.shape
    n_rows = ls // GRID_W
    xp, xs = x_prompt, x_sample
    new_gla, new_gdn = [], []
    for layer in range(DEPTH):
        mod_ctx = (jax.nn.silu(c_ctx) @ w_ada[layer] + b_ada[layer])[None, None, :]
        mod_lat = (jax.nn.silu(c) @ w_ada[layer] + b_ada[layer])[:, None, :]
        sh1c, sc1c, g1c, sh2c, sc2c, g2c = jnp.split(mod_ctx, 6, axis=-1)
        sh1s, sc1s, g1s, sh2s, sc2s, g2s = jnp.split(mod_lat, 6, axis=-1)
        mix_w = (w_in[layer], conv_w[layer], gla_lr_w[layer], gla_lr_b[layer], gdn_a_log[layer],
                 gdn_dt_bias[layer], gla_norm_w[layer], gdn_norm_w[layer], w_o[layer])

        hp = modulate(rmsnorm(xp, norm1_w[layer]), sh1c, sc1c)
        zero_gla = jnp.zeros((bp, N_DIR, H_A, DK_A, DV_A), jnp.float32)
        zero_gdn = jnp.zeros((bp, N_DIR, H_B, DK_B, DV_B), jnp.float32)
        yp, sg, sd = token_mixer(hp, zero_gla, zero_gdn, 1, lp, *mix_w)
        xp = xp + g1c * yp
        new_gla.append(sg)
        new_gdn.append(sd)

        hs = modulate(rmsnorm(xs, norm1_w[layer]), sh1s, sc1s)
        ys, _, _ = token_mixer(hs, state_gla[:, layer], state_gdn[:, layer], n_rows, GRID_W, *mix_w)
        xs = xs + g1s * ys

        hp2 = modulate(rmsnorm(xp, norm2_w[layer]), sh2c, sc2c).reshape(bp * lp, d)
        hs2 = modulate(rmsnorm(xs, norm2_w[layer]), sh2s, sc2s).reshape(bs * ls, d)
        m = moe(jnp.concatenate([hp2, hs2], axis=0), router_w[layer], router_bias[layer], exp_w_gate[layer],
                exp_w_up[layer], exp_w_down[layer], sh_w_gate[layer], sh_w_up[layer], sh_w_down[layer])
        xp = xp + g2c * m[:bp * lp].reshape(bp, lp, d)
        xs = xs + g2s * m[bp * lp:].reshape(bs, ls, d)

    y_prompt = rmsnorm(xp, final_norm_w)
    y_sample = rmsnorm(xs, final_norm_w)
    new_state_gla = jnp.stack(new_gla, axis=1).astype(x_prompt.dtype)
    new_state_gdn = jnp.stack(new_gdn, axis=1).astype(x_prompt.dtype)
    return (y_prompt, y_sample, new_state_gla, new_state_gdn)
```

```python
import functools

import numpy as np
import jax
import jax.numpy as jnp
from jax import lax
from jax.experimental import pallas as pl
from jax.experimental.pallas import tpu as pltpu

F32 = jnp.float32
BF16 = jnp.bfloat16

D_MODEL = 1024
N_DIR = 2
H_A, DK_A, DV_A = 4, 64, 128
GLA_RANK = 16
GLA_NORMALIZER = 16.0
H_B, DK_B, DV_B = 4, 128, 128
CONV_K = 5
CHUNK = 64
QA, VA = H_A * DK_A, H_A * DV_A
QB, VB = H_B * DK_B, H_B * DV_B
N_EXPERTS = 256
TOP_K = 8
N_GROUP = 8
TOPK_GROUP = 4
D_EXPERT = 256
ROUTED_SCALE = 2.5
EXPERT_BLOCK = 128
EPS = 1e-6

UNIT = 256
CHUNKS_PER_UNIT = UNIT // CHUNK
LANES = 128
SUBLANES = 8
VMEM_LIMIT = 56 * 1024 * 1024

C_QKVA = 0
C_GA = 1024
C_QKVB = 1536
C_ZB = 3072
C_SMALL = 3584
D_PROJ = 3712
SM_AB = 32
SM_BETA = 40


def _cparams(sem):
    return pltpu.CompilerParams(dimension_semantics=sem, vmem_limit_bytes=VMEM_LIMIT)


def _split(a):
    hi = a.astype(BF16)
    lo = (a - hi.astype(F32)).astype(BF16)
    return hi, lo


def _dg(a, b, dims):
    return lax.dot_general(a, b, (dims, ((), ())), preferred_element_type=F32)


NN = ((1,), (0,))
NT = ((1,), (1,))
TN = ((0,), (0,))


def _mm(a, b, dims=NN):
    return _dg(a.astype(BF16), b.astype(BF16), dims)


def _mm3(a, b, dims=NN):
    ah, al = _split(a)
    bh, bl = _split(b)
    return _dg(ah, bh, dims) + (_dg(ah, bl, dims) + _dg(al, bh, dims))


def _mm_exact_lhs(a_bf16, b, dims=NN):
    bh, bl = _split(b)
    return _dg(a_bf16, bh, dims) + _dg(a_bf16, bl, dims)


def _silu(x):
    return x * (1.0 / (1.0 + jnp.exp(-x)))


def _sigmoid(x):
    return 1.0 / (1.0 + jnp.exp(-x))


def _softplus(x):
    return jnp.maximum(x, 0.0) + jnp.log1p(jnp.exp(-jnp.abs(x)))


def _log_sigmoid(x):
    return -_softplus(-x)


def _iota2(shape, dim):
    return lax.broadcasted_iota(jnp.int32, shape, dim)


def _ada_kernel(c_ref, w_ref, b_ref, o_ref):
    o_ref[...] = _mm3(_silu(c_ref[...]), w_ref[...]) + b_ref[...]


def _ada(cond, w, b):
    n = w.shape[1]
    tn = 1536
    return pl.pallas_call(
        _ada_kernel,
        grid=(n // tn,),
        in_specs=[pl.BlockSpec((SUBLANES, D_MODEL), lambda i: (0, 0)),
                  pl.BlockSpec((D_MODEL, tn), lambda i: (0, i)),
                  pl.BlockSpec((1, tn), lambda i: (0, i))],
        out_specs=pl.BlockSpec((SUBLANES, tn), lambda i: (0, i)),
        out_shape=jax.ShapeDtypeStruct((SUBLANES, n), F32),
        compiler_params=_cparams(("parallel",)),
        name="ada",
    )(cond, w, b)


def _rms(x):
    return x * lax.rsqrt(jnp.mean(x * x, axis=-1, keepdims=True) + EPS)


def _inproj_kernel(x_ref, mod_ref, nw_ref, w_ref, ws_ref, o_ref):
    m = mod_ref[0]
    h = (_rms(x_ref[...]) * nw_ref[...]) * (1.0 + m[:, D_MODEL:2 * D_MODEL]) + m[:, 0:D_MODEL]
    hb = h.astype(BF16)
    for c0 in range(0, C_SMALL, 512):
        o_ref[:, c0:c0 + 512] = _dg(hb, w_ref[:, c0:c0 + 512], NN)
    o_ref[:, C_SMALL:D_PROJ] = _mm3(h, ws_ref[...])


def _cond_row(i, tiles_prompt, tiles_per_seq):
    return jnp.where(i < tiles_prompt, 0, 1 + (i - tiles_prompt) // tiles_per_seq)


def _inproj(x_all, mod3, norm_w, w_main, w_small, n_prompt_tok, sample_len, tm):
    n_tok = x_all.shape[0]
    cond = functools.partial(_cond_row, tiles_prompt=n_prompt_tok // tm, tiles_per_seq=sample_len // tm)
    return pl.pallas_call(
        _inproj_kernel,
        grid=(n_tok // tm,),
        in_specs=[pl.BlockSpec((tm, D_MODEL), lambda i: (i, 0)),
                  pl.BlockSpec((1, 1, 6 * D_MODEL), lambda i: (cond(i), 0, 0)),
                  pl.BlockSpec((1, D_MODEL), lambda i: (0, 0)),
                  pl.BlockSpec((D_MODEL, C_SMALL), lambda i: (0, 0)),
                  pl.BlockSpec((D_MODEL, LANES), lambda i: (0, 0))],
        out_specs=pl.BlockSpec((tm, D_PROJ), lambda i: (i, 0)),
        out_shape=jax.ShapeDtypeStruct((n_tok, D_PROJ), F32),
        compiler_params=_cparams(("parallel",)),
        name="inproj",
    )(x_all, mod3, norm_w, w_main, w_small)


def _unit_ids(j, n_prompt_units, units_per_seq):
    jj = j - n_prompt_units
    b = jj // units_per_seq
    r = jj % units_per_seq
    is_prompt = j < n_prompt_units
    uf = j
    ub = jnp.where(is_prompt, j, n_prompt_units + b * units_per_seq + (units_per_seq - 1 - r))
    init_row = jnp.where(is_prompt, 0, 1 + b)
    first = jnp.logical_or(is_prompt, r == 0)
    return uf, ub, init_row, first


def _tri(rev):
    t = _iota2((CHUNK, CHUNK), 0)
    s = _iota2((CHUNK, CHUNK), 1)
    return (t <= s) if rev else (t >= s)


def _gla_chunk(q, k, v, la, states, rev):
    tri = _tri(rev).astype(BF16)
    b = _mm_exact_lhs(tri, la)
    mid, last = (CHUNK // 2 - 1, 0) if rev else (CHUNK // 2, CHUNK - 1)
    bref = b[mid:mid + 1, :]
    blast = b[last:last + 1, :]
    scale = DK_A ** -0.5
    qg = q * jnp.exp(b - bref) * scale
    kg = k * jnp.exp(bref - b)
    kd = k * jnp.exp(blast - b)
    qs = q * jnp.exp(b) * scale
    ones = jnp.ones((CHUNK, LANES), BF16)
    lane = _iota2((CHUNK, LANES), 1)
    row = _iota2((CHUNK, LANES), 0)
    s_in = lane % DK_A
    causal = (row <= s_in) if rev else (row >= s_in)
    zeros_v = jnp.zeros((CHUNK, DV_A), F32)
    outs, new_states = [], []
    for p in range(H_A // 2):
        ls = slice(p * LANES, (p + 1) * LANES)
        vs0 = v[:, (2 * p) * DV_A:(2 * p + 1) * DV_A]
        vs1 = v[:, (2 * p + 1) * DV_A:(2 * p + 2) * DV_A]
        s0, s1 = states[2 * p], states[2 * p + 1]
        kg_p = kg[:, ls]
        rhs_att = jnp.concatenate([jnp.where(lane < DK_A, kg_p, 0.0), jnp.where(lane >= DK_A, kg_p, 0.0)], axis=0)
        att = jnp.where(causal, _mm(qg[:, ls], rhs_att, NT), 0.0)
        rhs_o = jnp.concatenate([jnp.concatenate([vs0, zeros_v], axis=1),
                                 jnp.concatenate([zeros_v, vs1], axis=1),
                                 jnp.concatenate([s0, zeros_v], axis=1),
                                 jnp.concatenate([zeros_v, s1], axis=1)], axis=0)
        lhs_o = jnp.concatenate([att, qs[:, ls]], axis=1)
        outs.append(_mm(lhs_o, rhs_o))
        u = _mm(kd[:, ls], v[:, 2 * p * DV_A:(2 * p + 2) * DV_A], TN)
        dec = jnp.exp(_mm_exact_lhs_tn(la[:, ls], ones))
        new_states.append(dec[0:DK_A] * s0 + u[0:DK_A, 0:DV_A])
        new_states.append(dec[DK_A:2 * DK_A] * s1 + u[DK_A:2 * DK_A, DV_A:2 * DV_A])
    return jnp.concatenate(outs, axis=1), new_states


def _mm_exact_lhs_tn(a, ones_bf16):
    ah, al = _split(a)
    return _dg(ah, ones_bf16, TN) + _dg(al, ones_bf16, TN)


def _gla_kernel(qf_ref, qb_ref, sf_ref, sb_ref, wlr_ref, blr_ref, init_ref, of_ref, ob_ref, so_ref, s_ref,
                *, n_prompt_units, units_per_seq):
    j = pl.program_id(0)
    _, _, _, first = _unit_ids(j, n_prompt_units, units_per_seq)

    @pl.when(first)
    def _():
        s_ref[...] = init_ref[0]

    for d in range(N_DIR):
        x_ref, sm_ref, o_ref = (qf_ref, sf_ref, of_ref) if d == 0 else (qb_ref, sb_ref, ob_ref)
        la_unit = _log_sigmoid(_mm3(sm_ref[...], wlr_ref[d]) + blr_ref[d]) * (1.0 / GLA_NORMALIZER)
        states = [s_ref[d, h] for h in range(H_A)]
        order = range(CHUNKS_PER_UNIT - 1, -1, -1) if d == 1 else range(CHUNKS_PER_UNIT)
        for c in order:
            rows = slice(c * CHUNK, (c + 1) * CHUNK)
            o, states = _gla_chunk(x_ref[rows, 0:QA], x_ref[rows, QA:2 * QA], x_ref[rows, 2 * QA:2 * QA + VA],
                                   la_unit[rows], states, rev=(d == 1))
            o_ref[rows, :] = o
        for h in range(H_A):
            s_ref[d, h] = states[h]
    so_ref[0] = s_ref[...]


def _gla(proj, wlr, blr, init, n_prompt_units, units_per_seq):
    n_tok = proj.shape[0]
    n_units = n_tok // UNIT
    ids = functools.partial(_unit_ids, n_prompt_units=n_prompt_units, units_per_seq=units_per_seq)
    small_blk = C_SMALL // LANES
    st_blk = (1, N_DIR, H_A, DK_A, DV_A)
    kern = functools.partial(_gla_kernel, n_prompt_units=n_prompt_units, units_per_seq=units_per_seq)
    return pl.pallas_call(
        kern,
        grid=(n_units,),
        in_specs=[pl.BlockSpec((UNIT, 1024), lambda j: (ids(j)[0], 0)),
                  pl.BlockSpec((UNIT, 1024), lambda j: (ids(j)[1], 0)),
                  pl.BlockSpec((UNIT, LANES), lambda j: (ids(j)[0], small_blk)),
                  pl.BlockSpec((UNIT, LANES), lambda j: (ids(j)[1], small_blk)),
                  pl.BlockSpec((N_DIR, LANES, QA), lambda j: (0, 0, 0)),
                  pl.BlockSpec((N_DIR, 1, QA), lambda j: (0, 0, 0)),
                  pl.BlockSpec(st_blk, lambda j: (ids(j)[2], 0, 0, 0, 0))],
        out_specs=[pl.BlockSpec((UNIT, VA), lambda j: (ids(j)[0], 0)),
                   pl.BlockSpec((UNIT, VA), lambda j: (ids(j)[1], 0)),
                   pl.BlockSpec(st_blk, lambda j: (jnp.minimum(j, n_prompt_units), 0, 0, 0, 0))],
        out_shape=[jax.ShapeDtypeStruct((n_tok, VA), F32),
                   jax.ShapeDtypeStruct((n_tok, VA), F32),
                   jax.ShapeDtypeStruct((n_prompt_units + 1, N_DIR, H_A, DK_A, DV_A), F32)],
        scratch_shapes=[pltpu.VMEM((N_DIR, H_A, DK_A, DV_A), F32)],
        compiler_params=_cparams(("arbitrary",)),
        name="gla",
    )(proj, proj, proj, proj, wlr, blr, init)


def _neumann_inverse(a):
    eye = (_iota2((CHUNK, CHUNK), 0) == _iota2((CHUNK, CHUNK), 1)).astype(F32)
    p = eye - a
    pw = a
    n = 2
    while n < CHUNK:
        pw = _mm3(pw, pw)
        p = p + _mm3(p, pw)
        n *= 2
    return p


def _gdn_chunk_head(q, k, v, gcb, grow, glast, beta, s, rev):
    incl = _tri(rev)
    t = _iota2((CHUNK, CHUNK), 0)
    sidx = _iota2((CHUNK, CHUNK), 1)
    strict = (t < sidx) if rev else (t > sidx)
    decay = jnp.where(incl, jnp.exp(jnp.where(incl, gcb[:, 0:CHUNK] - grow, 0.0)), 0.0)
    kb = k * beta
    m1 = _mm(jnp.concatenate([kb, q], axis=0), k, NT)
    a_low = jnp.where(strict, m1[0:CHUNK] * decay, 0.0)
    att = m1[CHUNK:2 * CHUNK] * decay
    t_inv = _neumann_inverse(a_low)
    egc = jnp.exp(gcb)
    sol = _mm3(t_inv, jnp.concatenate([v * beta, kb * egc], axis=1))
    value, k_cum = sol[:, 0:DV_B], sol[:, DV_B:2 * DV_B]
    q_dec = q * egc
    k_dec = k * jnp.exp(glast - gcb)
    kq = _mm(jnp.concatenate([k_cum, q_dec], axis=0), s)
    v_new = value - kq[0:CHUNK]
    o = kq[CHUNK:2 * CHUNK] + _mm(att, v_new)
    s_new = s * jnp.exp(glast) + _mm(k_dec, v_new, TN)
    return o, s_new


def _gdn_kernel(xf_ref, xb_ref, sf_ref, sb_ref, cw_ref, gco_ref, gdt_ref, init_ref, of_ref, ob_ref, so_ref, s_ref,
                cv_ref, *, n_prompt_units, units_per_seq, prompt_row, sample_row):
    j = pl.program_id(0)
    _, _, _, first = _unit_ids(j, n_prompt_units, units_per_seq)

    @pl.when(first)
    def _():
        s_ref[...] = init_ref[0]

    row_len = jnp.where(j < n_prompt_units, prompt_row, sample_row)
    pos = _iota2((UNIT, LANES), 0) & (row_len - 1)
    tri_f = _tri(False).astype(BF16)
    tri_b = _tri(True).astype(BF16)

    for d in range(N_DIR):
        x_ref, sm_ref, o_ref = (xf_ref, sf_ref, of_ref) if d == 0 else (xb_ref, sb_ref, ob_ref)
        rev = d == 1
        sm = sm_ref[...]
        g_all = gco_ref[...] * _softplus(sm + gdt_ref[...])
        beta_all = _sigmoid(sm)
        for part in range(3 * H_B):
            cs = slice(part * LANES, (part + 1) * LANES)
            x = x_ref[:, cs]
            acc = jnp.zeros((UNIT, LANES), F32)
            for jj in range(CONV_K):
                off = jj - CONV_K // 2
                xs = x if off == 0 else pltpu.roll(x, (-off) % UNIT, 0)
                ok = jnp.logical_and(pos + off >= 0, pos + off < row_len)
                acc = acc + jnp.where(ok, xs, 0.0) * cw_ref[jj:jj + 1, cs]
            y = _silu(acc)
            if part < 2 * H_B:
                y = y * lax.rsqrt(jnp.sum(y * y, axis=-1, keepdims=True) + EPS)
                if part < H_B:
                    y = y * (DK_B ** -0.5)
            cv_ref[:, cs] = y
        order = range(CHUNKS_PER_UNIT - 1, -1, -1) if rev else range(CHUNKS_PER_UNIT)
        states = [s_ref[d, h] for h in range(H_B)]
        for c in order:
            rows = slice(c * CHUNK, (c + 1) * CHUNK)
            gc_all = _mm_exact_lhs(tri_b if rev else tri_f, g_all[rows])
            gc_t = gc_all.T
            last = 0 if rev else CHUNK - 1
            outs = []
            for h in range(H_B):
                col = SM_AB + d * H_B + h
                colb = SM_BETA + d * H_B + h
                gcb = jnp.broadcast_to(gc_all[:, col:col + 1], (CHUNK, LANES))
                grow = jnp.broadcast_to(gc_t[col:col + 1, :], (CHUNK, CHUNK))
                glast = gcb[last:last + 1, :]
                beta = jnp.broadcast_to(beta_all[rows, colb:colb + 1], (CHUNK, LANES))
                hs = slice(h * LANES, (h + 1) * LANES)
                ks = slice(QB + h * LANES, QB + (h + 1) * LANES)
                vs = slice(2 * QB + h * LANES, 2 * QB + (h + 1) * LANES)
                o, states[h] = _gdn_chunk_head(cv_ref[rows, hs], cv_ref[rows, ks], cv_ref[rows, vs],
                                               gcb, grow, glast, beta, states[h], rev)
                outs.append(o)
            o_ref[rows, :] = jnp.concatenate(outs, axis=1)
        for h in range(H_B):
            s_ref[d, h] = states[h]
    so_ref[0] = s_ref[...]


def _gdn(proj, conv_w, gcoef, gdt, init, n_prompt_units, units_per_seq, prompt_row, sample_row):
    n_tok = proj.shape[0]
    n_units = n_tok // UNIT
    ids = functools.partial(_unit_ids, n_prompt_units=n_prompt_units, units_per_seq=units_per_seq)
    small_blk = C_SMALL // LANES
    qkv_blk = C_QKVB // (3 * QB)
    st_blk = (1, N_DIR, H_B, DK_B, DV_B)
    kern = functools.partial(_gdn_kernel, n_prompt_units=n_prompt_units, units_per_seq=units_per_seq,
                             prompt_row=prompt_row, sample_row=sample_row)
    return pl.pallas_call(
        kern,
        grid=(n_units,),
        in_specs=[pl.BlockSpec((UNIT, 3 * QB), lambda j: (ids(j)[0], qkv_blk)),
                  pl.BlockSpec((UNIT, 3 * QB), lambda j: (ids(j)[1], qkv_blk)),
                  pl.BlockSpec((UNIT, LANES), lambda j: (ids(j)[0], small_blk)),
                  pl.BlockSpec((UNIT, LANES), lambda j: (ids(j)[1], small_blk)),
                  pl.BlockSpec((SUBLANES, 3 * QB), lambda j: (0, 0)),
                  pl.BlockSpec((1, LANES), lambda j: (0, 0)),
                  pl.BlockSpec((1, LANES), lambda j: (0, 0)),
                  pl.BlockSpec(st_blk, lambda j: (ids(j)[2], 0, 0, 0, 0))],
        out_specs=[pl.BlockSpec((UNIT, VB), lambda j: (ids(j)[0], 0)),
                   pl.BlockSpec((UNIT, VB), lambda j: (ids(j)[1], 0)),
                   pl.BlockSpec(st_blk, lambda j: (jnp.minimum(j, n_prompt_units), 0, 0, 0, 0))],
        out_shape=[jax.ShapeDtypeStruct((n_tok, VB), F32),
                   jax.ShapeDtypeStruct((n_tok, VB), F32),
                   jax.ShapeDtypeStruct((n_prompt_units + 1, N_DIR, H_B, DK_B, DV_B), F32)],
        scratch_shapes=[pltpu.VMEM((N_DIR, H_B, DK_B, DV_B), F32), pltpu.VMEM((UNIT, 3 * QB), F32)],
        compiler_params=_cparams(("arbitrary",)),
        name="gdn",
    )(proj, proj, proj, proj, conv_w, gcoef, gdt, init)


def _head_rms(o, w):
    parts = []
    for h in range(o.shape[1] // LANES):
        parts.append(_rms(o[:, h * LANES:(h + 1) * LANES]))
    return jnp.concatenate(parts, axis=1) * w


def _post_kernel(x_ref, ga_ref, zb_ref, af_ref, ab_ref, df_ref, db_ref, mod_ref, wo_ref, nwa_ref, nwb_ref, n2_ref,
                 rw_ref, x1_ref, h2_ref, sc_ref):
    m = mod_ref[0]
    gla = _head_rms(af_ref[...] + ab_ref[...], nwa_ref[...]) * _silu(ga_ref[...])
    gdn = _head_rms(df_ref[...] + db_ref[...], nwb_ref[...]) * _silu(zb_ref[...])
    y = _dg(gla.astype(BF16), wo_ref[0:VA, :], NN) + _dg(gdn.astype(BF16), wo_ref[VA:VA + VB, :], NN)
    x1 = x_ref[...] + m[:, 2 * D_MODEL:3 * D_MODEL] * y
    x1_ref[...] = x1
    h2 = (_rms(x1) * n2_ref[...]) * (1.0 + m[:, 4 * D_MODEL:5 * D_MODEL]) + m[:, 3 * D_MODEL:4 * D_MODEL]
    for s in range(D_MODEL // LANES):
        h2_ref[:, s, :] = h2[:, s * LANES:(s + 1) * LANES]
    sc_ref[...] = _sigmoid(_mm3(h2, rw_ref[...]))


def _post(x_all, proj, o_af, o_ab, o_df, o_db, mod3, w_o, nwa, nwb, n2w, router_w, n_prompt_tok, sample_len, tm):
    n_tok = x_all.shape[0]
    cond = functools.partial(_cond_row, tiles_prompt=n_prompt_tok // tm, tiles_per_seq=sample_len // tm)
    tok = lambda i: (i, 0)
    const = lambda i: (0, 0)
    return pl.pallas_call(
        _post_kernel,
        grid=(n_tok // tm,),
        in_specs=[pl.BlockSpec((tm, D_MODEL), tok),
                  pl.BlockSpec((tm, VA), lambda i: (i, C_GA // VA)),
                  pl.BlockSpec((tm, VB), lambda i: (i, C_ZB // VB)),
                  pl.BlockSpec((tm, VA), tok), pl.BlockSpec((tm, VA), tok),
                  pl.BlockSpec((tm, VB), tok), pl.BlockSpec((tm, VB), tok),
                  pl.BlockSpec((1, 1, 6 * D_MODEL), lambda i: (cond(i), 0, 0)),
                  pl.BlockSpec((VA + VB, D_MODEL), const),
                  pl.BlockSpec((1, VA), const), pl.BlockSpec((1, VB), const), pl.BlockSpec((1, D_MODEL), const),
                  pl.BlockSpec((D_MODEL, N_EXPERTS), const)],
        out_specs=[pl.BlockSpec((tm, D_MODEL), tok),
                   pl.BlockSpec((tm, D_MODEL // LANES, LANES), lambda i: (i, 0, 0)),
                   pl.BlockSpec((tm, N_EXPERTS), tok)],
        out_shape=[jax.ShapeDtypeStruct((n_tok, D_MODEL), F32),
                   jax.ShapeDtypeStruct((n_tok, D_MODEL // LANES, LANES), F32),
                   jax.ShapeDtypeStruct((n_tok, N_EXPERTS), F32)],
        compiler_params=_cparams(("parallel",)),
        name="post",
    )(x_all, proj, proj, o_af, o_ab, o_df, o_db, mod3, w_o, nwa, nwb, n2w, router_w)


N_SLAB = D_MODEL // LANES


def _experts_kernel(bexp_ref, nused_ref, nvalid_ref, idx_hbm, h2_hbm, wg_ref, wu_ref, wd_ref, yk_hbm,
                    idx_smem, xbuf, ybuf, sem_i, sem_g, sem_s):
    b = pl.program_id(0)
    n_used = nused_ref[0]
    last = pl.num_programs(0) - 1

    def idx_copy(blk, slot):
        return pltpu.make_async_copy(idx_hbm.at[blk], idx_smem.at[slot], sem_i.at[slot])

    def gather_copy(tok, slot, i):
        return pltpu.make_async_copy(h2_hbm.at[tok], xbuf.at[slot, i], sem_g.at[slot])

    def scatter_copy(dst, slot, i):
        return pltpu.make_async_copy(ybuf.at[slot, i], yk_hbm.at[dst], sem_s.at[slot])

    def issue_gathers(blk, slot3, slot2):
        def body(i, carry):
            gather_copy(idx_smem[slot3, 0, i], slot2, i).start()
            return carry
        lax.fori_loop(0, nvalid_ref[blk], body, 0)

    def wait_rows(make, blk, slot):
        def body(i, carry):
            make(0, slot, i).wait()
            return carry
        lax.fori_loop(0, nvalid_ref[blk], body, 0)

    @pl.when(b == 0)
    def _():
        xbuf[...] = jnp.zeros(xbuf.shape, F32)
        idx_copy(0, 0).start()
        idx_copy(0, 0).wait()
        issue_gathers(0, 0, 0)

        @pl.when(n_used > 1)
        def _():
            idx_copy(1, 1).start()

    active = b < n_used

    @pl.when(jnp.logical_and(active, b + 1 < n_used))
    def _():
        idx_copy(b + 1, (b + 1) % 3).wait()
        issue_gathers(b + 1, (b + 1) % 3, (b + 1) % 2)

        @pl.when(b + 2 < n_used)
        def _():
            idx_copy(b + 2, (b + 2) % 3).start()

    @pl.when(active)
    def _():
        slot = b % 2
        wait_rows(gather_copy, b, slot)
        acc_g = jnp.zeros((EXPERT_BLOCK, D_EXPERT), F32)
        acc_u = jnp.zeros((EXPERT_BLOCK, D_EXPERT), F32)
        for s in range(N_SLAB):
            xs = xbuf[slot, :, s, :].astype(BF16)
            acc_g = acc_g + _dg(xs, wg_ref[0, s * LANES:(s + 1) * LANES, :].astype(BF16), NN)
            acc_u = acc_u + _dg(xs, wu_ref[0, s * LANES:(s + 1) * LANES, :].astype(BF16), NN)
        hmid = (_silu(acc_g) * acc_u).astype(BF16)

        @pl.when(b >= 2)
        def _():
            wait_rows(scatter_copy, b - 2, slot)

        for s in range(N_SLAB):
            ybuf[slot, :, s, :] = _dg(hmid, wd_ref[0, :, s * LANES:(s + 1) * LANES].astype(BF16), NN)

        def body(i, carry):
            scatter_copy(idx_smem[b % 3, 1, i], slot, i).start()
            return carry
        lax.fori_loop(0, nvalid_ref[b], body, 0)

    @pl.when(b == last)
    def _():
        @pl.when(n_used >= 2)
        def _():
            wait_rows(scatter_copy, n_used - 2, n_used % 2)

        @pl.when(n_used >= 1)
        def _():
            wait_rows(scatter_copy, n_used - 1, (n_used - 1) % 2)


def _experts(block_exp, n_used, n_valid, idx, h2_slab, w_gate, w_up, w_down, n_rows_out):
    n_blocks = block_exp.shape[0]
    wmap = lambda b, bexp, nu, nv: (bexp[b], 0, 0)
    grid_spec = pltpu.PrefetchScalarGridSpec(
        num_scalar_prefetch=3,
        grid=(n_blocks,),
        in_specs=[pl.BlockSpec(memory_space=pl.ANY),
                  pl.BlockSpec(memory_space=pl.ANY),
                  pl.BlockSpec((1, D_MODEL, D_EXPERT), wmap),
                  pl.BlockSpec((1, D_MODEL, D_EXPERT), wmap),
                  pl.BlockSpec((1, D_EXPERT, D_MODEL), wmap)],
        out_specs=pl.BlockSpec(memory_space=pl.ANY),
        scratch_shapes=[pltpu.SMEM((3, 2, EXPERT_BLOCK), jnp.int32),
                        pltpu.VMEM((2, EXPERT_BLOCK, N_SLAB, LANES), F32),
                        pltpu.VMEM((2, EXPERT_BLOCK, N_SLAB, LANES), F32),
                        pltpu.SemaphoreType.DMA((3,)),
                        pltpu.SemaphoreType.DMA((2,)),
                        pltpu.SemaphoreType.DMA((2,))])
    return pl.pallas_call(
        _experts_kernel,
        grid_spec=grid_spec,
        out_shape=jax.ShapeDtypeStruct((n_rows_out, N_SLAB, LANES), F32),
        compiler_params=_cparams(("arbitrary",)),
        name="experts",
    )(block_exp, n_used, n_valid, idx, h2_slab, w_gate, w_up, w_down)


def _final_kernel(x1_ref, h2_ref, yk_ref, w_ref, mod_ref, sg_ref, su_ref, sd_ref, fn_ref, o_ref):
    m = mod_ref[0]
    w = w_ref[...]
    parts = []
    for s in range(N_SLAB):
        acc = None
        for kk in range(TOP_K):
            term = yk_ref[kk, :, s, :] * w[:, kk:kk + 1]
            acc = term if acc is None else acc + term
        parts.append(acc)
    routed = jnp.concatenate(parts, axis=1)
    hb = jnp.concatenate([h2_ref[:, s, :] for s in range(N_SLAB)], axis=1).astype(BF16)
    hm = (_silu(_dg(hb, sg_ref[...], NN)) * _dg(hb, su_ref[...], NN)).astype(BF16)
    shared = _dg(hm, sd_ref[...], NN)
    x2 = x1_ref[...] + m[:, 5 * D_MODEL:6 * D_MODEL] * (routed + shared)
    o_ref[...] = _rms(x2) * fn_ref[...]


def _final(x1, h2_slab, yk4, wts, mod3, sg, su, sd, fnw, n_prompt_tok, sample_len, tm):
    n_tok = x1.shape[0]
    cond = functools.partial(_cond_row, tiles_prompt=n_prompt_tok // tm, tiles_per_seq=sample_len // tm)
    tok = lambda i: (i, 0)
    const = lambda i: (0, 0)
    return pl.pallas_call(
        _final_kernel,
        grid=(n_tok // tm,),
        in_specs=[pl.BlockSpec((tm, D_MODEL), tok),
                  pl.BlockSpec((tm, N_SLAB, LANES), lambda i: (i, 0, 0)),
                  pl.BlockSpec((TOP_K, tm, N_SLAB, LANES), lambda i: (0, i, 0, 0)),
                  pl.BlockSpec((tm, TOP_K), tok),
                  pl.BlockSpec((1, 1, 6 * D_MODEL), lambda i: (cond(i), 0, 0)),
                  pl.BlockSpec((D_MODEL, D_EXPERT), const),
                  pl.BlockSpec((D_MODEL, D_EXPERT), const),
                  pl.BlockSpec((D_EXPERT, D_MODEL), const),
                  pl.BlockSpec((1, D_MODEL), const)],
        out_specs=pl.BlockSpec((tm, D_MODEL), tok),
        out_shape=jax.ShapeDtypeStruct((n_tok, D_MODEL), F32),
        compiler_params=_cparams(("parallel",)),
        name="final",
    )(x1, h2_slab, yk4, wts, mod3, sg, su, sd, fnw)


def _route(scores, router_bias):
    n = scores.shape[0]
    biased = scores + router_bias.astype(F32)
    grp = jnp.sum(lax.top_k(biased.reshape(n, N_GROUP, N_EXPERTS // N_GROUP), 2)[0], axis=-1)
    _, gidx = lax.top_k(grp, TOPK_GROUP)
    gmask = jnp.any(jax.nn.one_hot(gidx, N_GROUP, dtype=jnp.int32) > 0, axis=1)
    masked = jnp.where(jnp.repeat(gmask, N_EXPERTS // N_GROUP, axis=1), biased, -jnp.inf)
    _, idx = lax.top_k(masked, TOP_K)
    w = jnp.take_along_axis(scores, idx, axis=1)
    w = w / jnp.sum(w, axis=-1, keepdims=True) * ROUTED_SCALE
    return idx, w


def _plan(idx, n_tok):
    m = n_tok * TOP_K
    e_flat = idx.reshape(-1).astype(jnp.int32)
    slot_flat = (jnp.arange(m, dtype=jnp.int32) % TOP_K) * n_tok + jnp.arange(m, dtype=jnp.int32) // TOP_K
    tok_flat = jnp.arange(m, dtype=jnp.int32) // TOP_K
    order = jnp.argsort(e_flat)
    e_sorted = e_flat[order]
    counts = jnp.bincount(e_flat, length=N_EXPERTS).astype(jnp.int32)
    padded = (counts + EXPERT_BLOCK - 1) // EXPERT_BLOCK * EXPERT_BLOCK
    pad_end = jnp.cumsum(padded)
    pad_start = pad_end - padded
    start = jnp.cumsum(counts) - counts
    dest = pad_start[e_sorted] + (jnp.arange(m, dtype=jnp.int32) - start[e_sorted])
    n_blocks = -(-(m + N_EXPERTS * (EXPERT_BLOCK - 1)) // EXPERT_BLOCK)
    p = n_blocks * EXPERT_BLOCK
    row_tok = jnp.zeros((p,), jnp.int32).at[dest].set(tok_flat[order])
    row_dst = jnp.zeros((p,), jnp.int32).at[dest].set(slot_flat[order])
    blk_start = jnp.arange(n_blocks, dtype=jnp.int32) * EXPERT_BLOCK
    block_exp = jnp.minimum(jnp.searchsorted(pad_end, blk_start, side='right'), N_EXPERTS - 1).astype(jnp.int32)
    n_used = (pad_end[-1] // EXPERT_BLOCK).astype(jnp.int32).reshape(1)
    real_end = pad_start[block_exp] + counts[block_exp]
    n_valid = jnp.where(blk_start < pad_end[-1], jnp.clip(real_end - blk_start, 0, EXPERT_BLOCK), 0).astype(jnp.int32)
    idx2 = jnp.stack([row_tok.reshape(n_blocks, EXPERT_BLOCK), row_dst.reshape(n_blocks, EXPERT_BLOCK)], axis=1)
    return block_exp, n_used, n_valid, idx2


def kernel(x_prompt, x_sample, state_gla, state_gdn, c, c_ctx, w_ada, b_ada, norm1_w, w_in, conv_w, gla_lr_w, gla_lr_b, gdn_a_log, gdn_dt_bias, gla_norm_w, gdn_norm_w, w_o, norm2_w, router_w, router_bias, exp_w_gate, exp_w_up, exp_w_down, sh_w_gate, sh_w_up, sh_w_down, final_norm_w):
    bp, lp, d = x_prompt.shape
    bs, ls, _ = x_sample.shape
    assert d == D_MODEL and lp == UNIT and ls % UNIT == 0 and w_ada.shape[0] == 1
    n_prompt_tok = bp * lp
    n_tok = n_prompt_tok + bs * ls
    n_prompt_units = n_prompt_tok // UNIT
    units_per_seq = ls // UNIT
    grid_w = 64
    layer = 0

    x_all = jnp.concatenate([x_prompt.reshape(n_prompt_tok, d), x_sample.reshape(bs * ls, d)], axis=0)
    cond = jnp.concatenate([c_ctx[None, :], c, jnp.zeros((SUBLANES - 1 - bs, d), F32)], axis=0)
    mod3 = _ada(cond, w_ada[layer], b_ada[layer][None, :]).reshape(SUBLANES, 1, 6 * d)

    wi = w_in[layer]
    o_lr = 2 * QA + 2 * VA
    o_qkvb = o_lr + N_DIR * GLA_RANK
    o_zb = o_qkvb + 3 * QB
    o_ab = o_zb + VB
    w_main = jnp.concatenate([wi[:, 0:o_lr], wi[:, o_qkvb:o_ab]], axis=1).astype(BF16)
    w_small = jnp.concatenate([wi[:, o_lr:o_qkvb], wi[:, o_ab:], jnp.zeros((d, LANES - 48), F32)], axis=1)
    proj = _inproj(x_all, mod3, norm1_w[layer][None, :], w_main, w_small, n_prompt_tok, ls, tm=512)

    wlr = jnp.zeros((N_DIR, LANES, QA), F32)
    for dd in range(N_DIR):
        wlr = wlr.at[dd, dd * GLA_RANK:(dd + 1) * GLA_RANK, :].set(gla_lr_w[layer, dd])
    blr = gla_lr_b[layer][:, None, :]
    init_gla = jnp.concatenate([jnp.zeros((1,) + state_gla.shape[2:], F32), state_gla[:, layer].astype(F32)], axis=0)
    o_af, o_ab_, s_gla = _gla(proj, wlr, blr, init_gla, n_prompt_units, units_per_seq)

    cw = jnp.concatenate([conv_w[layer], jnp.zeros((SUBLANES - CONV_K, 3 * QB), F32)], axis=0)
    gcoef = jnp.zeros((1, LANES), F32).at[0, SM_AB:SM_AB + N_DIR * H_B].set(-jnp.exp(gdn_a_log[layer].reshape(-1)))
    gdt = jnp.zeros((1, LANES), F32).at[0, SM_AB:SM_AB + N_DIR * H_B].set(gdn_dt_bias[layer].reshape(-1))
    init_gdn = jnp.concatenate([jnp.zeros((1,) + state_gdn.shape[2:], F32), state_gdn[:, layer].astype(F32)], axis=0)
    o_df, o_db, s_gdn = _gdn(proj, cw, gcoef, gdt, init_gdn, n_prompt_units, units_per_seq, lp, grid_w)

    nwa = jnp.tile(gla_norm_w[layer], H_A)[None, :]
    nwb = jnp.tile(gdn_norm_w[layer], H_B)[None, :]
    x1, h2_slab, scores = _post(x_all, proj, o_af, o_ab_, o_df, o_db, mod3, w_o[layer].astype(BF16), nwa, nwb,
                                norm2_w[layer][None, :], router_w[layer], n_prompt_tok, ls, tm=256)

    idx, wts = _route(scores, router_bias[layer])
    block_exp, n_used, n_valid, idx2 = _plan(idx, n_tok)
    yk = _experts(block_exp, n_used, n_valid, idx2, h2_slab, exp_w_gate[layer], exp_w_up[layer], exp_w_down[layer],
                  TOP_K * n_tok)
    yk4 = yk.reshape(TOP_K, n_tok, N_SLAB, LANES)

    y_all = _final(x1, h2_slab, yk4, wts, mod3, sh_w_gate[layer].astype(BF16), sh_w_up[layer].astype(BF16),
                   sh_w_down[layer].astype(BF16), final_norm_w[None, :], n_prompt_tok, ls, tm=256)

    y_prompt = y_all[:n_prompt_tok].reshape(bp, lp, d)
    y_sample = y_all[n_prompt_tok:].reshape(bs, ls, d)
    new_state_gla = s_gla[:n_prompt_units].reshape(bp, 1, N_DIR, H_A, DK_A, DV_A).astype(x_prompt.dtype)
    new_state_gdn = s_gdn[:n_prompt_units].reshape(bp, 1, N_DIR, H_B, DK_B, DV_B).astype(x_prompt.dtype)
    return (y_prompt, y_sample, new_state_gla, new_state_gdn)
```

```python
import functools

import jax
import jax.numpy as jnp
from jax import lax
from jax.experimental import pallas as pl
from jax.experimental.pallas import tpu as pltpu

F32 = jnp.float32
BF16 = jnp.bfloat16

D_MODEL = 1024
N_DIR = 2
H_A, DK_A, DV_A = 4, 64, 128
GLA_RANK = 16
GLA_NORMALIZER = 16.0
H_B, DK_B, DV_B = 4, 128, 128
CONV_K = 5
CHUNK = 64
QA, VA = H_A * DK_A, H_A * DV_A
QB, VB = H_B * DK_B, H_B * DV_B
N_EXPERTS = 256
TOP_K = 8
N_GROUP = 8
TOPK_GROUP = 4
GROUP_SIZE = N_EXPERTS // N_GROUP
D_EXPERT = 256
ROUTED_SCALE = 2.5
EPS = 1e-6
NEG_INF = float("-inf")

UNIT = 256
CHUNKS_PER_UNIT = UNIT // CHUNK
HC = H_B * CHUNK
ROUTE_TILE = 256
ROW_BLOCK = 256
LANES = 128
SUBLANES = 8
VMEM_LIMIT = 56 * 1024 * 1024

C_QKVA = 0
C_GA = 1024
C_QKVB = 1536
C_ZB = 3072
C_SMALL = 3584
D_PROJ = 3712
SM_AB = 32
SM_BETA = 40


def _cparams(sem):
    return pltpu.CompilerParams(dimension_semantics=sem, vmem_limit_bytes=VMEM_LIMIT)


def _split(a):
    hi = a.astype(BF16)
    lo = (a - hi.astype(F32)).astype(BF16)
    return hi, lo


def _dg(a, b, dims):
    return lax.dot_general(a, b, (dims, ((), ())), preferred_element_type=F32)


NN = ((1,), (0,))
NT = ((1,), (1,))
TN = ((0,), (0,))


def _mm(a, b, dims=NN):
    return _dg(a.astype(BF16), b.astype(BF16), dims)


def _mm3(a, b, dims=NN):
    ah, al = _split(a)
    bh, bl = _split(b)
    return _dg(ah, bh, dims) + (_dg(ah, bl, dims) + _dg(al, bh, dims))


def _mm_exact_lhs(a_bf16, b, dims=NN):
    bh, bl = _split(b)
    return _dg(a_bf16, bh, dims) + _dg(a_bf16, bl, dims)


def _silu(x):
    return x * (1.0 / (1.0 + jnp.exp(-x)))


def _sigmoid(x):
    return 1.0 / (1.0 + jnp.exp(-x))


def _softplus(x):
    return jnp.maximum(x, 0.0) + jnp.log1p(jnp.exp(-jnp.abs(x)))


def _log_sigmoid(x):
    return -_softplus(-x)


def _iota2(shape, dim):
    return lax.broadcasted_iota(jnp.int32, shape, dim)


def _rms(x):
    return x * lax.rsqrt(jnp.mean(x * x, axis=-1, keepdims=True) + EPS)


def _ada_kernel(c_ref, w_ref, b_ref, o_ref):
    o_ref[...] = _mm3(_silu(c_ref[...]), w_ref[...]) + b_ref[...]


def _ada(cond, w, b):
    n = w.shape[1]
    tn = 1536
    return pl.pallas_call(
        _ada_kernel,
        grid=(n // tn,),
        in_specs=[pl.BlockSpec((SUBLANES, D_MODEL), lambda i: (0, 0)),
                  pl.BlockSpec((D_MODEL, tn), lambda i: (0, i)),
                  pl.BlockSpec((1, tn), lambda i: (0, i))],
        out_specs=pl.BlockSpec((SUBLANES, tn), lambda i: (0, i)),
        out_shape=jax.ShapeDtypeStruct((SUBLANES, n), F32),
        compiler_params=_cparams(("parallel",)),
        name="ada",
    )(cond, w, b)


def _inproj_kernel(x_ref, mod_ref, nw_ref, w_ref, ws_ref, cw_ref, o_ref, *, tiles_prompt, prompt_row, sample_row):
    m = mod_ref[0]
    h = (_rms(x_ref[...]) * nw_ref[...]) * (1.0 + m[:, D_MODEL:2 * D_MODEL]) + m[:, 0:D_MODEL]
    hb = h.astype(BF16)
    for c0 in range(0, C_SMALL, 512):
        o_ref[:, c0:c0 + 512] = _dg(hb, w_ref[:, c0:c0 + 512], NN)
    o_ref[:, C_SMALL:D_PROJ] = _mm3(h, ws_ref[...])
    tm = x_ref.shape[0]
    row_len = jnp.where(pl.program_id(0) < tiles_prompt, prompt_row, sample_row)
    pos = _iota2((tm, LANES), 0) & (row_len - 1)
    for part in range(3 * H_B):
        cs = slice(C_QKVB + part * LANES, C_QKVB + (part + 1) * LANES)
        x = o_ref[:, cs]
        acc = jnp.zeros((tm, LANES), F32)
        for jj in range(CONV_K):
            off = jj - CONV_K // 2
            xs = x if off == 0 else pltpu.roll(x, (-off) % tm, 0)
            ok = jnp.logical_and(pos + off >= 0, pos + off < row_len)
            acc = acc + jnp.where(ok, xs, 0.0) * cw_ref[jj:jj + 1, part * LANES:(part + 1) * LANES]
        y = _silu(acc)
        if part < 2 * H_B:
            y = y * lax.rsqrt(jnp.sum(y * y, axis=-1, keepdims=True) + EPS)
            if part < H_B:
                y = y * (DK_B ** -0.5)
        o_ref[:, cs] = y


def _cond_row(i, tiles_prompt, tiles_per_seq):
    return jnp.where(i < tiles_prompt, 0, 1 + (i - tiles_prompt) // tiles_per_seq)


def _inproj(x_all, mod3, norm_w, w_main, w_small, conv_w, n_prompt_tok, sample_len, prompt_row, sample_row, tm):
    n_tok = x_all.shape[0]
    assert tm % prompt_row == 0 and tm % sample_row == 0
    cond = functools.partial(_cond_row, tiles_prompt=n_prompt_tok // tm, tiles_per_seq=sample_len // tm)
    kern = functools.partial(_inproj_kernel, tiles_prompt=n_prompt_tok // tm, prompt_row=prompt_row,
                             sample_row=sample_row)
    return pl.pallas_call(
        kern,
        grid=(n_tok // tm,),
        in_specs=[pl.BlockSpec((tm, D_MODEL), lambda i: (i, 0)),
                  pl.BlockSpec((1, 1, 6 * D_MODEL), lambda i: (cond(i), 0, 0)),
                  pl.BlockSpec((1, D_MODEL), lambda i: (0, 0)),
                  pl.BlockSpec((D_MODEL, C_SMALL), lambda i: (0, 0)),
                  pl.BlockSpec((D_MODEL, LANES), lambda i: (0, 0)),
                  pl.BlockSpec((SUBLANES, 3 * QB), lambda i: (0, 0))],
        out_specs=pl.BlockSpec((tm, D_PROJ), lambda i: (i, 0)),
        out_shape=jax.ShapeDtypeStruct((n_tok, D_PROJ), F32),
        compiler_params=_cparams(("parallel",)),
        name="inproj",
    )(x_all, mod3, norm_w, w_main, w_small, conv_w)


def _unit_ids(j, n_prompt_units, units_per_seq):
    jj = j - n_prompt_units
    b = jj // units_per_seq
    r = jj % units_per_seq
    is_prompt = j < n_prompt_units
    uf = j
    ub = jnp.where(is_prompt, j, n_prompt_units + b * units_per_seq + (units_per_seq - 1 - r))
    init_row = jnp.where(is_prompt, 0, 1 + b)
    first = jnp.logical_or(is_prompt, r == 0)
    return uf, ub, init_row, first


def _tri(rev):
    t = _iota2((CHUNK, CHUNK), 0)
    s = _iota2((CHUNK, CHUNK), 1)
    return (t <= s) if rev else (t >= s)


def _mm_exact_lhs_tn(a, ones_bf16):
    ah, al = _split(a)
    return _dg(ah, ones_bf16, TN) + _dg(al, ones_bf16, TN)


def _gla_chunk(q, k, v, la, states, rev):
    tri = _tri(rev).astype(BF16)
    b = _mm_exact_lhs(tri, la)
    mid, last = (CHUNK // 2 - 1, 0) if rev else (CHUNK // 2, CHUNK - 1)
    bref = b[mid:mid + 1, :]
    blast = b[last:last + 1, :]
    scale = DK_A ** -0.5
    qg = q * jnp.exp(b - bref) * scale
    kg = k * jnp.exp(bref - b)
    kd = k * jnp.exp(blast - b)
    qs = q * jnp.exp(b) * scale
    ones = jnp.ones((CHUNK, LANES), BF16)
    lane = _iota2((CHUNK, LANES), 1)
    row = _iota2((CHUNK, LANES), 0)
    s_in = lane % DK_A
    causal = (row <= s_in) if rev else (row >= s_in)
    zeros_v = jnp.zeros((CHUNK, DV_A), F32)
    outs, new_states = [], []
    for p in range(H_A // 2):
        ls = slice(p * LANES, (p + 1) * LANES)
        vs0 = v[:, (2 * p) * DV_A:(2 * p + 1) * DV_A]
        vs1 = v[:, (2 * p + 1) * DV_A:(2 * p + 2) * DV_A]
        s0, s1 = states[2 * p], states[2 * p + 1]
        kg_p = kg[:, ls]
        rhs_att = jnp.concatenate([jnp.where(lane < DK_A, kg_p, 0.0), jnp.where(lane >= DK_A, kg_p, 0.0)], axis=0)
        att = jnp.where(causal, _mm(qg[:, ls], rhs_att, NT), 0.0)
        rhs_o = jnp.concatenate([jnp.concatenate([vs0, zeros_v], axis=1),
                                 jnp.concatenate([zeros_v, vs1], axis=1),
                                 jnp.concatenate([s0, zeros_v], axis=1),
                                 jnp.concatenate([zeros_v, s1], axis=1)], axis=0)
        lhs_o = jnp.concatenate([att, qs[:, ls]], axis=1)
        outs.append(_mm(lhs_o, rhs_o))
        u = _mm(kd[:, ls], v[:, 2 * p * DV_A:(2 * p + 2) * DV_A], TN)
        dec = jnp.exp(_mm_exact_lhs_tn(la[:, ls], ones))
        new_states.append(dec[0:DK_A] * s0 + u[0:DK_A, 0:DV_A])
        new_states.append(dec[DK_A:2 * DK_A] * s1 + u[DK_A:2 * DK_A, DV_A:2 * DV_A])
    return jnp.concatenate(outs, axis=1), new_states


def _gla_kernel(qf_ref, qb_ref, sf_ref, sb_ref, wlr_ref, blr_ref, init_ref, of_ref, ob_ref, so_ref, s_ref,
                *, n_prompt_units, units_per_seq):
    j = pl.program_id(0)
    _, _, _, first = _unit_ids(j, n_prompt_units, units_per_seq)

    @pl.when(first)
    def _():
        s_ref[...] = init_ref[0]

    for d in range(N_DIR):
        x_ref, sm_ref, o_ref = (qf_ref, sf_ref, of_ref) if d == 0 else (qb_ref, sb_ref, ob_ref)
        la_unit = _log_sigmoid(_mm3(sm_ref[...], wlr_ref[d]) + blr_ref[d]) * (1.0 / GLA_NORMALIZER)
        states = [s_ref[d, h] for h in range(H_A)]
        order = range(CHUNKS_PER_UNIT - 1, -1, -1) if d == 1 else range(CHUNKS_PER_UNIT)
        for c in order:
            rows = slice(c * CHUNK, (c + 1) * CHUNK)
            o, states = _gla_chunk(x_ref[rows, 0:QA], x_ref[rows, QA:2 * QA], x_ref[rows, 2 * QA:2 * QA + VA],
                                   la_unit[rows], states, rev=(d == 1))
            o_ref[rows, :] = o
        for h in range(H_A):
            s_ref[d, h] = states[h]
    so_ref[0] = s_ref[...]


def _gla(proj, wlr, blr, init, n_prompt_units, units_per_seq):
    n_tok = proj.shape[0]
    n_units = n_tok // UNIT
    ids = functools.partial(_unit_ids, n_prompt_units=n_prompt_units, units_per_seq=units_per_seq)
    small_blk = C_SMALL // LANES
    st_blk = (1, N_DIR, H_A, DK_A, DV_A)
    kern = functools.partial(_gla_kernel, n_prompt_units=n_prompt_units, units_per_seq=units_per_seq)
    return pl.pallas_call(
        kern,
        grid=(n_units,),
        in_specs=[pl.BlockSpec((UNIT, 1024), lambda j: (ids(j)[0], 0)),
                  pl.BlockSpec((UNIT, 1024), lambda j: (ids(j)[1], 0)),
                  pl.BlockSpec((UNIT, LANES), lambda j: (ids(j)[0], small_blk)),
                  pl.BlockSpec((UNIT, LANES), lambda j: (ids(j)[1], small_blk)),
                  pl.BlockSpec((N_DIR, LANES, QA), lambda j: (0, 0, 0)),
                  pl.BlockSpec((N_DIR, 1, QA), lambda j: (0, 0, 0)),
                  pl.BlockSpec(st_blk, lambda j: (ids(j)[2], 0, 0, 0, 0))],
        out_specs=[pl.BlockSpec((UNIT, VA), lambda j: (ids(j)[0], 0)),
                   pl.BlockSpec((UNIT, VA), lambda j: (ids(j)[1], 0)),
                   pl.BlockSpec(st_blk, lambda j: (jnp.minimum(j, n_prompt_units), 0, 0, 0, 0))],
        out_shape=[jax.ShapeDtypeStruct((n_tok, VA), F32),
                   jax.ShapeDtypeStruct((n_tok, VA), F32),
                   jax.ShapeDtypeStruct((n_prompt_units + 1, N_DIR, H_A, DK_A, DV_A), F32)],
        scratch_shapes=[pltpu.VMEM((N_DIR, H_A, DK_A, DV_A), F32)],
        compiler_params=_cparams(("arbitrary",)),
        name="gla",
    )(proj, proj, proj, proj, wlr, blr, init)


def _stack_masks(rev):
    r = _iota2((HC, HC), 0)
    c = _iota2((HC, HC), 1)
    same = (r // CHUNK) == (c // CHUNK)
    tr, tc = r % CHUNK, c % CHUNK
    incl = jnp.logical_and(same, (tr <= tc) if rev else (tr >= tc))
    strict = jnp.logical_and(same, (tr < tc) if rev else (tr > tc))
    return incl, strict


def _spread(x):
    z = jnp.zeros((CHUNK, LANES), x.dtype)
    rows = []
    for h in range(H_B):
        xh = x[h * CHUNK:(h + 1) * CHUNK]
        rows.append(jnp.concatenate([xh if g == h else z for g in range(H_B)], axis=1))
    return jnp.concatenate(rows, axis=0)


def _gdn_prepare(q, k, v, gcb, beta, glast, rev):
    incl, strict = _stack_masks(rev)
    grow = gcb.T[0:1, :]
    diff = gcb[:, 0:1] - grow
    decay = jnp.where(incl, jnp.exp(jnp.where(incl, diff, 0.0)), 0.0)
    kb = k * beta
    m1 = _mm(jnp.concatenate([kb, q], axis=0), k, NT)
    a = jnp.where(strict, m1[0:HC] * decay, 0.0)
    att = m1[HC:2 * HC] * decay
    eye = (_iota2((HC, HC), 0) == _iota2((HC, HC), 1)).astype(F32)
    p = eye - a
    pw = a
    n = 2
    while n < CHUNK:
        pw = _mm(pw, pw)
        p = p + _mm(p, pw)
        n *= 2
    egc = jnp.exp(gcb)
    sol = _mm(p, jnp.concatenate([v * beta, kb * egc], axis=1))
    return sol[:, 0:DV_B], sol[:, DV_B:2 * DV_B], q * egc, k * jnp.exp(glast - gcb), att


def _gdn_scan_step(value, k_cum, q_dec, k_dec, att, gl_rows, s):
    kq = _mm(jnp.concatenate([_spread(k_cum), _spread(q_dec)], axis=0), s)
    v_new = value - kq[0:HC]
    o = kq[HC:2 * HC] + _mm(att, v_new)
    s_new = s * gl_rows + _mm(_spread(k_dec), v_new, TN)
    return o, s_new


def _gdn_kernel(xf_ref, xb_ref, sf_ref, sb_ref, gco_ref, gdt_ref, init_ref, of_ref, ob_ref, so_ref, s_ref,
                *, n_prompt_units, units_per_seq):
    j = pl.program_id(0)
    _, _, _, first = _unit_ids(j, n_prompt_units, units_per_seq)

    @pl.when(first)
    def _():
        s_ref[...] = init_ref[0]

    for d in range(N_DIR):
        x_ref, sm_ref, o_ref = (xf_ref, sf_ref, of_ref) if d == 0 else (xb_ref, sb_ref, ob_ref)
        rev = d == 1
        sm = sm_ref[...]
        g_all = gco_ref[...] * _softplus(sm + gdt_ref[...])
        beta_all = _sigmoid(sm)
        tri = _tri(rev).astype(BF16)
        last = 0 if rev else CHUNK - 1
        prepared = []
        for c in range(CHUNKS_PER_UNIT):
            rows = slice(c * CHUNK, (c + 1) * CHUNK)
            gc_all = _mm_exact_lhs(tri, g_all[rows])
            gcb, beta, glast = [], [], []
            for h in range(H_B):
                col = SM_AB + d * H_B + h
                colb = SM_BETA + d * H_B + h
                gh = jnp.broadcast_to(gc_all[:, col:col + 1], (CHUNK, LANES))
                gcb.append(gh)
                glast.append(jnp.broadcast_to(gh[last:last + 1, :], (CHUNK, LANES)))
                beta.append(jnp.broadcast_to(beta_all[rows, colb:colb + 1], (CHUNK, LANES)))
            stack = lambda base: jnp.concatenate(
                [x_ref[rows, base + h * LANES:base + (h + 1) * LANES] for h in range(H_B)], axis=0)
            prep = _gdn_prepare(stack(0), stack(QB), stack(2 * QB), jnp.concatenate(gcb, axis=0),
                                jnp.concatenate(beta, axis=0), jnp.concatenate(glast, axis=0), rev)
            gl_rows = jnp.concatenate([jnp.broadcast_to(jnp.exp(g[0:1, :]), (DK_B, DV_B)) for g in glast], axis=0)
            prepared.append(prep + (gl_rows,))
        s = s_ref[d]
        order = range(CHUNKS_PER_UNIT - 1, -1, -1) if rev else range(CHUNKS_PER_UNIT)
        for c in order:
            o, s = _gdn_scan_step(*prepared[c], s)
            for h in range(H_B):
                o_ref[c * CHUNK:(c + 1) * CHUNK, h * DV_B:(h + 1) * DV_B] = o[h * CHUNK:(h + 1) * CHUNK]
        s_ref[d] = s
    so_ref[0] = s_ref[...]


def _gdn(proj, gcoef, gdt, init, n_prompt_units, units_per_seq):
    n_tok = proj.shape[0]
    n_units = n_tok // UNIT
    ids = functools.partial(_unit_ids, n_prompt_units=n_prompt_units, units_per_seq=units_per_seq)
    small_blk = C_SMALL // LANES
    qkv_blk = C_QKVB // (3 * QB)
    st_blk = (1, N_DIR, H_B * DK_B, DV_B)
    kern = functools.partial(_gdn_kernel, n_prompt_units=n_prompt_units, units_per_seq=units_per_seq)
    return pl.pallas_call(
        kern,
        grid=(n_units,),
        in_specs=[pl.BlockSpec((UNIT, 3 * QB), lambda j: (ids(j)[0], qkv_blk)),
                  pl.BlockSpec((UNIT, 3 * QB), lambda j: (ids(j)[1], qkv_blk)),
                  pl.BlockSpec((UNIT, LANES), lambda j: (ids(j)[0], small_blk)),
                  pl.BlockSpec((UNIT, LANES), lambda j: (ids(j)[1], small_blk)),
                  pl.BlockSpec((1, LANES), lambda j: (0, 0)),
                  pl.BlockSpec((1, LANES), lambda j: (0, 0)),
                  pl.BlockSpec(st_blk, lambda j: (ids(j)[2], 0, 0, 0))],
        out_specs=[pl.BlockSpec((UNIT, VB), lambda j: (ids(j)[0], 0)),
                   pl.BlockSpec((UNIT, VB), lambda j: (ids(j)[1], 0)),
                   pl.BlockSpec(st_blk, lambda j: (jnp.minimum(j, n_prompt_units), 0, 0, 0))],
        out_shape=[jax.ShapeDtypeStruct((n_tok, VB), F32),
                   jax.ShapeDtypeStruct((n_tok, VB), F32),
                   jax.ShapeDtypeStruct((n_prompt_units + 1, N_DIR, H_B * DK_B, DV_B), F32)],
        scratch_shapes=[pltpu.VMEM((N_DIR, H_B * DK_B, DV_B), F32)],
        compiler_params=_cparams(("arbitrary",)),
        name="gdn",
    )(proj, proj, proj, proj, gcoef, gdt, init)


def _head_rms(o, w):
    parts = []
    for h in range(o.shape[1] // LANES):
        parts.append(_rms(o[:, h * LANES:(h + 1) * LANES]))
    return jnp.concatenate(parts, axis=1) * w


def _post_kernel(x_ref, ga_ref, zb_ref, af_ref, ab_ref, df_ref, db_ref, mod_ref, wo_ref, nwa_ref, nwb_ref, n2_ref,
                 rw_ref, x1_ref, h2_ref, sc_ref):
    m = mod_ref[0]
    gla = _head_rms(af_ref[...] + ab_ref[...], nwa_ref[...]) * _silu(ga_ref[...])
    gdn = _head_rms(df_ref[...] + db_ref[...], nwb_ref[...]) * _silu(zb_ref[...])
    y = _dg(gla.astype(BF16), wo_ref[0:VA, :], NN) + _dg(gdn.astype(BF16), wo_ref[VA:VA + VB, :], NN)
    x1 = x_ref[...] + m[:, 2 * D_MODEL:3 * D_MODEL] * y
    x1_ref[...] = x1
    h2 = (_rms(x1) * n2_ref[...]) * (1.0 + m[:, 4 * D_MODEL:5 * D_MODEL]) + m[:, 3 * D_MODEL:4 * D_MODEL]
    h2_ref[...] = h2
    sc_ref[...] = _sigmoid(_mm3(rw_ref[...], h2, NT))


def _post(x_all, proj, o_af, o_ab, o_df, o_db, mod3, w_o, nwa, nwb, n2w, router_wt, n_prompt_tok, sample_len, tm):
    n_tok = x_all.shape[0]
    cond = functools.partial(_cond_row, tiles_prompt=n_prompt_tok // tm, tiles_per_seq=sample_len // tm)
    tok = lambda i: (i, 0)
    const = lambda i: (0, 0)
    return pl.pallas_call(
        _post_kernel,
        grid=(n_tok // tm,),
        in_specs=[pl.BlockSpec((tm, D_MODEL), tok),
                  pl.BlockSpec((tm, VA), lambda i: (i, C_GA // VA)),
                  pl.BlockSpec((tm, VB), lambda i: (i, C_ZB // VB)),
                  pl.BlockSpec((tm, VA), tok), pl.BlockSpec((tm, VA), tok),
                  pl.BlockSpec((tm, VB), tok), pl.BlockSpec((tm, VB), tok),
                  pl.BlockSpec((1, 1, 6 * D_MODEL), lambda i: (cond(i), 0, 0)),
                  pl.BlockSpec((VA + VB, D_MODEL), const),
                  pl.BlockSpec((1, VA), const), pl.BlockSpec((1, VB), const), pl.BlockSpec((1, D_MODEL), const),
                  pl.BlockSpec((N_EXPERTS, D_MODEL), const)],
        out_specs=[pl.BlockSpec((tm, D_MODEL), tok),
                   pl.BlockSpec((tm, D_MODEL), tok),
                   pl.BlockSpec((N_EXPERTS, tm), lambda i: (0, i))],
        out_shape=[jax.ShapeDtypeStruct((n_tok, D_MODEL), F32),
                   jax.ShapeDtypeStruct((n_tok, D_MODEL), F32),
                   jax.ShapeDtypeStruct((N_EXPERTS, n_tok), F32)],
        compiler_params=_cparams(("parallel",)),
        name="post",
    )(x_all, proj, proj, o_af, o_ab, o_df, o_db, mod3, w_o, nwa, nwb, n2w, router_wt)


def _route_kernel(sc_ref, bias_ref, e_ref, r_ref, w_ref, cnt_ref, carry_ref):
    i = pl.program_id(0)
    t = sc_ref.shape[1]

    @pl.when(i == 0)
    def _():
        carry_ref[...] = jnp.zeros(carry_ref.shape, F32)

    s = sc_ref[...]
    biased = s + bias_ref[:, 0:1]
    gs = []
    for g in range(N_GROUP):
        blk = biased[g * GROUP_SIZE:(g + 1) * GROUP_SIZE]
        m1 = jnp.max(blk, axis=0, keepdims=True)
        n1 = jnp.sum((blk == m1).astype(F32), axis=0, keepdims=True)
        m2 = jnp.max(jnp.where(blk < m1, blk, NEG_INF), axis=0, keepdims=True)
        gs.append(m1 + jnp.where(n1 >= 2.0, m1, m2))
    gsc = jnp.concatenate(gs, axis=0)
    gid = _iota2((N_GROUP, t), 0)
    beaten = jnp.zeros((N_GROUP, t), F32)
    for g in range(N_GROUP):
        other = gsc[g:g + 1, :]
        wins = jnp.logical_or(other > gsc, jnp.logical_and(other == gsc, g < gid))
        beaten = beaten + wins.astype(F32)
    masked = jnp.concatenate(
        [jnp.where(beaten[g:g + 1, :] < float(TOPK_GROUP), biased[g * GROUP_SIZE:(g + 1) * GROUP_SIZE], NEG_INF)
         for g in range(N_GROUP)], axis=0)
    eid = _iota2((N_EXPERTS, t), 0).astype(F32)
    sel = jnp.zeros((N_EXPERTS, t), F32)
    picks, scores = [], []
    for _ in range(TOP_K):
        m = jnp.max(masked, axis=0, keepdims=True)
        first = jnp.min(jnp.where(masked == m, eid, float(N_EXPERTS)), axis=0, keepdims=True)
        hit = eid == first
        scores.append(jnp.sum(jnp.where(hit, s, 0.0), axis=0, keepdims=True))
        masked = jnp.where(hit, NEG_INF, masked)
        sel = sel + hit.astype(F32)
        picks.append(first)
    upper = (_iota2((t, t), 0) < _iota2((t, t), 1)).astype(BF16)
    carry = carry_ref[...]
    prefix = _dg(sel.astype(BF16), upper, NN) + jnp.concatenate([carry] * (t // LANES), axis=1)
    ranks = [jnp.sum(jnp.where(eid == p, prefix, 0.0), axis=0, keepdims=True) for p in picks]
    carry = carry + _dg(sel.astype(BF16), jnp.ones((t, LANES), BF16), NN)
    carry_ref[...] = carry
    cnt_ref[...] = carry
    sc8 = jnp.concatenate(scores, axis=0)
    e_ref[...] = jnp.concatenate(picks, axis=0).astype(jnp.int32)
    r_ref[...] = jnp.concatenate(ranks, axis=0).astype(jnp.int32)
    w_ref[...] = sc8 / jnp.sum(sc8, axis=0, keepdims=True) * ROUTED_SCALE


def _route(scores_t, bias_col):
    n_tok = scores_t.shape[1]
    t = ROUTE_TILE
    slot = lambda i: (0, i)
    return pl.pallas_call(
        _route_kernel,
        grid=(n_tok // t,),
        in_specs=[pl.BlockSpec((N_EXPERTS, t), slot),
                  pl.BlockSpec((N_EXPERTS, LANES), lambda i: (0, 0))],
        out_specs=[pl.BlockSpec((TOP_K, t), slot), pl.BlockSpec((TOP_K, t), slot), pl.BlockSpec((TOP_K, t), slot),
                   pl.BlockSpec((N_EXPERTS, LANES), lambda i: (0, 0))],
        out_shape=[jax.ShapeDtypeStruct((TOP_K, n_tok), jnp.int32),
                   jax.ShapeDtypeStruct((TOP_K, n_tok), jnp.int32),
                   jax.ShapeDtypeStruct((TOP_K, n_tok), F32),
                   jax.ShapeDtypeStruct((N_EXPERTS, LANES), F32)],
        scratch_shapes=[pltpu.VMEM((N_EXPERTS, LANES), F32)],
        compiler_params=_cparams(("arbitrary",)),
        name="route",
    )(scores_t, bias_col)


def _dest_kernel(start_ref, e_ref, r_ref, d_ref):
    e = e_ref[...]

    def body(x, acc):
        return jnp.where(e == x, start_ref[x], acc)

    d_ref[...] = r_ref[...] + lax.fori_loop(0, N_EXPERTS, body, jnp.zeros(e.shape, jnp.int32))


def _dest(start, e8, r8, tt):
    n_tok = e8.shape[1]
    slot = lambda i, st: (0, i)
    return pl.pallas_call(
        _dest_kernel,
        grid_spec=pltpu.PrefetchScalarGridSpec(
            num_scalar_prefetch=1, grid=(n_tok // tt,),
            in_specs=[pl.BlockSpec((TOP_K, tt), slot), pl.BlockSpec((TOP_K, tt), slot)],
            out_specs=pl.BlockSpec((TOP_K, tt), slot)),
        out_shape=jax.ShapeDtypeStruct((TOP_K, n_tok), jnp.int32),
        compiler_params=_cparams(("parallel",)),
        name="dest",
    )(start, e8, r8)


def _items(counts, n_rows):
    n_blocks = n_rows // ROW_BLOCK
    max_items = n_blocks + N_EXPERTS - 1
    end = jnp.cumsum(counts)
    start = end - counts
    first_blk = start // ROW_BLOCK
    n_it = jnp.where(counts > 0, (end - 1) // ROW_BLOCK - first_blk + 1, 0)
    it_end = jnp.cumsum(n_it)
    it_start = it_end - n_it
    i = jnp.arange(max_items, dtype=jnp.int32)
    valid = i < it_end[-1]
    ex = jnp.minimum(jnp.sum((it_end[None, :] <= i[:, None]).astype(jnp.int32), axis=1), N_EXPERTS - 1)
    onehot = ex[:, None] == jnp.arange(N_EXPERTS, dtype=jnp.int32)[None, :]
    pick = lambda tab: jnp.sum(jnp.where(onehot, tab[None, :], 0), axis=1)
    blk = pick(first_blk) + (i - pick(it_start))
    lo = jnp.maximum(pick(start), blk * ROW_BLOCK) - blk * ROW_BLOCK
    hi = jnp.minimum(pick(end), (blk + 1) * ROW_BLOCK) - blk * ROW_BLOCK
    blk = jnp.where(valid, blk, n_blocks - 1).astype(jnp.int32)
    lo = jnp.where(valid, lo, 0).astype(jnp.int32)
    hi = jnp.where(valid, hi, 0).astype(jnp.int32)
    return start.astype(jnp.int32), blk, ex.astype(jnp.int32), lo, hi


def _dispatch_kernel(h2_ref, dest_hbm, xs_hbm, dsm, sem_d, sem_s):
    i = pl.program_id(0)
    n = pl.num_programs(0)
    tm = h2_ref.shape[0]

    def dest_copy(step, slot):
        return pltpu.make_async_copy(dest_hbm.at[:, pl.ds(step * tm, tm)], dsm.at[slot], sem_d.at[slot])

    def row_copy(t, dst):
        return pltpu.make_async_copy(h2_ref.at[pl.ds(t, 1), :], xs_hbm.at[pl.ds(dst, 1), :], sem_s.at[0])

    @pl.when(i == 0)
    def _():
        dest_copy(0, 0).start()

    slot = i % 2
    dest_copy(i, slot).wait()

    @pl.when(i + 1 < n)
    def _():
        dest_copy(i + 1, 1 - slot).start()

    def issue(t, carry):
        for k in range(TOP_K):
            row_copy(t, dsm[slot, k, t]).start()
        return carry
    lax.fori_loop(0, tm, issue, 0)

    def drain(t, carry):
        for k in range(TOP_K):
            row_copy(t, 0).wait()
        return carry
    lax.fori_loop(0, tm, drain, 0)


def _dispatch(h2, dest, tm):
    n_tok = h2.shape[0]
    return pl.pallas_call(
        _dispatch_kernel,
        grid=(n_tok // tm,),
        in_specs=[pl.BlockSpec((tm, D_MODEL), lambda i: (i, 0)),
                  pl.BlockSpec(memory_space=pl.ANY)],
        out_specs=pl.BlockSpec(memory_space=pl.ANY),
        out_shape=jax.ShapeDtypeStruct((n_tok * TOP_K, D_MODEL), F32),
        scratch_shapes=[pltpu.SMEM((2, TOP_K, tm), jnp.int32),
                        pltpu.SemaphoreType.DMA((2,)),
                        pltpu.SemaphoreType.DMA((1,))],
        compiler_params=_cparams(("arbitrary",)),
        name="dispatch",
    )(h2, dest)


def _experts_kernel(blk_ref, exp_ref, lo_ref, hi_ref, x_ref, wg_ref, wu_ref, wd_ref, y_ref):
    i = pl.program_id(0)
    lo, hi = lo_ref[i], hi_ref[i]

    @pl.when(hi > lo)
    def _():
        x = x_ref[...].astype(BF16)
        g = _dg(x, wg_ref[0].astype(BF16), NN)
        u = _dg(x, wu_ref[0].astype(BF16), NN)
        hmid = (_silu(g) * u).astype(BF16)
        y = _dg(hmid, wd_ref[0].astype(BF16), NN)
        row = _iota2((ROW_BLOCK, D_MODEL), 0)
        mine = jnp.logical_and(row >= lo, row < hi)

        @pl.when(lo == 0)
        def _():
            y_ref[...] = jnp.where(mine, y, 0.0)

        @pl.when(lo > 0)
        def _():
            y_ref[...] = jnp.where(mine, y, y_ref[...])


def _experts(item_blk, item_exp, item_lo, item_hi, xs, w_gate, w_up, w_down):
    n_items = item_blk.shape[0]
    xmap = lambda i, blk, ex, lo, hi: (blk[i], 0)
    wmap = lambda i, blk, ex, lo, hi: (ex[i], 0, 0)
    return pl.pallas_call(
        _experts_kernel,
        grid_spec=pltpu.PrefetchScalarGridSpec(
            num_scalar_prefetch=4, grid=(n_items,),
            in_specs=[pl.BlockSpec((ROW_BLOCK, D_MODEL), xmap),
                      pl.BlockSpec((1, D_MODEL, D_EXPERT), wmap),
                      pl.BlockSpec((1, D_MODEL, D_EXPERT), wmap),
                      pl.BlockSpec((1, D_EXPERT, D_MODEL), wmap)],
            out_specs=pl.BlockSpec((ROW_BLOCK, D_MODEL), xmap)),
        out_shape=jax.ShapeDtypeStruct(xs.shape, F32),
        compiler_params=_cparams(("arbitrary",)),
        name="experts",
    )(item_blk, item_exp, item_lo, item_hi, xs, w_gate, w_up, w_down)


def _final_kernel(x1_ref, h2_ref, w_ref, mod_ref, sg_ref, su_ref, sd_ref, fn_ref, dest_hbm, ys_hbm, o_ref,
                  dsm, gbuf, sem_d, sem_g):
    i = pl.program_id(0)
    n = pl.num_programs(0)
    tm = x1_ref.shape[0]

    def dest_copy(step, slot):
        return pltpu.make_async_copy(dest_hbm.at[:, pl.ds(step * tm, tm)], dsm.at[slot], sem_d.at[slot])

    def row_copy(src, slot, k, t):
        return pltpu.make_async_copy(ys_hbm.at[pl.ds(src, 1), :], gbuf.at[slot, k, pl.ds(t, 1), :], sem_g.at[slot])

    def issue_gathers(slot3, slot2):
        def body(t, carry):
            for k in range(TOP_K):
                row_copy(dsm[slot3, k, t], slot2, k, t).start()
            return carry
        lax.fori_loop(0, tm, body, 0)

    @pl.when(i == 0)
    def _():
        dest_copy(0, 0).start()
        dest_copy(0, 0).wait()
        issue_gathers(0, 0)

        @pl.when(n > 1)
        def _():
            dest_copy(1, 1).start()

    @pl.when(i + 1 < n)
    def _():
        dest_copy(i + 1, (i + 1) % 3).wait()
        issue_gathers((i + 1) % 3, (i + 1) % 2)

        @pl.when(i + 2 < n)
        def _():
            dest_copy(i + 2, (i + 2) % 3).start()

    slot = i % 2

    def drain(t, carry):
        for k in range(TOP_K):
            row_copy(0, slot, k, t).wait()
        return carry
    lax.fori_loop(0, tm, drain, 0)

    m = mod_ref[0]
    w = w_ref[...]
    routed = gbuf[slot, 0] * w[:, 0:1]
    for kk in range(1, TOP_K):
        routed = routed + gbuf[slot, kk] * w[:, kk:kk + 1]
    hb = h2_ref[...].astype(BF16)
    hm = (_silu(_dg(hb, sg_ref[...], NN)) * _dg(hb, su_ref[...], NN)).astype(BF16)
    shared = _dg(hm, sd_ref[...], NN)
    x2 = x1_ref[...] + m[:, 5 * D_MODEL:6 * D_MODEL] * (routed + shared)
    o_ref[...] = _rms(x2) * fn_ref[...]


def _final(x1, h2, wts, mod3, sg, su, sd, fnw, dest, ys, n_prompt_tok, sample_len, tm):
    n_tok = x1.shape[0]
    cond = functools.partial(_cond_row, tiles_prompt=n_prompt_tok // tm, tiles_per_seq=sample_len // tm)
    tok = lambda i: (i, 0)
    const = lambda i: (0, 0)
    return pl.pallas_call(
        _final_kernel,
        grid=(n_tok // tm,),
        in_specs=[pl.BlockSpec((tm, D_MODEL), tok),
                  pl.BlockSpec((tm, D_MODEL), tok),
                  pl.BlockSpec((tm, TOP_K), tok),
                  pl.BlockSpec((1, 1, 6 * D_MODEL), lambda i: (cond(i), 0, 0)),
                  pl.BlockSpec((D_MODEL, D_EXPERT), const),
                  pl.BlockSpec((D_MODEL, D_EXPERT), const),
                  pl.BlockSpec((D_EXPERT, D_MODEL), const),
                  pl.BlockSpec((1, D_MODEL), const),
                  pl.BlockSpec(memory_space=pl.ANY),
                  pl.BlockSpec(memory_space=pl.ANY)],
        out_specs=pl.BlockSpec((tm, D_MODEL), tok),
        out_shape=jax.ShapeDtypeStruct((n_tok, D_MODEL), F32),
        scratch_shapes=[pltpu.SMEM((3, TOP_K, tm), jnp.int32),
                        pltpu.VMEM((2, TOP_K, tm, D_MODEL), F32),
                        pltpu.SemaphoreType.DMA((3,)),
                        pltpu.SemaphoreType.DMA((2,))],
        compiler_params=_cparams(("arbitrary",)),
        name="final",
    )(x1, h2, wts, mod3, sg, su, sd, fnw, dest, ys)


def kernel(x_prompt, x_sample, state_gla, state_gdn, c, c_ctx, w_ada, b_ada, norm1_w, w_in, conv_w, gla_lr_w, gla_lr_b, gdn_a_log, gdn_dt_bias, gla_norm_w, gdn_norm_w, w_o, norm2_w, router_w, router_bias, exp_w_gate, exp_w_up, exp_w_down, sh_w_gate, sh_w_up, sh_w_down, final_norm_w):
    bp, lp, d = x_prompt.shape
    bs, ls, _ = x_sample.shape
    assert d == D_MODEL and lp == UNIT and ls % UNIT == 0 and w_ada.shape[0] == 1
    n_prompt_tok = bp * lp
    n_tok = n_prompt_tok + bs * ls
    n_prompt_units = n_prompt_tok // UNIT
    units_per_seq = ls // UNIT
    grid_w = 64
    layer = 0
    tm_in = 512
    tm = 256
    assert n_prompt_tok % tm_in == 0 and ls % tm_in == 0 and n_tok % ROUTE_TILE == 0

    x_all = jnp.concatenate([x_prompt.reshape(n_prompt_tok, d), x_sample.reshape(bs * ls, d)], axis=0)
    cond = jnp.concatenate([c_ctx[None, :], c, jnp.zeros((SUBLANES - 1 - bs, d), F32)], axis=0)
    mod3 = _ada(cond, w_ada[layer], b_ada[layer][None, :]).reshape(SUBLANES, 1, 6 * d)

    wi = w_in[layer]
    o_lr = 2 * QA + 2 * VA
    o_qkvb = o_lr + N_DIR * GLA_RANK
    o_zb = o_qkvb + 3 * QB
    o_ab = o_zb + VB
    w_main = jnp.concatenate([wi[:, 0:o_lr], wi[:, o_qkvb:o_ab]], axis=1).astype(BF16)
    w_small = jnp.concatenate([wi[:, o_lr:o_qkvb], wi[:, o_ab:], jnp.zeros((d, LANES - 48), F32)], axis=1)
    cw = jnp.concatenate([conv_w[layer], jnp.zeros((SUBLANES - CONV_K, 3 * QB), F32)], axis=0)
    proj = _inproj(x_all, mod3, norm1_w[layer][None, :], w_main, w_small, cw, n_prompt_tok, ls, lp, grid_w, tm_in)

    wlr = jnp.zeros((N_DIR, LANES, QA), F32)
    for dd in range(N_DIR):
        wlr = wlr.at[dd, dd * GLA_RANK:(dd + 1) * GLA_RANK, :].set(gla_lr_w[layer, dd])
    blr = gla_lr_b[layer][:, None, :]
    init_gla = jnp.concatenate([jnp.zeros((1,) + state_gla.shape[2:], F32), state_gla[:, layer].astype(F32)], axis=0)
    o_af, o_ab_, s_gla = _gla(proj, wlr, blr, init_gla, n_prompt_units, units_per_seq)

    gcoef = jnp.zeros((1, LANES), F32).at[0, SM_AB:SM_AB + N_DIR * H_B].set(-jnp.exp(gdn_a_log[layer].reshape(-1)))
    gdt = jnp.zeros((1, LANES), F32).at[0, SM_AB:SM_AB + N_DIR * H_B].set(gdn_dt_bias[layer].reshape(-1))
    init_gdn = jnp.concatenate([jnp.zeros((1, N_DIR, H_B * DK_B, DV_B), F32),
                                state_gdn[:, layer].astype(F32).reshape(bs, N_DIR, H_B * DK_B, DV_B)], axis=0)
    o_df, o_db, s_gdn = _gdn(proj, gcoef, gdt, init_gdn, n_prompt_units, units_per_seq)

    nwa = jnp.tile(gla_norm_w[layer], H_A)[None, :]
    nwb = jnp.tile(gdn_norm_w[layer], H_B)[None, :]
    x1, h2, scores_t = _post(x_all, proj, o_af, o_ab_, o_df, o_db, mod3, w_o[layer].astype(BF16), nwa, nwb,
                             norm2_w[layer][None, :], router_w[layer].T, n_prompt_tok, ls, tm)

    bias_col = jnp.broadcast_to(router_bias[layer].astype(F32)[:, None], (N_EXPERTS, LANES))
    e8, r8, w8, cnt = _route(scores_t, bias_col)
    start, item_blk, item_exp, item_lo, item_hi = _items(cnt[:, 0].astype(jnp.int32), n_tok * TOP_K)
    dest = _dest(start, e8, r8, tt=min(2048, n_tok))

    xs = _dispatch(h2, dest, tm)
    ys = _experts(item_blk, item_exp, item_lo, item_hi, xs, exp_w_gate[layer], exp_w_up[layer], exp_w_down[layer])
    y_all = _final(x1, h2, w8.T, mod3, sh_w_gate[layer].astype(BF16), sh_w_up[layer].astype(BF16),
                   sh_w_down[layer].astype(BF16), final_norm_w[None, :], dest, ys, n_prompt_tok, ls, tm)

    y_prompt = y_all[:n_prompt_tok].reshape(bp, lp, d)
    y_sample = y_all[n_prompt_tok:].reshape(bs, ls, d)
    new_state_gla = s_gla[:n_prompt_units].reshape(bp, 1, N_DIR, H_A, DK_A, DV_A).astype(x_prompt.dtype)
    new_state_gdn = s_gdn[:n_prompt_units].reshape(bp, 1, N_DIR, H_B, DK_B, DV_B).astype(x_prompt.dtype)
    return (y_prompt, y_sample, new_state_gla, new_state_gdn)
```

```python
import functools

import jax
import jax.numpy as jnp
from jax import lax
from jax.experimental import pallas as pl
from jax.experimental.pallas import tpu as pltpu

F32 = jnp.float32
BF16 = jnp.bfloat16

D_MODEL = 1024
N_DIR = 2
H_A, DK_A, DV_A = 4, 64, 128
GLA_RANK = 16
GLA_NORMALIZER = 16.0
H_B, DK_B, DV_B = 4, 128, 128
CONV_K = 5
CHUNK = 64
QA, VA = H_A * DK_A, H_A * DV_A
QB, VB = H_B * DK_B, H_B * DV_B
N_EXPERTS = 256
TOP_K = 8
N_GROUP = 8
TOPK_GROUP = 4
GROUP_SIZE = N_EXPERTS // N_GROUP
D_EXPERT = 256
ROUTED_SCALE = 2.5
EPS = 1e-6
NEG_INF = float("-inf")

UNIT = 256
CHUNKS_PER_UNIT = UNIT // CHUNK
HC = H_B * CHUNK
INV_BLOCK = 8
D_PACK = D_MODEL // 2
ROUTE_TILE = 256
ROW_BLOCK = 256
N_DMA_QUEUES = 2
LANES = 128
SUBLANES = 8
VMEM_LIMIT = 56 * 1024 * 1024

C_QKVA = 0
C_GA = 1024
C_QKVB = 1536
C_ZB = 3072
C_SMALL = 3584
D_PROJ = 3712
SM_AB = 32
SM_BETA = 40


def _cparams(sem):
    return pltpu.CompilerParams(dimension_semantics=sem, vmem_limit_bytes=VMEM_LIMIT)


def _split(a):
    hi = a.astype(BF16)
    lo = (a - hi.astype(F32)).astype(BF16)
    return hi, lo


def _dg(a, b, dims):
    return lax.dot_general(a, b, (dims, ((), ())), preferred_element_type=F32)


NN = ((1,), (0,))
NT = ((1,), (1,))
TN = ((0,), (0,))


def _mm(a, b, dims=NN):
    return _dg(a.astype(BF16), b.astype(BF16), dims)


def _mm3(a, b, dims=NN):
    ah, al = _split(a)
    bh, bl = _split(b)
    return _dg(ah, bh, dims) + (_dg(ah, bl, dims) + _dg(al, bh, dims))


def _mm_exact_lhs(a_bf16, b, dims=NN):
    bh, bl = _split(b)
    return _dg(a_bf16, bh, dims) + _dg(a_bf16, bl, dims)


def _silu(x):
    return x * (1.0 / (1.0 + jnp.exp(-x)))


def _sigmoid(x):
    return 1.0 / (1.0 + jnp.exp(-x))


def _softplus(x):
    return jnp.maximum(x, 0.0) + jnp.log1p(jnp.exp(-jnp.abs(x)))


def _log_sigmoid(x):
    return -_softplus(-x)


def _iota2(shape, dim):
    return lax.broadcasted_iota(jnp.int32, shape, dim)


def _rms(x):
    return x * lax.rsqrt(jnp.mean(x * x, axis=-1, keepdims=True) + EPS)


def _pack_rows(x):
    lo = lax.bitcast_convert_type(x[:, 0:D_PACK].astype(BF16).astype(F32), jnp.uint32)
    hi = lax.bitcast_convert_type(x[:, D_PACK:D_MODEL].astype(BF16).astype(F32), jnp.uint32)
    return (lo >> 16) | (hi & jnp.uint32(0xFFFF0000))


def _unpack_rows(p):
    lo = lax.bitcast_convert_type(p << 16, F32)
    hi = lax.bitcast_convert_type(p & jnp.uint32(0xFFFF0000), F32)
    return lo, hi


def _ada_kernel(c_ref, w_ref, b_ref, o_ref):
    o_ref[...] = _mm3(_silu(c_ref[...]), w_ref[...]) + b_ref[...]


def _ada(cond, w, b):
    n = w.shape[1]
    tn = 1536
    return pl.pallas_call(
        _ada_kernel,
        grid=(n // tn,),
        in_specs=[pl.BlockSpec((SUBLANES, D_MODEL), lambda i: (0, 0)),
                  pl.BlockSpec((D_MODEL, tn), lambda i: (0, i)),
                  pl.BlockSpec((1, tn), lambda i: (0, i))],
        out_specs=pl.BlockSpec((SUBLANES, tn), lambda i: (0, i)),
        out_shape=jax.ShapeDtypeStruct((SUBLANES, n), F32),
        compiler_params=_cparams(("parallel",)),
        name="ada",
    )(cond, w, b)


def _inproj_kernel(x_ref, mod_ref, nw_ref, w_ref, ws_ref, cw_ref, o_ref, *, tiles_prompt, prompt_row, sample_row):
    m = mod_ref[0]
    h = (_rms(x_ref[...]) * nw_ref[...]) * (1.0 + m[:, D_MODEL:2 * D_MODEL]) + m[:, 0:D_MODEL]
    hb = h.astype(BF16)
    for c0 in range(0, C_SMALL, 512):
        o_ref[:, c0:c0 + 512] = _dg(hb, w_ref[:, c0:c0 + 512], NN)
    o_ref[:, C_SMALL:D_PROJ] = _mm3(h, ws_ref[...])
    tm = x_ref.shape[0]
    row_len = jnp.where(pl.program_id(0) < tiles_prompt, prompt_row, sample_row)
    pos = _iota2((tm, LANES), 0) & (row_len - 1)
    for part in range(3 * H_B):
        cs = slice(C_QKVB + part * LANES, C_QKVB + (part + 1) * LANES)
        x = o_ref[:, cs]
        acc = jnp.zeros((tm, LANES), F32)
        for jj in range(CONV_K):
            off = jj - CONV_K // 2
            xs = x if off == 0 else pltpu.roll(x, (-off) % tm, 0)
            ok = jnp.logical_and(pos + off >= 0, pos + off < row_len)
            acc = acc + jnp.where(ok, xs, 0.0) * cw_ref[jj:jj + 1, part * LANES:(part + 1) * LANES]
        y = _silu(acc)
        if part < 2 * H_B:
            y = y * lax.rsqrt(jnp.sum(y * y, axis=-1, keepdims=True) + EPS)
            if part < H_B:
                y = y * (DK_B ** -0.5)
        o_ref[:, cs] = y


def _cond_row(i, tiles_prompt, tiles_per_seq):
    return jnp.where(i < tiles_prompt, 0, 1 + (i - tiles_prompt) // tiles_per_seq)


def _inproj(x_all, mod3, norm_w, w_main, w_small, conv_w, n_prompt_tok, sample_len, prompt_row, sample_row, tm):
    n_tok = x_all.shape[0]
    assert tm % prompt_row == 0 and tm % sample_row == 0
    cond = functools.partial(_cond_row, tiles_prompt=n_prompt_tok // tm, tiles_per_seq=sample_len // tm)
    kern = functools.partial(_inproj_kernel, tiles_prompt=n_prompt_tok // tm, prompt_row=prompt_row,
                             sample_row=sample_row)
    return pl.pallas_call(
        kern,
        grid=(n_tok // tm,),
        in_specs=[pl.BlockSpec((tm, D_MODEL), lambda i: (i, 0)),
                  pl.BlockSpec((1, 1, 6 * D_MODEL), lambda i: (cond(i), 0, 0)),
                  pl.BlockSpec((1, D_MODEL), lambda i: (0, 0)),
                  pl.BlockSpec((D_MODEL, C_SMALL), lambda i: (0, 0)),
                  pl.BlockSpec((D_MODEL, LANES), lambda i: (0, 0)),
                  pl.BlockSpec((SUBLANES, 3 * QB), lambda i: (0, 0))],
        out_specs=pl.BlockSpec((tm, D_PROJ), lambda i: (i, 0)),
        out_shape=jax.ShapeDtypeStruct((n_tok, D_PROJ), F32),
        compiler_params=_cparams(("parallel",)),
        name="inproj",
    )(x_all, mod3, norm_w, w_main, w_small, conv_w)


def _unit_ids(j, n_prompt_units, units_per_seq):
    jj = j - n_prompt_units
    b = jj // units_per_seq
    r = jj % units_per_seq
    is_prompt = j < n_prompt_units
    uf = j
    ub = jnp.where(is_prompt, j, n_prompt_units + b * units_per_seq + (units_per_seq - 1 - r))
    init_row = jnp.where(is_prompt, 0, 1 + b)
    first = jnp.logical_or(is_prompt, r == 0)
    return uf, ub, init_row, first


def _tri(rev):
    t = _iota2((CHUNK, CHUNK), 0)
    s = _iota2((CHUNK, CHUNK), 1)
    return (t <= s) if rev else (t >= s)


def _mm_exact_lhs_tn(a, ones_bf16):
    ah, al = _split(a)
    return _dg(ah, ones_bf16, TN) + _dg(al, ones_bf16, TN)


def _gla_chunk(q, k, v, la, states, rev):
    tri = _tri(rev).astype(BF16)
    b = _mm_exact_lhs(tri, la)
    mid, last = (CHUNK // 2 - 1, 0) if rev else (CHUNK // 2, CHUNK - 1)
    bref = b[mid:mid + 1, :]
    blast = b[last:last + 1, :]
    scale = DK_A ** -0.5
    qg = q * jnp.exp(b - bref) * scale
    kg = k * jnp.exp(bref - b)
    kd = k * jnp.exp(blast - b)
    qs = q * jnp.exp(b) * scale
    ones = jnp.ones((CHUNK, LANES), BF16)
    lane = _iota2((CHUNK, LANES), 1)
    row = _iota2((CHUNK, LANES), 0)
    s_in = lane % DK_A
    causal = (row <= s_in) if rev else (row >= s_in)
    zeros_v = jnp.zeros((CHUNK, DV_A), F32)
    outs, new_states = [], []
    for p in range(H_A // 2):
        ls = slice(p * LANES, (p + 1) * LANES)
        vs0 = v[:, (2 * p) * DV_A:(2 * p + 1) * DV_A]
        vs1 = v[:, (2 * p + 1) * DV_A:(2 * p + 2) * DV_A]
        s0, s1 = states[2 * p], states[2 * p + 1]
        kg_p = kg[:, ls]
        rhs_att = jnp.concatenate([jnp.where(lane < DK_A, kg_p, 0.0), jnp.where(lane >= DK_A, kg_p, 0.0)], axis=0)
        att = jnp.where(causal, _mm(qg[:, ls], rhs_att, NT), 0.0)
        rhs_o = jnp.concatenate([jnp.concatenate([vs0, zeros_v], axis=1),
                                 jnp.concatenate([zeros_v, vs1], axis=1),
                                 jnp.concatenate([s0, zeros_v], axis=1),
                                 jnp.concatenate([zeros_v, s1], axis=1)], axis=0)
        lhs_o = jnp.concatenate([att, qs[:, ls]], axis=1)
        outs.append(_mm(lhs_o, rhs_o))
        u = _mm(kd[:, ls], v[:, 2 * p * DV_A:(2 * p + 2) * DV_A], TN)
        dec = jnp.exp(_mm_exact_lhs_tn(la[:, ls], ones))
        new_states.append(dec[0:DK_A] * s0 + u[0:DK_A, 0:DV_A])
        new_states.append(dec[DK_A:2 * DK_A] * s1 + u[DK_A:2 * DK_A, DV_A:2 * DV_A])
    return jnp.concatenate(outs, axis=1), new_states


def _gla_kernel(qf_ref, qb_ref, sf_ref, sb_ref, wlr_ref, blr_ref, init_ref, of_ref, ob_ref, so_ref, s_ref,
                *, n_prompt_units, units_per_seq):
    j = pl.program_id(0)
    _, _, _, first = _unit_ids(j, n_prompt_units, units_per_seq)

    @pl.when(first)
    def _():
        s_ref[...] = init_ref[0]

    for d in range(N_DIR):
        x_ref, sm_ref, o_ref = (qf_ref, sf_ref, of_ref) if d == 0 else (qb_ref, sb_ref, ob_ref)
        la_unit = _log_sigmoid(_mm3(sm_ref[...], wlr_ref[d]) + blr_ref[d]) * (1.0 / GLA_NORMALIZER)
        states = [s_ref[d, h] for h in range(H_A)]
        order = range(CHUNKS_PER_UNIT - 1, -1, -1) if d == 1 else range(CHUNKS_PER_UNIT)
        for c in order:
            rows = slice(c * CHUNK, (c + 1) * CHUNK)
            o, states = _gla_chunk(x_ref[rows, 0:QA], x_ref[rows, QA:2 * QA], x_ref[rows, 2 * QA:2 * QA + VA],
                                   la_unit[rows], states, rev=(d == 1))
            o_ref[rows, :] = o
        for h in range(H_A):
            s_ref[d, h] = states[h]
    so_ref[0] = s_ref[...]


def _gla(proj, wlr, blr, init, n_prompt_units, units_per_seq):
    n_tok = proj.shape[0]
    n_units = n_tok // UNIT
    ids = functools.partial(_unit_ids, n_prompt_units=n_prompt_units, units_per_seq=units_per_seq)
    small_blk = C_SMALL // LANES
    st_blk = (1, N_DIR, H_A, DK_A, DV_A)
    kern = functools.partial(_gla_kernel, n_prompt_units=n_prompt_units, units_per_seq=units_per_seq)
    return pl.pallas_call(
        kern,
        grid=(n_units,),
        in_specs=[pl.BlockSpec((UNIT, 1024), lambda j: (ids(j)[0], 0)),
                  pl.BlockSpec((UNIT, 1024), lambda j: (ids(j)[1], 0)),
                  pl.BlockSpec((UNIT, LANES), lambda j: (ids(j)[0], small_blk)),
                  pl.BlockSpec((UNIT, LANES), lambda j: (ids(j)[1], small_blk)),
                  pl.BlockSpec((N_DIR, LANES, QA), lambda j: (0, 0, 0)),
                  pl.BlockSpec((N_DIR, 1, QA), lambda j: (0, 0, 0)),
                  pl.BlockSpec(st_blk, lambda j: (ids(j)[2], 0, 0, 0, 0))],
        out_specs=[pl.BlockSpec((UNIT, VA), lambda j: (ids(j)[0], 0)),
                   pl.BlockSpec((UNIT, VA), lambda j: (ids(j)[1], 0)),
                   pl.BlockSpec(st_blk, lambda j: (jnp.minimum(j, n_prompt_units), 0, 0, 0, 0))],
        out_shape=[jax.ShapeDtypeStruct((n_tok, VA), F32),
                   jax.ShapeDtypeStruct((n_tok, VA), F32),
                   jax.ShapeDtypeStruct((n_prompt_units + 1, N_DIR, H_A, DK_A, DV_A), F32)],
        scratch_shapes=[pltpu.VMEM((N_DIR, H_A, DK_A, DV_A), F32)],
        compiler_params=_cparams(("arbitrary",)),
        name="gla",
    )(proj, proj, proj, proj, wlr, blr, init)


def _stack_masks(rev):
    r = _iota2((HC, HC), 0)
    c = _iota2((HC, HC), 1)
    same = (r // CHUNK) == (c // CHUNK)
    tr, tc = r % CHUNK, c % CHUNK
    incl = jnp.logical_and(same, (tr <= tc) if rev else (tr >= tc))
    strict = jnp.logical_and(same, (tr < tc) if rev else (tr > tc))
    return incl, strict


def _spread(x):
    z = jnp.zeros((CHUNK, LANES), x.dtype)
    rows = []
    for h in range(H_B):
        xh = x[h * CHUNK:(h + 1) * CHUNK]
        rows.append(jnp.concatenate([xh if g == h else z for g in range(H_B)], axis=1))
    return jnp.concatenate(rows, axis=0)


def _gdn_prepare(q, k, v, gcb, beta, glast, rev):
    incl, strict = _stack_masks(rev)
    grow = gcb.T[0:1, :]
    diff = gcb[:, 0:1] - grow
    decay = jnp.where(incl, jnp.exp(jnp.where(incl, diff, 0.0)), 0.0)
    kb = k * beta
    m1 = _mm(jnp.concatenate([kb, q], axis=0), k, NT)
    a = jnp.where(strict, m1[0:HC] * decay, 0.0)
    att = m1[HC:2 * HC] * decay
    r = _iota2((HC, HC), 0)
    c = _iota2((HC, HC), 1)
    a0 = jnp.where((r // INV_BLOCK) == (c // INV_BLOCK), a, 0.0)
    qm = -a0
    pw = a0
    n = 2
    while n < INV_BLOCK:
        pw = _mm(pw, pw)
        qm = qm + pw + _mm(qm, pw)
        n *= 2
    b = INV_BLOCK
    while b < CHUNK:
        off = jnp.logical_and((r // (2 * b)) == (c // (2 * b)), (r // b) != (c // b))
        al = jnp.where(off, a, 0.0)
        t1 = al + _mm(qm, al)
        qm = qm - (t1 + _mm(t1, qm))
        b *= 2
    egc = jnp.exp(gcb)
    rhs = jnp.concatenate([v * beta, kb * egc], axis=1)
    sol = rhs + _mm(qm, rhs)
    return sol[:, 0:DV_B], sol[:, DV_B:2 * DV_B], q * egc, k * jnp.exp(glast - gcb), att


def _gdn_scan_step(value, k_cum, q_dec, k_dec, att, gl_rows, s):
    kq = _mm(jnp.concatenate([_spread(k_cum), _spread(q_dec)], axis=0), s)
    v_new = value - kq[0:HC]
    o = kq[HC:2 * HC] + _mm(att, v_new)
    s_new = s * gl_rows + _mm(_spread(k_dec), v_new, TN)
    return o, s_new


def _gdn_kernel(xf_ref, xb_ref, sf_ref, sb_ref, gco_ref, gdt_ref, init_ref, of_ref, ob_ref, so_ref, s_ref,
                *, n_prompt_units, units_per_seq):
    j = pl.program_id(0)
    _, _, _, first = _unit_ids(j, n_prompt_units, units_per_seq)

    @pl.when(first)
    def _():
        s_ref[...] = init_ref[0]

    for d in range(N_DIR):
        x_ref, sm_ref, o_ref = (xf_ref, sf_ref, of_ref) if d == 0 else (xb_ref, sb_ref, ob_ref)
        rev = d == 1
        sm = sm_ref[...]
        g_all = gco_ref[...] * _softplus(sm + gdt_ref[...])
        beta_all = _sigmoid(sm)
        tri = _tri(rev).astype(BF16)
        last = 0 if rev else CHUNK - 1
        prepared = []
        for c in range(CHUNKS_PER_UNIT):
            rows = slice(c * CHUNK, (c + 1) * CHUNK)
            gc_all = _mm_exact_lhs(tri, g_all[rows])
            gcb, beta, glast = [], [], []
            for h in range(H_B):
                col = SM_AB + d * H_B + h
                colb = SM_BETA + d * H_B + h
                gh = jnp.broadcast_to(gc_all[:, col:col + 1], (CHUNK, LANES))
                gcb.append(gh)
                glast.append(jnp.broadcast_to(gh[last:last + 1, :], (CHUNK, LANES)))
                beta.append(jnp.broadcast_to(beta_all[rows, colb:colb + 1], (CHUNK, LANES)))
            stack = lambda base: jnp.concatenate(
                [x_ref[rows, base + h * LANES:base + (h + 1) * LANES] for h in range(H_B)], axis=0)
            prep = _gdn_prepare(stack(0), stack(QB), stack(2 * QB), jnp.concatenate(gcb, axis=0),
                                jnp.concatenate(beta, axis=0), jnp.concatenate(glast, axis=0), rev)
            gl_rows = jnp.concatenate([jnp.broadcast_to(jnp.exp(g[0:1, :]), (DK_B, DV_B)) for g in glast], axis=0)
            prepared.append(prep + (gl_rows,))
        s = s_ref[d]
        order = range(CHUNKS_PER_UNIT - 1, -1, -1) if rev else range(CHUNKS_PER_UNIT)
        for c in order:
            o, s = _gdn_scan_step(*prepared[c], s)
            for h in range(H_B):
                o_ref[c * CHUNK:(c + 1) * CHUNK, h * DV_B:(h + 1) * DV_B] = o[h * CHUNK:(h + 1) * CHUNK]
        s_ref[d] = s
    so_ref[0] = s_ref[...]


def _gdn(proj, gcoef, gdt, init, n_prompt_units, units_per_seq):
    n_tok = proj.shape[0]
    n_units = n_tok // UNIT
    ids = functools.partial(_unit_ids, n_prompt_units=n_prompt_units, units_per_seq=units_per_seq)
    small_blk = C_SMALL // LANES
    qkv_blk = C_QKVB // (3 * QB)
    st_blk = (1, N_DIR, H_B * DK_B, DV_B)
    kern = functools.partial(_gdn_kernel, n_prompt_units=n_prompt_units, units_per_seq=units_per_seq)
    return pl.pallas_call(
        kern,
        grid=(n_units,),
        in_specs=[pl.BlockSpec((UNIT, 3 * QB), lambda j: (ids(j)[0], qkv_blk)),
                  pl.BlockSpec((UNIT, 3 * QB), lambda j: (ids(j)[1], qkv_blk)),
                  pl.BlockSpec((UNIT, LANES), lambda j: (ids(j)[0], small_blk)),
                  pl.BlockSpec((UNIT, LANES), lambda j: (ids(j)[1], small_blk)),
                  pl.BlockSpec((1, LANES), lambda j: (0, 0)),
                  pl.BlockSpec((1, LANES), lambda j: (0, 0)),
                  pl.BlockSpec(st_blk, lambda j: (ids(j)[2], 0, 0, 0))],
        out_specs=[pl.BlockSpec((UNIT, VB), lambda j: (ids(j)[0], 0)),
                   pl.BlockSpec((UNIT, VB), lambda j: (ids(j)[1], 0)),
                   pl.BlockSpec(st_blk, lambda j: (jnp.minimum(j, n_prompt_units), 0, 0, 0))],
        out_shape=[jax.ShapeDtypeStruct((n_tok, VB), F32),
                   jax.ShapeDtypeStruct((n_tok, VB), F32),
                   jax.ShapeDtypeStruct((n_prompt_units + 1, N_DIR, H_B * DK_B, DV_B), F32)],
        scratch_shapes=[pltpu.VMEM((N_DIR, H_B * DK_B, DV_B), F32)],
        compiler_params=_cparams(("arbitrary",)),
        name="gdn",
    )(proj, proj, proj, proj, gcoef, gdt, init)


def _head_rms(o, w):
    parts = []
    for h in range(o.shape[1] // LANES):
        parts.append(_rms(o[:, h * LANES:(h + 1) * LANES]))
    return jnp.concatenate(parts, axis=1) * w


def _post_kernel(x_ref, ga_ref, zb_ref, af_ref, ab_ref, df_ref, db_ref, mod_ref, wo_ref, nwa_ref, nwb_ref, n2_ref,
                 rw_ref, x1_ref, h2_ref, sc_ref):
    m = mod_ref[0]
    gla = _head_rms(af_ref[...] + ab_ref[...], nwa_ref[...]) * _silu(ga_ref[...])
    gdn = _head_rms(df_ref[...] + db_ref[...], nwb_ref[...]) * _silu(zb_ref[...])
    y = _dg(gla.astype(BF16), wo_ref[0:VA, :], NN) + _dg(gdn.astype(BF16), wo_ref[VA:VA + VB, :], NN)
    x1 = x_ref[...] + m[:, 2 * D_MODEL:3 * D_MODEL] * y
    x1_ref[...] = x1
    h2 = (_rms(x1) * n2_ref[...]) * (1.0 + m[:, 4 * D_MODEL:5 * D_MODEL]) + m[:, 3 * D_MODEL:4 * D_MODEL]
    h2_ref[...] = _pack_rows(h2)
    sc_ref[...] = _sigmoid(_mm3(rw_ref[...], h2, NT))


def _post(x_all, proj, o_af, o_ab, o_df, o_db, mod3, w_o, nwa, nwb, n2w, router_wt, n_prompt_tok, sample_len, tm):
    n_tok = x_all.shape[0]
    cond = functools.partial(_cond_row, tiles_prompt=n_prompt_tok // tm, tiles_per_seq=sample_len // tm)
    tok = lambda i: (i, 0)
    const = lambda i: (0, 0)
    return pl.pallas_call(
        _post_kernel,
        grid=(n_tok // tm,),
        in_specs=[pl.BlockSpec((tm, D_MODEL), tok),
                  pl.BlockSpec((tm, VA), lambda i: (i, C_GA // VA)),
                  pl.BlockSpec((tm, VB), lambda i: (i, C_ZB // VB)),
                  pl.BlockSpec((tm, VA), tok), pl.BlockSpec((tm, VA), tok),
                  pl.BlockSpec((tm, VB), tok), pl.BlockSpec((tm, VB), tok),
                  pl.BlockSpec((1, 1, 6 * D_MODEL), lambda i: (cond(i), 0, 0)),
                  pl.BlockSpec((VA + VB, D_MODEL), const),
                  pl.BlockSpec((1, VA), const), pl.BlockSpec((1, VB), const), pl.BlockSpec((1, D_MODEL), const),
                  pl.BlockSpec((N_EXPERTS, D_MODEL), const)],
        out_specs=[pl.BlockSpec((tm, D_MODEL), tok),
                   pl.BlockSpec((tm, D_PACK), tok),
                   pl.BlockSpec((N_EXPERTS, tm), lambda i: (0, i))],
        out_shape=[jax.ShapeDtypeStruct((n_tok, D_MODEL), F32),
                   jax.ShapeDtypeStruct((n_tok, D_PACK), jnp.uint32),
                   jax.ShapeDtypeStruct((N_EXPERTS, n_tok), F32)],
        compiler_params=_cparams(("parallel",)),
        name="post",
    )(x_all, proj, proj, o_af, o_ab, o_df, o_db, mod3, w_o, nwa, nwb, n2w, router_wt)


def _route_kernel(sc_ref, bias_ref, e_ref, r_ref, w_ref, cnt_ref, carry_ref):
    i = pl.program_id(0)
    t = sc_ref.shape[1]

    @pl.when(i == 0)
    def _():
        carry_ref[...] = jnp.zeros(carry_ref.shape, F32)

    s = sc_ref[...]
    biased = s + bias_ref[:, 0:1]
    gs = []
    for g in range(N_GROUP):
        blk = biased[g * GROUP_SIZE:(g + 1) * GROUP_SIZE]
        m1 = jnp.max(blk, axis=0, keepdims=True)
        n1 = jnp.sum((blk == m1).astype(F32), axis=0, keepdims=True)
        m2 = jnp.max(jnp.where(blk < m1, blk, NEG_INF), axis=0, keepdims=True)
        gs.append(m1 + jnp.where(n1 >= 2.0, m1, m2))
    gsc = jnp.concatenate(gs, axis=0)
    gid = _iota2((N_GROUP, t), 0)
    beaten = jnp.zeros((N_GROUP, t), F32)
    for g in range(N_GROUP):
        other = gsc[g:g + 1, :]
        wins = jnp.logical_or(other > gsc, jnp.logical_and(other == gsc, g < gid))
        beaten = beaten + wins.astype(F32)
    masked = jnp.concatenate(
        [jnp.where(beaten[g:g + 1, :] < float(TOPK_GROUP), biased[g * GROUP_SIZE:(g + 1) * GROUP_SIZE], NEG_INF)
         for g in range(N_GROUP)], axis=0)
    eid = _iota2((N_EXPERTS, t), 0).astype(F32)
    sel = jnp.zeros((N_EXPERTS, t), F32)
    picks, scores = [], []
    for _ in range(TOP_K):
        m = jnp.max(masked, axis=0, keepdims=True)
        first = jnp.min(jnp.where(masked == m, eid, float(N_EXPERTS)), axis=0, keepdims=True)
        hit = eid == first
        scores.append(jnp.sum(jnp.where(hit, s, 0.0), axis=0, keepdims=True))
        masked = jnp.where(hit, NEG_INF, masked)
        sel = sel + hit.astype(F32)
        picks.append(first)
    upper = (_iota2((t, t), 0) < _iota2((t, t), 1)).astype(BF16)
    carry = carry_ref[...]
    prefix = _dg(sel.astype(BF16), upper, NN) + jnp.concatenate([carry] * (t // LANES), axis=1)
    ranks = [jnp.sum(jnp.where(eid == p, prefix, 0.0), axis=0, keepdims=True) for p in picks]
    carry = carry + _dg(sel.astype(BF16), jnp.ones((t, LANES), BF16), NN)
    carry_ref[...] = carry
    cnt_ref[...] = carry
    sc8 = jnp.concatenate(scores, axis=0)
    e_ref[...] = jnp.concatenate(picks, axis=0).astype(jnp.int32)
    r_ref[...] = jnp.concatenate(ranks, axis=0).astype(jnp.int32)
    w_ref[...] = sc8 / jnp.sum(sc8, axis=0, keepdims=True) * ROUTED_SCALE


def _route(scores_t, bias_col):
    n_tok = scores_t.shape[1]
    t = ROUTE_TILE
    slot = lambda i: (0, i)
    return pl.pallas_call(
        _route_kernel,
        grid=(n_tok // t,),
        in_specs=[pl.BlockSpec((N_EXPERTS, t), slot),
                  pl.BlockSpec((N_EXPERTS, LANES), lambda i: (0, 0))],
        out_specs=[pl.BlockSpec((TOP_K, t), slot), pl.BlockSpec((TOP_K, t), slot), pl.BlockSpec((TOP_K, t), slot),
                   pl.BlockSpec((N_EXPERTS, LANES), lambda i: (0, 0))],
        out_shape=[jax.ShapeDtypeStruct((TOP_K, n_tok), jnp.int32),
                   jax.ShapeDtypeStruct((TOP_K, n_tok), jnp.int32),
                   jax.ShapeDtypeStruct((TOP_K, n_tok), F32),
                   jax.ShapeDtypeStruct((N_EXPERTS, LANES), F32)],
        scratch_shapes=[pltpu.VMEM((N_EXPERTS, LANES), F32)],
        compiler_params=_cparams(("arbitrary",)),
        name="route",
    )(scores_t, bias_col)


def _dest_kernel(start_ref, e_ref, r_ref, d_ref):
    e = e_ref[...]

    def body(x, acc):
        return jnp.where(e == x, start_ref[x], acc)

    d_ref[...] = r_ref[...] + lax.fori_loop(0, N_EXPERTS, body, jnp.zeros(e.shape, jnp.int32))


def _dest(start, e8, r8, tt):
    n_tok = e8.shape[1]
    slot = lambda i, st: (0, i)
    return pl.pallas_call(
        _dest_kernel,
        grid_spec=pltpu.PrefetchScalarGridSpec(
            num_scalar_prefetch=1, grid=(n_tok // tt,),
            in_specs=[pl.BlockSpec((TOP_K, tt), slot), pl.BlockSpec((TOP_K, tt), slot)],
            out_specs=pl.BlockSpec((TOP_K, tt), slot)),
        out_shape=jax.ShapeDtypeStruct((TOP_K, n_tok), jnp.int32),
        compiler_params=_cparams(("parallel",)),
        name="dest",
    )(start, e8, r8)


def _items(counts, n_rows):
    n_blocks = n_rows // ROW_BLOCK
    max_items = n_blocks + N_EXPERTS - 1
    end = jnp.cumsum(counts)
    start = end - counts
    first_blk = start // ROW_BLOCK
    n_it = jnp.where(counts > 0, (end - 1) // ROW_BLOCK - first_blk + 1, 0)
    it_end = jnp.cumsum(n_it)
    it_start = it_end - n_it
    i = jnp.arange(max_items, dtype=jnp.int32)
    valid = i < it_end[-1]
    ex = jnp.minimum(jnp.sum((it_end[None, :] <= i[:, None]).astype(jnp.int32), axis=1), N_EXPERTS - 1)
    onehot = ex[:, None] == jnp.arange(N_EXPERTS, dtype=jnp.int32)[None, :]
    pick = lambda tab: jnp.sum(jnp.where(onehot, tab[None, :], 0), axis=1)
    blk = pick(first_blk) + (i - pick(it_start))
    lo = jnp.maximum(pick(start), blk * ROW_BLOCK) - blk * ROW_BLOCK
    hi = jnp.minimum(pick(end), (blk + 1) * ROW_BLOCK) - blk * ROW_BLOCK
    blk = jnp.where(valid, blk, n_blocks - 1).astype(jnp.int32)
    lo = jnp.where(valid, lo, 0).astype(jnp.int32)
    hi = jnp.where(valid, hi, 0).astype(jnp.int32)
    return start.astype(jnp.int32), blk, ex.astype(jnp.int32), lo, hi


def _dispatch_kernel(h2_ref, dest_hbm, xs_hbm, dsm, sem_d, sem_s):
    i = pl.program_id(0)
    n = pl.num_programs(0)
    tm = h2_ref.shape[0]

    def dest_copy(step, slot):
        return pltpu.make_async_copy(dest_hbm.at[:, pl.ds(step * tm, tm)], dsm.at[slot], sem_d.at[slot])

    def row_copy(t, dst):
        return pltpu.make_async_copy(h2_ref.at[pl.ds(t, 1), :], xs_hbm.at[pl.ds(dst, 1), :], sem_s.at[0])

    @pl.when(i == 0)
    def _():
        dest_copy(0, 0).start()

    slot = i % 2
    dest_copy(i, slot).wait()

    @pl.when(i + 1 < n)
    def _():
        dest_copy(i + 1, 1 - slot).start()

    def issue(t, carry):
        for k in range(TOP_K):
            row_copy(t, dsm[slot, k, t]).start(priority=k % N_DMA_QUEUES)
        return carry
    lax.fori_loop(0, tm, issue, 0)

    def drain(t, carry):
        for k in range(TOP_K):
            row_copy(t, 0).wait()
        return carry
    lax.fori_loop(0, tm, drain, 0)


def _dispatch(h2, dest, tm):
    n_tok = h2.shape[0]
    return pl.pallas_call(
        _dispatch_kernel,
        grid=(n_tok // tm,),
        in_specs=[pl.BlockSpec((tm, D_PACK), lambda i: (i, 0)),
                  pl.BlockSpec(memory_space=pl.ANY)],
        out_specs=pl.BlockSpec(memory_space=pl.ANY),
        out_shape=jax.ShapeDtypeStruct((n_tok * TOP_K, D_PACK), jnp.uint32),
        scratch_shapes=[pltpu.SMEM((2, TOP_K, tm), jnp.int32),
                        pltpu.SemaphoreType.DMA((2,)),
                        pltpu.SemaphoreType.DMA((1,))],
        compiler_params=_cparams(("arbitrary",)),
        name="dispatch",
    )(h2, dest)


def _experts_kernel(blk_ref, exp_ref, lo_ref, hi_ref, x_ref, wg_ref, wu_ref, wd_ref, y_ref):
    i = pl.program_id(0)
    lo, hi = lo_ref[i], hi_ref[i]

    @pl.when(hi > lo)
    def _():
        x_lo, x_hi = _unpack_rows(x_ref[...])
        x = jnp.concatenate([x_lo.astype(BF16), x_hi.astype(BF16)], axis=1)
        g = _dg(x, wg_ref[0].astype(BF16), NN)
        u = _dg(x, wu_ref[0].astype(BF16), NN)
        hmid = (_silu(g) * u).astype(BF16)
        y = _pack_rows(_dg(hmid, wd_ref[0].astype(BF16), NN))
        row = _iota2((ROW_BLOCK, D_PACK), 0)
        mine = jnp.logical_and(row >= lo, row < hi)

        @pl.when(lo == 0)
        def _():
            y_ref[...] = jnp.where(mine, y, jnp.uint32(0))

        @pl.when(lo > 0)
        def _():
            y_ref[...] = jnp.where(mine, y, y_ref[...])


def _experts(item_blk, item_exp, item_lo, item_hi, xs, w_gate, w_up, w_down):
    n_items = item_blk.shape[0]
    xmap = lambda i, blk, ex, lo, hi: (blk[i], 0)
    wmap = lambda i, blk, ex, lo, hi: (ex[i], 0, 0)
    return pl.pallas_call(
        _experts_kernel,
        grid_spec=pltpu.PrefetchScalarGridSpec(
            num_scalar_prefetch=4, grid=(n_items,),
            in_specs=[pl.BlockSpec((ROW_BLOCK, D_PACK), xmap),
                      pl.BlockSpec((1, D_MODEL, D_EXPERT), wmap),
                      pl.BlockSpec((1, D_MODEL, D_EXPERT), wmap),
                      pl.BlockSpec((1, D_EXPERT, D_MODEL), wmap)],
            out_specs=pl.BlockSpec((ROW_BLOCK, D_PACK), xmap)),
        out_shape=jax.ShapeDtypeStruct(xs.shape, jnp.uint32),
        compiler_params=_cparams(("arbitrary",)),
        name="experts",
    )(item_blk, item_exp, item_lo, item_hi, xs, w_gate, w_up, w_down)


def _final_kernel(x1_ref, h2_ref, w_ref, mod_ref, sg_ref, su_ref, sd_ref, fn_ref, dest_hbm, ys_hbm, o_ref,
                  dsm, gbuf, sem_d, sem_g):
    i = pl.program_id(0)
    n = pl.num_programs(0)
    tm = x1_ref.shape[0]

    def dest_copy(step, slot):
        return pltpu.make_async_copy(dest_hbm.at[:, pl.ds(step * tm, tm)], dsm.at[slot], sem_d.at[slot])

    def row_copy(src, slot, k, t):
        return pltpu.make_async_copy(ys_hbm.at[pl.ds(src, 1), :], gbuf.at[slot, k, pl.ds(t, 1), :], sem_g.at[slot])

    def issue_gathers(slot3, slot2):
        def body(t, carry):
            for k in range(TOP_K):
                row_copy(dsm[slot3, k, t], slot2, k, t).start(priority=k % N_DMA_QUEUES)
            return carry
        lax.fori_loop(0, tm, body, 0)

    @pl.when(i == 0)
    def _():
        dest_copy(0, 0).start()
        dest_copy(0, 0).wait()
        issue_gathers(0, 0)

        @pl.when(n > 1)
        def _():
            dest_copy(1, 1).start()

    @pl.when(i + 1 < n)
    def _():
        dest_copy(i + 1, (i + 1) % 3).wait()
        issue_gathers((i + 1) % 3, (i + 1) % 2)

        @pl.when(i + 2 < n)
        def _():
            dest_copy(i + 2, (i + 2) % 3).start()

    slot = i % 2

    def drain(t, carry):
        for k in range(TOP_K):
            row_copy(0, slot, k, t).wait()
        return carry
    lax.fori_loop(0, tm, drain, 0)

    m = mod_ref[0]
    w = w_ref[...]
    r_lo = r_hi = None
    for kk in range(TOP_K):
        y_lo, y_hi = _unpack_rows(gbuf[slot, kk])
        wk = w[:, kk:kk + 1]
        r_lo = y_lo * wk if r_lo is None else r_lo + y_lo * wk
        r_hi = y_hi * wk if r_hi is None else r_hi + y_hi * wk
    routed = jnp.concatenate([r_lo, r_hi], axis=1)
    h_lo, h_hi = _unpack_rows(h2_ref[...])
    hb = jnp.concatenate([h_lo.astype(BF16), h_hi.astype(BF16)], axis=1)
    hm = (_silu(_dg(hb, sg_ref[...], NN)) * _dg(hb, su_ref[...], NN)).astype(BF16)
    shared = _dg(hm, sd_ref[...], NN)
    x2 = x1_ref[...] + m[:, 5 * D_MODEL:6 * D_MODEL] * (routed + shared)
    o_ref[...] = _rms(x2) * fn_ref[...]


def _final(x1, h2, wts, mod3, sg, su, sd, fnw, dest, ys, n_prompt_tok, sample_len, tm):
    n_tok = x1.shape[0]
    cond = functools.partial(_cond_row, tiles_prompt=n_prompt_tok // tm, tiles_per_seq=sample_len // tm)
    tok = lambda i: (i, 0)
    const = lambda i: (0, 0)
    return pl.pallas_call(
        _final_kernel,
        grid=(n_tok // tm,),
        in_specs=[pl.BlockSpec((tm, D_MODEL), tok),
                  pl.BlockSpec((tm, D_PACK), tok),
                  pl.BlockSpec((tm, TOP_K), tok),
                  pl.BlockSpec((1, 1, 6 * D_MODEL), lambda i: (cond(i), 0, 0)),
                  pl.BlockSpec((D_MODEL, D_EXPERT), const),
                  pl.BlockSpec((D_MODEL, D_EXPERT), const),
                  pl.BlockSpec((D_EXPERT, D_MODEL), const),
                  pl.BlockSpec((1, D_MODEL), const),
                  pl.BlockSpec(memory_space=pl.ANY),
                  pl.BlockSpec(memory_space=pl.ANY)],
        out_specs=pl.BlockSpec((tm, D_MODEL), tok),
        out_shape=jax.ShapeDtypeStruct((n_tok, D_MODEL), F32),
        scratch_shapes=[pltpu.SMEM((3, TOP_K, tm), jnp.int32),
                        pltpu.VMEM((2, TOP_K, tm, D_PACK), jnp.uint32),
                        pltpu.SemaphoreType.DMA((3,)),
                        pltpu.SemaphoreType.DMA((2,))],
        compiler_params=_cparams(("arbitrary",)),
        name="final",
    )(x1, h2, wts, mod3, sg, su, sd, fnw, dest, ys)


def kernel(x_prompt, x_sample, state_gla, state_gdn, c, c_ctx, w_ada, b_ada, norm1_w, w_in, conv_w, gla_lr_w, gla_lr_b, gdn_a_log, gdn_dt_bias, gla_norm_w, gdn_norm_w, w_o, norm2_w, router_w, router_bias, exp_w_gate, exp_w_up, exp_w_down, sh_w_gate, sh_w_up, sh_w_down, final_norm_w):
    bp, lp, d = x_prompt.shape
    bs, ls, _ = x_sample.shape
    assert d == D_MODEL and lp == UNIT and ls % UNIT == 0 and w_ada.shape[0] == 1
    n_prompt_tok = bp * lp
    n_tok = n_prompt_tok + bs * ls
    n_prompt_units = n_prompt_tok // UNIT
    units_per_seq = ls // UNIT
    grid_w = 64
    layer = 0
    tm_in = 512
    tm = 256
    assert n_prompt_tok % tm_in == 0 and ls % tm_in == 0 and n_tok % ROUTE_TILE == 0

    x_all = jnp.concatenate([x_prompt.reshape(n_prompt_tok, d), x_sample.reshape(bs * ls, d)], axis=0)
    cond = jnp.concatenate([c_ctx[None, :], c, jnp.zeros((SUBLANES - 1 - bs, d), F32)], axis=0)
    mod3 = _ada(cond, w_ada[layer], b_ada[layer][None, :]).reshape(SUBLANES, 1, 6 * d)

    wi = w_in[layer]
    o_lr = 2 * QA + 2 * VA
    o_qkvb = o_lr + N_DIR * GLA_RANK
    o_zb = o_qkvb + 3 * QB
    o_ab = o_zb + VB
    w_main = jnp.concatenate([wi[:, 0:o_lr], wi[:, o_qkvb:o_ab]], axis=1).astype(BF16)
    w_small = jnp.concatenate([wi[:, o_lr:o_qkvb], wi[:, o_ab:], jnp.zeros((d, LANES - 48), F32)], axis=1)
    cw = jnp.concatenate([conv_w[layer], jnp.zeros((SUBLANES - CONV_K, 3 * QB), F32)], axis=0)
    proj = _inproj(x_all, mod3, norm1_w[layer][None, :], w_main, w_small, cw, n_prompt_tok, ls, lp, grid_w, tm_in)

    wlr = jnp.zeros((N_DIR, LANES, QA), F32)
    for dd in range(N_DIR):
        wlr = wlr.at[dd, dd * GLA_RANK:(dd + 1) * GLA_RANK, :].set(gla_lr_w[layer, dd])
    blr = gla_lr_b[layer][:, None, :]
    init_gla = jnp.concatenate([jnp.zeros((1,) + state_gla.shape[2:], F32), state_gla[:, layer].astype(F32)], axis=0)
    o_af, o_ab_, s_gla = _gla(proj, wlr, blr, init_gla, n_prompt_units, units_per_seq)

    gcoef = jnp.zeros((1, LANES), F32).at[0, SM_AB:SM_AB + N_DIR * H_B].set(-jnp.exp(gdn_a_log[layer].reshape(-1)))
    gdt = jnp.zeros((1, LANES), F32).at[0, SM_AB:SM_AB + N_DIR * H_B].set(gdn_dt_bias[layer].reshape(-1))
    init_gdn = jnp.concatenate([jnp.zeros((1, N_DIR, H_B * DK_B, DV_B), F32),
                                state_gdn[:, layer].astype(F32).reshape(bs, N_DIR, H_B * DK_B, DV_B)], axis=0)
    o_df, o_db, s_gdn = _gdn(proj, gcoef, gdt, init_gdn, n_prompt_units, units_per_seq)

    nwa = jnp.tile(gla_norm_w[layer], H_A)[None, :]
    nwb = jnp.tile(gdn_norm_w[layer], H_B)[None, :]
    x1, h2, scores_t = _post(x_all, proj, o_af, o_ab_, o_df, o_db, mod3, w_o[layer].astype(BF16), nwa, nwb,
                             norm2_w[layer][None, :], router_w[layer].T, n_prompt_tok, ls, tm)

    bias_col = jnp.broadcast_to(router_bias[layer].astype(F32)[:, None], (N_EXPERTS, LANES))
    e8, r8, w8, cnt = _route(scores_t, bias_col)
    start, item_blk, item_exp, item_lo, item_hi = _items(cnt[:, 0].astype(jnp.int32), n_tok * TOP_K)
    dest = _dest(start, e8, r8, tt=min(2048, n_tok))

    xs = _dispatch(h2, dest, tm)
    ys = _experts(item_blk, item_exp, item_lo, item_hi, xs, exp_w_gate[layer], exp_w_up[layer], exp_w_down[layer])
    y_all = _final(x1, h2, w8.T, mod3, sh_w_gate[layer].astype(BF16), sh_w_up[layer].astype(BF16),
                   sh_w_down[layer].astype(BF16), final_norm_w[None, :], dest, ys, n_prompt_tok, ls, tm)

    y_prompt = y_all[:n_prompt_tok].reshape(bp, lp, d)
    y_sample = y_all[n_prompt_tok:].reshape(bs, ls, d)
    new_state_gla = s_gla[:n_prompt_units].reshape(bp, 1, N_DIR, H_A, DK_A, DV_A).astype(x_prompt.dtype)
    new_state_gdn = s_gdn[:n_prompt_units].reshape(bp, 1, N_DIR, H_B, DK_B, DV_B).astype(x_prompt.dtype)
    return (y_prompt, y_sample, new_state_gla, new_state_gdn)
```

```python
import functools

import jax
import jax.numpy as jnp
from jax import lax
from jax.experimental import pallas as pl
from jax.experimental.pallas import tpu as pltpu

F32 = jnp.float32
BF16 = jnp.bfloat16

D_MODEL = 1024
N_DIR = 2
H_A, DK_A, DV_A = 4, 64, 128
GLA_RANK = 16
GLA_NORMALIZER = 16.0
H_B, DK_B, DV_B = 4, 128, 128
CONV_K = 5
CHUNK = 64
QA, VA = H_A * DK_A, H_A * DV_A
QB, VB = H_B * DK_B, H_B * DV_B
N_EXPERTS = 256
TOP_K = 8
N_GROUP = 8
TOPK_GROUP = 4
GROUP_SIZE = N_EXPERTS // N_GROUP
D_EXPERT = 256
ROUTED_SCALE = 2.5
EPS = 1e-6
NEG_INF = float("-inf")

UNIT = 256
CHUNKS_PER_UNIT = UNIT // CHUNK
HC = H_B * CHUNK
INV_BLOCK = 8
D_PACK = D_MODEL // 2
ROUTE_TILE = 256
ROW_BLOCK = 256
N_DMA_QUEUES = 2
LANES = 128
SUBLANES = 8
VMEM_LIMIT = 56 * 1024 * 1024

C_QKVA = 0
C_GA = 1024
C_QKVB = 1536
C_ZB = 3072
C_SMALL = 3584
D_PROJ = 3712
SM_AB = 32
SM_BETA = 40


def _cparams(sem):
    return pltpu.CompilerParams(dimension_semantics=sem, vmem_limit_bytes=VMEM_LIMIT)


def _split(a):
    hi = a.astype(BF16)
    lo = (a - hi.astype(F32)).astype(BF16)
    return hi, lo


def _dg(a, b, dims):
    return lax.dot_general(a, b, (dims, ((), ())), preferred_element_type=F32)


NN = ((1,), (0,))
NT = ((1,), (1,))
TN = ((0,), (0,))


def _mm(a, b, dims=NN):
    return _dg(a.astype(BF16), b.astype(BF16), dims)


def _mm3(a, b, dims=NN):
    ah, al = _split(a)
    bh, bl = _split(b)
    return _dg(ah, bh, dims) + (_dg(ah, bl, dims) + _dg(al, bh, dims))


def _mm_exact_lhs(a_bf16, b, dims=NN):
    bh, bl = _split(b)
    return _dg(a_bf16, bh, dims) + _dg(a_bf16, bl, dims)


def _silu(x):
    return x * (1.0 / (1.0 + jnp.exp(-x)))


def _sigmoid(x):
    return 1.0 / (1.0 + jnp.exp(-x))


def _softplus(x):
    return jnp.maximum(x, 0.0) + jnp.log1p(jnp.exp(-jnp.abs(x)))


def _log_sigmoid(x):
    return -_softplus(-x)


def _iota2(shape, dim):
    return lax.broadcasted_iota(jnp.int32, shape, dim)


def _rms(x):
    return x * lax.rsqrt(jnp.mean(x * x, axis=-1, keepdims=True) + EPS)


def _pack_rows(x):
    lo = lax.bitcast_convert_type(x[:, 0:D_PACK].astype(BF16).astype(F32), jnp.uint32)
    hi = lax.bitcast_convert_type(x[:, D_PACK:D_MODEL].astype(BF16).astype(F32), jnp.uint32)
    return (lo >> 16) | (hi & jnp.uint32(0xFFFF0000))


def _unpack_rows(p):
    lo = lax.bitcast_convert_type(p << 16, F32)
    hi = lax.bitcast_convert_type(p & jnp.uint32(0xFFFF0000), F32)
    return lo, hi


def _ada_kernel(c_ref, w_ref, b_ref, o_ref):
    o_ref[...] = _mm3(_silu(c_ref[...]), w_ref[...]) + b_ref[...]


def _ada(cond, w, b):
    n = w.shape[1]
    tn = 1536
    return pl.pallas_call(
        _ada_kernel,
        grid=(n // tn,),
        in_specs=[pl.BlockSpec((SUBLANES, D_MODEL), lambda i: (0, 0)),
                  pl.BlockSpec((D_MODEL, tn), lambda i: (0, i)),
                  pl.BlockSpec((1, tn), lambda i: (0, i))],
        out_specs=pl.BlockSpec((SUBLANES, tn), lambda i: (0, i)),
        out_shape=jax.ShapeDtypeStruct((SUBLANES, n), F32),
        compiler_params=_cparams(("parallel",)),
        name="ada",
    )(cond, w, b)


def _inproj_kernel(x_ref, mod_ref, nw_ref, w_ref, ws_ref, cw_ref, o_ref, *, tiles_prompt, prompt_row, sample_row):
    m = mod_ref[0]
    h = (_rms(x_ref[...]) * nw_ref[...]) * (1.0 + m[:, D_MODEL:2 * D_MODEL]) + m[:, 0:D_MODEL]
    hb = h.astype(BF16)
    for c0 in range(0, C_SMALL, 512):
        o_ref[:, c0:c0 + 512] = _dg(hb, w_ref[:, c0:c0 + 512], NN)
    o_ref[:, C_SMALL:D_PROJ] = _mm3(h, ws_ref[...])
    tm = x_ref.shape[0]
    row_len = jnp.where(pl.program_id(0) < tiles_prompt, prompt_row, sample_row)
    pos = _iota2((tm, LANES), 0) & (row_len - 1)
    for part in range(3 * H_B):
        cs = slice(C_QKVB + part * LANES, C_QKVB + (part + 1) * LANES)
        x = o_ref[:, cs]
        acc = jnp.zeros((tm, LANES), F32)
        for jj in range(CONV_K):
            off = jj - CONV_K // 2
            xs = x if off == 0 else pltpu.roll(x, (-off) % tm, 0)
            ok = jnp.logical_and(pos + off >= 0, pos + off < row_len)
            acc = acc + jnp.where(ok, xs, 0.0) * cw_ref[jj:jj + 1, part * LANES:(part + 1) * LANES]
        y = _silu(acc)
        if part < 2 * H_B:
            y = y * lax.rsqrt(jnp.sum(y * y, axis=-1, keepdims=True) + EPS)
            if part < H_B:
                y = y * (DK_B ** -0.5)
        o_ref[:, cs] = y


def _cond_row(i, tiles_prompt, tiles_per_seq):
    return jnp.where(i < tiles_prompt, 0, 1 + (i - tiles_prompt) // tiles_per_seq)


def _inproj(x_all, mod3, norm_w, w_main, w_small, conv_w, n_prompt_tok, sample_len, prompt_row, sample_row, tm):
    n_tok = x_all.shape[0]
    assert tm % prompt_row == 0 and tm % sample_row == 0
    cond = functools.partial(_cond_row, tiles_prompt=n_prompt_tok // tm, tiles_per_seq=sample_len // tm)
    kern = functools.partial(_inproj_kernel, tiles_prompt=n_prompt_tok // tm, prompt_row=prompt_row,
                             sample_row=sample_row)
    return pl.pallas_call(
        kern,
        grid=(n_tok // tm,),
        in_specs=[pl.BlockSpec((tm, D_MODEL), lambda i: (i, 0)),
                  pl.BlockSpec((1, 1, 6 * D_MODEL), lambda i: (cond(i), 0, 0)),
                  pl.BlockSpec((1, D_MODEL), lambda i: (0, 0)),
                  pl.BlockSpec((D_MODEL, C_SMALL), lambda i: (0, 0)),
                  pl.BlockSpec((D_MODEL, LANES), lambda i: (0, 0)),
                  pl.BlockSpec((SUBLANES, 3 * QB), lambda i: (0, 0))],
        out_specs=pl.BlockSpec((tm, D_PROJ), lambda i: (i, 0)),
        out_shape=jax.ShapeDtypeStruct((n_tok, D_PROJ), F32),
        compiler_params=_cparams(("parallel",)),
        name="inproj",
    )(x_all, mod3, norm_w, w_main, w_small, conv_w)


def _unit_ids(j, n_prompt_units, units_per_seq):
    jj = j - n_prompt_units
    b = jj // units_per_seq
    r = jj % units_per_seq
    is_prompt = j < n_prompt_units
    uf = j
    ub = jnp.where(is_prompt, j, n_prompt_units + b * units_per_seq + (units_per_seq - 1 - r))
    init_row = jnp.where(is_prompt, 0, 1 + b)
    first = jnp.logical_or(is_prompt, r == 0)
    return uf, ub, init_row, first


def _tri(rev):
    t = _iota2((CHUNK, CHUNK), 0)
    s = _iota2((CHUNK, CHUNK), 1)
    return (t <= s) if rev else (t >= s)


def _mm_exact_lhs_tn(a, ones_bf16):
    ah, al = _split(a)
    return _dg(ah, ones_bf16, TN) + _dg(al, ones_bf16, TN)


def _gla_chunk(q, k, v, la, states, rev):
    tri = _tri(rev).astype(BF16)
    b = _mm_exact_lhs(tri, la)
    mid, last = (CHUNK // 2 - 1, 0) if rev else (CHUNK // 2, CHUNK - 1)
    bref = b[mid:mid + 1, :]
    blast = b[last:last + 1, :]
    scale = DK_A ** -0.5
    qg = q * jnp.exp(b - bref) * scale
    kg = k * jnp.exp(bref - b)
    kd = k * jnp.exp(blast - b)
    qs = q * jnp.exp(b) * scale
    ones = jnp.ones((CHUNK, LANES), BF16)
    lane = _iota2((CHUNK, LANES), 1)
    row = _iota2((CHUNK, LANES), 0)
    s_in = lane % DK_A
    causal = (row <= s_in) if rev else (row >= s_in)
    zeros_v = jnp.zeros((CHUNK, DV_A), F32)
    outs, new_states = [], []
    for p in range(H_A // 2):
        ls = slice(p * LANES, (p + 1) * LANES)
        vs0 = v[:, (2 * p) * DV_A:(2 * p + 1) * DV_A]
        vs1 = v[:, (2 * p + 1) * DV_A:(2 * p + 2) * DV_A]
        s0, s1 = states[2 * p], states[2 * p + 1]
        kg_p = kg[:, ls]
        rhs_att = jnp.concatenate([jnp.where(lane < DK_A, kg_p, 0.0), jnp.where(lane >= DK_A, kg_p, 0.0)], axis=0)
        att = jnp.where(causal, _mm(qg[:, ls], rhs_att, NT), 0.0)
        rhs_o = jnp.concatenate([jnp.concatenate([vs0, zeros_v], axis=1),
                                 jnp.concatenate([zeros_v, vs1], axis=1),
                                 jnp.concatenate([s0, zeros_v], axis=1),
                                 jnp.concatenate([zeros_v, s1], axis=1)], axis=0)
        lhs_o = jnp.concatenate([att, qs[:, ls]], axis=1)
        outs.append(_mm(lhs_o, rhs_o))
        u = _mm(kd[:, ls], v[:, 2 * p * DV_A:(2 * p + 2) * DV_A], TN)
        dec = jnp.exp(_mm_exact_lhs_tn(la[:, ls], ones))
        new_states.append(dec[0:DK_A] * s0 + u[0:DK_A, 0:DV_A])
        new_states.append(dec[DK_A:2 * DK_A] * s1 + u[DK_A:2 * DK_A, DV_A:2 * DV_A])
    return jnp.concatenate(outs, axis=1), new_states


def _gla_kernel(qf_ref, qb_ref, sf_ref, sb_ref, wlr_ref, blr_ref, init_ref, of_ref, ob_ref, so_ref, s_ref,
                *, n_prompt_units, units_per_seq):
    j = pl.program_id(0)
    _, _, _, first = _unit_ids(j, n_prompt_units, units_per_seq)

    @pl.when(first)
    def _():
        s_ref[...] = init_ref[0]

    for d in range(N_DIR):
        x_ref, sm_ref, o_ref = (qf_ref, sf_ref, of_ref) if d == 0 else (qb_ref, sb_ref, ob_ref)
        la_unit = _log_sigmoid(_mm3(sm_ref[...], wlr_ref[d]) + blr_ref[d]) * (1.0 / GLA_NORMALIZER)
        states = [s_ref[d, h] for h in range(H_A)]
        order = range(CHUNKS_PER_UNIT - 1, -1, -1) if d == 1 else range(CHUNKS_PER_UNIT)
        for c in order:
            rows = slice(c * CHUNK, (c + 1) * CHUNK)
            o, states = _gla_chunk(x_ref[rows, 0:QA], x_ref[rows, QA:2 * QA], x_ref[rows, 2 * QA:2 * QA + VA],
                                   la_unit[rows], states, rev=(d == 1))
            o_ref[rows, :] = o
        for h in range(H_A):
            s_ref[d, h] = states[h]
    so_ref[0] = s_ref[...]


def _gla(proj, wlr, blr, init, n_prompt_units, units_per_seq):
    n_tok = proj.shape[0]
    n_units = n_tok // UNIT
    ids = functools.partial(_unit_ids, n_prompt_units=n_prompt_units, units_per_seq=units_per_seq)
    small_blk = C_SMALL // LANES
    st_blk = (1, N_DIR, H_A, DK_A, DV_A)
    kern = functools.partial(_gla_kernel, n_prompt_units=n_prompt_units, units_per_seq=units_per_seq)
    return pl.pallas_call(
        kern,
        grid=(n_units,),
        in_specs=[pl.BlockSpec((UNIT, 1024), lambda j: (ids(j)[0], 0)),
                  pl.BlockSpec((UNIT, 1024), lambda j: (ids(j)[1], 0)),
                  pl.BlockSpec((UNIT, LANES), lambda j: (ids(j)[0], small_blk)),
                  pl.BlockSpec((UNIT, LANES), lambda j: (ids(j)[1], small_blk)),
                  pl.BlockSpec((N_DIR, LANES, QA), lambda j: (0, 0, 0)),
                  pl.BlockSpec((N_DIR, 1, QA), lambda j: (0, 0, 0)),
                  pl.BlockSpec(st_blk, lambda j: (ids(j)[2], 0, 0, 0, 0))],
        out_specs=[pl.BlockSpec((UNIT, VA), lambda j: (ids(j)[0], 0)),
                   pl.BlockSpec((UNIT, VA), lambda j: (ids(j)[1], 0)),
                   pl.BlockSpec(st_blk, lambda j: (jnp.minimum(j, n_prompt_units), 0, 0, 0, 0))],
        out_shape=[jax.ShapeDtypeStruct((n_tok, VA), F32),
                   jax.ShapeDtypeStruct((n_tok, VA), F32),
                   jax.ShapeDtypeStruct((n_prompt_units + 1, N_DIR, H_A, DK_A, DV_A), F32)],
        scratch_shapes=[pltpu.VMEM((N_DIR, H_A, DK_A, DV_A), F32)],
        compiler_params=_cparams(("arbitrary",)),
        name="gla",
    )(proj, proj, proj, proj, wlr, blr, init)


def _stack_masks(rev):
    r = _iota2((HC, HC), 0)
    c = _iota2((HC, HC), 1)
    same = (r // CHUNK) == (c // CHUNK)
    tr, tc = r % CHUNK, c % CHUNK
    incl = jnp.logical_and(same, (tr <= tc) if rev else (tr >= tc))
    strict = jnp.logical_and(same, (tr < tc) if rev else (tr > tc))
    return incl, strict


def _spread(x):
    z = jnp.zeros((CHUNK, LANES), x.dtype)
    rows = []
    for h in range(H_B):
        xh = x[h * CHUNK:(h + 1) * CHUNK]
        rows.append(jnp.concatenate([xh if g == h else z for g in range(H_B)], axis=1))
    return jnp.concatenate(rows, axis=0)


def _gdn_prepare(chunks, rev):
    incl, strict = _stack_masks(rev)
    r = _iota2((HC, HC), 0)
    c = _iota2((HC, HC), 1)
    a_l, att_l, kb_l = [], [], []
    for q, k, v, gcb, beta, glast in chunks:
        grow = gcb.T[0:1, :]
        diff = gcb[:, 0:1] - grow
        decay = jnp.where(incl, jnp.exp(jnp.where(incl, diff, 0.0)), 0.0)
        kb = k * beta
        m1 = _mm(jnp.concatenate([kb, q], axis=0), k, NT)
        a_l.append(jnp.where(strict, m1[0:HC] * decay, 0.0))
        att_l.append(m1[HC:2 * HC] * decay)
        kb_l.append(kb)
    diag = (r // INV_BLOCK) == (c // INV_BLOCK)
    pw_l = [jnp.where(diag, a, 0.0) for a in a_l]
    qm_l = [-p for p in pw_l]
    n = 2
    while n < INV_BLOCK:
        pw_l = [_mm(p, p) for p in pw_l]
        qm_l = [qm + p + _mm(qm, p) for qm, p in zip(qm_l, pw_l)]
        n *= 2
    b = INV_BLOCK
    while b < CHUNK:
        off = jnp.logical_and((r // (2 * b)) == (c // (2 * b)), (r // b) != (c // b))
        al_l = [jnp.where(off, a, 0.0) for a in a_l]
        t1_l = [al + _mm(qm, al) for qm, al in zip(qm_l, al_l)]
        qm_l = [qm - (t1 + _mm(t1, qm)) for qm, t1 in zip(qm_l, t1_l)]
        b *= 2
    out = []
    for (q, k, v, gcb, beta, glast), qm, kb, att in zip(chunks, qm_l, kb_l, att_l):
        egc = jnp.exp(gcb)
        rhs = jnp.concatenate([v * beta, kb * egc], axis=1)
        sol = rhs + _mm(qm, rhs)
        out.append((sol[:, 0:DV_B], sol[:, DV_B:2 * DV_B], q * egc, k * jnp.exp(glast - gcb), att))
    return out


def _gdn_scan_step(value, k_cum, q_dec, k_dec, att, gl_rows, s):
    kq = _mm(jnp.concatenate([_spread(k_cum), _spread(q_dec)], axis=0), s)
    v_new = value - kq[0:HC]
    o = kq[HC:2 * HC] + _mm(att, v_new)
    s_new = s * gl_rows + _mm(_spread(k_dec), v_new, TN)
    return o, s_new


def _gdn_kernel(xf_ref, xb_ref, sf_ref, sb_ref, gco_ref, gdt_ref, init_ref, of_ref, ob_ref, so_ref, s_ref,
                *, n_prompt_units, units_per_seq):
    j = pl.program_id(0)
    _, _, _, first = _unit_ids(j, n_prompt_units, units_per_seq)

    @pl.when(first)
    def _():
        s_ref[...] = init_ref[0]

    for d in range(N_DIR):
        x_ref, sm_ref, o_ref = (xf_ref, sf_ref, of_ref) if d == 0 else (xb_ref, sb_ref, ob_ref)
        rev = d == 1
        sm = sm_ref[...]
        g_all = gco_ref[...] * _softplus(sm + gdt_ref[...])
        beta_all = _sigmoid(sm)
        tri = _tri(rev).astype(BF16)
        last = 0 if rev else CHUNK - 1
        chunks, gl_rows = [], []
        for c in range(CHUNKS_PER_UNIT):
            rows = slice(c * CHUNK, (c + 1) * CHUNK)
            gc_all = _mm_exact_lhs(tri, g_all[rows])
            gcb, beta, glast = [], [], []
            for h in range(H_B):
                col = SM_AB + d * H_B + h
                colb = SM_BETA + d * H_B + h
                gh = jnp.broadcast_to(gc_all[:, col:col + 1], (CHUNK, LANES))
                gcb.append(gh)
                glast.append(jnp.broadcast_to(gh[last:last + 1, :], (CHUNK, LANES)))
                beta.append(jnp.broadcast_to(beta_all[rows, colb:colb + 1], (CHUNK, LANES)))
            stack = lambda base: jnp.concatenate(
                [x_ref[rows, base + h * LANES:base + (h + 1) * LANES] for h in range(H_B)], axis=0)
            chunks.append((stack(0), stack(QB), stack(2 * QB), jnp.concatenate(gcb, axis=0),
                           jnp.concatenate(beta, axis=0), jnp.concatenate(glast, axis=0)))
            gl_rows.append(jnp.concatenate([jnp.broadcast_to(jnp.exp(g[0:1, :]), (DK_B, DV_B)) for g in glast],
                                           axis=0))
        prepared = _gdn_prepare(chunks, rev)
        s = s_ref[d]
        order = range(CHUNKS_PER_UNIT - 1, -1, -1) if rev else range(CHUNKS_PER_UNIT)
        for c in order:
            o, s = _gdn_scan_step(*prepared[c], gl_rows[c], s)
            for h in range(H_B):
                o_ref[c * CHUNK:(c + 1) * CHUNK, h * DV_B:(h + 1) * DV_B] = o[h * CHUNK:(h + 1) * CHUNK]
        s_ref[d] = s
    so_ref[0] = s_ref[...]


def _gdn(proj, gcoef, gdt, init, n_prompt_units, units_per_seq):
    n_tok = proj.shape[0]
    n_units = n_tok // UNIT
    ids = functools.partial(_unit_ids, n_prompt_units=n_prompt_units, units_per_seq=units_per_seq)
    small_blk = C_SMALL // LANES
    qkv_blk = C_QKVB // (3 * QB)
    st_blk = (1, N_DIR, H_B * DK_B, DV_B)
    kern = functools.partial(_gdn_kernel, n_prompt_units=n_prompt_units, units_per_seq=units_per_seq)
    return pl.pallas_call(
        kern,
        grid=(n_units,),
        in_specs=[pl.BlockSpec((UNIT, 3 * QB), lambda j: (ids(j)[0], qkv_blk)),
                  pl.BlockSpec((UNIT, 3 * QB), lambda j: (ids(j)[1], qkv_blk)),
                  pl.BlockSpec((UNIT, LANES), lambda j: (ids(j)[0], small_blk)),
                  pl.BlockSpec((UNIT, LANES), lambda j: (ids(j)[1], small_blk)),
                  pl.BlockSpec((1, LANES), lambda j: (0, 0)),
                  pl.BlockSpec((1, LANES), lambda j: (0, 0)),
                  pl.BlockSpec(st_blk, lambda j: (ids(j)[2], 0, 0, 0))],
        out_specs=[pl.BlockSpec((UNIT, VB), lambda j: (ids(j)[0], 0)),
                   pl.BlockSpec((UNIT, VB), lambda j: (ids(j)[1], 0)),
                   pl.BlockSpec(st_blk, lambda j: (jnp.minimum(j, n_prompt_units), 0, 0, 0))],
        out_shape=[jax.ShapeDtypeStruct((n_tok, VB), F32),
                   jax.ShapeDtypeStruct((n_tok, VB), F32),
                   jax.ShapeDtypeStruct((n_prompt_units + 1, N_DIR, H_B * DK_B, DV_B), F32)],
        scratch_shapes=[pltpu.VMEM((N_DIR, H_B * DK_B, DV_B), F32)],
        compiler_params=_cparams(("arbitrary",)),
        name="gdn",
    )(proj, proj, proj, proj, gcoef, gdt, init)


def _head_rms(o, w):
    parts = []
    for h in range(o.shape[1] // LANES):
        parts.append(_rms(o[:, h * LANES:(h + 1) * LANES]))
    return jnp.concatenate(parts, axis=1) * w


def _post_kernel(x_ref, ga_ref, zb_ref, af_ref, ab_ref, df_ref, db_ref, mod_ref, wo_ref, nwa_ref, nwb_ref, n2_ref,
                 rw_ref, x1_ref, h2_ref, sc_ref):
    m = mod_ref[0]
    gla = _head_rms(af_ref[...] + ab_ref[...], nwa_ref[...]) * _silu(ga_ref[...])
    gdn = _head_rms(df_ref[...] + db_ref[...], nwb_ref[...]) * _silu(zb_ref[...])
    y = _dg(gla.astype(BF16), wo_ref[0:VA, :], NN) + _dg(gdn.astype(BF16), wo_ref[VA:VA + VB, :], NN)
    x1 = x_ref[...] + m[:, 2 * D_MODEL:3 * D_MODEL] * y
    x1_ref[...] = x1
    h2 = (_rms(x1) * n2_ref[...]) * (1.0 + m[:, 4 * D_MODEL:5 * D_MODEL]) + m[:, 3 * D_MODEL:4 * D_MODEL]
    h2_ref[...] = _pack_rows(h2)
    sc_ref[...] = _sigmoid(_mm3(rw_ref[...], h2, NT))


def _post(x_all, proj, o_af, o_ab, o_df, o_db, mod3, w_o, nwa, nwb, n2w, router_wt, n_prompt_tok, sample_len, tm):
    n_tok = x_all.shape[0]
    cond = functools.partial(_cond_row, tiles_prompt=n_prompt_tok // tm, tiles_per_seq=sample_len // tm)
    tok = lambda i: (i, 0)
    const = lambda i: (0, 0)
    return pl.pallas_call(
        _post_kernel,
        grid=(n_tok // tm,),
        in_specs=[pl.BlockSpec((tm, D_MODEL), tok),
                  pl.BlockSpec((tm, VA), lambda i: (i, C_GA // VA)),
                  pl.BlockSpec((tm, VB), lambda i: (i, C_ZB // VB)),
                  pl.BlockSpec((tm, VA), tok), pl.BlockSpec((tm, VA), tok),
                  pl.BlockSpec((tm, VB), tok), pl.BlockSpec((tm, VB), tok),
                  pl.BlockSpec((1, 1, 6 * D_MODEL), lambda i: (cond(i), 0, 0)),
                  pl.BlockSpec((VA + VB, D_MODEL), const),
                  pl.BlockSpec((1, VA), const), pl.BlockSpec((1, VB), const), pl.BlockSpec((1, D_MODEL), const),
                  pl.BlockSpec((N_EXPERTS, D_MODEL), const)],
        out_specs=[pl.BlockSpec((tm, D_MODEL), tok),
                   pl.BlockSpec((tm, D_PACK), tok),
                   pl.BlockSpec((N_EXPERTS, tm), lambda i: (0, i))],
        out_shape=[jax.ShapeDtypeStruct((n_tok, D_MODEL), F32),
                   jax.ShapeDtypeStruct((n_tok, D_PACK), jnp.uint32),
                   jax.ShapeDtypeStruct((N_EXPERTS, n_tok), F32)],
        compiler_params=_cparams(("parallel",)),
        name="post",
    )(x_all, proj, proj, o_af, o_ab, o_df, o_db, mod3, w_o, nwa, nwb, n2w, router_wt)


def _route_kernel(sc_ref, bias_ref, e_ref, r_ref, w_ref, cnt_ref, carry_ref):
    i = pl.program_id(0)
    t = sc_ref.shape[1]

    @pl.when(i == 0)
    def _():
        carry_ref[...] = jnp.zeros(carry_ref.shape, F32)

    s = sc_ref[...]
    biased = s + bias_ref[:, 0:1]
    gs = []
    for g in range(N_GROUP):
        blk = biased[g * GROUP_SIZE:(g + 1) * GROUP_SIZE]
        m1 = jnp.max(blk, axis=0, keepdims=True)
        n1 = jnp.sum((blk == m1).astype(F32), axis=0, keepdims=True)
        m2 = jnp.max(jnp.where(blk < m1, blk, NEG_INF), axis=0, keepdims=True)
        gs.append(m1 + jnp.where(n1 >= 2.0, m1, m2))
    gsc = jnp.concatenate(gs, axis=0)
    gid = _iota2((N_GROUP, t), 0)
    beaten = jnp.zeros((N_GROUP, t), F32)
    for g in range(N_GROUP):
        other = gsc[g:g + 1, :]
        wins = jnp.logical_or(other > gsc, jnp.logical_and(other == gsc, g < gid))
        beaten = beaten + wins.astype(F32)
    masked = jnp.concatenate(
        [jnp.where(beaten[g:g + 1, :] < float(TOPK_GROUP), biased[g * GROUP_SIZE:(g + 1) * GROUP_SIZE], NEG_INF)
         for g in range(N_GROUP)], axis=0)
    eid = _iota2((N_EXPERTS, t), 0).astype(F32)
    sel = jnp.zeros((N_EXPERTS, t), F32)
    picks, scores = [], []
    for _ in range(TOP_K):
        m = jnp.max(masked, axis=0, keepdims=True)
        first = jnp.min(jnp.where(masked == m, eid, float(N_EXPERTS)), axis=0, keepdims=True)
        hit = eid == first
        scores.append(jnp.sum(jnp.where(hit, s, 0.0), axis=0, keepdims=True))
        masked = jnp.where(hit, NEG_INF, masked)
        sel = sel + hit.astype(F32)
        picks.append(first)
    upper = (_iota2((t, t), 0) < _iota2((t, t), 1)).astype(BF16)
    carry = carry_ref[...]
    prefix = _dg(sel.astype(BF16), upper, NN) + jnp.concatenate([carry] * (t // LANES), axis=1)
    ranks = [jnp.sum(jnp.where(eid == p, prefix, 0.0), axis=0, keepdims=True) for p in picks]
    carry = carry + _dg(sel.astype(BF16), jnp.ones((t, LANES), BF16), NN)
    carry_ref[...] = carry
    cnt_ref[...] = carry
    sc8 = jnp.concatenate(scores, axis=0)
    e_ref[...] = jnp.concatenate(picks, axis=0).astype(jnp.int32)
    r_ref[...] = jnp.concatenate(ranks, axis=0).astype(jnp.int32)
    w_ref[...] = sc8 / jnp.sum(sc8, axis=0, keepdims=True) * ROUTED_SCALE


def _route(scores_t, bias_col):
    n_tok = scores_t.shape[1]
    t = ROUTE_TILE
    slot = lambda i: (0, i)
    return pl.pallas_call(
        _route_kernel,
        grid=(n_tok // t,),
        in_specs=[pl.BlockSpec((N_EXPERTS, t), slot),
                  pl.BlockSpec((N_EXPERTS, LANES), lambda i: (0, 0))],
        out_specs=[pl.BlockSpec((TOP_K, t), slot), pl.BlockSpec((TOP_K, t), slot), pl.BlockSpec((TOP_K, t), slot),
                   pl.BlockSpec((N_EXPERTS, LANES), lambda i: (0, 0))],
        out_shape=[jax.ShapeDtypeStruct((TOP_K, n_tok), jnp.int32),
                   jax.ShapeDtypeStruct((TOP_K, n_tok), jnp.int32),
                   jax.ShapeDtypeStruct((TOP_K, n_tok), F32),
                   jax.ShapeDtypeStruct((N_EXPERTS, LANES), F32)],
        scratch_shapes=[pltpu.VMEM((N_EXPERTS, LANES), F32)],
        compiler_params=_cparams(("arbitrary",)),
        name="route",
    )(scores_t, bias_col)


def _dest_kernel(start_ref, e_ref, r_ref, d_ref):
    e = e_ref[...]

    def body(x, acc):
        return jnp.where(e == x, start_ref[x], acc)

    d_ref[...] = r_ref[...] + lax.fori_loop(0, N_EXPERTS, body, jnp.zeros(e.shape, jnp.int32))


def _dest(start, e8, r8, tt):
    n_tok = e8.shape[1]
    slot = lambda i, st: (0, i)
    return pl.pallas_call(
        _dest_kernel,
        grid_spec=pltpu.PrefetchScalarGridSpec(
            num_scalar_prefetch=1, grid=(n_tok // tt,),
            in_specs=[pl.BlockSpec((TOP_K, tt), slot), pl.BlockSpec((TOP_K, tt), slot)],
            out_specs=pl.BlockSpec((TOP_K, tt), slot)),
        out_shape=jax.ShapeDtypeStruct((TOP_K, n_tok), jnp.int32),
        compiler_params=_cparams(("parallel",)),
        name="dest",
    )(start, e8, r8)


def _items(counts, n_rows):
    n_blocks = n_rows // ROW_BLOCK
    max_items = n_blocks + N_EXPERTS - 1
    end = jnp.cumsum(counts)
    start = end - counts
    first_blk = start // ROW_BLOCK
    n_it = jnp.where(counts > 0, (end - 1) // ROW_BLOCK - first_blk + 1, 0)
    it_end = jnp.cumsum(n_it)
    it_start = it_end - n_it
    i = jnp.arange(max_items, dtype=jnp.int32)
    valid = i < it_end[-1]
    ex = jnp.minimum(jnp.sum((it_end[None, :] <= i[:, None]).astype(jnp.int32), axis=1), N_EXPERTS - 1)
    onehot = ex[:, None] == jnp.arange(N_EXPERTS, dtype=jnp.int32)[None, :]
    pick = lambda tab: jnp.sum(jnp.where(onehot, tab[None, :], 0), axis=1)
    blk = pick(first_blk) + (i - pick(it_start))
    lo = jnp.maximum(pick(start), blk * ROW_BLOCK) - blk * ROW_BLOCK
    hi = jnp.minimum(pick(end), (blk + 1) * ROW_BLOCK) - blk * ROW_BLOCK
    blk = jnp.where(valid, blk, n_blocks - 1).astype(jnp.int32)
    lo = jnp.where(valid, lo, 0).astype(jnp.int32)
    hi = jnp.where(valid, hi, 0).astype(jnp.int32)
    return start.astype(jnp.int32), blk, ex.astype(jnp.int32), lo, hi


def _dispatch_kernel(h2_ref, dest_hbm, xs_hbm, dsm, sem_d, sem_s):
    i = pl.program_id(0)
    n = pl.num_programs(0)
    tm = h2_ref.shape[0]

    def dest_copy(step, slot):
        return pltpu.make_async_copy(dest_hbm.at[:, pl.ds(step * tm, tm)], dsm.at[slot], sem_d.at[slot])

    def row_copy(t, dst):
        return pltpu.make_async_copy(h2_ref.at[pl.ds(t, 1), :], xs_hbm.at[pl.ds(dst, 1), :], sem_s.at[0])

    @pl.when(i == 0)
    def _():
        dest_copy(0, 0).start()

    slot = i % 2
    dest_copy(i, slot).wait()

    @pl.when(i + 1 < n)
    def _():
        dest_copy(i + 1, 1 - slot).start()

    def issue(t, carry):
        for k in range(TOP_K):
            row_copy(t, dsm[slot, k, t]).start(priority=k % N_DMA_QUEUES)
        return carry
    lax.fori_loop(0, tm, issue, 0)

    def drain(t, carry):
        for k in range(TOP_K):
            row_copy(t, 0).wait()
        return carry
    lax.fori_loop(0, tm, drain, 0)


def _dispatch(h2, dest, tm):
    n_tok = h2.shape[0]
    return pl.pallas_call(
        _dispatch_kernel,
        grid=(n_tok // tm,),
        in_specs=[pl.BlockSpec((tm, D_PACK), lambda i: (i, 0)),
                  pl.BlockSpec(memory_space=pl.ANY)],
        out_specs=pl.BlockSpec(memory_space=pl.ANY),
        out_shape=jax.ShapeDtypeStruct((n_tok * TOP_K, D_PACK), jnp.uint32),
        scratch_shapes=[pltpu.SMEM((2, TOP_K, tm), jnp.int32),
                        pltpu.SemaphoreType.DMA((2,)),
                        pltpu.SemaphoreType.DMA((1,))],
        compiler_params=_cparams(("arbitrary",)),
        name="dispatch",
    )(h2, dest)


def _experts_kernel(blk_ref, exp_ref, lo_ref, hi_ref, x_ref, wg_ref, wu_ref, wd_ref, y_ref, wgb, wub, wdb):
    i = pl.program_id(0)
    lo, hi = lo_ref[i], hi_ref[i]

    @pl.when(hi > lo)
    def _():
        @pl.when(jnp.logical_or(i == 0, exp_ref[i] != exp_ref[jnp.maximum(i - 1, 0)]))
        def _():
            wgb[...] = wg_ref[0].astype(BF16)
            wub[...] = wu_ref[0].astype(BF16)
            wdb[...] = wd_ref[0].astype(BF16)

        x_lo, x_hi = _unpack_rows(x_ref[...])
        x = jnp.concatenate([x_lo.astype(BF16), x_hi.astype(BF16)], axis=1)
        g = _dg(x, wgb[...], NN)
        u = _dg(x, wub[...], NN)
        hmid = (_silu(g) * u).astype(BF16)
        y = _pack_rows(_dg(hmid, wdb[...], NN))
        row = _iota2((ROW_BLOCK, D_PACK), 0)
        mine = jnp.logical_and(row >= lo, row < hi)

        @pl.when(lo == 0)
        def _():
            y_ref[...] = jnp.where(mine, y, jnp.uint32(0))

        @pl.when(lo > 0)
        def _():
            y_ref[...] = jnp.where(mine, y, y_ref[...])


def _experts(item_blk, item_exp, item_lo, item_hi, xs, w_gate, w_up, w_down):
    n_items = item_blk.shape[0]
    xmap = lambda i, blk, ex, lo, hi: (blk[i], 0)
    wmap = lambda i, blk, ex, lo, hi: (ex[i], 0, 0)
    return pl.pallas_call(
        _experts_kernel,
        grid_spec=pltpu.PrefetchScalarGridSpec(
            num_scalar_prefetch=4, grid=(n_items,),
            in_specs=[pl.BlockSpec((ROW_BLOCK, D_PACK), xmap),
                      pl.BlockSpec((1, D_MODEL, D_EXPERT), wmap),
                      pl.BlockSpec((1, D_MODEL, D_EXPERT), wmap),
                      pl.BlockSpec((1, D_EXPERT, D_MODEL), wmap)],
            out_specs=pl.BlockSpec((ROW_BLOCK, D_PACK), xmap),
            scratch_shapes=[pltpu.VMEM((D_MODEL, D_EXPERT), BF16), pltpu.VMEM((D_MODEL, D_EXPERT), BF16),
                            pltpu.VMEM((D_EXPERT, D_MODEL), BF16)]),
        out_shape=jax.ShapeDtypeStruct(xs.shape, jnp.uint32),
        compiler_params=_cparams(("arbitrary",)),
        name="experts",
    )(item_blk, item_exp, item_lo, item_hi, xs, w_gate, w_up, w_down)


def _final_kernel(x1_ref, h2_ref, w_ref, mod_ref, sg_ref, su_ref, sd_ref, fn_ref, dest_hbm, ys_hbm, o_ref,
                  dsm, gbuf, sem_d, sem_g):
    i = pl.program_id(0)
    n = pl.num_programs(0)
    tm = x1_ref.shape[0]

    def dest_copy(step, slot):
        return pltpu.make_async_copy(dest_hbm.at[:, pl.ds(step * tm, tm)], dsm.at[slot], sem_d.at[slot])

    def row_copy(src, slot, k, t):
        return pltpu.make_async_copy(ys_hbm.at[pl.ds(src, 1), :], gbuf.at[slot, k, pl.ds(t, 1), :], sem_g.at[slot])

    def issue_gathers(slot3, slot2):
        def body(t, carry):
            for k in range(TOP_K):
                row_copy(dsm[slot3, k, t], slot2, k, t).start(priority=k % N_DMA_QUEUES)
            return carry
        lax.fori_loop(0, tm, body, 0)

    @pl.when(i == 0)
    def _():
        dest_copy(0, 0).start()
        dest_copy(0, 0).wait()
        issue_gathers(0, 0)

        @pl.when(n > 1)
        def _():
            dest_copy(1, 1).start()

    @pl.when(i + 1 < n)
    def _():
        dest_copy(i + 1, (i + 1) % 3).wait()
        issue_gathers((i + 1) % 3, (i + 1) % 2)

        @pl.when(i + 2 < n)
        def _():
            dest_copy(i + 2, (i + 2) % 3).start()

    slot = i % 2

    def drain(t, carry):
        for k in range(TOP_K):
            row_copy(0, slot, k, t).wait()
        return carry
    lax.fori_loop(0, tm, drain, 0)

    m = mod_ref[0]
    w = w_ref[...]
    r_lo = r_hi = None
    for kk in range(TOP_K):
        y_lo, y_hi = _unpack_rows(gbuf[slot, kk])
        wk = w[:, kk:kk + 1]
        r_lo = y_lo * wk if r_lo is None else r_lo + y_lo * wk
        r_hi = y_hi * wk if r_hi is None else r_hi + y_hi * wk
    routed = jnp.concatenate([r_lo, r_hi], axis=1)
    h_lo, h_hi = _unpack_rows(h2_ref[...])
    hb = jnp.concatenate([h_lo.astype(BF16), h_hi.astype(BF16)], axis=1)
    hm = (_silu(_dg(hb, sg_ref[...], NN)) * _dg(hb, su_ref[...], NN)).astype(BF16)
    shared = _dg(hm, sd_ref[...], NN)
    x2 = x1_ref[...] + m[:, 5 * D_MODEL:6 * D_MODEL] * (routed + shared)
    o_ref[...] = _rms(x2) * fn_ref[...]


def _final(x1, h2, wts, mod3, sg, su, sd, fnw, dest, ys, n_prompt_tok, sample_len, tm):
    n_tok = x1.shape[0]
    cond = functools.partial(_cond_row, tiles_prompt=n_prompt_tok // tm, tiles_per_seq=sample_len // tm)
    tok = lambda i: (i, 0)
    const = lambda i: (0, 0)
    return pl.pallas_call(
        _final_kernel,
        grid=(n_tok // tm,),
        in_specs=[pl.BlockSpec((tm, D_MODEL), tok),
                  pl.BlockSpec((tm, D_PACK), tok),
                  pl.BlockSpec((tm, TOP_K), tok),
                  pl.BlockSpec((1, 1, 6 * D_MODEL), lambda i: (cond(i), 0, 0)),
                  pl.BlockSpec((D_MODEL, D_EXPERT), const),
                  pl.BlockSpec((D_MODEL, D_EXPERT), const),
                  pl.BlockSpec((D_EXPERT, D_MODEL), const),
                  pl.BlockSpec((1, D_MODEL), const),
                  pl.BlockSpec(memory_space=pl.ANY),
                  pl.BlockSpec(memory_space=pl.ANY)],
        out_specs=pl.BlockSpec((tm, D_MODEL), tok),
        out_shape=jax.ShapeDtypeStruct((n_tok, D_MODEL), F32),
        scratch_shapes=[pltpu.SMEM((3, TOP_K, tm), jnp.int32),
                        pltpu.VMEM((2, TOP_K, tm, D_PACK), jnp.uint32),
                        pltpu.SemaphoreType.DMA((3,)),
                        pltpu.SemaphoreType.DMA((2,))],
        compiler_params=_cparams(("arbitrary",)),
        name="final",
    )(x1, h2, wts, mod3, sg, su, sd, fnw, dest, ys)


def kernel(x_prompt, x_sample, state_gla, state_gdn, c, c_ctx, w_ada, b_ada, norm1_w, w_in, conv_w, gla_lr_w, gla_lr_b, gdn_a_log, gdn_dt_bias, gla_norm_w, gdn_norm_w, w_o, norm2_w, router_w, router_bias, exp_w_gate, exp_w_up, exp_w_down, sh_w_gate, sh_w_up, sh_w_down, final_norm_w):
    bp, lp, d = x_prompt.shape
    bs, ls, _ = x_sample.shape
    assert d == D_MODEL and lp == UNIT and ls % UNIT == 0 and w_ada.shape[0] == 1
    n_prompt_tok = bp * lp
    n_tok = n_prompt_tok + bs * ls
    n_prompt_units = n_prompt_tok // UNIT
    units_per_seq = ls // UNIT
    grid_w = 64
    layer = 0
    tm_in = 512
    tm = 256
    assert n_prompt_tok % tm_in == 0 and ls % tm_in == 0 and n_tok % ROUTE_TILE == 0

    x_all = jnp.concatenate([x_prompt.reshape(n_prompt_tok, d), x_sample.reshape(bs * ls, d)], axis=0)
    cond = jnp.concatenate([c_ctx[None, :], c, jnp.zeros((SUBLANES - 1 - bs, d), F32)], axis=0)
    mod3 = _ada(cond, w_ada[layer], b_ada[layer][None, :]).reshape(SUBLANES, 1, 6 * d)

    wi = w_in[layer]
    o_lr = 2 * QA + 2 * VA
    o_qkvb = o_lr + N_DIR * GLA_RANK
    o_zb = o_qkvb + 3 * QB
    o_ab = o_zb + VB
    w_main = jnp.concatenate([wi[:, 0:o_lr], wi[:, o_qkvb:o_ab]], axis=1).astype(BF16)
    w_small = jnp.concatenate([wi[:, o_lr:o_qkvb], wi[:, o_ab:], jnp.zeros((d, LANES - 48), F32)], axis=1)
    cw = jnp.concatenate([conv_w[layer], jnp.zeros((SUBLANES - CONV_K, 3 * QB), F32)], axis=0)
    proj = _inproj(x_all, mod3, norm1_w[layer][None, :], w_main, w_small, cw, n_prompt_tok, ls, lp, grid_w, tm_in)

    wlr = jnp.zeros((N_DIR, LANES, QA), F32)
    for dd in range(N_DIR):
        wlr = wlr.at[dd, dd * GLA_RANK:(dd + 1) * GLA_RANK, :].set(gla_lr_w[layer, dd])
    blr = gla_lr_b[layer][:, None, :]
    init_gla = jnp.concatenate([jnp.zeros((1,) + state_gla.shape[2:], F32), state_gla[:, layer].astype(F32)], axis=0)
    o_af, o_ab_, s_gla = _gla(proj, wlr, blr, init_gla, n_prompt_units, units_per_seq)

    gcoef = jnp.zeros((1, LANES), F32).at[0, SM_AB:SM_AB + N_DIR * H_B].set(-jnp.exp(gdn_a_log[layer].reshape(-1)))
    gdt = jnp.zeros((1, LANES), F32).at[0, SM_AB:SM_AB + N_DIR * H_B].set(gdn_dt_bias[layer].reshape(-1))
    init_gdn = jnp.concatenate([jnp.zeros((1, N_DIR, H_B * DK_B, DV_B), F32),
                                state_gdn[:, layer].astype(F32).reshape(bs, N_DIR, H_B * DK_B, DV_B)], axis=0)
    o_df, o_db, s_gdn = _gdn(proj, gcoef, gdt, init_gdn, n_prompt_units, units_per_seq)

    nwa = jnp.tile(gla_norm_w[layer], H_A)[None, :]
    nwb = jnp.tile(gdn_norm_w[layer], H_B)[None, :]
    x1, h2, scores_t = _post(x_all, proj, o_af, o_ab_, o_df, o_db, mod3, w_o[layer].astype(BF16), nwa, nwb,
                             norm2_w[layer][None, :], router_w[layer].T, n_prompt_tok, ls, tm)

    bias_col = jnp.broadcast_to(router_bias[layer].astype(F32)[:, None], (N_EXPERTS, LANES))
    e8, r8, w8, cnt = _route(scores_t, bias_col)
    start, item_blk, item_exp, item_lo, item_hi = _items(cnt[:, 0].astype(jnp.int32), n_tok * TOP_K)
    dest = _dest(start, e8, r8, tt=min(2048, n_tok))

    xs = _dispatch(h2, dest, tm)
    ys = _experts(item_blk, item_exp, item_lo, item_hi, xs, exp_w_gate[layer], exp_w_up[layer], exp_w_down[layer])
    y_all = _final(x1, h2, w8.T, mod3, sh_w_gate[layer].astype(BF16), sh_w_up[layer].astype(BF16),
                   sh_w_down[layer].astype(BF16), final_norm_w[None, :], dest, ys, n_prompt_tok, ls, tm)

    y_prompt = y_all[:n_prompt_tok].reshape(bp, lp, d)
    y_sample = y_all[n_prompt_tok:].reshape(bs, ls, d)
    new_state_gla = s_gla[:n_prompt_units].reshape(bp, 1, N_DIR, H_A, DK_A, DV_A).astype(x_prompt.dtype)
    new_state_gdn = s_gdn[:n_prompt_units].reshape(bp, 1, N_DIR, H_B, DK_B, DV_B).astype(x_prompt.dtype)
    return (y_prompt, y_sample, new_state_gla, new_state_gdn)
```

```python
import functools

import jax
import jax.numpy as jnp
from jax import lax
from jax.experimental import pallas as pl
from jax.experimental.pallas import tpu as pltpu

F32 = jnp.float32
BF16 = jnp.bfloat16

D_MODEL = 1024
N_DIR = 2
H_A, DK_A, DV_A = 4, 64, 128
GLA_RANK = 16
GLA_NORMALIZER = 16.0
H_B, DK_B, DV_B = 4, 128, 128
CONV_K = 5
CHUNK = 64
QA, VA = H_A * DK_A, H_A * DV_A
QB, VB = H_B * DK_B, H_B * DV_B
N_EXPERTS = 256
TOP_K = 8
N_GROUP = 8
TOPK_GROUP = 4
GROUP_SIZE = N_EXPERTS // N_GROUP
D_EXPERT = 256
ROUTED_SCALE = 2.5
EPS = 1e-6
NEG_INF = float("-inf")

UNIT = 256
CHUNKS_PER_UNIT = UNIT // CHUNK
HC = H_B * CHUNK
INV_BLOCK = 8
D_PACK = D_MODEL // 2
ROUTE_TILE = 256
ROW_BLOCK = 512
N_DMA_QUEUES = 2
LANES = 128
SUBLANES = 8
VMEM_LIMIT = 56 * 1024 * 1024

C_QKVA = 0
C_GA = 1024
C_QKVB = 1536
C_ZB = 3072
C_SMALL = 3584
D_PROJ = 3712
SM_AB = 32
SM_BETA = 40


def _cparams(sem):
    return pltpu.CompilerParams(dimension_semantics=sem, vmem_limit_bytes=VMEM_LIMIT)


def _split(a):
    hi = a.astype(BF16)
    lo = (a - hi.astype(F32)).astype(BF16)
    return hi, lo


def _dg(a, b, dims):
    return lax.dot_general(a, b, (dims, ((), ())), preferred_element_type=F32)


NN = ((1,), (0,))
NT = ((1,), (1,))
TN = ((0,), (0,))


def _mm(a, b, dims=NN):
    return _dg(a.astype(BF16), b.astype(BF16), dims)


def _mm3(a, b, dims=NN):
    ah, al = _split(a)
    bh, bl = _split(b)
    return _dg(ah, bh, dims) + (_dg(ah, bl, dims) + _dg(al, bh, dims))


def _mm_exact_lhs(a_bf16, b, dims=NN):
    bh, bl = _split(b)
    return _dg(a_bf16, bh, dims) + _dg(a_bf16, bl, dims)


def _silu(x):
    return x * (1.0 / (1.0 + jnp.exp(-x)))


def _sigmoid(x):
    return 1.0 / (1.0 + jnp.exp(-x))


def _softplus(x):
    return jnp.maximum(x, 0.0) + jnp.log1p(jnp.exp(-jnp.abs(x)))


def _log_sigmoid(x):
    return -_softplus(-x)


def _iota2(shape, dim):
    return lax.broadcasted_iota(jnp.int32, shape, dim)


def _rms(x):
    return x * lax.rsqrt(jnp.mean(x * x, axis=-1, keepdims=True) + EPS)


def _pack_rows(x):
    lo = lax.bitcast_convert_type(x[:, 0:D_PACK].astype(BF16).astype(F32), jnp.uint32)
    hi = lax.bitcast_convert_type(x[:, D_PACK:D_MODEL].astype(BF16).astype(F32), jnp.uint32)
    return (lo >> 16) | (hi & jnp.uint32(0xFFFF0000))


def _unpack_rows(p):
    lo = lax.bitcast_convert_type(p << 16, F32)
    hi = lax.bitcast_convert_type(p & jnp.uint32(0xFFFF0000), F32)
    return lo, hi


def _ada_kernel(c_ref, w_ref, b_ref, o_ref):
    o_ref[...] = _mm3(_silu(c_ref[...]), w_ref[...]) + b_ref[...]


def _ada(cond, w, b):
    n = w.shape[1]
    tn = 1536
    return pl.pallas_call(
        _ada_kernel,
        grid=(n // tn,),
        in_specs=[pl.BlockSpec((SUBLANES, D_MODEL), lambda i: (0, 0)),
                  pl.BlockSpec((D_MODEL, tn), lambda i: (0, i)),
                  pl.BlockSpec((1, tn), lambda i: (0, i))],
        out_specs=pl.BlockSpec((SUBLANES, tn), lambda i: (0, i)),
        out_shape=jax.ShapeDtypeStruct((SUBLANES, n), F32),
        compiler_params=_cparams(("parallel",)),
        name="ada",
    )(cond, w, b)


def _inproj_kernel(x_ref, mod_ref, nw_ref, w_ref, ws_ref, cw_ref, o_ref, *, tiles_prompt, prompt_row, sample_row):
    m = mod_ref[0]
    h = (_rms(x_ref[...]) * nw_ref[...]) * (1.0 + m[:, D_MODEL:2 * D_MODEL]) + m[:, 0:D_MODEL]
    hb = h.astype(BF16)
    for c0 in range(0, C_SMALL, 512):
        o_ref[:, c0:c0 + 512] = _dg(hb, w_ref[:, c0:c0 + 512], NN)
    o_ref[:, C_SMALL:D_PROJ] = _mm3(h, ws_ref[...])
    tm = x_ref.shape[0]
    row_len = jnp.where(pl.program_id(0) < tiles_prompt, prompt_row, sample_row)
    pos = _iota2((tm, LANES), 0) & (row_len - 1)
    for part in range(3 * H_B):
        cs = slice(C_QKVB + part * LANES, C_QKVB + (part + 1) * LANES)
        x = o_ref[:, cs]
        acc = jnp.zeros((tm, LANES), F32)
        for jj in range(CONV_K):
            off = jj - CONV_K // 2
            xs = x if off == 0 else pltpu.roll(x, (-off) % tm, 0)
            ok = jnp.logical_and(pos + off >= 0, pos + off < row_len)
            acc = acc + jnp.where(ok, xs, 0.0) * cw_ref[jj:jj + 1, part * LANES:(part + 1) * LANES]
        y = _silu(acc)
        if part < 2 * H_B:
            y = y * lax.rsqrt(jnp.sum(y * y, axis=-1, keepdims=True) + EPS)
            if part < H_B:
                y = y * (DK_B ** -0.5)
        o_ref[:, cs] = y


def _cond_row(i, tiles_prompt, tiles_per_seq):
    return jnp.where(i < tiles_prompt, 0, 1 + (i - tiles_prompt) // tiles_per_seq)


def _inproj(x_all, mod3, norm_w, w_main, w_small, conv_w, n_prompt_tok, sample_len, prompt_row, sample_row, tm):
    n_tok = x_all.shape[0]
    assert tm % prompt_row == 0 and tm % sample_row == 0
    cond = functools.partial(_cond_row, tiles_prompt=n_prompt_tok // tm, tiles_per_seq=sample_len // tm)
    kern = functools.partial(_inproj_kernel, tiles_prompt=n_prompt_tok // tm, prompt_row=prompt_row,
                             sample_row=sample_row)
    return pl.pallas_call(
        kern,
        grid=(n_tok // tm,),
        in_specs=[pl.BlockSpec((tm, D_MODEL), lambda i: (i, 0)),
                  pl.BlockSpec((1, 1, 6 * D_MODEL), lambda i: (cond(i), 0, 0)),
                  pl.BlockSpec((1, D_MODEL), lambda i: (0, 0)),
                  pl.BlockSpec((D_MODEL, C_SMALL), lambda i: (0, 0)),
                  pl.BlockSpec((D_MODEL, LANES), lambda i: (0, 0)),
                  pl.BlockSpec((SUBLANES, 3 * QB), lambda i: (0, 0))],
        out_specs=pl.BlockSpec((tm, D_PROJ), lambda i: (i, 0)),
        out_shape=jax.ShapeDtypeStruct((n_tok, D_PROJ), F32),
        compiler_params=_cparams(("parallel",)),
        name="inproj",
    )(x_all, mod3, norm_w, w_main, w_small, conv_w)


def _unit_ids(j, n_prompt_units, units_per_seq):
    jj = j - n_prompt_units
    b = jj // units_per_seq
    r = jj % units_per_seq
    is_prompt = j < n_prompt_units
    uf = j
    ub = jnp.where(is_prompt, j, n_prompt_units + b * units_per_seq + (units_per_seq - 1 - r))
    init_row = jnp.where(is_prompt, 0, 1 + b)
    first = jnp.logical_or(is_prompt, r == 0)
    return uf, ub, init_row, first


def _tri(rev):
    t = _iota2((CHUNK, CHUNK), 0)
    s = _iota2((CHUNK, CHUNK), 1)
    return (t <= s) if rev else (t >= s)


def _mm_exact_lhs_tn(a, ones_bf16):
    ah, al = _split(a)
    return _dg(ah, ones_bf16, TN) + _dg(al, ones_bf16, TN)


def _gla_unit(chunks, order, states, rev):
    tri = _tri(rev).astype(BF16)
    mid, last = (CHUNK // 2 - 1, 0) if rev else (CHUNK // 2, CHUNK - 1)
    scale = DK_A ** -0.5
    ones = jnp.ones((CHUNK, LANES), BF16)
    lane = _iota2((CHUNK, LANES), 1)
    row = _iota2((CHUNK, LANES), 0)
    s_in = lane % DK_A
    causal = (row <= s_in) if rev else (row >= s_in)
    zeros_v = jnp.zeros((CHUNK, DV_A), F32)
    pairs = range(H_A // 2)
    att, qs_l, u_l, dec_l = {}, {}, {}, {}
    for c, (q, k, v, la) in enumerate(chunks):
        b = _mm_exact_lhs(tri, la)
        bref = b[mid:mid + 1, :]
        blast = b[last:last + 1, :]
        qg = q * jnp.exp(b - bref) * scale
        kg = k * jnp.exp(bref - b)
        kd = k * jnp.exp(blast - b)
        qs_l[c] = q * jnp.exp(b) * scale
        for p in pairs:
            ls = slice(p * LANES, (p + 1) * LANES)
            kg_p = kg[:, ls]
            rhs_att = jnp.concatenate([jnp.where(lane < DK_A, kg_p, 0.0), jnp.where(lane >= DK_A, kg_p, 0.0)], axis=0)
            att[c, p] = jnp.where(causal, _mm(qg[:, ls], rhs_att, NT), 0.0)
            u_l[c, p] = _mm(kd[:, ls], v[:, 2 * p * DV_A:(2 * p + 2) * DV_A], TN)
            dec_l[c, p] = jnp.exp(_mm_exact_lhs_tn(la[:, ls], ones))
    start = {}
    for c in order:
        start[c] = list(states)
        nxt = []
        for p in pairs:
            u, dec = u_l[c, p], dec_l[c, p]
            nxt.append(dec[0:DK_A] * states[2 * p] + u[0:DK_A, 0:DV_A])
            nxt.append(dec[DK_A:2 * DK_A] * states[2 * p + 1] + u[DK_A:2 * DK_A, DV_A:2 * DV_A])
        states = nxt
    outs = []
    for c, (q, k, v, la) in enumerate(chunks):
        o = []
        for p in pairs:
            ls = slice(p * LANES, (p + 1) * LANES)
            vs0 = v[:, (2 * p) * DV_A:(2 * p + 1) * DV_A]
            vs1 = v[:, (2 * p + 1) * DV_A:(2 * p + 2) * DV_A]
            s0, s1 = start[c][2 * p], start[c][2 * p + 1]
            rhs_o = jnp.concatenate([jnp.concatenate([vs0, zeros_v], axis=1),
                                     jnp.concatenate([zeros_v, vs1], axis=1),
                                     jnp.concatenate([s0, zeros_v], axis=1),
                                     jnp.concatenate([zeros_v, s1], axis=1)], axis=0)
            lhs_o = jnp.concatenate([att[c, p], qs_l[c][:, ls]], axis=1)
            o.append(_mm(lhs_o, rhs_o))
        outs.append(jnp.concatenate(o, axis=1))
    return outs, states


def _gla_kernel(qf_ref, qb_ref, sf_ref, sb_ref, wlr_ref, blr_ref, init_ref, of_ref, ob_ref, so_ref, s_ref,
                *, n_prompt_units, units_per_seq):
    j = pl.program_id(0)
    _, _, _, first = _unit_ids(j, n_prompt_units, units_per_seq)

    @pl.when(first)
    def _():
        s_ref[...] = init_ref[0]

    for d in range(N_DIR):
        x_ref, sm_ref, o_ref = (qf_ref, sf_ref, of_ref) if d == 0 else (qb_ref, sb_ref, ob_ref)
        la_unit = _log_sigmoid(_mm3(sm_ref[...], wlr_ref[d]) + blr_ref[d]) * (1.0 / GLA_NORMALIZER)
        states = [s_ref[d, h] for h in range(H_A)]
        order = range(CHUNKS_PER_UNIT - 1, -1, -1) if d == 1 else range(CHUNKS_PER_UNIT)
        chunks = []
        for c in range(CHUNKS_PER_UNIT):
            rows = slice(c * CHUNK, (c + 1) * CHUNK)
            chunks.append((x_ref[rows, 0:QA], x_ref[rows, QA:2 * QA], x_ref[rows, 2 * QA:2 * QA + VA], la_unit[rows]))
        outs, states = _gla_unit(chunks, order, states, rev=(d == 1))
        for c in range(CHUNKS_PER_UNIT):
            o_ref[c * CHUNK:(c + 1) * CHUNK, :] = outs[c]
        for h in range(H_A):
            s_ref[d, h] = states[h]
    so_ref[0] = s_ref[...]


def _gla(proj, wlr, blr, init, n_prompt_units, units_per_seq):
    n_tok = proj.shape[0]
    n_units = n_tok // UNIT
    ids = functools.partial(_unit_ids, n_prompt_units=n_prompt_units, units_per_seq=units_per_seq)
    small_blk = C_SMALL // LANES
    st_blk = (1, N_DIR, H_A, DK_A, DV_A)
    kern = functools.partial(_gla_kernel, n_prompt_units=n_prompt_units, units_per_seq=units_per_seq)
    return pl.pallas_call(
        kern,
        grid=(n_units,),
        in_specs=[pl.BlockSpec((UNIT, 1024), lambda j: (ids(j)[0], 0)),
                  pl.BlockSpec((UNIT, 1024), lambda j: (ids(j)[1], 0)),
                  pl.BlockSpec((UNIT, LANES), lambda j: (ids(j)[0], small_blk)),
                  pl.BlockSpec((UNIT, LANES), lambda j: (ids(j)[1], small_blk)),
                  pl.BlockSpec((N_DIR, LANES, QA), lambda j: (0, 0, 0)),
                  pl.BlockSpec((N_DIR, 1, QA), lambda j: (0, 0, 0)),
                  pl.BlockSpec(st_blk, lambda j: (ids(j)[2], 0, 0, 0, 0))],
        out_specs=[pl.BlockSpec((UNIT, VA), lambda j: (ids(j)[0], 0)),
                   pl.BlockSpec((UNIT, VA), lambda j: (ids(j)[1], 0)),
                   pl.BlockSpec(st_blk, lambda j: (jnp.minimum(j, n_prompt_units), 0, 0, 0, 0))],
        out_shape=[jax.ShapeDtypeStruct((n_tok, VA), F32),
                   jax.ShapeDtypeStruct((n_tok, VA), F32),
                   jax.ShapeDtypeStruct((n_prompt_units + 1, N_DIR, H_A, DK_A, DV_A), F32)],
        scratch_shapes=[pltpu.VMEM((N_DIR, H_A, DK_A, DV_A), F32)],
        compiler_params=_cparams(("arbitrary",)),
        name="gla",
    )(proj, proj, proj, proj, wlr, blr, init)


def _stack_masks(rev):
    r = _iota2((HC, HC), 0)
    c = _iota2((HC, HC), 1)
    same = (r // CHUNK) == (c // CHUNK)
    tr, tc = r % CHUNK, c % CHUNK
    incl = jnp.logical_and(same, (tr <= tc) if rev else (tr >= tc))
    strict = jnp.logical_and(same, (tr < tc) if rev else (tr > tc))
    return incl, strict


def _spread(x):
    z = jnp.zeros((CHUNK, LANES), x.dtype)
    rows = []
    for h in range(H_B):
        xh = x[h * CHUNK:(h + 1) * CHUNK]
        rows.append(jnp.concatenate([xh if g == h else z for g in range(H_B)], axis=1))
    return jnp.concatenate(rows, axis=0)


def _gdn_prepare(chunks, rev):
    incl, strict = _stack_masks(rev)
    r = _iota2((HC, HC), 0)
    c = _iota2((HC, HC), 1)
    a_l, att_l, kb_l = [], [], []
    for q, k, v, gcb, beta, glast in chunks:
        grow = gcb.T[0:1, :]
        diff = gcb[:, 0:1] - grow
        decay = jnp.where(incl, jnp.exp(jnp.where(incl, diff, 0.0)), 0.0)
        kb = k * beta
        m1 = _mm(jnp.concatenate([kb, q], axis=0), k, NT)
        a_l.append(jnp.where(strict, m1[0:HC] * decay, 0.0))
        att_l.append(m1[HC:2 * HC] * decay)
        kb_l.append(kb)
    diag = (r // INV_BLOCK) == (c // INV_BLOCK)
    pw_l = [jnp.where(diag, a, 0.0) for a in a_l]
    qm_l = [-p for p in pw_l]
    n = 2
    while n < INV_BLOCK:
        pw_l = [_mm(p, p) for p in pw_l]
        qm_l = [qm + p + _mm(qm, p) for qm, p in zip(qm_l, pw_l)]
        n *= 2
    b = INV_BLOCK
    while b < CHUNK:
        off = jnp.logical_and((r // (2 * b)) == (c // (2 * b)), (r // b) != (c // b))
        al_l = [jnp.where(off, a, 0.0) for a in a_l]
        t1_l = [al + _mm(qm, al) for qm, al in zip(qm_l, al_l)]
        qm_l = [qm - (t1 + _mm(t1, qm)) for qm, t1 in zip(qm_l, t1_l)]
        b *= 2
    out = []
    for (q, k, v, gcb, beta, glast), qm, kb, att in zip(chunks, qm_l, kb_l, att_l):
        egc = jnp.exp(gcb)
        rhs = jnp.concatenate([v * beta, kb * egc], axis=1)
        sol = rhs + _mm(qm, rhs)
        out.append((sol[:, 0:DV_B], sol[:, DV_B:2 * DV_B], q * egc, k * jnp.exp(glast - gcb), att))
    return out


def _gdn_scan_step(value, k_cum, q_dec, k_dec, att, gl_rows, s):
    kq = _mm(jnp.concatenate([_spread(k_cum), _spread(q_dec)], axis=0), s)
    v_new = value - kq[0:HC]
    o = kq[HC:2 * HC] + _mm(att, v_new)
    s_new = s * gl_rows + _mm(_spread(k_dec), v_new, TN)
    return o, s_new


def _gdn_kernel(xf_ref, xb_ref, sf_ref, sb_ref, gco_ref, gdt_ref, init_ref, of_ref, ob_ref, so_ref, s_ref,
                *, n_prompt_units, units_per_seq):
    j = pl.program_id(0)
    _, _, _, first = _unit_ids(j, n_prompt_units, units_per_seq)

    @pl.when(first)
    def _():
        s_ref[...] = init_ref[0]

    for d in range(N_DIR):
        x_ref, sm_ref, o_ref = (xf_ref, sf_ref, of_ref) if d == 0 else (xb_ref, sb_ref, ob_ref)
        rev = d == 1
        sm = sm_ref[...]
        g_all = gco_ref[...] * _softplus(sm + gdt_ref[...])
        beta_all = _sigmoid(sm)
        tri = _tri(rev).astype(BF16)
        last = 0 if rev else CHUNK - 1
        chunks, gl_rows = [], []
        for c in range(CHUNKS_PER_UNIT):
            rows = slice(c * CHUNK, (c + 1) * CHUNK)
            gc_all = _mm_exact_lhs(tri, g_all[rows])
            gcb, beta, glast = [], [], []
            for h in range(H_B):
                col = SM_AB + d * H_B + h
                colb = SM_BETA + d * H_B + h
                gh = jnp.broadcast_to(gc_all[:, col:col + 1], (CHUNK, LANES))
                gcb.append(gh)
                glast.append(jnp.broadcast_to(gh[last:last + 1, :], (CHUNK, LANES)))
                beta.append(jnp.broadcast_to(beta_all[rows, colb:colb + 1], (CHUNK, LANES)))
            stack = lambda base: jnp.concatenate(
                [x_ref[rows, base + h * LANES:base + (h + 1) * LANES] for h in range(H_B)], axis=0)
            chunks.append((stack(0), stack(QB), stack(2 * QB), jnp.concatenate(gcb, axis=0),
                           jnp.concatenate(beta, axis=0), jnp.concatenate(glast, axis=0)))
            gl_rows.append(jnp.concatenate([jnp.broadcast_to(jnp.exp(g[0:1, :]), (DK_B, DV_B)) for g in glast],
                                           axis=0))
        prepared = _gdn_prepare(chunks, rev)
        s = s_ref[d]
        order = range(CHUNKS_PER_UNIT - 1, -1, -1) if rev else range(CHUNKS_PER_UNIT)
        for c in order:
            o, s = _gdn_scan_step(*prepared[c], gl_rows[c], s)
            for h in range(H_B):
                o_ref[c * CHUNK:(c + 1) * CHUNK, h * DV_B:(h + 1) * DV_B] = o[h * CHUNK:(h + 1) * CHUNK]
        s_ref[d] = s
    so_ref[0] = s_ref[...]


def _gdn(proj, gcoef, gdt, init, n_prompt_units, units_per_seq):
    n_tok = proj.shape[0]
    n_units = n_tok // UNIT
    ids = functools.partial(_unit_ids, n_prompt_units=n_prompt_units, units_per_seq=units_per_seq)
    small_blk = C_SMALL // LANES
    qkv_blk = C_QKVB // (3 * QB)
    st_blk = (1, N_DIR, H_B * DK_B, DV_B)
    kern = functools.partial(_gdn_kernel, n_prompt_units=n_prompt_units, units_per_seq=units_per_seq)
    return pl.pallas_call(
        kern,
        grid=(n_units,),
        in_specs=[pl.BlockSpec((UNIT, 3 * QB), lambda j: (ids(j)[0], qkv_blk)),
                  pl.BlockSpec((UNIT, 3 * QB), lambda j: (ids(j)[1], qkv_blk)),
                  pl.BlockSpec((UNIT, LANES), lambda j: (ids(j)[0], small_blk)),
                  pl.BlockSpec((UNIT, LANES), lambda j: (ids(j)[1], small_blk)),
                  pl.BlockSpec((1, LANES), lambda j: (0, 0)),
                  pl.BlockSpec((1, LANES), lambda j: (0, 0)),
                  pl.BlockSpec(st_blk, lambda j: (ids(j)[2], 0, 0, 0))],
        out_specs=[pl.BlockSpec((UNIT, VB), lambda j: (ids(j)[0], 0)),
                   pl.BlockSpec((UNIT, VB), lambda j: (ids(j)[1], 0)),
                   pl.BlockSpec(st_blk, lambda j: (jnp.minimum(j, n_prompt_units), 0, 0, 0))],
        out_shape=[jax.ShapeDtypeStruct((n_tok, VB), F32),
                   jax.ShapeDtypeStruct((n_tok, VB), F32),
                   jax.ShapeDtypeStruct((n_prompt_units + 1, N_DIR, H_B * DK_B, DV_B), F32)],
        scratch_shapes=[pltpu.VMEM((N_DIR, H_B * DK_B, DV_B), F32)],
        compiler_params=_cparams(("arbitrary",)),
        name="gdn",
    )(proj, proj, proj, proj, gcoef, gdt, init)


def _head_rms(o, w):
    parts = []
    for h in range(o.shape[1] // LANES):
        parts.append(_rms(o[:, h * LANES:(h + 1) * LANES]))
    return jnp.concatenate(parts, axis=1) * w


def _post_kernel(x_ref, ga_ref, zb_ref, af_ref, ab_ref, df_ref, db_ref, mod_ref, wo_ref, nwa_ref, nwb_ref, n2_ref,
                 rw_ref, x1_ref, h2_ref, sc_ref):
    m = mod_ref[0]
    gla = _head_rms(af_ref[...] + ab_ref[...], nwa_ref[...]) * _silu(ga_ref[...])
    gdn = _head_rms(df_ref[...] + db_ref[...], nwb_ref[...]) * _silu(zb_ref[...])
    y = _dg(gla.astype(BF16), wo_ref[0:VA, :], NN) + _dg(gdn.astype(BF16), wo_ref[VA:VA + VB, :], NN)
    x1 = x_ref[...] + m[:, 2 * D_MODEL:3 * D_MODEL] * y
    x1_ref[...] = x1
    h2 = (_rms(x1) * n2_ref[...]) * (1.0 + m[:, 4 * D_MODEL:5 * D_MODEL]) + m[:, 3 * D_MODEL:4 * D_MODEL]
    h2_ref[...] = _pack_rows(h2)
    sc_ref[...] = _sigmoid(_mm3(rw_ref[...], h2, NT))


def _post(x_all, proj, o_af, o_ab, o_df, o_db, mod3, w_o, nwa, nwb, n2w, router_wt, n_prompt_tok, sample_len, tm):
    n_tok = x_all.shape[0]
    cond = functools.partial(_cond_row, tiles_prompt=n_prompt_tok // tm, tiles_per_seq=sample_len // tm)
    tok = lambda i: (i, 0)
    const = lambda i: (0, 0)
    return pl.pallas_call(
        _post_kernel,
        grid=(n_tok // tm,),
        in_specs=[pl.BlockSpec((tm, D_MODEL), tok),
                  pl.BlockSpec((tm, VA), lambda i: (i, C_GA // VA)),
                  pl.BlockSpec((tm, VB), lambda i: (i, C_ZB // VB)),
                  pl.BlockSpec((tm, VA), tok), pl.BlockSpec((tm, VA), tok),
                  pl.BlockSpec((tm, VB), tok), pl.BlockSpec((tm, VB), tok),
                  pl.BlockSpec((1, 1, 6 * D_MODEL), lambda i: (cond(i), 0, 0)),
                  pl.BlockSpec((VA + VB, D_MODEL), const),
                  pl.BlockSpec((1, VA), const), pl.BlockSpec((1, VB), const), pl.BlockSpec((1, D_MODEL), const),
                  pl.BlockSpec((N_EXPERTS, D_MODEL), const)],
        out_specs=[pl.BlockSpec((tm, D_MODEL), tok),
                   pl.BlockSpec((tm, D_PACK), tok),
                   pl.BlockSpec((N_EXPERTS, tm), lambda i: (0, i))],
        out_shape=[jax.ShapeDtypeStruct((n_tok, D_MODEL), F32),
                   jax.ShapeDtypeStruct((n_tok, D_PACK), jnp.uint32),
                   jax.ShapeDtypeStruct((N_EXPERTS, n_tok), F32)],
        compiler_params=_cparams(("parallel",)),
        name="post",
    )(x_all, proj, proj, o_af, o_ab, o_df, o_db, mod3, w_o, nwa, nwb, n2w, router_wt)


def _route_kernel(sc_ref, bias_ref, e_ref, r_ref, w_ref, cnt_ref, carry_ref):
    i = pl.program_id(0)
    t = sc_ref.shape[1]

    @pl.when(i == 0)
    def _():
        carry_ref[...] = jnp.zeros(carry_ref.shape, F32)

    s = sc_ref[...]
    biased = s + bias_ref[:, 0:1]
    gs = []
    for g in range(N_GROUP):
        blk = biased[g * GROUP_SIZE:(g + 1) * GROUP_SIZE]
        m1 = jnp.max(blk, axis=0, keepdims=True)
        n1 = jnp.sum((blk == m1).astype(F32), axis=0, keepdims=True)
        m2 = jnp.max(jnp.where(blk < m1, blk, NEG_INF), axis=0, keepdims=True)
        gs.append(m1 + jnp.where(n1 >= 2.0, m1, m2))
    gsc = jnp.concatenate(gs, axis=0)
    gid = _iota2((N_GROUP, t), 0)
    beaten = jnp.zeros((N_GROUP, t), F32)
    for g in range(N_GROUP):
        other = gsc[g:g + 1, :]
        wins = jnp.logical_or(other > gsc, jnp.logical_and(other == gsc, g < gid))
        beaten = beaten + wins.astype(F32)
    masked = jnp.concatenate(
        [jnp.where(beaten[g:g + 1, :] < float(TOPK_GROUP), biased[g * GROUP_SIZE:(g + 1) * GROUP_SIZE], NEG_INF)
         for g in range(N_GROUP)], axis=0)
    eid = _iota2((N_EXPERTS, t), 0).astype(F32)
    sel = jnp.zeros((N_EXPERTS, t), F32)
    picks, scores = [], []
    for _ in range(TOP_K):
        m = jnp.max(masked, axis=0, keepdims=True)
        first = jnp.min(jnp.where(masked == m, eid, float(N_EXPERTS)), axis=0, keepdims=True)
        hit = eid == first
        scores.append(jnp.sum(jnp.where(hit, s, 0.0), axis=0, keepdims=True))
        masked = jnp.where(hit, NEG_INF, masked)
        sel = sel + hit.astype(F32)
        picks.append(first)
    upper = (_iota2((t, t), 0) < _iota2((t, t), 1)).astype(BF16)
    carry = carry_ref[...]
    prefix = _dg(sel.astype(BF16), upper, NN) + jnp.concatenate([carry] * (t // LANES), axis=1)
    ranks = [jnp.sum(jnp.where(eid == p, prefix, 0.0), axis=0, keepdims=True) for p in picks]
    carry = carry + _dg(sel.astype(BF16), jnp.ones((t, LANES), BF16), NN)
    carry_ref[...] = carry
    cnt_ref[...] = carry
    sc8 = jnp.concatenate(scores, axis=0)
    e_ref[...] = jnp.concatenate(picks, axis=0).astype(jnp.int32)
    r_ref[...] = jnp.concatenate(ranks, axis=0).astype(jnp.int32)
    w_ref[...] = sc8 / jnp.sum(sc8, axis=0, keepdims=True) * ROUTED_SCALE


def _route(scores_t, bias_col):
    n_tok = scores_t.shape[1]
    t = ROUTE_TILE
    slot = lambda i: (0, i)
    return pl.pallas_call(
        _route_kernel,
        grid=(n_tok // t,),
        in_specs=[pl.BlockSpec((N_EXPERTS, t), slot),
                  pl.BlockSpec((N_EXPERTS, LANES), lambda i: (0, 0))],
        out_specs=[pl.BlockSpec((TOP_K, t), slot), pl.BlockSpec((TOP_K, t), slot), pl.BlockSpec((TOP_K, t), slot),
                   pl.BlockSpec((N_EXPERTS, LANES), lambda i: (0, 0))],
        out_shape=[jax.ShapeDtypeStruct((TOP_K, n_tok), jnp.int32),
                   jax.ShapeDtypeStruct((TOP_K, n_tok), jnp.int32),
                   jax.ShapeDtypeStruct((TOP_K, n_tok), F32),
                   jax.ShapeDtypeStruct((N_EXPERTS, LANES), F32)],
        scratch_shapes=[pltpu.VMEM((N_EXPERTS, LANES), F32)],
        compiler_params=_cparams(("arbitrary",)),
        name="route",
    )(scores_t, bias_col)


def _dest_kernel(start_ref, e_ref, r_ref, d_ref):
    e = e_ref[...]

    def body(x, acc):
        return jnp.where(e == x, start_ref[x], acc)

    d_ref[...] = r_ref[...] + lax.fori_loop(0, N_EXPERTS, body, jnp.zeros(e.shape, jnp.int32))


def _dest(start, e8, r8, tt):
    n_tok = e8.shape[1]
    slot = lambda i, st: (0, i)
    return pl.pallas_call(
        _dest_kernel,
        grid_spec=pltpu.PrefetchScalarGridSpec(
            num_scalar_prefetch=1, grid=(n_tok // tt,),
            in_specs=[pl.BlockSpec((TOP_K, tt), slot), pl.BlockSpec((TOP_K, tt), slot)],
            out_specs=pl.BlockSpec((TOP_K, tt), slot)),
        out_shape=jax.ShapeDtypeStruct((TOP_K, n_tok), jnp.int32),
        compiler_params=_cparams(("parallel",)),
        name="dest",
    )(start, e8, r8)


def _items(counts, n_rows):
    n_blocks = n_rows // ROW_BLOCK
    max_items = n_blocks + N_EXPERTS - 1
    end = jnp.cumsum(counts)
    start = end - counts
    first_blk = start // ROW_BLOCK
    n_it = jnp.where(counts > 0, (end - 1) // ROW_BLOCK - first_blk + 1, 0)
    it_end = jnp.cumsum(n_it)
    it_start = it_end - n_it
    i = jnp.arange(max_items, dtype=jnp.int32)
    valid = i < it_end[-1]
    ex = jnp.minimum(jnp.sum((it_end[None, :] <= i[:, None]).astype(jnp.int32), axis=1), N_EXPERTS - 1)
    onehot = ex[:, None] == jnp.arange(N_EXPERTS, dtype=jnp.int32)[None, :]
    pick = lambda tab: jnp.sum(jnp.where(onehot, tab[None, :], 0), axis=1)
    blk = pick(first_blk) + (i - pick(it_start))
    lo = jnp.maximum(pick(start), blk * ROW_BLOCK) - blk * ROW_BLOCK
    hi = jnp.minimum(pick(end), (blk + 1) * ROW_BLOCK) - blk * ROW_BLOCK
    blk = jnp.where(valid, blk, n_blocks - 1).astype(jnp.int32)
    lo = jnp.where(valid, lo, 0).astype(jnp.int32)
    hi = jnp.where(valid, hi, 0).astype(jnp.int32)
    return start.astype(jnp.int32), blk, ex.astype(jnp.int32), lo, hi


def _dispatch_kernel(h2_ref, dest_hbm, xs_hbm, dsm, sem_d, sem_s):
    i = pl.program_id(0)
    n = pl.num_programs(0)
    tm = h2_ref.shape[0]

    def dest_copy(step, slot):
        return pltpu.make_async_copy(dest_hbm.at[:, pl.ds(step * tm, tm)], dsm.at[slot], sem_d.at[slot])

    def row_copy(t, dst):
        return pltpu.make_async_copy(h2_ref.at[pl.ds(t, 1), :], xs_hbm.at[pl.ds(dst, 1), :], sem_s.at[0])

    @pl.when(i == 0)
    def _():
        dest_copy(0, 0).start()

    slot = i % 2
    dest_copy(i, slot).wait()

    @pl.when(i + 1 < n)
    def _():
        dest_copy(i + 1, 1 - slot).start()

    def issue(t, carry):
        for k in range(TOP_K):
            row_copy(t, dsm[slot, k, t]).start(priority=k % N_DMA_QUEUES)
        return carry
    lax.fori_loop(0, tm, issue, 0, unroll=2)

    def drain(t, carry):
        for k in range(TOP_K):
            row_copy(t, 0).wait()
        return carry
    lax.fori_loop(0, tm, drain, 0)


def _dispatch(h2, dest, tm):
    n_tok = h2.shape[0]
    return pl.pallas_call(
        _dispatch_kernel,
        grid=(n_tok // tm,),
        in_specs=[pl.BlockSpec((tm, D_PACK), lambda i: (i, 0)),
                  pl.BlockSpec(memory_space=pl.ANY)],
        out_specs=pl.BlockSpec(memory_space=pl.ANY),
        out_shape=jax.ShapeDtypeStruct((n_tok * TOP_K, D_PACK), jnp.uint32),
        scratch_shapes=[pltpu.SMEM((2, TOP_K, tm), jnp.int32),
                        pltpu.SemaphoreType.DMA((2,)),
                        pltpu.SemaphoreType.DMA((1,))],
        compiler_params=_cparams(("arbitrary",)),
        name="dispatch",
    )(h2, dest)


def _experts_kernel(blk_ref, exp_ref, lo_ref, hi_ref, x_ref, wg_ref, wu_ref, wd_ref, y_ref, wgb, wub, wdb):
    i = pl.program_id(0)
    lo, hi = lo_ref[i], hi_ref[i]

    @pl.when(hi > lo)
    def _():
        @pl.when(jnp.logical_or(i == 0, exp_ref[i] != exp_ref[jnp.maximum(i - 1, 0)]))
        def _():
            wgb[...] = wg_ref[0].astype(BF16)
            wub[...] = wu_ref[0].astype(BF16)
            wdb[...] = wd_ref[0].astype(BF16)

        x_lo, x_hi = _unpack_rows(x_ref[...])
        x = jnp.concatenate([x_lo.astype(BF16), x_hi.astype(BF16)], axis=1)
        g = _dg(x, wgb[...], NN)
        u = _dg(x, wub[...], NN)
        hmid = (_silu(g) * u).astype(BF16)
        y = _pack_rows(_dg(hmid, wdb[...], NN))
        row = _iota2((ROW_BLOCK, D_PACK), 0)
        mine = jnp.logical_and(row >= lo, row < hi)

        @pl.when(lo == 0)
        def _():
            y_ref[...] = jnp.where(mine, y, jnp.uint32(0))

        @pl.when(lo > 0)
        def _():
            y_ref[...] = jnp.where(mine, y, y_ref[...])


def _experts(item_blk, item_exp, item_lo, item_hi, xs, w_gate, w_up, w_down):
    n_items = item_blk.shape[0]
    xmap = lambda i, blk, ex, lo, hi: (blk[i], 0)
    wmap = lambda i, blk, ex, lo, hi: (ex[i], 0, 0)
    return pl.pallas_call(
        _experts_kernel,
        grid_spec=pltpu.PrefetchScalarGridSpec(
            num_scalar_prefetch=4, grid=(n_items,),
            in_specs=[pl.BlockSpec((ROW_BLOCK, D_PACK), xmap),
                      pl.BlockSpec((1, D_MODEL, D_EXPERT), wmap),
                      pl.BlockSpec((1, D_MODEL, D_EXPERT), wmap),
                      pl.BlockSpec((1, D_EXPERT, D_MODEL), wmap)],
            out_specs=pl.BlockSpec((ROW_BLOCK, D_PACK), xmap),
            scratch_shapes=[pltpu.VMEM((D_MODEL, D_EXPERT), BF16), pltpu.VMEM((D_MODEL, D_EXPERT), BF16),
                            pltpu.VMEM((D_EXPERT, D_MODEL), BF16)]),
        out_shape=jax.ShapeDtypeStruct(xs.shape, jnp.uint32),
        compiler_params=_cparams(("arbitrary",)),
        name="experts",
    )(item_blk, item_exp, item_lo, item_hi, xs, w_gate, w_up, w_down)


def _final_kernel(x1_ref, h2_ref, w_ref, mod_ref, sg_ref, su_ref, sd_ref, fn_ref, dest_hbm, ys_hbm, o_ref,
                  dsm, gbuf, sem_d, sem_g):
    i = pl.program_id(0)
    n = pl.num_programs(0)
    tm = x1_ref.shape[0]

    def dest_copy(step, slot):
        return pltpu.make_async_copy(dest_hbm.at[:, pl.ds(step * tm, tm)], dsm.at[slot], sem_d.at[slot])

    def row_copy(src, slot, k, t):
        return pltpu.make_async_copy(ys_hbm.at[pl.ds(src, 1), :], gbuf.at[slot, k, pl.ds(t, 1), :], sem_g.at[slot])

    def issue_gathers(slot3, slot2):
        def body(t, carry):
            for k in range(TOP_K):
                row_copy(dsm[slot3, k, t], slot2, k, t).start(priority=k % N_DMA_QUEUES)
            return carry
        lax.fori_loop(0, tm, body, 0, unroll=2)

    @pl.when(i == 0)
    def _():
        dest_copy(0, 0).start()
        dest_copy(0, 0).wait()
        issue_gathers(0, 0)

        @pl.when(n > 1)
        def _():
            dest_copy(1, 1).start()

    @pl.when(i + 1 < n)
    def _():
        dest_copy(i + 1, (i + 1) % 3).wait()
        issue_gathers((i + 1) % 3, (i + 1) % 2)

        @pl.when(i + 2 < n)
        def _():
            dest_copy(i + 2, (i + 2) % 3).start()

    slot = i % 2

    def drain(t, carry):
        for k in range(TOP_K):
            row_copy(0, slot, k, t).wait()
        return carry
    lax.fori_loop(0, tm, drain, 0)

    m = mod_ref[0]
    w = w_ref[...]
    r_lo = r_hi = None
    for kk in range(TOP_K):
        y_lo, y_hi = _unpack_rows(gbuf[slot, kk])
        wk = w[:, kk:kk + 1]
        r_lo = y_lo * wk if r_lo is None else r_lo + y_lo * wk
        r_hi = y_hi * wk if r_hi is None else r_hi + y_hi * wk
    routed = jnp.concatenate([r_lo, r_hi], axis=1)
    h_lo, h_hi = _unpack_rows(h2_ref[...])
    hb = jnp.concatenate([h_lo.astype(BF16), h_hi.astype(BF16)], axis=1)
    hm = (_silu(_dg(hb, sg_ref[...], NN)) * _dg(hb, su_ref[...], NN)).astype(BF16)
    shared = _dg(hm, sd_ref[...], NN)
    x2 = x1_ref[...] + m[:, 5 * D_MODEL:6 * D_MODEL] * (routed + shared)
    o_ref[...] = _rms(x2) * fn_ref[...]


def _final(x1, h2, wts, mod3, sg, su, sd, fnw, dest, ys, n_prompt_tok, sample_len, tm):
    n_tok = x1.shape[0]
    cond = functools.partial(_cond_row, tiles_prompt=n_prompt_tok // tm, tiles_per_seq=sample_len // tm)
    tok = lambda i: (i, 0)
    const = lambda i: (0, 0)
    return pl.pallas_call(
        _final_kernel,
        grid=(n_tok // tm,),
        in_specs=[pl.BlockSpec((tm, D_MODEL), tok),
                  pl.BlockSpec((tm, D_PACK), tok),
                  pl.BlockSpec((tm, TOP_K), tok),
                  pl.BlockSpec((1, 1, 6 * D_MODEL), lambda i: (cond(i), 0, 0)),
                  pl.BlockSpec((D_MODEL, D_EXPERT), const),
                  pl.BlockSpec((D_MODEL, D_EXPERT), const),
                  pl.BlockSpec((D_EXPERT, D_MODEL), const),
                  pl.BlockSpec((1, D_MODEL), const),
                  pl.BlockSpec(memory_space=pl.ANY),
                  pl.BlockSpec(memory_space=pl.ANY)],
        out_specs=pl.BlockSpec((tm, D_MODEL), tok),
        out_shape=jax.ShapeDtypeStruct((n_tok, D_MODEL), F32),
        scratch_shapes=[pltpu.SMEM((3, TOP_K, tm), jnp.int32),
                        pltpu.VMEM((2, TOP_K, tm, D_PACK), jnp.uint32),
                        pltpu.SemaphoreType.DMA((3,)),
                        pltpu.SemaphoreType.DMA((2,))],
        compiler_params=_cparams(("arbitrary",)),
        name="final",
    )(x1, h2, wts, mod3, sg, su, sd, fnw, dest, ys)


def kernel(x_prompt, x_sample, state_gla, state_gdn, c, c_ctx, w_ada, b_ada, norm1_w, w_in, conv_w, gla_lr_w, gla_lr_b, gdn_a_log, gdn_dt_bias, gla_norm_w, gdn_norm_w, w_o, norm2_w, router_w, router_bias, exp_w_gate, exp_w_up, exp_w_down, sh_w_gate, sh_w_up, sh_w_down, final_norm_w):
    bp, lp, d = x_prompt.shape
    bs, ls, _ = x_sample.shape
    assert d == D_MODEL and lp == UNIT and ls % UNIT == 0 and w_ada.shape[0] == 1
    n_prompt_tok = bp * lp
    n_tok = n_prompt_tok + bs * ls
    n_prompt_units = n_prompt_tok // UNIT
    units_per_seq = ls // UNIT
    grid_w = 64
    layer = 0
    tm_in = 512
    tm = 256
    assert n_prompt_tok % tm_in == 0 and ls % tm_in == 0 and n_tok % ROUTE_TILE == 0

    x_all = jnp.concatenate([x_prompt.reshape(n_prompt_tok, d), x_sample.reshape(bs * ls, d)], axis=0)
    cond = jnp.concatenate([c_ctx[None, :], c, jnp.zeros((SUBLANES - 1 - bs, d), F32)], axis=0)
    mod3 = _ada(cond, w_ada[layer], b_ada[layer][None, :]).reshape(SUBLANES, 1, 6 * d)

    wi = w_in[layer]
    o_lr = 2 * QA + 2 * VA
    o_qkvb = o_lr + N_DIR * GLA_RANK
    o_zb = o_qkvb + 3 * QB
    o_ab = o_zb + VB
    w_main = jnp.concatenate([wi[:, 0:o_lr], wi[:, o_qkvb:o_ab]], axis=1).astype(BF16)
    w_small = jnp.concatenate([wi[:, o_lr:o_qkvb], wi[:, o_ab:], jnp.zeros((d, LANES - 48), F32)], axis=1)
    cw = jnp.concatenate([conv_w[layer], jnp.zeros((SUBLANES - CONV_K, 3 * QB), F32)], axis=0)
    proj = _inproj(x_all, mod3, norm1_w[layer][None, :], w_main, w_small, cw, n_prompt_tok, ls, lp, grid_w, tm_in)

    wlr = jnp.zeros((N_DIR, LANES, QA), F32)
    for dd in range(N_DIR):
        wlr = wlr.at[dd, dd * GLA_RANK:(dd + 1) * GLA_RANK, :].set(gla_lr_w[layer, dd])
    blr = gla_lr_b[layer][:, None, :]
    init_gla = jnp.concatenate([jnp.zeros((1,) + state_gla.shape[2:], F32), state_gla[:, layer].astype(F32)], axis=0)
    o_af, o_ab_, s_gla = _gla(proj, wlr, blr, init_gla, n_prompt_units, units_per_seq)

    gcoef = jnp.zeros((1, LANES), F32).at[0, SM_AB:SM_AB + N_DIR * H_B].set(-jnp.exp(gdn_a_log[layer].reshape(-1)))
    gdt = jnp.zeros((1, LANES), F32).at[0, SM_AB:SM_AB + N_DIR * H_B].set(gdn_dt_bias[layer].reshape(-1))
    init_gdn = jnp.concatenate([jnp.zeros((1, N_DIR, H_B * DK_B, DV_B), F32),
                                state_gdn[:, layer].astype(F32).reshape(bs, N_DIR, H_B * DK_B, DV_B)], axis=0)
    o_df, o_db, s_gdn = _gdn(proj, gcoef, gdt, init_gdn, n_prompt_units, units_per_seq)

    nwa = jnp.tile(gla_norm_w[layer], H_A)[None, :]
    nwb = jnp.tile(gdn_norm_w[layer], H_B)[None, :]
    x1, h2, scores_t = _post(x_all, proj, o_af, o_ab_, o_df, o_db, mod3, w_o[layer].astype(BF16), nwa, nwb,
                             norm2_w[layer][None, :], router_w[layer].T, n_prompt_tok, ls, tm)

    bias_col = jnp.broadcast_to(router_bias[layer].astype(F32)[:, None], (N_EXPERTS, LANES))
    e8, r8, w8, cnt = _route(scores_t, bias_col)
    start, item_blk, item_exp, item_lo, item_hi = _items(cnt[:, 0].astype(jnp.int32), n_tok * TOP_K)
    dest = _dest(start, e8, r8, tt=min(2048, n_tok))

    xs = _dispatch(h2, dest, tm)
    ys = _experts(item_blk, item_exp, item_lo, item_hi, xs, exp_w_gate[layer], exp_w_up[layer], exp_w_down[layer])
    y_all = _final(x1, h2, w8.T, mod3, sh_w_gate[layer].astype(BF16), sh_w_up[layer].astype(BF16),
                   sh_w_down[layer].astype(BF16), final_norm_w[None, :], dest, ys, n_prompt_tok, ls, tm)

    y_prompt = y_all[:n_prompt_tok].reshape(bp, lp, d)
    y_sample = y_all[n_prompt_tok:].reshape(bs, ls, d)
    new_state_gla = s_gla[:n_prompt_units].reshape(bp, 1, N_DIR, H_A, DK_A, DV_A).astype(x_prompt.dtype)
    new_state_gdn = s_gdn[:n_prompt_units].reshape(bp, 1, N_DIR, H_B, DK_B, DV_B).astype(x_prompt.dtype)
    return (y_prompt, y_sample, new_state_gla, new_state_gdn)
```

```python
import functools

import jax
import jax.numpy as jnp
from jax import lax
from jax.experimental import pallas as pl
from jax.experimental.pallas import tpu as pltpu

F32 = jnp.float32
BF16 = jnp.bfloat16

D_MODEL = 1024
N_DIR = 2
H_A, DK_A, DV_A = 4, 64, 128
GLA_RANK = 16
GLA_NORMALIZER = 16.0
H_B, DK_B, DV_B = 4, 128, 128
CONV_K = 5
CHUNK = 64
QA, VA = H_A * DK_A, H_A * DV_A
QB, VB = H_B * DK_B, H_B * DV_B
N_EXPERTS = 256
TOP_K = 8
N_GROUP = 8
TOPK_GROUP = 4
GROUP_SIZE = N_EXPERTS // N_GROUP
D_EXPERT = 256
ROUTED_SCALE = 2.5
EPS = 1e-6
NEG_INF = float("-inf")

UNIT = 256
CHUNKS_PER_UNIT = UNIT // CHUNK
HC = H_B * CHUNK
INV_BLOCK = 8
D_PACK = D_MODEL // 2
ROUTE_TILE = 256
ROW_BLOCK = 512
N_DMA_QUEUES = 2
LANES = 128
SUBLANES = 8
VMEM_LIMIT = 56 * 1024 * 1024

C_QKVA = 0
C_GA = 1024
C_QKVB = 1536
C_ZB = 3072
C_SMALL = 3584
D_PROJ = 3712
SM_AB = 32
SM_BETA = 40


def _cparams(sem):
    return pltpu.CompilerParams(dimension_semantics=sem, vmem_limit_bytes=VMEM_LIMIT)


def _split(a):
    hi = a.astype(BF16)
    lo = (a - hi.astype(F32)).astype(BF16)
    return hi, lo


def _dg(a, b, dims):
    return lax.dot_general(a, b, (dims, ((), ())), preferred_element_type=F32)


NN = ((1,), (0,))
NT = ((1,), (1,))
TN = ((0,), (0,))


def _mm(a, b, dims=NN):
    return _dg(a.astype(BF16), b.astype(BF16), dims)


def _mm3(a, b, dims=NN):
    ah, al = _split(a)
    bh, bl = _split(b)
    return _dg(ah, bh, dims) + (_dg(ah, bl, dims) + _dg(al, bh, dims))


def _mm_exact_lhs(a_bf16, b, dims=NN):
    bh, bl = _split(b)
    return _dg(a_bf16, bh, dims) + _dg(a_bf16, bl, dims)


def _silu(x):
    return x * (1.0 / (1.0 + jnp.exp(-x)))


def _sigmoid(x):
    return 1.0 / (1.0 + jnp.exp(-x))


def _softplus(x):
    return jnp.maximum(x, 0.0) + jnp.log1p(jnp.exp(-jnp.abs(x)))


def _log_sigmoid(x):
    return -_softplus(-x)


def _iota2(shape, dim):
    return lax.broadcasted_iota(jnp.int32, shape, dim)


def _rms(x):
    return x * lax.rsqrt(jnp.mean(x * x, axis=-1, keepdims=True) + EPS)


def _pack_rows(x):
    lo = lax.bitcast_convert_type(x[:, 0:D_PACK].astype(BF16).astype(F32), jnp.uint32)
    hi = lax.bitcast_convert_type(x[:, D_PACK:D_MODEL].astype(BF16).astype(F32), jnp.uint32)
    return (lo >> 16) | (hi & jnp.uint32(0xFFFF0000))


def _unpack_rows(p):
    lo = lax.bitcast_convert_type(p << 16, F32)
    hi = lax.bitcast_convert_type(p & jnp.uint32(0xFFFF0000), F32)
    return lo, hi


def _ada_kernel(c_ref, w_ref, b_ref, o_ref):
    o_ref[...] = _mm3(_silu(c_ref[...]), w_ref[...]) + b_ref[...]


def _ada(cond, w, b):
    n = w.shape[1]
    tn = 1536
    return pl.pallas_call(
        _ada_kernel,
        grid=(n // tn,),
        in_specs=[pl.BlockSpec((SUBLANES, D_MODEL), lambda i: (0, 0)),
                  pl.BlockSpec((D_MODEL, tn), lambda i: (0, i)),
                  pl.BlockSpec((1, tn), lambda i: (0, i))],
        out_specs=pl.BlockSpec((SUBLANES, tn), lambda i: (0, i)),
        out_shape=jax.ShapeDtypeStruct((SUBLANES, n), F32),
        compiler_params=_cparams(("parallel",)),
        name="ada",
    )(cond, w, b)


def _inproj_kernel(x_ref, mod_ref, nw_ref, w_ref, ws_ref, cw_ref, o_ref, *, tiles_prompt, prompt_row, sample_row):
    m = mod_ref[0]
    h = (_rms(x_ref[...]) * nw_ref[...]) * (1.0 + m[:, D_MODEL:2 * D_MODEL]) + m[:, 0:D_MODEL]
    hb = h.astype(BF16)
    for c0 in range(0, C_SMALL, 512):
        o_ref[:, c0:c0 + 512] = _dg(hb, w_ref[:, c0:c0 + 512], NN)
    o_ref[:, C_SMALL:D_PROJ] = _mm3(h, ws_ref[...])
    tm = x_ref.shape[0]
    row_len = jnp.where(pl.program_id(0) < tiles_prompt, prompt_row, sample_row)
    pos = _iota2((tm, LANES), 0) & (row_len - 1)
    for part in range(3 * H_B):
        cs = slice(C_QKVB + part * LANES, C_QKVB + (part + 1) * LANES)
        x = o_ref[:, cs]
        acc = jnp.zeros((tm, LANES), F32)
        for jj in range(CONV_K):
            off = jj - CONV_K // 2
            xs = x if off == 0 else pltpu.roll(x, (-off) % tm, 0)
            ok = jnp.logical_and(pos + off >= 0, pos + off < row_len)
            acc = acc + jnp.where(ok, xs, 0.0) * cw_ref[jj:jj + 1, part * LANES:(part + 1) * LANES]
        y = _silu(acc)
        if part < 2 * H_B:
            y = y * lax.rsqrt(jnp.sum(y * y, axis=-1, keepdims=True) + EPS)
            if part < H_B:
                y = y * (DK_B ** -0.5)
        o_ref[:, cs] = y


def _cond_row(i, tiles_prompt, tiles_per_seq):
    return jnp.where(i < tiles_prompt, 0, 1 + (i - tiles_prompt) // tiles_per_seq)


def _inproj(x_all, mod3, norm_w, w_main, w_small, conv_w, n_prompt_tok, sample_len, prompt_row, sample_row, tm):
    n_tok = x_all.shape[0]
    assert tm % prompt_row == 0 and tm % sample_row == 0
    cond = functools.partial(_cond_row, tiles_prompt=n_prompt_tok // tm, tiles_per_seq=sample_len // tm)
    kern = functools.partial(_inproj_kernel, tiles_prompt=n_prompt_tok // tm, prompt_row=prompt_row,
                             sample_row=sample_row)
    return pl.pallas_call(
        kern,
        grid=(n_tok // tm,),
        in_specs=[pl.BlockSpec((tm, D_MODEL), lambda i: (i, 0)),
                  pl.BlockSpec((1, 1, 6 * D_MODEL), lambda i: (cond(i), 0, 0)),
                  pl.BlockSpec((1, D_MODEL), lambda i: (0, 0)),
                  pl.BlockSpec((D_MODEL, C_SMALL), lambda i: (0, 0)),
                  pl.BlockSpec((D_MODEL, LANES), lambda i: (0, 0)),
                  pl.BlockSpec((SUBLANES, 3 * QB), lambda i: (0, 0))],
        out_specs=pl.BlockSpec((tm, D_PROJ), lambda i: (i, 0)),
        out_shape=jax.ShapeDtypeStruct((n_tok, D_PROJ), F32),
        compiler_params=_cparams(("parallel",)),
        name="inproj",
    )(x_all, mod3, norm_w, w_main, w_small, conv_w)


def _unit_ids(j, n_prompt_units, units_per_seq):
    jj = j - n_prompt_units
    b = jj // units_per_seq
    r = jj % units_per_seq
    is_prompt = j < n_prompt_units
    uf = j
    ub = jnp.where(is_prompt, j, n_prompt_units + b * units_per_seq + (units_per_seq - 1 - r))
    init_row = jnp.where(is_prompt, 0, 1 + b)
    first = jnp.logical_or(is_prompt, r == 0)
    return uf, ub, init_row, first


def _tri(rev):
    t = _iota2((CHUNK, CHUNK), 0)
    s = _iota2((CHUNK, CHUNK), 1)
    return (t <= s) if rev else (t >= s)


def _mm_exact_lhs_tn(a, ones_bf16):
    ah, al = _split(a)
    return _dg(ah, ones_bf16, TN) + _dg(al, ones_bf16, TN)


def _gla_unit(chunks, order, states, rev):
    tri = _tri(rev).astype(BF16)
    mid, last = (CHUNK // 2 - 1, 0) if rev else (CHUNK // 2, CHUNK - 1)
    scale = DK_A ** -0.5
    ones = jnp.ones((CHUNK, LANES), BF16)
    lane = _iota2((CHUNK, LANES), 1)
    row = _iota2((CHUNK, LANES), 0)
    s_in = lane % DK_A
    causal = (row <= s_in) if rev else (row >= s_in)
    zeros_v = jnp.zeros((CHUNK, DV_A), F32)
    pairs = range(H_A // 2)
    att, qs_l, u_l, dec_l = {}, {}, {}, {}
    for c, (q, k, v, la) in enumerate(chunks):
        b = _mm_exact_lhs(tri, la)
        bref = b[mid:mid + 1, :]
        blast = b[last:last + 1, :]
        qg = q * jnp.exp(b - bref) * scale
        kg = k * jnp.exp(bref - b)
        kd = k * jnp.exp(blast - b)
        qs_l[c] = q * jnp.exp(b) * scale
        for p in pairs:
            ls = slice(p * LANES, (p + 1) * LANES)
            kg_p = kg[:, ls]
            rhs_att = jnp.concatenate([jnp.where(lane < DK_A, kg_p, 0.0), jnp.where(lane >= DK_A, kg_p, 0.0)], axis=0)
            att[c, p] = jnp.where(causal, _mm(qg[:, ls], rhs_att, NT), 0.0)
            u_l[c, p] = _mm(kd[:, ls], v[:, 2 * p * DV_A:(2 * p + 2) * DV_A], TN)
            dec_l[c, p] = jnp.exp(_mm_exact_lhs_tn(la[:, ls], ones))
    start = {}
    for c in order:
        start[c] = list(states)
        nxt = []
        for p in pairs:
            u, dec = u_l[c, p], dec_l[c, p]
            nxt.append(dec[0:DK_A] * states[2 * p] + u[0:DK_A, 0:DV_A])
            nxt.append(dec[DK_A:2 * DK_A] * states[2 * p + 1] + u[DK_A:2 * DK_A, DV_A:2 * DV_A])
        states = nxt
    outs = []
    for c, (q, k, v, la) in enumerate(chunks):
        o = []
        for p in pairs:
            ls = slice(p * LANES, (p + 1) * LANES)
            vs0 = v[:, (2 * p) * DV_A:(2 * p + 1) * DV_A]
            vs1 = v[:, (2 * p + 1) * DV_A:(2 * p + 2) * DV_A]
            s0, s1 = start[c][2 * p], start[c][2 * p + 1]
            rhs_o = jnp.concatenate([jnp.concatenate([vs0, zeros_v], axis=1),
                                     jnp.concatenate([zeros_v, vs1], axis=1),
                                     jnp.concatenate([s0, zeros_v], axis=1),
                                     jnp.concatenate([zeros_v, s1], axis=1)], axis=0)
            lhs_o = jnp.concatenate([att[c, p], qs_l[c][:, ls]], axis=1)
            o.append(_mm(lhs_o, rhs_o))
        outs.append(jnp.concatenate(o, axis=1))
    return outs, states


def _gla_kernel(qf_ref, qb_ref, sf_ref, sb_ref, wlr_ref, blr_ref, init_ref, of_ref, ob_ref, so_ref, s_ref,
                *, n_prompt_units, units_per_seq):
    j = pl.program_id(0)
    _, _, _, first = _unit_ids(j, n_prompt_units, units_per_seq)

    @pl.when(first)
    def _():
        s_ref[...] = init_ref[0]

    for d in range(N_DIR):
        x_ref, sm_ref, o_ref = (qf_ref, sf_ref, of_ref) if d == 0 else (qb_ref, sb_ref, ob_ref)
        la_unit = _log_sigmoid(_mm3(sm_ref[...], wlr_ref[d]) + blr_ref[d]) * (1.0 / GLA_NORMALIZER)
        states = [s_ref[d, h] for h in range(H_A)]
        order = range(CHUNKS_PER_UNIT - 1, -1, -1) if d == 1 else range(CHUNKS_PER_UNIT)
        chunks = []
        for c in range(CHUNKS_PER_UNIT):
            rows = slice(c * CHUNK, (c + 1) * CHUNK)
            chunks.append((x_ref[rows, 0:QA], x_ref[rows, QA:2 * QA], x_ref[rows, 2 * QA:2 * QA + VA], la_unit[rows]))
        outs, states = _gla_unit(chunks, order, states, rev=(d == 1))
        for c in range(CHUNKS_PER_UNIT):
            o_ref[c * CHUNK:(c + 1) * CHUNK, :] = outs[c]
        for h in range(H_A):
            s_ref[d, h] = states[h]
    so_ref[0] = s_ref[...]


def _gla(proj, wlr, blr, init, n_prompt_units, units_per_seq):
    n_tok = proj.shape[0]
    n_units = n_tok // UNIT
    ids = functools.partial(_unit_ids, n_prompt_units=n_prompt_units, units_per_seq=units_per_seq)
    small_blk = C_SMALL // LANES
    st_blk = (1, N_DIR, H_A, DK_A, DV_A)
    kern = functools.partial(_gla_kernel, n_prompt_units=n_prompt_units, units_per_seq=units_per_seq)
    return pl.pallas_call(
        kern,
        grid=(n_units,),
        in_specs=[pl.BlockSpec((UNIT, 1024), lambda j: (ids(j)[0], 0)),
                  pl.BlockSpec((UNIT, 1024), lambda j: (ids(j)[1], 0)),
                  pl.BlockSpec((UNIT, LANES), lambda j: (ids(j)[0], small_blk)),
                  pl.BlockSpec((UNIT, LANES), lambda j: (ids(j)[1], small_blk)),
                  pl.BlockSpec((N_DIR, LANES, QA), lambda j: (0, 0, 0)),
                  pl.BlockSpec((N_DIR, 1, QA), lambda j: (0, 0, 0)),
                  pl.BlockSpec(st_blk, lambda j: (ids(j)[2], 0, 0, 0, 0))],
        out_specs=[pl.BlockSpec((UNIT, VA), lambda j: (ids(j)[0], 0)),
                   pl.BlockSpec((UNIT, VA), lambda j: (ids(j)[1], 0)),
                   pl.BlockSpec(st_blk, lambda j: (jnp.minimum(j, n_prompt_units), 0, 0, 0, 0))],
        out_shape=[jax.ShapeDtypeStruct((n_tok, VA), F32),
                   jax.ShapeDtypeStruct((n_tok, VA), F32),
                   jax.ShapeDtypeStruct((n_prompt_units + 1, N_DIR, H_A, DK_A, DV_A), F32)],
        scratch_shapes=[pltpu.VMEM((N_DIR, H_A, DK_A, DV_A), F32)],
        compiler_params=_cparams(("arbitrary",)),
        name="gla",
    )(proj, proj, proj, proj, wlr, blr, init)


def _stack_masks(rev):
    r = _iota2((HC, HC), 0)
    c = _iota2((HC, HC), 1)
    same = (r // CHUNK) == (c // CHUNK)
    tr, tc = r % CHUNK, c % CHUNK
    incl = jnp.logical_and(same, (tr <= tc) if rev else (tr >= tc))
    strict = jnp.logical_and(same, (tr < tc) if rev else (tr > tc))
    return incl, strict


def _spread(x):
    z = jnp.zeros((CHUNK, LANES), x.dtype)
    rows = []
    for h in range(H_B):
        xh = x[h * CHUNK:(h + 1) * CHUNK]
        rows.append(jnp.concatenate([xh if g == h else z for g in range(H_B)], axis=1))
    return jnp.concatenate(rows, axis=0)


def _gdn_prepare(chunks):
    masks = {rev: _stack_masks(rev) for rev in {ch[6] for ch in chunks}}
    r = _iota2((HC, HC), 0)
    c = _iota2((HC, HC), 1)
    a_l, att_l, kb_l = [], [], []
    for q, k, v, gcb, beta, glast, rev in chunks:
        incl, strict = masks[rev]
        grow = gcb.T[0:1, :]
        diff = gcb[:, 0:1] - grow
        decay = jnp.where(incl, jnp.exp(jnp.where(incl, diff, 0.0)), 0.0)
        kb = k * beta
        m1 = _mm(jnp.concatenate([kb, q], axis=0), k, NT)
        a_l.append(jnp.where(strict, m1[0:HC] * decay, 0.0))
        att_l.append(m1[HC:2 * HC] * decay)
        kb_l.append(kb)
    diag = (r // INV_BLOCK) == (c // INV_BLOCK)
    pw_l = [jnp.where(diag, a, 0.0) for a in a_l]
    qm_l = [-p for p in pw_l]
    n = 2
    while n < INV_BLOCK:
        pw_l = [_mm(p, p) for p in pw_l]
        qm_l = [qm + p + _mm(qm, p) for qm, p in zip(qm_l, pw_l)]
        n *= 2
    b = INV_BLOCK
    while b < CHUNK:
        off = jnp.logical_and((r // (2 * b)) == (c // (2 * b)), (r // b) != (c // b))
        al_l = [jnp.where(off, a, 0.0) for a in a_l]
        t1_l = [al + _mm(qm, al) for qm, al in zip(qm_l, al_l)]
        qm_l = [qm - (t1 + _mm(t1, qm)) for qm, t1 in zip(qm_l, t1_l)]
        b *= 2
    out = []
    for (q, k, v, gcb, beta, glast, _), qm, kb, att in zip(chunks, qm_l, kb_l, att_l):
        egc = jnp.exp(gcb)
        rhs = jnp.concatenate([v * beta, kb * egc], axis=1)
        sol = rhs + _mm(qm, rhs)
        out.append((sol[:, 0:DV_B], sol[:, DV_B:2 * DV_B], q * egc, k * jnp.exp(glast - gcb), att))
    return out


def _gdn_scan_step(value, k_cum, q_dec, k_dec, att, gl_rows, s):
    kq = _mm(jnp.concatenate([_spread(k_cum), _spread(q_dec)], axis=0), s)
    v_new = value - kq[0:HC]
    o = kq[HC:2 * HC] + _mm(att, v_new)
    s_new = s * gl_rows + _mm(_spread(k_dec), v_new, TN)
    return o, s_new


def _gdn_kernel(xf_ref, xb_ref, sf_ref, sb_ref, gco_ref, gdt_ref, init_ref, of_ref, ob_ref, so_ref, s_ref,
                *, n_prompt_units, units_per_seq):
    j = pl.program_id(0)
    _, _, _, first = _unit_ids(j, n_prompt_units, units_per_seq)

    @pl.when(first)
    def _():
        s_ref[...] = init_ref[0]

    chunks, gl_rows = [], []
    for d in range(N_DIR):
        x_ref, sm_ref = (xf_ref, sf_ref) if d == 0 else (xb_ref, sb_ref)
        rev = d == 1
        sm = sm_ref[...]
        g_all = gco_ref[...] * _softplus(sm + gdt_ref[...])
        beta_all = _sigmoid(sm)
        tri = _tri(rev).astype(BF16)
        last = 0 if rev else CHUNK - 1
        for c in range(CHUNKS_PER_UNIT):
            rows = slice(c * CHUNK, (c + 1) * CHUNK)
            gc_all = _mm_exact_lhs(tri, g_all[rows])
            gcb, beta, glast = [], [], []
            for h in range(H_B):
                col = SM_AB + d * H_B + h
                colb = SM_BETA + d * H_B + h
                gh = jnp.broadcast_to(gc_all[:, col:col + 1], (CHUNK, LANES))
                gcb.append(gh)
                glast.append(jnp.broadcast_to(gh[last:last + 1, :], (CHUNK, LANES)))
                beta.append(jnp.broadcast_to(beta_all[rows, colb:colb + 1], (CHUNK, LANES)))
            stack = lambda base: jnp.concatenate(
                [x_ref[rows, base + h * LANES:base + (h + 1) * LANES] for h in range(H_B)], axis=0)
            chunks.append((stack(0), stack(QB), stack(2 * QB), jnp.concatenate(gcb, axis=0),
                           jnp.concatenate(beta, axis=0), jnp.concatenate(glast, axis=0), rev))
            gl_rows.append(jnp.concatenate([jnp.broadcast_to(jnp.exp(g[0:1, :]), (DK_B, DV_B)) for g in glast],
                                           axis=0))
    prepared = _gdn_prepare(chunks)
    s = [s_ref[d] for d in range(N_DIR)]
    for step in range(CHUNKS_PER_UNIT):
        for d, o_ref in ((0, of_ref), (1, ob_ref)):
            c = step if d == 0 else CHUNKS_PER_UNIT - 1 - step
            o, s[d] = _gdn_scan_step(*prepared[d * CHUNKS_PER_UNIT + c], gl_rows[d * CHUNKS_PER_UNIT + c], s[d])
            for h in range(H_B):
                o_ref[c * CHUNK:(c + 1) * CHUNK, h * DV_B:(h + 1) * DV_B] = o[h * CHUNK:(h + 1) * CHUNK]
    for d in range(N_DIR):
        s_ref[d] = s[d]
    so_ref[0] = s_ref[...]


def _gdn(proj, gcoef, gdt, init, n_prompt_units, units_per_seq):
    n_tok = proj.shape[0]
    n_units = n_tok // UNIT
    ids = functools.partial(_unit_ids, n_prompt_units=n_prompt_units, units_per_seq=units_per_seq)
    small_blk = C_SMALL // LANES
    qkv_blk = C_QKVB // (3 * QB)
    st_blk = (1, N_DIR, H_B * DK_B, DV_B)
    kern = functools.partial(_gdn_kernel, n_prompt_units=n_prompt_units, units_per_seq=units_per_seq)
    return pl.pallas_call(
        kern,
        grid=(n_units,),
        in_specs=[pl.BlockSpec((UNIT, 3 * QB), lambda j: (ids(j)[0], qkv_blk)),
                  pl.BlockSpec((UNIT, 3 * QB), lambda j: (ids(j)[1], qkv_blk)),
                  pl.BlockSpec((UNIT, LANES), lambda j: (ids(j)[0], small_blk)),
                  pl.BlockSpec((UNIT, LANES), lambda j: (ids(j)[1], small_blk)),
                  pl.BlockSpec((1, LANES), lambda j: (0, 0)),
                  pl.BlockSpec((1, LANES), lambda j: (0, 0)),
                  pl.BlockSpec(st_blk, lambda j: (ids(j)[2], 0, 0, 0))],
        out_specs=[pl.BlockSpec((UNIT, VB), lambda j: (ids(j)[0], 0)),
                   pl.BlockSpec((UNIT, VB), lambda j: (ids(j)[1], 0)),
                   pl.BlockSpec(st_blk, lambda j: (jnp.minimum(j, n_prompt_units), 0, 0, 0))],
        out_shape=[jax.ShapeDtypeStruct((n_tok, VB), F32),
                   jax.ShapeDtypeStruct((n_tok, VB), F32),
                   jax.ShapeDtypeStruct((n_prompt_units + 1, N_DIR, H_B * DK_B, DV_B), F32)],
        scratch_shapes=[pltpu.VMEM((N_DIR, H_B * DK_B, DV_B), F32)],
        compiler_params=_cparams(("arbitrary",)),
        name="gdn",
    )(proj, proj, proj, proj, gcoef, gdt, init)


def _head_rms(o, w):
    parts = []
    for h in range(o.shape[1] // LANES):
        parts.append(_rms(o[:, h * LANES:(h + 1) * LANES]))
    return jnp.concatenate(parts, axis=1) * w


def _post_kernel(x_ref, ga_ref, zb_ref, af_ref, ab_ref, df_ref, db_ref, mod_ref, wo_ref, nwa_ref, nwb_ref, n2_ref,
                 rw_ref, x1_ref, h2_ref, sc_ref):
    m = mod_ref[0]
    gla = _head_rms(af_ref[...] + ab_ref[...], nwa_ref[...]) * _silu(ga_ref[...])
    gdn = _head_rms(df_ref[...] + db_ref[...], nwb_ref[...]) * _silu(zb_ref[...])
    y = _dg(gla.astype(BF16), wo_ref[0:VA, :], NN) + _dg(gdn.astype(BF16), wo_ref[VA:VA + VB, :], NN)
    x1 = x_ref[...] + m[:, 2 * D_MODEL:3 * D_MODEL] * y
    x1_ref[...] = x1
    h2 = (_rms(x1) * n2_ref[...]) * (1.0 + m[:, 4 * D_MODEL:5 * D_MODEL]) + m[:, 3 * D_MODEL:4 * D_MODEL]
    h2_ref[...] = _pack_rows(h2)
    sc_ref[...] = _sigmoid(_mm3(rw_ref[...], h2, NT))


def _post(x_all, proj, o_af, o_ab, o_df, o_db, mod3, w_o, nwa, nwb, n2w, router_wt, n_prompt_tok, sample_len, tm):
    n_tok = x_all.shape[0]
    cond = functools.partial(_cond_row, tiles_prompt=n_prompt_tok // tm, tiles_per_seq=sample_len // tm)
    tok = lambda i: (i, 0)
    const = lambda i: (0, 0)
    return pl.pallas_call(
        _post_kernel,
        grid=(n_tok // tm,),
        in_specs=[pl.BlockSpec((tm, D_MODEL), tok),
                  pl.BlockSpec((tm, VA), lambda i: (i, C_GA // VA)),
                  pl.BlockSpec((tm, VB), lambda i: (i, C_ZB // VB)),
                  pl.BlockSpec((tm, VA), tok), pl.BlockSpec((tm, VA), tok),
                  pl.BlockSpec((tm, VB), tok), pl.BlockSpec((tm, VB), tok),
                  pl.BlockSpec((1, 1, 6 * D_MODEL), lambda i: (cond(i), 0, 0)),
                  pl.BlockSpec((VA + VB, D_MODEL), const),
                  pl.BlockSpec((1, VA), const), pl.BlockSpec((1, VB), const), pl.BlockSpec((1, D_MODEL), const),
                  pl.BlockSpec((N_EXPERTS, D_MODEL), const)],
        out_specs=[pl.BlockSpec((tm, D_MODEL), tok),
                   pl.BlockSpec((tm, D_PACK), tok),
                   pl.BlockSpec((N_EXPERTS, tm), lambda i: (0, i))],
        out_shape=[jax.ShapeDtypeStruct((n_tok, D_MODEL), F32),
                   jax.ShapeDtypeStruct((n_tok, D_PACK), jnp.uint32),
                   jax.ShapeDtypeStruct((N_EXPERTS, n_tok), F32)],
        compiler_params=_cparams(("parallel",)),
        name="post",
    )(x_all, proj, proj, o_af, o_ab, o_df, o_db, mod3, w_o, nwa, nwb, n2w, router_wt)


def _route_kernel(sc_ref, bias_ref, e_ref, r_ref, w_ref, cnt_ref, carry_ref):
    i = pl.program_id(0)
    t = sc_ref.shape[1]

    @pl.when(i == 0)
    def _():
        carry_ref[...] = jnp.zeros(carry_ref.shape, F32)

    s = sc_ref[...]
    biased = s + bias_ref[:, 0:1]
    gs = []
    for g in range(N_GROUP):
        blk = biased[g * GROUP_SIZE:(g + 1) * GROUP_SIZE]
        m1 = jnp.max(blk, axis=0, keepdims=True)
        n1 = jnp.sum((blk == m1).astype(F32), axis=0, keepdims=True)
        m2 = jnp.max(jnp.where(blk < m1, blk, NEG_INF), axis=0, keepdims=True)
        gs.append(m1 + jnp.where(n1 >= 2.0, m1, m2))
    gsc = jnp.concatenate(gs, axis=0)
    gid = _iota2((N_GROUP, t), 0)
    beaten = jnp.zeros((N_GROUP, t), F32)
    for g in range(N_GROUP):
        other = gsc[g:g + 1, :]
        wins = jnp.logical_or(other > gsc, jnp.logical_and(other == gsc, g < gid))
        beaten = beaten + wins.astype(F32)
    masked = jnp.concatenate(
        [jnp.where(beaten[g:g + 1, :] < float(TOPK_GROUP), biased[g * GROUP_SIZE:(g + 1) * GROUP_SIZE], NEG_INF)
         for g in range(N_GROUP)], axis=0)
    eid = _iota2((N_EXPERTS, t), 0).astype(F32)
    sel = jnp.zeros((N_EXPERTS, t), F32)
    picks, scores = [], []
    for _ in range(TOP_K):
        m = jnp.max(masked, axis=0, keepdims=True)
        first = jnp.min(jnp.where(masked == m, eid, float(N_EXPERTS)), axis=0, keepdims=True)
        hit = eid == first
        scores.append(jnp.sum(jnp.where(hit, s, 0.0), axis=0, keepdims=True))
        masked = jnp.where(hit, NEG_INF, masked)
        sel = sel + hit.astype(F32)
        picks.append(first)
    upper = (_iota2((t, t), 0) < _iota2((t, t), 1)).astype(BF16)
    carry = carry_ref[...]
    prefix = _dg(sel.astype(BF16), upper, NN) + jnp.concatenate([carry] * (t // LANES), axis=1)
    ranks = [jnp.sum(jnp.where(eid == p, prefix, 0.0), axis=0, keepdims=True) for p in picks]
    carry = carry + _dg(sel.astype(BF16), jnp.ones((t, LANES), BF16), NN)
    carry_ref[...] = carry
    cnt_ref[...] = carry
    sc8 = jnp.concatenate(scores, axis=0)
    e_ref[...] = jnp.concatenate(picks, axis=0).astype(jnp.int32)
    r_ref[...] = jnp.concatenate(ranks, axis=0).astype(jnp.int32)
    w_ref[...] = sc8 / jnp.sum(sc8, axis=0, keepdims=True) * ROUTED_SCALE


def _route(scores_t, bias_col):
    n_tok = scores_t.shape[1]
    t = ROUTE_TILE
    slot = lambda i: (0, i)
    return pl.pallas_call(
        _route_kernel,
        grid=(n_tok // t,),
        in_specs=[pl.BlockSpec((N_EXPERTS, t), slot),
                  pl.BlockSpec((N_EXPERTS, LANES), lambda i: (0, 0))],
        out_specs=[pl.BlockSpec((TOP_K, t), slot), pl.BlockSpec((TOP_K, t), slot), pl.BlockSpec((TOP_K, t), slot),
                   pl.BlockSpec((N_EXPERTS, LANES), lambda i: (0, 0))],
        out_shape=[jax.ShapeDtypeStruct((TOP_K, n_tok), jnp.int32),
                   jax.ShapeDtypeStruct((TOP_K, n_tok), jnp.int32),
                   jax.ShapeDtypeStruct((TOP_K, n_tok), F32),
                   jax.ShapeDtypeStruct((N_EXPERTS, LANES), F32)],
        scratch_shapes=[pltpu.VMEM((N_EXPERTS, LANES), F32)],
        compiler_params=_cparams(("arbitrary",)),
        name="route",
    )(scores_t, bias_col)


def _dest_kernel(start_ref, e_ref, r_ref, d_ref):
    e = e_ref[...]

    def body(x, acc):
        return jnp.where(e == x, start_ref[x], acc)

    d_ref[...] = r_ref[...] + lax.fori_loop(0, N_EXPERTS, body, jnp.zeros(e.shape, jnp.int32))


def _dest(start, e8, r8, tt):
    n_tok = e8.shape[1]
    slot = lambda i, st: (0, i)
    return pl.pallas_call(
        _dest_kernel,
        grid_spec=pltpu.PrefetchScalarGridSpec(
            num_scalar_prefetch=1, grid=(n_tok // tt,),
            in_specs=[pl.BlockSpec((TOP_K, tt), slot), pl.BlockSpec((TOP_K, tt), slot)],
            out_specs=pl.BlockSpec((TOP_K, tt), slot)),
        out_shape=jax.ShapeDtypeStruct((TOP_K, n_tok), jnp.int32),
        compiler_params=_cparams(("parallel",)),
        name="dest",
    )(start, e8, r8)


def _items(counts, n_rows):
    n_blocks = n_rows // ROW_BLOCK
    max_items = n_blocks + N_EXPERTS - 1
    end = jnp.cumsum(counts)
    start = end - counts
    first_blk = start // ROW_BLOCK
    n_it = jnp.where(counts > 0, (end - 1) // ROW_BLOCK - first_blk + 1, 0)
    it_end = jnp.cumsum(n_it)
    it_start = it_end - n_it
    i = jnp.arange(max_items, dtype=jnp.int32)
    valid = i < it_end[-1]
    ex = jnp.minimum(jnp.sum((it_end[None, :] <= i[:, None]).astype(jnp.int32), axis=1), N_EXPERTS - 1)
    onehot = ex[:, None] == jnp.arange(N_EXPERTS, dtype=jnp.int32)[None, :]
    pick = lambda tab: jnp.sum(jnp.where(onehot, tab[None, :], 0), axis=1)
    blk = pick(first_blk) + (i - pick(it_start))
    lo = jnp.maximum(pick(start), blk * ROW_BLOCK) - blk * ROW_BLOCK
    hi = jnp.minimum(pick(end), (blk + 1) * ROW_BLOCK) - blk * ROW_BLOCK
    blk = jnp.where(valid, blk, n_blocks - 1).astype(jnp.int32)
    lo = jnp.where(valid, lo, 0).astype(jnp.int32)
    hi = jnp.where(valid, hi, 0).astype(jnp.int32)
    eids = jnp.arange(N_EXPERTS, dtype=jnp.int32)
    later = jnp.logical_and(eids[None, :] > eids[:, None], (counts > 0)[None, :])
    nxt_e = jnp.min(jnp.where(later, eids[None, :], N_EXPERTS), axis=1)
    nxt = pick(jnp.where(nxt_e < N_EXPERTS, nxt_e, -1)).astype(jnp.int32)
    return start.astype(jnp.int32), blk, ex.astype(jnp.int32), nxt, lo, hi


def _dispatch_kernel(h2_ref, dest_hbm, xs_hbm, dsm, sem_d, sem_s):
    i = pl.program_id(0)
    n = pl.num_programs(0)
    tm = h2_ref.shape[0]

    def dest_copy(step, slot):
        return pltpu.make_async_copy(dest_hbm.at[:, pl.ds(step * tm, tm)], dsm.at[slot], sem_d.at[slot])

    def row_copy(t, dst):
        return pltpu.make_async_copy(h2_ref.at[pl.ds(t, 1), :], xs_hbm.at[pl.ds(dst, 1), :], sem_s.at[0])

    @pl.when(i == 0)
    def _():
        dest_copy(0, 0).start()

    slot = i % 2
    dest_copy(i, slot).wait()

    @pl.when(i + 1 < n)
    def _():
        dest_copy(i + 1, 1 - slot).start()

    def issue(t, carry):
        for k in range(TOP_K):
            row_copy(t, dsm[slot, k, t]).start(priority=k % N_DMA_QUEUES)
        return carry
    lax.fori_loop(0, tm, issue, 0, unroll=2)

    def drain(t, carry):
        for k in range(TOP_K):
            row_copy(t, 0).wait()
        return carry
    lax.fori_loop(0, tm, drain, 0)


def _dispatch(h2, dest, tm):
    n_tok = h2.shape[0]
    return pl.pallas_call(
        _dispatch_kernel,
        grid=(n_tok // tm,),
        in_specs=[pl.BlockSpec((tm, D_PACK), lambda i: (i, 0)),
                  pl.BlockSpec(memory_space=pl.ANY)],
        out_specs=pl.BlockSpec(memory_space=pl.ANY),
        out_shape=jax.ShapeDtypeStruct((n_tok * TOP_K, D_PACK), jnp.uint32),
        scratch_shapes=[pltpu.SMEM((2, TOP_K, tm), jnp.int32),
                        pltpu.SemaphoreType.DMA((2,)),
                        pltpu.SemaphoreType.DMA((1,))],
        compiler_params=_cparams(("arbitrary",)),
        name="dispatch",
    )(h2, dest)


def _experts_kernel(blk_ref, exp_ref, nxt_ref, lo_ref, hi_ref, x_ref, wg_hbm, wu_hbm, wd_hbm, y_ref,
                    wgf, wuf, wdf, wgb, wub, wdb, nchg, sem_w):
    i = pl.program_id(0)
    lo, hi = lo_ref[i], hi_ref[i]

    def weight_copies(e, slot):
        return (pltpu.make_async_copy(wg_hbm.at[e], wgf.at[slot], sem_w.at[slot]),
                pltpu.make_async_copy(wu_hbm.at[e], wuf.at[slot], sem_w.at[slot]),
                pltpu.make_async_copy(wd_hbm.at[e], wdf.at[slot], sem_w.at[slot]))

    @pl.when(i == 0)
    def _():
        nchg[0] = 0
        for cp in weight_copies(exp_ref[0], 0):
            cp.start()

    @pl.when(hi > lo)
    def _():
        @pl.when(jnp.logical_or(i == 0, exp_ref[i] != exp_ref[jnp.maximum(i - 1, 0)]))
        def _():
            slot = nchg[0] % 2
            for cp in weight_copies(exp_ref[i], slot):
                cp.wait()

            @pl.when(nxt_ref[i] >= 0)
            def _():
                for cp in weight_copies(nxt_ref[i], 1 - slot):
                    cp.start()

            wgb[...] = wgf[slot].astype(BF16)
            wub[...] = wuf[slot].astype(BF16)
            wdb[...] = wdf[slot].astype(BF16)
            nchg[0] = nchg[0] + 1

        x_lo, x_hi = _unpack_rows(x_ref[...])
        x = jnp.concatenate([x_lo.astype(BF16), x_hi.astype(BF16)], axis=1)
        g = _dg(x, wgb[...], NN)
        u = _dg(x, wub[...], NN)
        hmid = (_silu(g) * u).astype(BF16)
        y = _pack_rows(_dg(hmid, wdb[...], NN))
        row = _iota2((ROW_BLOCK, D_PACK), 0)
        mine = jnp.logical_and(row >= lo, row < hi)

        @pl.when(lo == 0)
        def _():
            y_ref[...] = jnp.where(mine, y, jnp.uint32(0))

        @pl.when(lo > 0)
        def _():
            y_ref[...] = jnp.where(mine, y, y_ref[...])


def _experts(item_blk, item_exp, item_nxt, item_lo, item_hi, xs, w_gate, w_up, w_down):
    n_items = item_blk.shape[0]
    xmap = lambda i, blk, ex, nxt, lo, hi: (blk[i], 0)
    return pl.pallas_call(
        _experts_kernel,
        grid_spec=pltpu.PrefetchScalarGridSpec(
            num_scalar_prefetch=5, grid=(n_items,),
            in_specs=[pl.BlockSpec((ROW_BLOCK, D_PACK), xmap),
                      pl.BlockSpec(memory_space=pl.ANY),
                      pl.BlockSpec(memory_space=pl.ANY),
                      pl.BlockSpec(memory_space=pl.ANY)],
            out_specs=pl.BlockSpec((ROW_BLOCK, D_PACK), xmap),
            scratch_shapes=[pltpu.VMEM((2, D_MODEL, D_EXPERT), F32), pltpu.VMEM((2, D_MODEL, D_EXPERT), F32),
                            pltpu.VMEM((2, D_EXPERT, D_MODEL), F32),
                            pltpu.VMEM((D_MODEL, D_EXPERT), BF16), pltpu.VMEM((D_MODEL, D_EXPERT), BF16),
                            pltpu.VMEM((D_EXPERT, D_MODEL), BF16),
                            pltpu.SMEM((1,), jnp.int32),
                            pltpu.SemaphoreType.DMA((2,))]),
        out_shape=jax.ShapeDtypeStruct(xs.shape, jnp.uint32),
        compiler_params=_cparams(("arbitrary",)),
        name="experts",
    )(item_blk, item_exp, item_nxt, item_lo, item_hi, xs, w_gate, w_up, w_down)


def _final_kernel(x1_ref, h2_ref, w_ref, mod_ref, sg_ref, su_ref, sd_ref, fn_ref, dest_hbm, ys_hbm, o_ref,
                  dsm, gbuf, sem_d, sem_g):
    i = pl.program_id(0)
    n = pl.num_programs(0)
    tm = x1_ref.shape[0]

    def dest_copy(step, slot):
        return pltpu.make_async_copy(dest_hbm.at[:, pl.ds(step * tm, tm)], dsm.at[slot], sem_d.at[slot])

    def row_copy(src, slot, k, t):
        return pltpu.make_async_copy(ys_hbm.at[pl.ds(src, 1), :], gbuf.at[slot, k, pl.ds(t, 1), :], sem_g.at[slot])

    def issue_gathers(slot3, slot2):
        def body(t, carry):
            for k in range(TOP_K):
                row_copy(dsm[slot3, k, t], slot2, k, t).start(priority=k % N_DMA_QUEUES)
            return carry
        lax.fori_loop(0, tm, body, 0, unroll=2)

    @pl.when(i == 0)
    def _():
        dest_copy(0, 0).start()
        dest_copy(0, 0).wait()
        issue_gathers(0, 0)

        @pl.when(n > 1)
        def _():
            dest_copy(1, 1).start()

    @pl.when(i + 1 < n)
    def _():
        dest_copy(i + 1, (i + 1) % 3).wait()
        issue_gathers((i + 1) % 3, (i + 1) % 2)

        @pl.when(i + 2 < n)
        def _():
            dest_copy(i + 2, (i + 2) % 3).start()

    slot = i % 2

    def drain(t, carry):
        for k in range(TOP_K):
            row_copy(0, slot, k, t).wait()
        return carry
    lax.fori_loop(0, tm, drain, 0)

    m = mod_ref[0]
    w = w_ref[...]
    r_lo = r_hi = None
    for kk in range(TOP_K):
        y_lo, y_hi = _unpack_rows(gbuf[slot, kk])
        wk = w[:, kk:kk + 1]
        r_lo = y_lo * wk if r_lo is None else r_lo + y_lo * wk
        r_hi = y_hi * wk if r_hi is None else r_hi + y_hi * wk
    routed = jnp.concatenate([r_lo, r_hi], axis=1)
    h_lo, h_hi = _unpack_rows(h2_ref[...])
    hb = jnp.concatenate([h_lo.astype(BF16), h_hi.astype(BF16)], axis=1)
    hm = (_silu(_dg(hb, sg_ref[...], NN)) * _dg(hb, su_ref[...], NN)).astype(BF16)
    shared = _dg(hm, sd_ref[...], NN)
    x2 = x1_ref[...] + m[:, 5 * D_MODEL:6 * D_MODEL] * (routed + shared)
    o_ref[...] = _rms(x2) * fn_ref[...]


def _final(x1, h2, wts, mod3, sg, su, sd, fnw, dest, ys, n_prompt_tok, sample_len, tm):
    n_tok = x1.shape[0]
    cond = functools.partial(_cond_row, tiles_prompt=n_prompt_tok // tm, tiles_per_seq=sample_len // tm)
    tok = lambda i: (i, 0)
    const = lambda i: (0, 0)
    return pl.pallas_call(
        _final_kernel,
        grid=(n_tok // tm,),
        in_specs=[pl.BlockSpec((tm, D_MODEL), tok),
                  pl.BlockSpec((tm, D_PACK), tok),
                  pl.BlockSpec((tm, TOP_K), tok),
                  pl.BlockSpec((1, 1, 6 * D_MODEL), lambda i: (cond(i), 0, 0)),
                  pl.BlockSpec((D_MODEL, D_EXPERT), const),
                  pl.BlockSpec((D_MODEL, D_EXPERT), const),
                  pl.BlockSpec((D_EXPERT, D_MODEL), const),
                  pl.BlockSpec((1, D_MODEL), const),
                  pl.BlockSpec(memory_space=pl.ANY),
                  pl.BlockSpec(memory_space=pl.ANY)],
        out_specs=pl.BlockSpec((tm, D_MODEL), tok),
        out_shape=jax.ShapeDtypeStruct((n_tok, D_MODEL), F32),
        scratch_shapes=[pltpu.SMEM((3, TOP_K, tm), jnp.int32),
                        pltpu.VMEM((2, TOP_K, tm, D_PACK), jnp.uint32),
                        pltpu.SemaphoreType.DMA((3,)),
                        pltpu.SemaphoreType.DMA((2,))],
        compiler_params=_cparams(("arbitrary",)),
        name="final",
    )(x1, h2, wts, mod3, sg, su, sd, fnw, dest, ys)


def kernel(x_prompt, x_sample, state_gla, state_gdn, c, c_ctx, w_ada, b_ada, norm1_w, w_in, conv_w, gla_lr_w, gla_lr_b, gdn_a_log, gdn_dt_bias, gla_norm_w, gdn_norm_w, w_o, norm2_w, router_w, router_bias, exp_w_gate, exp_w_up, exp_w_down, sh_w_gate, sh_w_up, sh_w_down, final_norm_w):
    bp, lp, d = x_prompt.shape
    bs, ls, _ = x_sample.shape
    assert d == D_MODEL and lp == UNIT and ls % UNIT == 0 and w_ada.shape[0] == 1
    n_prompt_tok = bp * lp
    n_tok = n_prompt_tok + bs * ls
    n_prompt_units = n_prompt_tok // UNIT
    units_per_seq = ls // UNIT
    grid_w = 64
    layer = 0
    tm_in = 512
    tm = 256
    assert n_prompt_tok % tm_in == 0 and ls % tm_in == 0 and n_tok % ROUTE_TILE == 0

    x_all = jnp.concatenate([x_prompt.reshape(n_prompt_tok, d), x_sample.reshape(bs * ls, d)], axis=0)
    cond = jnp.concatenate([c_ctx[None, :], c, jnp.zeros((SUBLANES - 1 - bs, d), F32)], axis=0)
    mod3 = _ada(cond, w_ada[layer], b_ada[layer][None, :]).reshape(SUBLANES, 1, 6 * d)

    wi = w_in[layer]
    o_lr = 2 * QA + 2 * VA
    o_qkvb = o_lr + N_DIR * GLA_RANK
    o_zb = o_qkvb + 3 * QB
    o_ab = o_zb + VB
    w_main = jnp.concatenate([wi[:, 0:o_lr], wi[:, o_qkvb:o_ab]], axis=1).astype(BF16)
    w_small = jnp.concatenate([wi[:, o_lr:o_qkvb], wi[:, o_ab:], jnp.zeros((d, LANES - 48), F32)], axis=1)
    cw = jnp.concatenate([conv_w[layer], jnp.zeros((SUBLANES - CONV_K, 3 * QB), F32)], axis=0)
    proj = _inproj(x_all, mod3, norm1_w[layer][None, :], w_main, w_small, cw, n_prompt_tok, ls, lp, grid_w, tm_in)

    wlr = jnp.zeros((N_DIR, LANES, QA), F32)
    for dd in range(N_DIR):
        wlr = wlr.at[dd, dd * GLA_RANK:(dd + 1) * GLA_RANK, :].set(gla_lr_w[layer, dd])
    blr = gla_lr_b[layer][:, None, :]
    init_gla = jnp.concatenate([jnp.zeros((1,) + state_gla.shape[2:], F32), state_gla[:, layer].astype(F32)], axis=0)
    o_af, o_ab_, s_gla = _gla(proj, wlr, blr, init_gla, n_prompt_units, units_per_seq)

    gcoef = jnp.zeros((1, LANES), F32).at[0, SM_AB:SM_AB + N_DIR * H_B].set(-jnp.exp(gdn_a_log[layer].reshape(-1)))
    gdt = jnp.zeros((1, LANES), F32).at[0, SM_AB:SM_AB + N_DIR * H_B].set(gdn_dt_bias[layer].reshape(-1))
    init_gdn = jnp.concatenate([jnp.zeros((1, N_DIR, H_B * DK_B, DV_B), F32),
                                state_gdn[:, layer].astype(F32).reshape(bs, N_DIR, H_B * DK_B, DV_B)], axis=0)
    o_df, o_db, s_gdn = _gdn(proj, gcoef, gdt, init_gdn, n_prompt_units, units_per_seq)

    nwa = jnp.tile(gla_norm_w[layer], H_A)[None, :]
    nwb = jnp.tile(gdn_norm_w[layer], H_B)[None, :]
    x1, h2, scores_t = _post(x_all, proj, o_af, o_ab_, o_df, o_db, mod3, w_o[layer].astype(BF16), nwa, nwb,
                             norm2_w[layer][None, :], router_w[layer].T, n_prompt_tok, ls, tm)

    bias_col = jnp.broadcast_to(router_bias[layer].astype(F32)[:, None], (N_EXPERTS, LANES))
    e8, r8, w8, cnt = _route(scores_t, bias_col)
    start, item_blk, item_exp, item_nxt, item_lo, item_hi = _items(cnt[:, 0].astype(jnp.int32), n_tok * TOP_K)
    dest = _dest(start, e8, r8, tt=min(2048, n_tok))

    xs = _dispatch(h2, dest, tm)
    ys = _experts(item_blk, item_exp, item_nxt, item_lo, item_hi, xs, exp_w_gate[layer], exp_w_up[layer],
                  exp_w_down[layer])
    y_all = _final(x1, h2, w8.T, mod3, sh_w_gate[layer].astype(BF16), sh_w_up[layer].astype(BF16),
                   sh_w_down[layer].astype(BF16), final_norm_w[None, :], dest, ys, n_prompt_tok, ls, tm)

    y_prompt = y_all[:n_prompt_tok].reshape(bp, lp, d)
    y_sample = y_all[n_prompt_tok:].reshape(bs, ls, d)
    new_state_gla = s_gla[:n_prompt_units].reshape(bp, 1, N_DIR, H_A, DK_A, DV_A).astype(x_prompt.dtype)
    new_state_gdn = s_gdn[:n_prompt_units].reshape(bp, 1, N_DIR, H_B, DK_B, DV_B).astype(x_prompt.dtype)
    return (y_prompt, y_sample, new_state_gla, new_state_gdn)
```

```python
import functools

import jax
import jax.numpy as jnp
from jax import lax
from jax.experimental import pallas as pl
from jax.experimental.pallas import tpu as pltpu

F32 = jnp.float32
BF16 = jnp.bfloat16

D_MODEL = 1024
N_DIR = 2
H_A, DK_A, DV_A = 4, 64, 128
GLA_RANK = 16
GLA_NORMALIZER = 16.0
H_B, DK_B, DV_B = 4, 128, 128
CONV_K = 5
CHUNK = 64
QA, VA = H_A * DK_A, H_A * DV_A
QB, VB = H_B * DK_B, H_B * DV_B
N_EXPERTS = 256
TOP_K = 8
N_GROUP = 8
TOPK_GROUP = 4
GROUP_SIZE = N_EXPERTS // N_GROUP
D_EXPERT = 256
ROUTED_SCALE = 2.5
EPS = 1e-6
NEG_INF = float("-inf")

UNIT = 256
CHUNKS_PER_UNIT = UNIT // CHUNK
HC = H_B * CHUNK
GDN_GROUP = 2
HG = GDN_GROUP * CHUNK
INV_BLOCK = 8
D_PACK = D_MODEL // 2
ROUTE_TILE = 256
ROW_BLOCK = 512
N_DMA_QUEUES = 2
LANES = 128
SUBLANES = 8
VMEM_LIMIT = 56 * 1024 * 1024

C_QKVA = 0
C_GA = 1024
C_QKVB = 1536
C_ZB = 3072
C_SMALL = 3584
D_PROJ = 3712
SM_AB = 32
SM_BETA = 40


def _cparams(sem):
    return pltpu.CompilerParams(dimension_semantics=sem, vmem_limit_bytes=VMEM_LIMIT)


def _split(a):
    hi = a.astype(BF16)
    lo = (a - hi.astype(F32)).astype(BF16)
    return hi, lo


def _dg(a, b, dims):
    return lax.dot_general(a, b, (dims, ((), ())), preferred_element_type=F32)


NN = ((1,), (0,))
NT = ((1,), (1,))
TN = ((0,), (0,))


def _mm(a, b, dims=NN):
    return _dg(a.astype(BF16), b.astype(BF16), dims)


def _mm3(a, b, dims=NN):
    ah, al = _split(a)
    bh, bl = _split(b)
    return _dg(ah, bh, dims) + (_dg(ah, bl, dims) + _dg(al, bh, dims))


def _mm_exact_lhs(a_bf16, b, dims=NN):
    bh, bl = _split(b)
    return _dg(a_bf16, bh, dims) + _dg(a_bf16, bl, dims)


def _silu(x):
    return x * (1.0 / (1.0 + jnp.exp(-x)))


def _sigmoid(x):
    return 1.0 / (1.0 + jnp.exp(-x))


def _softplus(x):
    return jnp.maximum(x, 0.0) + jnp.log1p(jnp.exp(-jnp.abs(x)))


def _log_sigmoid(x):
    return -_softplus(-x)


def _iota2(shape, dim):
    return lax.broadcasted_iota(jnp.int32, shape, dim)


def _rms(x):
    return x * lax.rsqrt(jnp.mean(x * x, axis=-1, keepdims=True) + EPS)


def _pack_rows(x):
    lo = lax.bitcast_convert_type(x[:, 0:D_PACK].astype(BF16).astype(F32), jnp.uint32)
    hi = lax.bitcast_convert_type(x[:, D_PACK:D_MODEL].astype(BF16).astype(F32), jnp.uint32)
    return (lo >> 16) | (hi & jnp.uint32(0xFFFF0000))


def _unpack_rows(p):
    lo = lax.bitcast_convert_type(p << 16, F32)
    hi = lax.bitcast_convert_type(p & jnp.uint32(0xFFFF0000), F32)
    return lo, hi


def _ada_kernel(c_ref, w_ref, b_ref, o_ref):
    o_ref[...] = _mm3(_silu(c_ref[...]), w_ref[...]) + b_ref[...]


def _ada(cond, w, b):
    n = w.shape[1]
    tn = 1536
    return pl.pallas_call(
        _ada_kernel,
        grid=(n // tn,),
        in_specs=[pl.BlockSpec((SUBLANES, D_MODEL), lambda i: (0, 0)),
                  pl.BlockSpec((D_MODEL, tn), lambda i: (0, i)),
                  pl.BlockSpec((1, tn), lambda i: (0, i))],
        out_specs=pl.BlockSpec((SUBLANES, tn), lambda i: (0, i)),
        out_shape=jax.ShapeDtypeStruct((SUBLANES, n), F32),
        compiler_params=_cparams(("parallel",)),
        name="ada",
    )(cond, w, b)


def _inproj_kernel(x_ref, mod_ref, nw_ref, w_ref, ws_ref, cw_ref, o_ref, *, tiles_prompt, prompt_row, sample_row):
    m = mod_ref[0]
    h = (_rms(x_ref[...]) * nw_ref[...]) * (1.0 + m[:, D_MODEL:2 * D_MODEL]) + m[:, 0:D_MODEL]
    hb = h.astype(BF16)
    for c0 in range(0, C_SMALL, 512):
        o_ref[:, c0:c0 + 512] = _dg(hb, w_ref[:, c0:c0 + 512], NN)
    o_ref[:, C_SMALL:D_PROJ] = _mm3(h, ws_ref[...])
    tm = x_ref.shape[0]
    row_len = jnp.where(pl.program_id(0) < tiles_prompt, prompt_row, sample_row)
    pos = _iota2((tm, LANES), 0) & (row_len - 1)
    inside = [jnp.logical_and(pos + (jj - CONV_K // 2) >= 0, pos + (jj - CONV_K // 2) < row_len).astype(F32)
              for jj in range(CONV_K)]
    for part in range(3 * H_B):
        cs = slice(C_QKVB + part * LANES, C_QKVB + (part + 1) * LANES)
        x = o_ref[:, cs]
        acc = x * cw_ref[CONV_K // 2:CONV_K // 2 + 1, part * LANES:(part + 1) * LANES]
        for jj in range(CONV_K):
            off = jj - CONV_K // 2
            if off == 0:
                continue
            xs = pltpu.roll(x, (-off) % tm, 0) * inside[jj]
            acc = acc + xs * cw_ref[jj:jj + 1, part * LANES:(part + 1) * LANES]
        y = _silu(acc)
        if part < 2 * H_B:
            y = y * lax.rsqrt(jnp.sum(y * y, axis=-1, keepdims=True) + EPS)
            if part < H_B:
                y = y * (DK_B ** -0.5)
        o_ref[:, cs] = y


def _cond_row(i, tiles_prompt, tiles_per_seq):
    return jnp.where(i < tiles_prompt, 0, 1 + (i - tiles_prompt) // tiles_per_seq)


def _inproj(x_all, mod3, norm_w, w_main, w_small, conv_w, n_prompt_tok, sample_len, prompt_row, sample_row, tm):
    n_tok = x_all.shape[0]
    assert tm % prompt_row == 0 and tm % sample_row == 0
    cond = functools.partial(_cond_row, tiles_prompt=n_prompt_tok // tm, tiles_per_seq=sample_len // tm)
    kern = functools.partial(_inproj_kernel, tiles_prompt=n_prompt_tok // tm, prompt_row=prompt_row,
                             sample_row=sample_row)
    return pl.pallas_call(
        kern,
        grid=(n_tok // tm,),
        in_specs=[pl.BlockSpec((tm, D_MODEL), lambda i: (i, 0)),
                  pl.BlockSpec((1, 1, 6 * D_MODEL), lambda i: (cond(i), 0, 0)),
                  pl.BlockSpec((1, D_MODEL), lambda i: (0, 0)),
                  pl.BlockSpec((D_MODEL, C_SMALL), lambda i: (0, 0)),
                  pl.BlockSpec((D_MODEL, LANES), lambda i: (0, 0)),
                  pl.BlockSpec((SUBLANES, 3 * QB), lambda i: (0, 0))],
        out_specs=pl.BlockSpec((tm, D_PROJ), lambda i: (i, 0)),
        out_shape=jax.ShapeDtypeStruct((n_tok, D_PROJ), F32),
        compiler_params=_cparams(("parallel",)),
        name="inproj",
    )(x_all, mod3, norm_w, w_main, w_small, conv_w)


def _unit_ids(j, n_prompt_units, units_per_seq):
    jj = j - n_prompt_units
    b = jj // units_per_seq
    r = jj % units_per_seq
    is_prompt = j < n_prompt_units
    uf = j
    ub = jnp.where(is_prompt, j, n_prompt_units + b * units_per_seq + (units_per_seq - 1 - r))
    init_row = jnp.where(is_prompt, 0, 1 + b)
    first = jnp.logical_or(is_prompt, r == 0)
    return uf, ub, init_row, first


def _tri(rev):
    t = _iota2((CHUNK, CHUNK), 0)
    s = _iota2((CHUNK, CHUNK), 1)
    return (t <= s) if rev else (t >= s)


def _mm_exact_lhs_tn(a, ones_bf16):
    ah, al = _split(a)
    return _dg(ah, ones_bf16, TN) + _dg(al, ones_bf16, TN)


def _gla_unit(chunks, order, states, rev):
    tri = _tri(rev).astype(BF16)
    mid, last = (CHUNK // 2 - 1, 0) if rev else (CHUNK // 2, CHUNK - 1)
    scale = DK_A ** -0.5
    ones = jnp.ones((CHUNK, LANES), BF16)
    lane = _iota2((CHUNK, LANES), 1)
    row = _iota2((CHUNK, LANES), 0)
    s_in = lane % DK_A
    causal = (row <= s_in) if rev else (row >= s_in)
    zeros_v = jnp.zeros((CHUNK, DV_A), F32)
    pairs = range(H_A // 2)
    att, qs_l, u_l, dec_l = {}, {}, {}, {}
    for c, (q, k, v, la) in enumerate(chunks):
        b = _mm_exact_lhs(tri, la)
        bref = b[mid:mid + 1, :]
        blast = b[last:last + 1, :]
        qg = q * jnp.exp(b - bref) * scale
        kg = k * jnp.exp(bref - b)
        kd = k * jnp.exp(blast - b)
        qs_l[c] = q * jnp.exp(b) * scale
        for p in pairs:
            ls = slice(p * LANES, (p + 1) * LANES)
            kg_p = kg[:, ls]
            rhs_att = jnp.concatenate([jnp.where(lane < DK_A, kg_p, 0.0), jnp.where(lane >= DK_A, kg_p, 0.0)], axis=0)
            att[c, p] = jnp.where(causal, _mm(qg[:, ls], rhs_att, NT), 0.0)
            u_l[c, p] = _mm(kd[:, ls], v[:, 2 * p * DV_A:(2 * p + 2) * DV_A], TN)
            dec_l[c, p] = jnp.exp(_mm_exact_lhs_tn(la[:, ls], ones))
    start = {}
    for c in order:
        start[c] = list(states)
        nxt = []
        for p in pairs:
            u, dec = u_l[c, p], dec_l[c, p]
            nxt.append(dec[0:DK_A] * states[2 * p] + u[0:DK_A, 0:DV_A])
            nxt.append(dec[DK_A:2 * DK_A] * states[2 * p + 1] + u[DK_A:2 * DK_A, DV_A:2 * DV_A])
        states = nxt
    outs = []
    for c, (q, k, v, la) in enumerate(chunks):
        o = []
        for p in pairs:
            ls = slice(p * LANES, (p + 1) * LANES)
            vs0 = v[:, (2 * p) * DV_A:(2 * p + 1) * DV_A]
            vs1 = v[:, (2 * p + 1) * DV_A:(2 * p + 2) * DV_A]
            s0, s1 = start[c][2 * p], start[c][2 * p + 1]
            rhs_o = jnp.concatenate([jnp.concatenate([vs0, zeros_v], axis=1),
                                     jnp.concatenate([zeros_v, vs1], axis=1),
                                     jnp.concatenate([s0, zeros_v], axis=1),
                                     jnp.concatenate([zeros_v, s1], axis=1)], axis=0)
            lhs_o = jnp.concatenate([att[c, p], qs_l[c][:, ls]], axis=1)
            o.append(_mm(lhs_o, rhs_o))
        outs.append(jnp.concatenate(o, axis=1))
    return outs, states


def _gla_kernel(qf_ref, qb_ref, sf_ref, sb_ref, wlr_ref, blr_ref, init_ref, of_ref, ob_ref, so_ref, s_ref,
                *, n_prompt_units, units_per_seq):
    j = pl.program_id(0)
    _, _, _, first = _unit_ids(j, n_prompt_units, units_per_seq)

    @pl.when(first)
    def _():
        s_ref[...] = init_ref[0]

    for d in range(N_DIR):
        x_ref, sm_ref, o_ref = (qf_ref, sf_ref, of_ref) if d == 0 else (qb_ref, sb_ref, ob_ref)
        la_unit = _log_sigmoid(_mm3(sm_ref[...], wlr_ref[d]) + blr_ref[d]) * (1.0 / GLA_NORMALIZER)
        states = [s_ref[d, h] for h in range(H_A)]
        order = range(CHUNKS_PER_UNIT - 1, -1, -1) if d == 1 else range(CHUNKS_PER_UNIT)
        chunks = []
        for c in range(CHUNKS_PER_UNIT):
            rows = slice(c * CHUNK, (c + 1) * CHUNK)
            chunks.append((x_ref[rows, 0:QA], x_ref[rows, QA:2 * QA], x_ref[rows, 2 * QA:2 * QA + VA], la_unit[rows]))
        outs, states = _gla_unit(chunks, order, states, rev=(d == 1))
        for c in range(CHUNKS_PER_UNIT):
            o_ref[c * CHUNK:(c + 1) * CHUNK, :] = outs[c]
        for h in range(H_A):
            s_ref[d, h] = states[h]
    so_ref[0] = s_ref[...]


def _gla(proj, wlr, blr, init, n_prompt_units, units_per_seq):
    n_tok = proj.shape[0]
    n_units = n_tok // UNIT
    ids = functools.partial(_unit_ids, n_prompt_units=n_prompt_units, units_per_seq=units_per_seq)
    small_blk = C_SMALL // LANES
    st_blk = (1, N_DIR, H_A, DK_A, DV_A)
    kern = functools.partial(_gla_kernel, n_prompt_units=n_prompt_units, units_per_seq=units_per_seq)
    return pl.pallas_call(
        kern,
        grid=(n_units,),
        in_specs=[pl.BlockSpec((UNIT, 1024), lambda j: (ids(j)[0], 0)),
                  pl.BlockSpec((UNIT, 1024), lambda j: (ids(j)[1], 0)),
                  pl.BlockSpec((UNIT, LANES), lambda j: (ids(j)[0], small_blk)),
                  pl.BlockSpec((UNIT, LANES), lambda j: (ids(j)[1], small_blk)),
                  pl.BlockSpec((N_DIR, LANES, QA), lambda j: (0, 0, 0)),
                  pl.BlockSpec((N_DIR, 1, QA), lambda j: (0, 0, 0)),
                  pl.BlockSpec(st_blk, lambda j: (ids(j)[2], 0, 0, 0, 0))],
        out_specs=[pl.BlockSpec((UNIT, VA), lambda j: (ids(j)[0], 0)),
                   pl.BlockSpec((UNIT, VA), lambda j: (ids(j)[1], 0)),
                   pl.BlockSpec(st_blk, lambda j: (jnp.minimum(j, n_prompt_units), 0, 0, 0, 0))],
        out_shape=[jax.ShapeDtypeStruct((n_tok, VA), F32),
                   jax.ShapeDtypeStruct((n_tok, VA), F32),
                   jax.ShapeDtypeStruct((n_prompt_units + 1, N_DIR, H_A, DK_A, DV_A), F32)],
        scratch_shapes=[pltpu.VMEM((N_DIR, H_A, DK_A, DV_A), F32)],
        compiler_params=_cparams(("arbitrary",)),
        name="gla",
    )(proj, proj, proj, proj, wlr, blr, init)


def _stack_masks(rev):
    r = _iota2((HG, HG), 0)
    c = _iota2((HG, HG), 1)
    same = (r // CHUNK) == (c // CHUNK)
    tr, tc = r % CHUNK, c % CHUNK
    incl = jnp.logical_and(same, (tr <= tc) if rev else (tr >= tc))
    strict = jnp.logical_and(same, (tr < tc) if rev else (tr > tc))
    return incl, strict


def _spread(x):
    z = jnp.zeros((CHUNK, LANES), x.dtype)
    rows = []
    for h in range(H_B):
        xh = x[h * CHUNK:(h + 1) * CHUNK]
        rows.append(jnp.concatenate([xh if g == h else z for g in range(H_B)], axis=1))
    return jnp.concatenate(rows, axis=0)


def _gdn_prepare(chunks):
    masks = {rev: _stack_masks(rev) for rev in {ch[6] for ch in chunks}}
    r = _iota2((HG, HG), 0)
    c = _iota2((HG, HG), 1)
    a_l, att_l, kb_l = [], [], []
    for q, k, v, gcb, beta, glast, rev in chunks:
        incl, strict = masks[rev]
        grow = gcb.T[0:1, :]
        diff = gcb[:, 0:1] - grow
        decay = jnp.where(incl, jnp.exp(jnp.where(incl, diff, 0.0)), 0.0)
        kb = k * beta
        m1 = _mm(jnp.concatenate([kb, q], axis=0), k, NT)
        a_l.append(jnp.where(strict, m1[0:HG] * decay, 0.0))
        att_l.append(m1[HG:2 * HG] * decay)
        kb_l.append(kb)
    diag = (r // INV_BLOCK) == (c // INV_BLOCK)
    pw_l = [jnp.where(diag, a, 0.0) for a in a_l]
    qm_l = [-p for p in pw_l]
    n = 2
    while n < INV_BLOCK:
        pw_l = [_mm(p, p) for p in pw_l]
        qm_l = [qm + p + _mm(qm, p) for qm, p in zip(qm_l, pw_l)]
        n *= 2
    b = INV_BLOCK
    while b < CHUNK:
        off = jnp.logical_and((r // (2 * b)) == (c // (2 * b)), (r // b) != (c // b))
        al_l = [jnp.where(off, a, 0.0) for a in a_l]
        t1_l = [al + _mm(qm, al) for qm, al in zip(qm_l, al_l)]
        qm_l = [qm - (t1 + _mm(t1, qm)) for qm, t1 in zip(qm_l, t1_l)]
        b *= 2
    out = []
    for (q, k, v, gcb, beta, glast, _), qm, kb, att in zip(chunks, qm_l, kb_l, att_l):
        egc = jnp.exp(gcb)
        rhs = jnp.concatenate([v * beta, kb * egc], axis=1)
        sol = rhs + _mm(qm, rhs)
        out.append((sol[:, 0:DV_B], sol[:, DV_B:2 * DV_B], q * egc, k * jnp.exp(glast - gcb), att))
    return out


def _gdn_scan_step(value, k_cum, q_dec, k_dec, att, gl_rows, s):
    kq = _mm(jnp.concatenate([_spread(k_cum), _spread(q_dec)], axis=0), s)
    v_new = value - kq[0:HC]
    o = kq[HC:2 * HC] + _mm(att, v_new)
    s_new = s * gl_rows + _mm(_spread(k_dec), v_new, TN)
    return o, s_new


def _gdn_kernel(xf_ref, xb_ref, sf_ref, sb_ref, gco_ref, gdt_ref, init_ref, of_ref, ob_ref, so_ref, s_ref,
                *, n_prompt_units, units_per_seq):
    j = pl.program_id(0)
    _, _, _, first = _unit_ids(j, n_prompt_units, units_per_seq)

    @pl.when(first)
    def _():
        s_ref[...] = init_ref[0]

    chunks, gl_rows = [], []
    for d in range(N_DIR):
        x_ref, sm_ref = (xf_ref, sf_ref) if d == 0 else (xb_ref, sb_ref)
        rev = d == 1
        sm = sm_ref[...]
        g_all = gco_ref[...] * _softplus(sm + gdt_ref[...])
        beta_all = _sigmoid(sm)
        tri = _tri(rev).astype(BF16)
        last = 0 if rev else CHUNK - 1
        for c in range(CHUNKS_PER_UNIT):
            rows = slice(c * CHUNK, (c + 1) * CHUNK)
            gc_all = _mm_exact_lhs(tri, g_all[rows])
            gcb, beta, glast = [], [], []
            for h in range(H_B):
                col = SM_AB + d * H_B + h
                colb = SM_BETA + d * H_B + h
                gh = jnp.broadcast_to(gc_all[:, col:col + 1], (CHUNK, LANES))
                gcb.append(gh)
                glast.append(jnp.broadcast_to(gh[last:last + 1, :], (CHUNK, LANES)))
                beta.append(jnp.broadcast_to(beta_all[rows, colb:colb + 1], (CHUNK, LANES)))
            for g0 in range(0, H_B, GDN_GROUP):
                hs = range(g0, g0 + GDN_GROUP)
                stack = lambda base: jnp.concatenate(
                    [x_ref[rows, base + h * LANES:base + (h + 1) * LANES] for h in hs], axis=0)
                cat = lambda parts: jnp.concatenate([parts[h] for h in hs], axis=0)
                chunks.append((stack(0), stack(QB), stack(2 * QB), cat(gcb), cat(beta), cat(glast), rev))
            gl_rows.append(jnp.concatenate([jnp.broadcast_to(jnp.exp(g[0:1, :]), (DK_B, DV_B)) for g in glast],
                                           axis=0))
    n_grp = H_B // GDN_GROUP
    zero_att = jnp.zeros((HG, HG), F32)
    prepared = []
    grouped = _gdn_prepare(chunks)
    for i in range(0, len(grouped), n_grp):
        parts = grouped[i:i + n_grp]
        rows_cat = [jnp.concatenate([p[f] for p in parts], axis=0) for f in range(4)]
        att = jnp.concatenate([jnp.concatenate([parts[g][4] if gg == g else zero_att for gg in range(n_grp)], axis=1)
                               for g in range(n_grp)], axis=0)
        prepared.append(tuple(rows_cat) + (att,))
    s = [s_ref[d] for d in range(N_DIR)]
    for step in range(CHUNKS_PER_UNIT):
        for d, o_ref in ((0, of_ref), (1, ob_ref)):
            c = step if d == 0 else CHUNKS_PER_UNIT - 1 - step
            o, s[d] = _gdn_scan_step(*prepared[d * CHUNKS_PER_UNIT + c], gl_rows[d * CHUNKS_PER_UNIT + c], s[d])
            for h in range(H_B):
                o_ref[c * CHUNK:(c + 1) * CHUNK, h * DV_B:(h + 1) * DV_B] = o[h * CHUNK:(h + 1) * CHUNK]
    for d in range(N_DIR):
        s_ref[d] = s[d]
    so_ref[0] = s_ref[...]


def _gdn(proj, gcoef, gdt, init, n_prompt_units, units_per_seq):
    n_tok = proj.shape[0]
    n_units = n_tok // UNIT
    ids = functools.partial(_unit_ids, n_prompt_units=n_prompt_units, units_per_seq=units_per_seq)
    small_blk = C_SMALL // LANES
    qkv_blk = C_QKVB // (3 * QB)
    st_blk = (1, N_DIR, H_B * DK_B, DV_B)
    kern = functools.partial(_gdn_kernel, n_prompt_units=n_prompt_units, units_per_seq=units_per_seq)
    return pl.pallas_call(
        kern,
        grid=(n_units,),
        in_specs=[pl.BlockSpec((UNIT, 3 * QB), lambda j: (ids(j)[0], qkv_blk)),
                  pl.BlockSpec((UNIT, 3 * QB), lambda j: (ids(j)[1], qkv_blk)),
                  pl.BlockSpec((UNIT, LANES), lambda j: (ids(j)[0], small_blk)),
                  pl.BlockSpec((UNIT, LANES), lambda j: (ids(j)[1], small_blk)),
                  pl.BlockSpec((1, LANES), lambda j: (0, 0)),
                  pl.BlockSpec((1, LANES), lambda j: (0, 0)),
                  pl.BlockSpec(st_blk, lambda j: (ids(j)[2], 0, 0, 0))],
        out_specs=[pl.BlockSpec((UNIT, VB), lambda j: (ids(j)[0], 0)),
                   pl.BlockSpec((UNIT, VB), lambda j: (ids(j)[1], 0)),
                   pl.BlockSpec(st_blk, lambda j: (jnp.minimum(j, n_prompt_units), 0, 0, 0))],
        out_shape=[jax.ShapeDtypeStruct((n_tok, VB), F32),
                   jax.ShapeDtypeStruct((n_tok, VB), F32),
                   jax.ShapeDtypeStruct((n_prompt_units + 1, N_DIR, H_B * DK_B, DV_B), F32)],
        scratch_shapes=[pltpu.VMEM((N_DIR, H_B * DK_B, DV_B), F32)],
        compiler_params=_cparams(("arbitrary",)),
        name="gdn",
    )(proj, proj, proj, proj, gcoef, gdt, init)


def _head_rms(o, w):
    parts = []
    for h in range(o.shape[1] // LANES):
        parts.append(_rms(o[:, h * LANES:(h + 1) * LANES]))
    return jnp.concatenate(parts, axis=1) * w


def _post_kernel(x_ref, ga_ref, zb_ref, af_ref, ab_ref, df_ref, db_ref, mod_ref, wo_ref, nwa_ref, nwb_ref, n2_ref,
                 rw_ref, x1_ref, h2_ref, sc_ref):
    m = mod_ref[0]
    gla = _head_rms(af_ref[...] + ab_ref[...], nwa_ref[...]) * _silu(ga_ref[...])
    gdn = _head_rms(df_ref[...] + db_ref[...], nwb_ref[...]) * _silu(zb_ref[...])
    y = _dg(gla.astype(BF16), wo_ref[0:VA, :], NN) + _dg(gdn.astype(BF16), wo_ref[VA:VA + VB, :], NN)
    x1 = x_ref[...] + m[:, 2 * D_MODEL:3 * D_MODEL] * y
    x1_ref[...] = x1
    h2 = (_rms(x1) * n2_ref[...]) * (1.0 + m[:, 4 * D_MODEL:5 * D_MODEL]) + m[:, 3 * D_MODEL:4 * D_MODEL]
    h2_ref[...] = _pack_rows(h2)
    sc_ref[...] = _sigmoid(_mm3(rw_ref[...], h2, NT))


def _post(x_all, proj, o_af, o_ab, o_df, o_db, mod3, w_o, nwa, nwb, n2w, router_wt, n_prompt_tok, sample_len, tm):
    n_tok = x_all.shape[0]
    cond = functools.partial(_cond_row, tiles_prompt=n_prompt_tok // tm, tiles_per_seq=sample_len // tm)
    tok = lambda i: (i, 0)
    const = lambda i: (0, 0)
    return pl.pallas_call(
        _post_kernel,
        grid=(n_tok // tm,),
        in_specs=[pl.BlockSpec((tm, D_MODEL), tok),
                  pl.BlockSpec((tm, VA), lambda i: (i, C_GA // VA)),
                  pl.BlockSpec((tm, VB), lambda i: (i, C_ZB // VB)),
                  pl.BlockSpec((tm, VA), tok), pl.BlockSpec((tm, VA), tok),
                  pl.BlockSpec((tm, VB), tok), pl.BlockSpec((tm, VB), tok),
                  pl.BlockSpec((1, 1, 6 * D_MODEL), lambda i: (cond(i), 0, 0)),
                  pl.BlockSpec((VA + VB, D_MODEL), const),
                  pl.BlockSpec((1, VA), const), pl.BlockSpec((1, VB), const), pl.BlockSpec((1, D_MODEL), const),
                  pl.BlockSpec((N_EXPERTS, D_MODEL), const)],
        out_specs=[pl.BlockSpec((tm, D_MODEL), tok),
                   pl.BlockSpec((tm, D_PACK), tok),
                   pl.BlockSpec((N_EXPERTS, tm), lambda i: (0, i))],
        out_shape=[jax.ShapeDtypeStruct((n_tok, D_MODEL), F32),
                   jax.ShapeDtypeStruct((n_tok, D_PACK), jnp.uint32),
                   jax.ShapeDtypeStruct((N_EXPERTS, n_tok), F32)],
        compiler_params=_cparams(("parallel",)),
        name="post",
    )(x_all, proj, proj, o_af, o_ab, o_df, o_db, mod3, w_o, nwa, nwb, n2w, router_wt)


def _route_kernel(sc_ref, bias_ref, e_ref, r_ref, w_ref, cnt_ref, carry_ref):
    i = pl.program_id(0)
    t = sc_ref.shape[1]

    @pl.when(i == 0)
    def _():
        carry_ref[...] = jnp.zeros(carry_ref.shape, F32)

    s = sc_ref[...]
    biased = s + bias_ref[:, 0:1]
    gs = []
    for g in range(N_GROUP):
        blk = biased[g * GROUP_SIZE:(g + 1) * GROUP_SIZE]
        m1 = jnp.max(blk, axis=0, keepdims=True)
        n1 = jnp.sum((blk == m1).astype(F32), axis=0, keepdims=True)
        m2 = jnp.max(jnp.where(blk < m1, blk, NEG_INF), axis=0, keepdims=True)
        gs.append(m1 + jnp.where(n1 >= 2.0, m1, m2))
    gsc = jnp.concatenate(gs, axis=0)
    gid = _iota2((N_GROUP, t), 0)
    beaten = jnp.zeros((N_GROUP, t), F32)
    for g in range(N_GROUP):
        other = gsc[g:g + 1, :]
        wins = jnp.logical_or(other > gsc, jnp.logical_and(other == gsc, g < gid))
        beaten = beaten + wins.astype(F32)
    masked = jnp.concatenate(
        [jnp.where(beaten[g:g + 1, :] < float(TOPK_GROUP), biased[g * GROUP_SIZE:(g + 1) * GROUP_SIZE], NEG_INF)
         for g in range(N_GROUP)], axis=0)
    eid = _iota2((N_EXPERTS, t), 0).astype(F32)
    sel = jnp.zeros((N_EXPERTS, t), F32)
    picks, scores = [], []
    for _ in range(TOP_K):
        m = jnp.max(masked, axis=0, keepdims=True)
        first = jnp.min(jnp.where(masked == m, eid, float(N_EXPERTS)), axis=0, keepdims=True)
        hit = eid == first
        scores.append(jnp.sum(jnp.where(hit, s, 0.0), axis=0, keepdims=True))
        masked = jnp.where(hit, NEG_INF, masked)
        sel = sel + hit.astype(F32)
        picks.append(first)
    upper = (_iota2((t, t), 0) < _iota2((t, t), 1)).astype(BF16)
    carry = carry_ref[...]
    prefix = _dg(sel.astype(BF16), upper, NN) + jnp.concatenate([carry] * (t // LANES), axis=1)
    ranks = [jnp.sum(jnp.where(eid == p, prefix, 0.0), axis=0, keepdims=True) for p in picks]
    carry = carry + _dg(sel.astype(BF16), jnp.ones((t, LANES), BF16), NN)
    carry_ref[...] = carry
    cnt_ref[...] = carry
    sc8 = jnp.concatenate(scores, axis=0)
    e_ref[...] = jnp.concatenate(picks, axis=0).astype(jnp.int32)
    r_ref[...] = jnp.concatenate(ranks, axis=0).astype(jnp.int32)
    w_ref[...] = sc8 / jnp.sum(sc8, axis=0, keepdims=True) * ROUTED_SCALE


def _route(scores_t, bias_col):
    n_tok = scores_t.shape[1]
    t = ROUTE_TILE
    slot = lambda i: (0, i)
    return pl.pallas_call(
        _route_kernel,
        grid=(n_tok // t,),
        in_specs=[pl.BlockSpec((N_EXPERTS, t), slot),
                  pl.BlockSpec((N_EXPERTS, LANES), lambda i: (0, 0))],
        out_specs=[pl.BlockSpec((TOP_K, t), slot), pl.BlockSpec((TOP_K, t), slot), pl.BlockSpec((TOP_K, t), slot),
                   pl.BlockSpec((N_EXPERTS, LANES), lambda i: (0, 0))],
        out_shape=[jax.ShapeDtypeStruct((TOP_K, n_tok), jnp.int32),
                   jax.ShapeDtypeStruct((TOP_K, n_tok), jnp.int32),
                   jax.ShapeDtypeStruct((TOP_K, n_tok), F32),
                   jax.ShapeDtypeStruct((N_EXPERTS, LANES), F32)],
        scratch_shapes=[pltpu.VMEM((N_EXPERTS, LANES), F32)],
        compiler_params=_cparams(("arbitrary",)),
        name="route",
    )(scores_t, bias_col)


def _dest_kernel(start_ref, e_ref, r_ref, d_ref):
    e = e_ref[...]

    def body(x, acc):
        return jnp.where(e == x, start_ref[x], acc)

    d_ref[...] = r_ref[...] + lax.fori_loop(0, N_EXPERTS, body, jnp.zeros(e.shape, jnp.int32))


def _dest(start, e8, r8, tt):
    n_tok = e8.shape[1]
    slot = lambda i, st: (0, i)
    return pl.pallas_call(
        _dest_kernel,
        grid_spec=pltpu.PrefetchScalarGridSpec(
            num_scalar_prefetch=1, grid=(n_tok // tt,),
            in_specs=[pl.BlockSpec((TOP_K, tt), slot), pl.BlockSpec((TOP_K, tt), slot)],
            out_specs=pl.BlockSpec((TOP_K, tt), slot)),
        out_shape=jax.ShapeDtypeStruct((TOP_K, n_tok), jnp.int32),
        compiler_params=_cparams(("parallel",)),
        name="dest",
    )(start, e8, r8)


def _items(counts, n_rows):
    n_blocks = n_rows // ROW_BLOCK
    max_items = n_blocks + N_EXPERTS - 1
    end = jnp.cumsum(counts)
    start = end - counts
    first_blk = start // ROW_BLOCK
    n_it = jnp.where(counts > 0, (end - 1) // ROW_BLOCK - first_blk + 1, 0)
    it_end = jnp.cumsum(n_it)
    it_start = it_end - n_it
    i = jnp.arange(max_items, dtype=jnp.int32)
    valid = i < it_end[-1]
    ex = jnp.minimum(jnp.sum((it_end[None, :] <= i[:, None]).astype(jnp.int32), axis=1), N_EXPERTS - 1)
    onehot = ex[:, None] == jnp.arange(N_EXPERTS, dtype=jnp.int32)[None, :]
    pick = lambda tab: jnp.sum(jnp.where(onehot, tab[None, :], 0), axis=1)
    blk = pick(first_blk) + (i - pick(it_start))
    lo = jnp.maximum(pick(start), blk * ROW_BLOCK) - blk * ROW_BLOCK
    hi = jnp.minimum(pick(end), (blk + 1) * ROW_BLOCK) - blk * ROW_BLOCK
    blk = jnp.where(valid, blk, n_blocks - 1).astype(jnp.int32)
    lo = jnp.where(valid, lo, 0).astype(jnp.int32)
    hi = jnp.where(valid, hi, 0).astype(jnp.int32)
    eids = jnp.arange(N_EXPERTS, dtype=jnp.int32)
    later = jnp.logical_and(eids[None, :] > eids[:, None], (counts > 0)[None, :])
    nxt_e = jnp.min(jnp.where(later, eids[None, :], N_EXPERTS), axis=1)
    nxt = pick(jnp.where(nxt_e < N_EXPERTS, nxt_e, -1)).astype(jnp.int32)
    return start.astype(jnp.int32), blk, ex.astype(jnp.int32), nxt, lo, hi


def _dispatch_kernel(h2_ref, dest_hbm, xs_hbm, dsm, sem_d, sem_s):
    i = pl.program_id(0)
    n = pl.num_programs(0)
    tm = h2_ref.shape[0]

    def dest_copy(step, slot):
        return pltpu.make_async_copy(dest_hbm.at[:, pl.ds(step * tm, tm)], dsm.at[slot], sem_d.at[slot])

    def row_copy(t, dst):
        return pltpu.make_async_copy(h2_ref.at[pl.ds(t, 1), :], xs_hbm.at[pl.ds(dst, 1), :], sem_s.at[0])

    @pl.when(i == 0)
    def _():
        dest_copy(0, 0).start()

    slot = i % 2
    dest_copy(i, slot).wait()

    @pl.when(i + 1 < n)
    def _():
        dest_copy(i + 1, 1 - slot).start()

    def issue(t, carry):
        for k in range(TOP_K):
            row_copy(t, dsm[slot, k, t]).start(priority=k % N_DMA_QUEUES)
        return carry
    lax.fori_loop(0, tm, issue, 0, unroll=2)

    def drain(t, carry):
        for k in range(TOP_K):
            row_copy(t, 0).wait()
        return carry
    lax.fori_loop(0, tm, drain, 0)


def _dispatch(h2, dest, tm):
    n_tok = h2.shape[0]
    return pl.pallas_call(
        _dispatch_kernel,
        grid=(n_tok // tm,),
        in_specs=[pl.BlockSpec((tm, D_PACK), lambda i: (i, 0)),
                  pl.BlockSpec(memory_space=pl.ANY)],
        out_specs=pl.BlockSpec(memory_space=pl.ANY),
        out_shape=jax.ShapeDtypeStruct((n_tok * TOP_K, D_PACK), jnp.uint32),
        scratch_shapes=[pltpu.SMEM((2, TOP_K, tm), jnp.int32),
                        pltpu.SemaphoreType.DMA((2,)),
                        pltpu.SemaphoreType.DMA((1,))],
        compiler_params=_cparams(("arbitrary",)),
        name="dispatch",
    )(h2, dest)


def _experts_kernel(blk_ref, exp_ref, nxt_ref, lo_ref, hi_ref, x_ref, wg_hbm, wu_hbm, wd_hbm, y_ref,
                    wgf, wuf, wdf, wgb, wub, wdb, nchg, sem_w):
    i = pl.program_id(0)
    lo, hi = lo_ref[i], hi_ref[i]

    def weight_copies(e, slot):
        return (pltpu.make_async_copy(wg_hbm.at[e], wgf.at[slot], sem_w.at[slot]),
                pltpu.make_async_copy(wu_hbm.at[e], wuf.at[slot], sem_w.at[slot]),
                pltpu.make_async_copy(wd_hbm.at[e], wdf.at[slot], sem_w.at[slot]))

    @pl.when(i == 0)
    def _():
        nchg[0] = 0
        for cp in weight_copies(exp_ref[0], 0):
            cp.start()

    @pl.when(hi > lo)
    def _():
        @pl.when(jnp.logical_or(i == 0, exp_ref[i] != exp_ref[jnp.maximum(i - 1, 0)]))
        def _():
            slot = nchg[0] % 2
            for cp in weight_copies(exp_ref[i], slot):
                cp.wait()

            @pl.when(nxt_ref[i] >= 0)
            def _():
                for cp in weight_copies(nxt_ref[i], 1 - slot):
                    cp.start()

            wgb[...] = wgf[slot].astype(BF16)
            wub[...] = wuf[slot].astype(BF16)
            wdb[...] = wdf[slot].astype(BF16)
            nchg[0] = nchg[0] + 1

        x_lo, x_hi = _unpack_rows(x_ref[...])
        x = jnp.concatenate([x_lo.astype(BF16), x_hi.astype(BF16)], axis=1)
        g = _dg(x, wgb[...], NN)
        u = _dg(x, wub[...], NN)
        hmid = (_silu(g) * u).astype(BF16)
        y = _pack_rows(_dg(hmid, wdb[...], NN))
        row = _iota2((ROW_BLOCK, D_PACK), 0)
        mine = jnp.logical_and(row >= lo, row < hi)

        @pl.when(lo == 0)
        def _():
            y_ref[...] = jnp.where(mine, y, jnp.uint32(0))

        @pl.when(lo > 0)
        def _():
            y_ref[...] = jnp.where(mine, y, y_ref[...])


def _experts(item_blk, item_exp, item_nxt, item_lo, item_hi, xs, w_gate, w_up, w_down):
    n_items = item_blk.shape[0]
    xmap = lambda i, blk, ex, nxt, lo, hi: (blk[i], 0)
    return pl.pallas_call(
        _experts_kernel,
        grid_spec=pltpu.PrefetchScalarGridSpec(
            num_scalar_prefetch=5, grid=(n_items,),
            in_specs=[pl.BlockSpec((ROW_BLOCK, D_PACK), xmap),
                      pl.BlockSpec(memory_space=pl.ANY),
                      pl.BlockSpec(memory_space=pl.ANY),
                      pl.BlockSpec(memory_space=pl.ANY)],
            out_specs=pl.BlockSpec((ROW_BLOCK, D_PACK), xmap),
            scratch_shapes=[pltpu.VMEM((2, D_MODEL, D_EXPERT), F32), pltpu.VMEM((2, D_MODEL, D_EXPERT), F32),
                            pltpu.VMEM((2, D_EXPERT, D_MODEL), F32),
                            pltpu.VMEM((D_MODEL, D_EXPERT), BF16), pltpu.VMEM((D_MODEL, D_EXPERT), BF16),
                            pltpu.VMEM((D_EXPERT, D_MODEL), BF16),
                            pltpu.SMEM((1,), jnp.int32),
                            pltpu.SemaphoreType.DMA((2,))]),
        out_shape=jax.ShapeDtypeStruct(xs.shape, jnp.uint32),
        compiler_params=_cparams(("arbitrary",)),
        name="experts",
    )(item_blk, item_exp, item_nxt, item_lo, item_hi, xs, w_gate, w_up, w_down)


def _final_kernel(x1_ref, h2_ref, w_ref, mod_ref, sg_ref, su_ref, sd_ref, fn_ref, dest_hbm, ys_hbm, o_ref,
                  dsm, gbuf, sem_d, sem_g):
    i = pl.program_id(0)
    n = pl.num_programs(0)
    tm = x1_ref.shape[0]

    def dest_copy(step, slot):
        return pltpu.make_async_copy(dest_hbm.at[:, pl.ds(step * tm, tm)], dsm.at[slot], sem_d.at[slot])

    def row_copy(src, slot, k, t):
        return pltpu.make_async_copy(ys_hbm.at[pl.ds(src, 1), :], gbuf.at[slot, k, pl.ds(t, 1), :], sem_g.at[slot])

    def issue_gathers(slot3, slot2):
        def body(t, carry):
            for k in range(TOP_K):
                row_copy(dsm[slot3, k, t], slot2, k, t).start(priority=k % N_DMA_QUEUES)
            return carry
        lax.fori_loop(0, tm, body, 0, unroll=2)

    @pl.when(i == 0)
    def _():
        dest_copy(0, 0).start()
        dest_copy(0, 0).wait()
        issue_gathers(0, 0)

        @pl.when(n > 1)
        def _():
            dest_copy(1, 1).start()

    @pl.when(i + 1 < n)
    def _():
        dest_copy(i + 1, (i + 1) % 3).wait()
        issue_gathers((i + 1) % 3, (i + 1) % 2)

        @pl.when(i + 2 < n)
        def _():
            dest_copy(i + 2, (i + 2) % 3).start()

    slot = i % 2

    def drain(t, carry):
        for k in range(TOP_K):
            row_copy(0, slot, k, t).wait()
        return carry
    lax.fori_loop(0, tm, drain, 0)

    m = mod_ref[0]
    w = w_ref[...]
    r_lo = r_hi = None
    for kk in range(TOP_K):
        y_lo, y_hi = _unpack_rows(gbuf[slot, kk])
        wk = w[:, kk:kk + 1]
        r_lo = y_lo * wk if r_lo is None else r_lo + y_lo * wk
        r_hi = y_hi * wk if r_hi is None else r_hi + y_hi * wk
    routed = jnp.concatenate([r_lo, r_hi], axis=1)
    h_lo, h_hi = _unpack_rows(h2_ref[...])
    hb = jnp.concatenate([h_lo.astype(BF16), h_hi.astype(BF16)], axis=1)
    hm = (_silu(_dg(hb, sg_ref[...], NN)) * _dg(hb, su_ref[...], NN)).astype(BF16)
    shared = _dg(hm, sd_ref[...], NN)
    x2 = x1_ref[...] + m[:, 5 * D_MODEL:6 * D_MODEL] * (routed + shared)
    o_ref[...] = _rms(x2) * fn_ref[...]


def _final(x1, h2, wts, mod3, sg, su, sd, fnw, dest, ys, n_prompt_tok, sample_len, tm):
    n_tok = x1.shape[0]
    cond = functools.partial(_cond_row, tiles_prompt=n_prompt_tok // tm, tiles_per_seq=sample_len // tm)
    tok = lambda i: (i, 0)
    const = lambda i: (0, 0)
    return pl.pallas_call(
        _final_kernel,
        grid=(n_tok // tm,),
        in_specs=[pl.BlockSpec((tm, D_MODEL), tok),
                  pl.BlockSpec((tm, D_PACK), tok),
                  pl.BlockSpec((tm, TOP_K), tok),
                  pl.BlockSpec((1, 1, 6 * D_MODEL), lambda i: (cond(i), 0, 0)),
                  pl.BlockSpec((D_MODEL, D_EXPERT), const),
                  pl.BlockSpec((D_MODEL, D_EXPERT), const),
                  pl.BlockSpec((D_EXPERT, D_MODEL), const),
                  pl.BlockSpec((1, D_MODEL), const),
                  pl.BlockSpec(memory_space=pl.ANY),
                  pl.BlockSpec(memory_space=pl.ANY)],
        out_specs=pl.BlockSpec((tm, D_MODEL), tok),
        out_shape=jax.ShapeDtypeStruct((n_tok, D_MODEL), F32),
        scratch_shapes=[pltpu.SMEM((3, TOP_K, tm), jnp.int32),
                        pltpu.VMEM((2, TOP_K, tm, D_PACK), jnp.uint32),
                        pltpu.SemaphoreType.DMA((3,)),
                        pltpu.SemaphoreType.DMA((2,))],
        compiler_params=_cparams(("arbitrary",)),
        name="final",
    )(x1, h2, wts, mod3, sg, su, sd, fnw, dest, ys)


def kernel(x_prompt, x_sample, state_gla, state_gdn, c, c_ctx, w_ada, b_ada, norm1_w, w_in, conv_w, gla_lr_w, gla_lr_b, gdn_a_log, gdn_dt_bias, gla_norm_w, gdn_norm_w, w_o, norm2_w, router_w, router_bias, exp_w_gate, exp_w_up, exp_w_down, sh_w_gate, sh_w_up, sh_w_down, final_norm_w):
    bp, lp, d = x_prompt.shape
    bs, ls, _ = x_sample.shape
    assert d == D_MODEL and lp == UNIT and ls % UNIT == 0 and w_ada.shape[0] == 1
    n_prompt_tok = bp * lp
    n_tok = n_prompt_tok + bs * ls
    n_prompt_units = n_prompt_tok // UNIT
    units_per_seq = ls // UNIT
    grid_w = 64
    layer = 0
    tm_in = 512
    tm = 256
    assert n_prompt_tok % tm_in == 0 and ls % tm_in == 0 and n_tok % ROUTE_TILE == 0

    x_all = jnp.concatenate([x_prompt.reshape(n_prompt_tok, d), x_sample.reshape(bs * ls, d)], axis=0)
    cond = jnp.concatenate([c_ctx[None, :], c, jnp.zeros((SUBLANES - 1 - bs, d), F32)], axis=0)
    mod3 = _ada(cond, w_ada[layer], b_ada[layer][None, :]).reshape(SUBLANES, 1, 6 * d)

    wi = w_in[layer]
    o_lr = 2 * QA + 2 * VA
    o_qkvb = o_lr + N_DIR * GLA_RANK
    o_zb = o_qkvb + 3 * QB
    o_ab = o_zb + VB
    w_main = jnp.concatenate([wi[:, 0:o_lr], wi[:, o_qkvb:o_ab]], axis=1).astype(BF16)
    w_small = jnp.concatenate([wi[:, o_lr:o_qkvb], wi[:, o_ab:], jnp.zeros((d, LANES - 48), F32)], axis=1)
    cw = jnp.concatenate([conv_w[layer], jnp.zeros((SUBLANES - CONV_K, 3 * QB), F32)], axis=0)
    proj = _inproj(x_all, mod3, norm1_w[layer][None, :], w_main, w_small, cw, n_prompt_tok, ls, lp, grid_w, tm_in)

    wlr = jnp.zeros((N_DIR, LANES, QA), F32)
    for dd in range(N_DIR):
        wlr = wlr.at[dd, dd * GLA_RANK:(dd + 1) * GLA_RANK, :].set(gla_lr_w[layer, dd])
    blr = gla_lr_b[layer][:, None, :]
    init_gla = jnp.concatenate([jnp.zeros((1,) + state_gla.shape[2:], F32), state_gla[:, layer].astype(F32)], axis=0)
    o_af, o_ab_, s_gla = _gla(proj, wlr, blr, init_gla, n_prompt_units, units_per_seq)

    gcoef = jnp.zeros((1, LANES), F32).at[0, SM_AB:SM_AB + N_DIR * H_B].set(-jnp.exp(gdn_a_log[layer].reshape(-1)))
    gdt = jnp.zeros((1, LANES), F32).at[0, SM_AB:SM_AB + N_DIR * H_B].set(gdn_dt_bias[layer].reshape(-1))
    init_gdn = jnp.concatenate([jnp.zeros((1, N_DIR, H_B * DK_B, DV_B), F32),
                                state_gdn[:, layer].astype(F32).reshape(bs, N_DIR, H_B * DK_B, DV_B)], axis=0)
    o_df, o_db, s_gdn = _gdn(proj, gcoef, gdt, init_gdn, n_prompt_units, units_per_seq)

    nwa = jnp.tile(gla_norm_w[layer], H_A)[None, :]
    nwb = jnp.tile(gdn_norm_w[layer], H_B)[None, :]
    x1, h2, scores_t = _post(x_all, proj, o_af, o_ab_, o_df, o_db, mod3, w_o[layer].astype(BF16), nwa, nwb,
                             norm2_w[layer][None, :], router_w[layer].T, n_prompt_tok, ls, tm)

    bias_col = jnp.broadcast_to(router_bias[layer].astype(F32)[:, None], (N_EXPERTS, LANES))
    e8, r8, w8, cnt = _route(scores_t, bias_col)
    start, item_blk, item_exp, item_nxt, item_lo, item_hi = _items(cnt[:, 0].astype(jnp.int32), n_tok * TOP_K)
    dest = _dest(start, e8, r8, tt=min(2048, n_tok))

    xs = _dispatch(h2, dest, tm)
    ys = _experts(item_blk, item_exp, item_nxt, item_lo, item_hi, xs, exp_w_gate[layer], exp_w_up[layer],
                  exp_w_down[layer])
    y_all = _final(x1, h2, w8.T, mod3, sh_w_gate[layer].astype(BF16), sh_w_up[layer].astype(BF16),
                   sh_w_down[layer].astype(BF16), final_norm_w[None, :], dest, ys, n_prompt_tok, ls, tm)

    y_prompt = y_all[:n_prompt_tok].reshape(bp, lp, d)
    y_sample = y_all[n_prompt_tok:].reshape(bs, ls, d)
    new_state_gla = s_gla[:n_prompt_units].reshape(bp, 1, N_DIR, H_A, DK_A, DV_A).astype(x_prompt.dtype)
    new_state_gdn = s_gdn[:n_prompt_units].reshape(bp, 1, N_DIR, H_B, DK_B, DV_B).astype(x_prompt.dtype)
    return (y_prompt, y_sample, new_state_gla, new_state_gdn)
```

```python
import functools

import jax
import jax.numpy as jnp
from jax import lax
from jax.experimental import pallas as pl
from jax.experimental.pallas import tpu as pltpu

F32 = jnp.float32
BF16 = jnp.bfloat16

D_MODEL = 1024
N_DIR = 2
H_A, DK_A, DV_A = 4, 64, 128
GLA_RANK = 16
GLA_NORMALIZER = 16.0
H_B, DK_B, DV_B = 4, 128, 128
CONV_K = 5
CHUNK = 64
QA, VA = H_A * DK_A, H_A * DV_A
QB, VB = H_B * DK_B, H_B * DV_B
N_EXPERTS = 256
TOP_K = 8
N_GROUP = 8
TOPK_GROUP = 4
GROUP_SIZE = N_EXPERTS // N_GROUP
D_EXPERT = 256
ROUTED_SCALE = 2.5
EPS = 1e-6
NEG_INF = float("-inf")

UNIT = 256
CHUNKS_PER_UNIT = UNIT // CHUNK
HC = H_B * CHUNK
GDN_GROUP = 2
HG = GDN_GROUP * CHUNK
INV_BLOCK = 8
D_PACK = D_MODEL // 2
ROUTE_TILE = 256
ROW_BLOCK = 512
N_DMA_QUEUES = 2
LANES = 128
SUBLANES = 8
VMEM_LIMIT = 56 * 1024 * 1024

C_QKVA = 0
C_GA = 1024
C_QKVB = 1536
C_ZB = 3072
C_SMALL = 3584
D_PROJ = 3712
SM_AB = 32
SM_BETA = 40


def _cparams(sem):
    return pltpu.CompilerParams(dimension_semantics=sem, vmem_limit_bytes=VMEM_LIMIT)


def _split(a):
    hi = a.astype(BF16)
    lo = (a - hi.astype(F32)).astype(BF16)
    return hi, lo


def _dg(a, b, dims):
    return lax.dot_general(a, b, (dims, ((), ())), preferred_element_type=F32)


NN = ((1,), (0,))
NT = ((1,), (1,))
TN = ((0,), (0,))


def _mm(a, b, dims=NN):
    return _dg(a.astype(BF16), b.astype(BF16), dims)


def _mm3(a, b, dims=NN):
    ah, al = _split(a)
    bh, bl = _split(b)
    return _dg(ah, bh, dims) + (_dg(ah, bl, dims) + _dg(al, bh, dims))


def _mm_exact_lhs(a_bf16, b, dims=NN):
    bh, bl = _split(b)
    return _dg(a_bf16, bh, dims) + _dg(a_bf16, bl, dims)


def _silu(x):
    return x * (1.0 / (1.0 + jnp.exp(-x)))


def _sigmoid(x):
    return 1.0 / (1.0 + jnp.exp(-x))


def _softplus(x):
    return jnp.maximum(x, 0.0) + jnp.log1p(jnp.exp(-jnp.abs(x)))


def _log_sigmoid(x):
    return -_softplus(-x)


def _iota2(shape, dim):
    return lax.broadcasted_iota(jnp.int32, shape, dim)


def _rms(x):
    return x * lax.rsqrt(jnp.mean(x * x, axis=-1, keepdims=True) + EPS)


def _pack_rows(x):
    lo = lax.bitcast_convert_type(x[:, 0:D_PACK].astype(BF16).astype(F32), jnp.uint32)
    hi = lax.bitcast_convert_type(x[:, D_PACK:D_MODEL].astype(BF16).astype(F32), jnp.uint32)
    return (lo >> 16) | (hi & jnp.uint32(0xFFFF0000))


def _unpack_rows(p):
    lo = lax.bitcast_convert_type(p << 16, F32)
    hi = lax.bitcast_convert_type(p & jnp.uint32(0xFFFF0000), F32)
    return lo, hi


def _ada_kernel(c_ref, w_ref, b_ref, o_ref):
    o_ref[...] = _mm3(_silu(c_ref[...]), w_ref[...]) + b_ref[...]


def _ada(cond, w, b):
    n = w.shape[1]
    tn = 1536
    return pl.pallas_call(
        _ada_kernel,
        grid=(n // tn,),
        in_specs=[pl.BlockSpec((SUBLANES, D_MODEL), lambda i: (0, 0)),
                  pl.BlockSpec((D_MODEL, tn), lambda i: (0, i)),
                  pl.BlockSpec((1, tn), lambda i: (0, i))],
        out_specs=pl.BlockSpec((SUBLANES, tn), lambda i: (0, i)),
        out_shape=jax.ShapeDtypeStruct((SUBLANES, n), F32),
        compiler_params=_cparams(("parallel",)),
        name="ada",
    )(cond, w, b)


def _two_streams(n_prompt_tok, tm):
    tp = n_prompt_tok // tm
    return [pl.BlockSpec((tm, D_MODEL), lambda i: (jnp.minimum(i, tp - 1), 0)),
            pl.BlockSpec((tm, D_MODEL), lambda i: (jnp.maximum(i - tp, 0), 0))]


def _inproj_kernel(xp_ref, xs_ref, mod_ref, nw_ref, w_ref, ws_ref, cw_ref, o_ref, *, tiles_prompt, prompt_row,
                   sample_row):
    m = mod_ref[0]
    x = jnp.where(pl.program_id(0) < tiles_prompt, xp_ref[...], xs_ref[...])
    h = (_rms(x) * nw_ref[...]) * (1.0 + m[:, D_MODEL:2 * D_MODEL]) + m[:, 0:D_MODEL]
    hb = h.astype(BF16)
    for c0 in range(0, C_SMALL, 512):
        o_ref[:, c0:c0 + 512] = _dg(hb, w_ref[:, c0:c0 + 512], NN)
    o_ref[:, C_SMALL:D_PROJ] = _mm3(h, ws_ref[...])
    tm = xp_ref.shape[0]
    row_len = jnp.where(pl.program_id(0) < tiles_prompt, prompt_row, sample_row)
    pos = _iota2((tm, LANES), 0) & (row_len - 1)
    inside = [jnp.logical_and(pos + (jj - CONV_K // 2) >= 0, pos + (jj - CONV_K // 2) < row_len).astype(F32)
              for jj in range(CONV_K)]
    for part in range(3 * H_B):
        cs = slice(C_QKVB + part * LANES, C_QKVB + (part + 1) * LANES)
        x = o_ref[:, cs]
        acc = x * cw_ref[CONV_K // 2:CONV_K // 2 + 1, part * LANES:(part + 1) * LANES]
        for jj in range(CONV_K):
            off = jj - CONV_K // 2
            if off == 0:
                continue
            xs = pltpu.roll(x, (-off) % tm, 0) * inside[jj]
            acc = acc + xs * cw_ref[jj:jj + 1, part * LANES:(part + 1) * LANES]
        y = _silu(acc)
        if part < 2 * H_B:
            y = y * lax.rsqrt(jnp.sum(y * y, axis=-1, keepdims=True) + EPS)
            if part < H_B:
                y = y * (DK_B ** -0.5)
        o_ref[:, cs] = y


def _cond_row(i, tiles_prompt, tiles_per_seq):
    return jnp.where(i < tiles_prompt, 0, 1 + (i - tiles_prompt) // tiles_per_seq)


def _inproj(xp, xs, mod3, norm_w, w_main, w_small, conv_w, sample_len, prompt_row, sample_row, tm):
    n_prompt_tok = xp.shape[0]
    n_tok = n_prompt_tok + xs.shape[0]
    assert tm % prompt_row == 0 and tm % sample_row == 0
    cond = functools.partial(_cond_row, tiles_prompt=n_prompt_tok // tm, tiles_per_seq=sample_len // tm)
    kern = functools.partial(_inproj_kernel, tiles_prompt=n_prompt_tok // tm, prompt_row=prompt_row,
                             sample_row=sample_row)
    return pl.pallas_call(
        kern,
        grid=(n_tok // tm,),
        in_specs=_two_streams(n_prompt_tok, tm) + [
                  pl.BlockSpec((1, 1, 6 * D_MODEL), lambda i: (cond(i), 0, 0)),
                  pl.BlockSpec((1, D_MODEL), lambda i: (0, 0)),
                  pl.BlockSpec((D_MODEL, C_SMALL), lambda i: (0, 0)),
                  pl.BlockSpec((D_MODEL, LANES), lambda i: (0, 0)),
                  pl.BlockSpec((SUBLANES, 3 * QB), lambda i: (0, 0))],
        out_specs=pl.BlockSpec((tm, D_PROJ), lambda i: (i, 0)),
        out_shape=jax.ShapeDtypeStruct((n_tok, D_PROJ), F32),
        compiler_params=_cparams(("parallel",)),
        name="inproj",
    )(xp, xs, mod3, norm_w, w_main, w_small, conv_w)


def _unit_ids(j, n_prompt_units, units_per_seq):
    jj = j - n_prompt_units
    b = jj // units_per_seq
    r = jj % units_per_seq
    is_prompt = j < n_prompt_units
    uf = j
    ub = jnp.where(is_prompt, j, n_prompt_units + b * units_per_seq + (units_per_seq - 1 - r))
    init_row = jnp.where(is_prompt, 0, 1 + b)
    first = jnp.logical_or(is_prompt, r == 0)
    return uf, ub, init_row, first


def _tri(rev):
    t = _iota2((CHUNK, CHUNK), 0)
    s = _iota2((CHUNK, CHUNK), 1)
    return (t <= s) if rev else (t >= s)


def _mm_exact_lhs_tn(a, ones_bf16):
    ah, al = _split(a)
    return _dg(ah, ones_bf16, TN) + _dg(al, ones_bf16, TN)


def _gla_unit(chunks, order, states, rev):
    tri = _tri(rev).astype(BF16)
    mid, last = (CHUNK // 2 - 1, 0) if rev else (CHUNK // 2, CHUNK - 1)
    scale = DK_A ** -0.5
    ones = jnp.ones((CHUNK, LANES), BF16)
    lane = _iota2((CHUNK, LANES), 1)
    row = _iota2((CHUNK, LANES), 0)
    s_in = lane % DK_A
    causal = (row <= s_in) if rev else (row >= s_in)
    zeros_v = jnp.zeros((CHUNK, DV_A), F32)
    pairs = range(H_A // 2)
    att, qs_l, u_l, dec_l = {}, {}, {}, {}
    for c, (q, k, v, la) in enumerate(chunks):
        b = _mm_exact_lhs(tri, la)
        bref = b[mid:mid + 1, :]
        blast = b[last:last + 1, :]
        qg = q * jnp.exp(b - bref) * scale
        kg = k * jnp.exp(bref - b)
        kd = k * jnp.exp(blast - b)
        qs_l[c] = q * jnp.exp(b) * scale
        for p in pairs:
            ls = slice(p * LANES, (p + 1) * LANES)
            kg_p = kg[:, ls]
            rhs_att = jnp.concatenate([jnp.where(lane < DK_A, kg_p, 0.0), jnp.where(lane >= DK_A, kg_p, 0.0)], axis=0)
            att[c, p] = jnp.where(causal, _mm(qg[:, ls], rhs_att, NT), 0.0)
            u_l[c, p] = _mm(kd[:, ls], v[:, 2 * p * DV_A:(2 * p + 2) * DV_A], TN)
            dec_l[c, p] = jnp.exp(_mm_exact_lhs_tn(la[:, ls], ones))
    start = {}
    for c in order:
        start[c] = list(states)
        nxt = []
        for p in pairs:
            u, dec = u_l[c, p], dec_l[c, p]
            nxt.append(dec[0:DK_A] * states[2 * p] + u[0:DK_A, 0:DV_A])
            nxt.append(dec[DK_A:2 * DK_A] * states[2 * p + 1] + u[DK_A:2 * DK_A, DV_A:2 * DV_A])
        states = nxt
    outs = []
    for c, (q, k, v, la) in enumerate(chunks):
        o = []
        for p in pairs:
            ls = slice(p * LANES, (p + 1) * LANES)
            vs0 = v[:, (2 * p) * DV_A:(2 * p + 1) * DV_A]
            vs1 = v[:, (2 * p + 1) * DV_A:(2 * p + 2) * DV_A]
            s0, s1 = start[c][2 * p], start[c][2 * p + 1]
            rhs_o = jnp.concatenate([jnp.concatenate([vs0, zeros_v], axis=1),
                                     jnp.concatenate([zeros_v, vs1], axis=1),
                                     jnp.concatenate([s0, zeros_v], axis=1),
                                     jnp.concatenate([zeros_v, s1], axis=1)], axis=0)
            lhs_o = jnp.concatenate([att[c, p], qs_l[c][:, ls]], axis=1)
            o.append(_mm(lhs_o, rhs_o))
        outs.append(jnp.concatenate(o, axis=1))
    return outs, states


def _gla_kernel(qf_ref, qb_ref, sf_ref, sb_ref, wlr_ref, blr_ref, init_ref, of_ref, ob_ref, so_ref, s_ref,
                *, n_prompt_units, units_per_seq):
    j = pl.program_id(0)
    _, _, _, first = _unit_ids(j, n_prompt_units, units_per_seq)

    @pl.when(first)
    def _():
        s_ref[...] = init_ref[0]

    for d in range(N_DIR):
        x_ref, sm_ref, o_ref = (qf_ref, sf_ref, of_ref) if d == 0 else (qb_ref, sb_ref, ob_ref)
        la_unit = _log_sigmoid(_mm3(sm_ref[...], wlr_ref[d]) + blr_ref[d]) * (1.0 / GLA_NORMALIZER)
        states = [s_ref[d, h] for h in range(H_A)]
        order = range(CHUNKS_PER_UNIT - 1, -1, -1) if d == 1 else range(CHUNKS_PER_UNIT)
        chunks = []
        for c in range(CHUNKS_PER_UNIT):
            rows = slice(c * CHUNK, (c + 1) * CHUNK)
            chunks.append((x_ref[rows, 0:QA], x_ref[rows, QA:2 * QA], x_ref[rows, 2 * QA:2 * QA + VA], la_unit[rows]))
        outs, states = _gla_unit(chunks, order, states, rev=(d == 1))
        for c in range(CHUNKS_PER_UNIT):
            o_ref[c * CHUNK:(c + 1) * CHUNK, :] = outs[c]
        for h in range(H_A):
            s_ref[d, h] = states[h]
    so_ref[0] = s_ref[...]


def _gla(proj, wlr, blr, init, n_prompt_units, units_per_seq):
    n_tok = proj.shape[0]
    n_units = n_tok // UNIT
    ids = functools.partial(_unit_ids, n_prompt_units=n_prompt_units, units_per_seq=units_per_seq)
    small_blk = C_SMALL // LANES
    st_blk = (1, N_DIR, H_A, DK_A, DV_A)
    kern = functools.partial(_gla_kernel, n_prompt_units=n_prompt_units, units_per_seq=units_per_seq)
    return pl.pallas_call(
        kern,
        grid=(n_units,),
        in_specs=[pl.BlockSpec((UNIT, 1024), lambda j: (ids(j)[0], 0)),
                  pl.BlockSpec((UNIT, 1024), lambda j: (ids(j)[1], 0)),
                  pl.BlockSpec((UNIT, LANES), lambda j: (ids(j)[0], small_blk)),
                  pl.BlockSpec((UNIT, LANES), lambda j: (ids(j)[1], small_blk)),
                  pl.BlockSpec((N_DIR, LANES, QA), lambda j: (0, 0, 0)),
                  pl.BlockSpec((N_DIR, 1, QA), lambda j: (0, 0, 0)),
                  pl.BlockSpec(st_blk, lambda j: (ids(j)[2], 0, 0, 0, 0))],
        out_specs=[pl.BlockSpec((UNIT, VA), lambda j: (ids(j)[0], 0)),
                   pl.BlockSpec((UNIT, VA), lambda j: (ids(j)[1], 0)),
                   pl.BlockSpec(st_blk, lambda j: (jnp.minimum(j, n_prompt_units), 0, 0, 0, 0))],
        out_shape=[jax.ShapeDtypeStruct((n_tok, VA), F32),
                   jax.ShapeDtypeStruct((n_tok, VA), F32),
                   jax.ShapeDtypeStruct((n_prompt_units + 1, N_DIR, H_A, DK_A, DV_A), F32)],
        scratch_shapes=[pltpu.VMEM((N_DIR, H_A, DK_A, DV_A), F32)],
        compiler_params=_cparams(("arbitrary",)),
        name="gla",
    )(proj, proj, proj, proj, wlr, blr, init)


def _stack_masks(rev):
    r = _iota2((HG, HG), 0)
    c = _iota2((HG, HG), 1)
    same = (r // CHUNK) == (c // CHUNK)
    tr, tc = r % CHUNK, c % CHUNK
    incl = jnp.logical_and(same, (tr <= tc) if rev else (tr >= tc))
    strict = jnp.logical_and(same, (tr < tc) if rev else (tr > tc))
    return incl, strict


def _spread(x):
    z = jnp.zeros((CHUNK, LANES), x.dtype)
    rows = []
    for h in range(H_B):
        xh = x[h * CHUNK:(h + 1) * CHUNK]
        rows.append(jnp.concatenate([xh if g == h else z for g in range(H_B)], axis=1))
    return jnp.concatenate(rows, axis=0)


def _gdn_prepare(chunks):
    masks = {rev: _stack_masks(rev) for rev in {ch[6] for ch in chunks}}
    r = _iota2((HG, HG), 0)
    c = _iota2((HG, HG), 1)
    a_l, att_l, kb_l = [], [], []
    for q, k, v, gcb, beta, glast, rev in chunks:
        incl, strict = masks[rev]
        grow = gcb.T[0:1, :]
        diff = gcb[:, 0:1] - grow
        decay = jnp.where(incl, jnp.exp(jnp.where(incl, diff, 0.0)), 0.0)
        kb = k * beta
        m1 = _mm(jnp.concatenate([kb, q], axis=0), k, NT)
        a_l.append(jnp.where(strict, m1[0:HG] * decay, 0.0))
        att_l.append(m1[HG:2 * HG] * decay)
        kb_l.append(kb)
    diag = (r // INV_BLOCK) == (c // INV_BLOCK)
    pw_l = [jnp.where(diag, a, 0.0) for a in a_l]
    qm_l = [-p for p in pw_l]
    n = 2
    while n < INV_BLOCK:
        pw_l = [_mm(p, p) for p in pw_l]
        qm_l = [qm + p + _mm(qm, p) for qm, p in zip(qm_l, pw_l)]
        n *= 2
    b = INV_BLOCK
    while b < CHUNK:
        off = jnp.logical_and((r // (2 * b)) == (c // (2 * b)), (r // b) != (c // b))
        al_l = [jnp.where(off, a, 0.0) for a in a_l]
        t1_l = [al + _mm(qm, al) for qm, al in zip(qm_l, al_l)]
        qm_l = [qm - (t1 + _mm(t1, qm)) for qm, t1 in zip(qm_l, t1_l)]
        b *= 2
    out = []
    for (q, k, v, gcb, beta, glast, _), qm, kb, att in zip(chunks, qm_l, kb_l, att_l):
        egc = jnp.exp(gcb)
        rhs = jnp.concatenate([v * beta, kb * egc], axis=1)
        sol = rhs + _mm(qm, rhs)
        out.append((sol[:, 0:DV_B], sol[:, DV_B:2 * DV_B], q * egc, k * jnp.exp(glast - gcb), att))
    return out


def _gdn_scan_step(value, k_cum, q_dec, k_dec, att, gl_rows, s):
    kq = _mm(jnp.concatenate([_spread(k_cum), _spread(q_dec)], axis=0), s)
    v_new = value - kq[0:HC]
    o = kq[HC:2 * HC] + _mm(att, v_new)
    s_new = s * gl_rows + _mm(_spread(k_dec), v_new, TN)
    return o, s_new


def _gdn_kernel(xf_ref, xb_ref, sf_ref, sb_ref, gco_ref, gdt_ref, init_ref, of_ref, ob_ref, so_ref, s_ref,
                *, n_prompt_units, units_per_seq):
    j = pl.program_id(0)
    _, _, _, first = _unit_ids(j, n_prompt_units, units_per_seq)

    @pl.when(first)
    def _():
        s_ref[...] = init_ref[0]

    chunks, gl_rows = [], []
    for d in range(N_DIR):
        x_ref, sm_ref = (xf_ref, sf_ref) if d == 0 else (xb_ref, sb_ref)
        rev = d == 1
        sm = sm_ref[...]
        g_all = gco_ref[...] * _softplus(sm + gdt_ref[...])
        beta_all = _sigmoid(sm)
        tri = _tri(rev).astype(BF16)
        last = 0 if rev else CHUNK - 1
        for c in range(CHUNKS_PER_UNIT):
            rows = slice(c * CHUNK, (c + 1) * CHUNK)
            gc_all = _mm_exact_lhs(tri, g_all[rows])
            gcb, beta, glast = [], [], []
            for h in range(H_B):
                col = SM_AB + d * H_B + h
                colb = SM_BETA + d * H_B + h
                gh = jnp.broadcast_to(gc_all[:, col:col + 1], (CHUNK, LANES))
                gcb.append(gh)
                glast.append(jnp.broadcast_to(gh[last:last + 1, :], (CHUNK, LANES)))
                beta.append(jnp.broadcast_to(beta_all[rows, colb:colb + 1], (CHUNK, LANES)))
            for g0 in range(0, H_B, GDN_GROUP):
                hs = range(g0, g0 + GDN_GROUP)
                stack = lambda base: jnp.concatenate(
                    [x_ref[rows, base + h * LANES:base + (h + 1) * LANES] for h in hs], axis=0)
                cat = lambda parts: jnp.concatenate([parts[h] for h in hs], axis=0)
                chunks.append((stack(0), stack(QB), stack(2 * QB), cat(gcb), cat(beta), cat(glast), rev))
            gl_rows.append(jnp.concatenate([jnp.broadcast_to(jnp.exp(g[0:1, :]), (DK_B, DV_B)) for g in glast],
                                           axis=0))
    n_grp = H_B // GDN_GROUP
    zero_att = jnp.zeros((HG, HG), F32)
    prepared = []
    grouped = _gdn_prepare(chunks)
    for i in range(0, len(grouped), n_grp):
        parts = grouped[i:i + n_grp]
        rows_cat = [jnp.concatenate([p[f] for p in parts], axis=0) for f in range(4)]
        att = jnp.concatenate([jnp.concatenate([parts[g][4] if gg == g else zero_att for gg in range(n_grp)], axis=1)
                               for g in range(n_grp)], axis=0)
        prepared.append(tuple(rows_cat) + (att,))
    s = [s_ref[d] for d in range(N_DIR)]
    for step in range(CHUNKS_PER_UNIT):
        for d, o_ref in ((0, of_ref), (1, ob_ref)):
            c = step if d == 0 else CHUNKS_PER_UNIT - 1 - step
            o, s[d] = _gdn_scan_step(*prepared[d * CHUNKS_PER_UNIT + c], gl_rows[d * CHUNKS_PER_UNIT + c], s[d])
            for h in range(H_B):
                o_ref[c * CHUNK:(c + 1) * CHUNK, h * DV_B:(h + 1) * DV_B] = o[h * CHUNK:(h + 1) * CHUNK]
    for d in range(N_DIR):
        s_ref[d] = s[d]
    so_ref[0] = s_ref[...]


def _gdn(proj, gcoef, gdt, init, n_prompt_units, units_per_seq):
    n_tok = proj.shape[0]
    n_units = n_tok // UNIT
    ids = functools.partial(_unit_ids, n_prompt_units=n_prompt_units, units_per_seq=units_per_seq)
    small_blk = C_SMALL // LANES
    qkv_blk = C_QKVB // (3 * QB)
    st_blk = (1, N_DIR, H_B * DK_B, DV_B)
    kern = functools.partial(_gdn_kernel, n_prompt_units=n_prompt_units, units_per_seq=units_per_seq)
    return pl.pallas_call(
        kern,
        grid=(n_units,),
        in_specs=[pl.BlockSpec((UNIT, 3 * QB), lambda j: (ids(j)[0], qkv_blk)),
                  pl.BlockSpec((UNIT, 3 * QB), lambda j: (ids(j)[1], qkv_blk)),
                  pl.BlockSpec((UNIT, LANES), lambda j: (ids(j)[0], small_blk)),
                  pl.BlockSpec((UNIT, LANES), lambda j: (ids(j)[1], small_blk)),
                  pl.BlockSpec((1, LANES), lambda j: (0, 0)),
                  pl.BlockSpec((1, LANES), lambda j: (0, 0)),
                  pl.BlockSpec(st_blk, lambda j: (ids(j)[2], 0, 0, 0))],
        out_specs=[pl.BlockSpec((UNIT, VB), lambda j: (ids(j)[0], 0)),
                   pl.BlockSpec((UNIT, VB), lambda j: (ids(j)[1], 0)),
                   pl.BlockSpec(st_blk, lambda j: (jnp.minimum(j, n_prompt_units), 0, 0, 0))],
        out_shape=[jax.ShapeDtypeStruct((n_tok, VB), F32),
                   jax.ShapeDtypeStruct((n_tok, VB), F32),
                   jax.ShapeDtypeStruct((n_prompt_units + 1, N_DIR, H_B * DK_B, DV_B), F32)],
        scratch_shapes=[pltpu.VMEM((N_DIR, H_B * DK_B, DV_B), F32)],
        compiler_params=_cparams(("arbitrary",)),
        name="gdn",
    )(proj, proj, proj, proj, gcoef, gdt, init)


def _head_rms(o, w):
    parts = []
    for h in range(o.shape[1] // LANES):
        parts.append(_rms(o[:, h * LANES:(h + 1) * LANES]))
    return jnp.concatenate(parts, axis=1) * w


def _post_kernel(xp_ref, xs_ref, ga_ref, zb_ref, af_ref, ab_ref, df_ref, db_ref, mod_ref, wo_ref, nwa_ref, nwb_ref,
                 n2_ref, rw_ref, x1_ref, h2_ref, sc_ref, *, tiles_prompt):
    m = mod_ref[0]
    x = jnp.where(pl.program_id(0) < tiles_prompt, xp_ref[...], xs_ref[...])
    gla = _head_rms(af_ref[...] + ab_ref[...], nwa_ref[...]) * _silu(ga_ref[...])
    gdn = _head_rms(df_ref[...] + db_ref[...], nwb_ref[...]) * _silu(zb_ref[...])
    y = _dg(gla.astype(BF16), wo_ref[0:VA, :], NN) + _dg(gdn.astype(BF16), wo_ref[VA:VA + VB, :], NN)
    x1 = x + m[:, 2 * D_MODEL:3 * D_MODEL] * y
    x1_ref[...] = x1
    h2 = (_rms(x1) * n2_ref[...]) * (1.0 + m[:, 4 * D_MODEL:5 * D_MODEL]) + m[:, 3 * D_MODEL:4 * D_MODEL]
    h2_ref[...] = _pack_rows(h2)
    sc_ref[...] = _sigmoid(_mm3(rw_ref[...], h2, NT))


def _post(xp, xs, proj, o_af, o_ab, o_df, o_db, mod3, w_o, nwa, nwb, n2w, router_wt, sample_len, tm):
    n_prompt_tok = xp.shape[0]
    n_tok = n_prompt_tok + xs.shape[0]
    cond = functools.partial(_cond_row, tiles_prompt=n_prompt_tok // tm, tiles_per_seq=sample_len // tm)
    tok = lambda i: (i, 0)
    const = lambda i: (0, 0)
    return pl.pallas_call(
        functools.partial(_post_kernel, tiles_prompt=n_prompt_tok // tm),
        grid=(n_tok // tm,),
        in_specs=_two_streams(n_prompt_tok, tm) + [
                  pl.BlockSpec((tm, VA), lambda i: (i, C_GA // VA)),
                  pl.BlockSpec((tm, VB), lambda i: (i, C_ZB // VB)),
                  pl.BlockSpec((tm, VA), tok), pl.BlockSpec((tm, VA), tok),
                  pl.BlockSpec((tm, VB), tok), pl.BlockSpec((tm, VB), tok),
                  pl.BlockSpec((1, 1, 6 * D_MODEL), lambda i: (cond(i), 0, 0)),
                  pl.BlockSpec((VA + VB, D_MODEL), const),
                  pl.BlockSpec((1, VA), const), pl.BlockSpec((1, VB), const), pl.BlockSpec((1, D_MODEL), const),
                  pl.BlockSpec((N_EXPERTS, D_MODEL), const)],
        out_specs=[pl.BlockSpec((tm, D_MODEL), tok),
                   pl.BlockSpec((tm, D_PACK), tok),
                   pl.BlockSpec((N_EXPERTS, tm), lambda i: (0, i))],
        out_shape=[jax.ShapeDtypeStruct((n_tok, D_MODEL), F32),
                   jax.ShapeDtypeStruct((n_tok, D_PACK), jnp.uint32),
                   jax.ShapeDtypeStruct((N_EXPERTS, n_tok), F32)],
        compiler_params=_cparams(("parallel",)),
        name="post",
    )(xp, xs, proj, proj, o_af, o_ab, o_df, o_db, mod3, w_o, nwa, nwb, n2w, router_wt)


def _route_kernel(sc_ref, bias_ref, e_ref, r_ref, w_ref, cnt_ref, carry_ref):
    i = pl.program_id(0)
    t = sc_ref.shape[1]

    @pl.when(i == 0)
    def _():
        carry_ref[...] = jnp.zeros(carry_ref.shape, F32)

    s = sc_ref[...]
    biased = s + bias_ref[:, 0:1]
    gs = []
    for g in range(N_GROUP):
        blk = biased[g * GROUP_SIZE:(g + 1) * GROUP_SIZE]
        m1 = jnp.max(blk, axis=0, keepdims=True)
        n1 = jnp.sum((blk == m1).astype(F32), axis=0, keepdims=True)
        m2 = jnp.max(jnp.where(blk < m1, blk, NEG_INF), axis=0, keepdims=True)
        gs.append(m1 + jnp.where(n1 >= 2.0, m1, m2))
    gsc = jnp.concatenate(gs, axis=0)
    gid = _iota2((N_GROUP, t), 0)
    beaten = jnp.zeros((N_GROUP, t), F32)
    for g in range(N_GROUP):
        other = gsc[g:g + 1, :]
        wins = jnp.logical_or(other > gsc, jnp.logical_and(other == gsc, g < gid))
        beaten = beaten + wins.astype(F32)
    masked = jnp.concatenate(
        [jnp.where(beaten[g:g + 1, :] < float(TOPK_GROUP), biased[g * GROUP_SIZE:(g + 1) * GROUP_SIZE], NEG_INF)
         for g in range(N_GROUP)], axis=0)
    eid = _iota2((N_EXPERTS, t), 0).astype(F32)
    sel = jnp.zeros((N_EXPERTS, t), F32)
    picks, scores = [], []
    for _ in range(TOP_K):
        m = jnp.max(masked, axis=0, keepdims=True)
        first = jnp.min(jnp.where(masked == m, eid, float(N_EXPERTS)), axis=0, keepdims=True)
        hit = eid == first
        scores.append(jnp.sum(jnp.where(hit, s, 0.0), axis=0, keepdims=True))
        masked = jnp.where(hit, NEG_INF, masked)
        sel = sel + hit.astype(F32)
        picks.append(first)
    upper = (_iota2((t, t), 0) < _iota2((t, t), 1)).astype(BF16)
    carry = carry_ref[...]
    prefix = _dg(sel.astype(BF16), upper, NN) + jnp.concatenate([carry] * (t // LANES), axis=1)
    ranks = [jnp.sum(jnp.where(eid == p, prefix, 0.0), axis=0, keepdims=True) for p in picks]
    carry = carry + _dg(sel.astype(BF16), jnp.ones((t, LANES), BF16), NN)
    carry_ref[...] = carry
    cnt_ref[...] = carry
    sc8 = jnp.concatenate(scores, axis=0)
    e_ref[...] = jnp.concatenate(picks, axis=0).astype(jnp.int32)
    r_ref[...] = jnp.concatenate(ranks, axis=0).astype(jnp.int32)
    w_ref[...] = sc8 / jnp.sum(sc8, axis=0, keepdims=True) * ROUTED_SCALE


def _route(scores_t, bias_col):
    n_tok = scores_t.shape[1]
    t = ROUTE_TILE
    slot = lambda i: (0, i)
    return pl.pallas_call(
        _route_kernel,
        grid=(n_tok // t,),
        in_specs=[pl.BlockSpec((N_EXPERTS, t), slot),
                  pl.BlockSpec((N_EXPERTS, LANES), lambda i: (0, 0))],
        out_specs=[pl.BlockSpec((TOP_K, t), slot), pl.BlockSpec((TOP_K, t), slot), pl.BlockSpec((TOP_K, t), slot),
                   pl.BlockSpec((N_EXPERTS, LANES), lambda i: (0, 0))],
        out_shape=[jax.ShapeDtypeStruct((TOP_K, n_tok), jnp.int32),
                   jax.ShapeDtypeStruct((TOP_K, n_tok), jnp.int32),
                   jax.ShapeDtypeStruct((TOP_K, n_tok), F32),
                   jax.ShapeDtypeStruct((N_EXPERTS, LANES), F32)],
        scratch_shapes=[pltpu.VMEM((N_EXPERTS, LANES), F32)],
        compiler_params=_cparams(("arbitrary",)),
        name="route",
    )(scores_t, bias_col)


def _dest_kernel(start_ref, e_ref, r_ref, d_ref):
    e = e_ref[...]

    def body(x, acc):
        return jnp.where(e == x, start_ref[x], acc)

    d_ref[...] = r_ref[...] + lax.fori_loop(0, N_EXPERTS, body, jnp.zeros(e.shape, jnp.int32))


def _dest(start, e8, r8, tt):
    n_tok = e8.shape[1]
    slot = lambda i, st: (0, i)
    return pl.pallas_call(
        _dest_kernel,
        grid_spec=pltpu.PrefetchScalarGridSpec(
            num_scalar_prefetch=1, grid=(n_tok // tt,),
            in_specs=[pl.BlockSpec((TOP_K, tt), slot), pl.BlockSpec((TOP_K, tt), slot)],
            out_specs=pl.BlockSpec((TOP_K, tt), slot)),
        out_shape=jax.ShapeDtypeStruct((TOP_K, n_tok), jnp.int32),
        compiler_params=_cparams(("parallel",)),
        name="dest",
    )(start, e8, r8)


def _items(counts, n_rows):
    n_blocks = n_rows // ROW_BLOCK
    max_items = n_blocks + N_EXPERTS - 1
    end = jnp.cumsum(counts)
    start = end - counts
    first_blk = start // ROW_BLOCK
    n_it = jnp.where(counts > 0, (end - 1) // ROW_BLOCK - first_blk + 1, 0)
    it_end = jnp.cumsum(n_it)
    it_start = it_end - n_it
    i = jnp.arange(max_items, dtype=jnp.int32)
    valid = i < it_end[-1]
    ex = jnp.minimum(jnp.sum((it_end[None, :] <= i[:, None]).astype(jnp.int32), axis=1), N_EXPERTS - 1)
    onehot = ex[:, None] == jnp.arange(N_EXPERTS, dtype=jnp.int32)[None, :]
    pick = lambda tab: jnp.sum(jnp.where(onehot, tab[None, :], 0), axis=1)
    blk = pick(first_blk) + (i - pick(it_start))
    lo = jnp.maximum(pick(start), blk * ROW_BLOCK) - blk * ROW_BLOCK
    hi = jnp.minimum(pick(end), (blk + 1) * ROW_BLOCK) - blk * ROW_BLOCK
    blk = jnp.where(valid, blk, n_blocks - 1).astype(jnp.int32)
    lo = jnp.where(valid, lo, 0).astype(jnp.int32)
    hi = jnp.where(valid, hi, 0).astype(jnp.int32)
    eids = jnp.arange(N_EXPERTS, dtype=jnp.int32)
    later = jnp.logical_and(eids[None, :] > eids[:, None], (counts > 0)[None, :])
    nxt_e = jnp.min(jnp.where(later, eids[None, :], N_EXPERTS), axis=1)
    nxt = pick(jnp.where(nxt_e < N_EXPERTS, nxt_e, -1)).astype(jnp.int32)
    return start.astype(jnp.int32), blk, ex.astype(jnp.int32), nxt, lo, hi


def _dispatch_kernel(h2_ref, dest_hbm, xs_hbm, dsm, sem_d, sem_s):
    i = pl.program_id(0)
    n = pl.num_programs(0)
    tm = h2_ref.shape[0]

    def dest_copy(step, slot):
        return pltpu.make_async_copy(dest_hbm.at[:, pl.ds(step * tm, tm)], dsm.at[slot], sem_d.at[slot])

    def row_copy(t, dst):
        return pltpu.make_async_copy(h2_ref.at[pl.ds(t, 1), :], xs_hbm.at[pl.ds(dst, 1), :], sem_s.at[0])

    @pl.when(i == 0)
    def _():
        dest_copy(0, 0).start()

    slot = i % 2
    dest_copy(i, slot).wait()

    @pl.when(i + 1 < n)
    def _():
        dest_copy(i + 1, 1 - slot).start()

    def issue(t, carry):
        for k in range(TOP_K):
            row_copy(t, dsm[slot, k, t]).start(priority=k % N_DMA_QUEUES)
        return carry
    lax.fori_loop(0, tm, issue, 0, unroll=2)

    def drain(t, carry):
        for k in range(TOP_K):
            row_copy(t, 0).wait()
        return carry
    lax.fori_loop(0, tm, drain, 0)


def _dispatch(h2, dest, tm):
    n_tok = h2.shape[0]
    return pl.pallas_call(
        _dispatch_kernel,
        grid=(n_tok // tm,),
        in_specs=[pl.BlockSpec((tm, D_PACK), lambda i: (i, 0)),
                  pl.BlockSpec(memory_space=pl.ANY)],
        out_specs=pl.BlockSpec(memory_space=pl.ANY),
        out_shape=jax.ShapeDtypeStruct((n_tok * TOP_K, D_PACK), jnp.uint32),
        scratch_shapes=[pltpu.SMEM((2, TOP_K, tm), jnp.int32),
                        pltpu.SemaphoreType.DMA((2,)),
                        pltpu.SemaphoreType.DMA((1,))],
        compiler_params=_cparams(("arbitrary",)),
        name="dispatch",
    )(h2, dest)


def _experts_kernel(blk_ref, exp_ref, nxt_ref, lo_ref, hi_ref, x_ref, wg_hbm, wu_hbm, wd_hbm, y_ref,
                    wgf, wuf, wdf, wgb, wub, wdb, nchg, sem_w):
    i = pl.program_id(0)
    lo, hi = lo_ref[i], hi_ref[i]

    def weight_copies(e, slot):
        return (pltpu.make_async_copy(wg_hbm.at[e], wgf.at[slot], sem_w.at[slot]),
                pltpu.make_async_copy(wu_hbm.at[e], wuf.at[slot], sem_w.at[slot]),
                pltpu.make_async_copy(wd_hbm.at[e], wdf.at[slot], sem_w.at[slot]))

    @pl.when(i == 0)
    def _():
        nchg[0] = 0
        for cp in weight_copies(exp_ref[0], 0):
            cp.start()

    @pl.when(hi > lo)
    def _():
        @pl.when(jnp.logical_or(i == 0, exp_ref[i] != exp_ref[jnp.maximum(i - 1, 0)]))
        def _():
            slot = nchg[0] % 2
            for cp in weight_copies(exp_ref[i], slot):
                cp.wait()

            @pl.when(nxt_ref[i] >= 0)
            def _():
                for cp in weight_copies(nxt_ref[i], 1 - slot):
                    cp.start()

            wgb[...] = wgf[slot].astype(BF16)
            wub[...] = wuf[slot].astype(BF16)
            wdb[...] = wdf[slot].astype(BF16)
            nchg[0] = nchg[0] + 1

        x_lo, x_hi = _unpack_rows(x_ref[...])
        x = jnp.concatenate([x_lo.astype(BF16), x_hi.astype(BF16)], axis=1)
        g = _dg(x, wgb[...], NN)
        u = _dg(x, wub[...], NN)
        hmid = (_silu(g) * u).astype(BF16)
        y = _pack_rows(_dg(hmid, wdb[...], NN))
        row = _iota2((ROW_BLOCK, D_PACK), 0)
        mine = jnp.logical_and(row >= lo, row < hi)

        @pl.when(lo == 0)
        def _():
            y_ref[...] = jnp.where(mine, y, jnp.uint32(0))

        @pl.when(lo > 0)
        def _():
            y_ref[...] = jnp.where(mine, y, y_ref[...])


def _experts(item_blk, item_exp, item_nxt, item_lo, item_hi, xs, w_gate, w_up, w_down):
    n_items = item_blk.shape[0]
    xmap = lambda i, blk, ex, nxt, lo, hi: (blk[i], 0)
    return pl.pallas_call(
        _experts_kernel,
        grid_spec=pltpu.PrefetchScalarGridSpec(
            num_scalar_prefetch=5, grid=(n_items,),
            in_specs=[pl.BlockSpec((ROW_BLOCK, D_PACK), xmap),
                      pl.BlockSpec(memory_space=pl.ANY),
                      pl.BlockSpec(memory_space=pl.ANY),
                      pl.BlockSpec(memory_space=pl.ANY)],
            out_specs=pl.BlockSpec((ROW_BLOCK, D_PACK), xmap),
            scratch_shapes=[pltpu.VMEM((2, D_MODEL, D_EXPERT), F32), pltpu.VMEM((2, D_MODEL, D_EXPERT), F32),
                            pltpu.VMEM((2, D_EXPERT, D_MODEL), F32),
                            pltpu.VMEM((D_MODEL, D_EXPERT), BF16), pltpu.VMEM((D_MODEL, D_EXPERT), BF16),
                            pltpu.VMEM((D_EXPERT, D_MODEL), BF16),
                            pltpu.SMEM((1,), jnp.int32),
                            pltpu.SemaphoreType.DMA((2,))]),
        out_shape=jax.ShapeDtypeStruct(xs.shape, jnp.uint32),
        compiler_params=_cparams(("arbitrary",)),
        name="experts",
    )(item_blk, item_exp, item_nxt, item_lo, item_hi, xs, w_gate, w_up, w_down)


def _final_kernel(x1_ref, h2_ref, w_ref, mod_ref, sg_ref, su_ref, sd_ref, fn_ref, dest_hbm, ys_hbm, o_ref,
                  dsm, gbuf, sem_d, sem_g):
    i = pl.program_id(0)
    n = pl.num_programs(0)
    tm = x1_ref.shape[0]

    def dest_copy(step, slot):
        return pltpu.make_async_copy(dest_hbm.at[:, pl.ds(step * tm, tm)], dsm.at[slot], sem_d.at[slot])

    def row_copy(src, slot, k, t):
        return pltpu.make_async_copy(ys_hbm.at[pl.ds(src, 1), :], gbuf.at[slot, k, pl.ds(t, 1), :], sem_g.at[slot])

    def issue_gathers(slot3, slot2):
        def body(t, carry):
            for k in range(TOP_K):
                row_copy(dsm[slot3, k, t], slot2, k, t).start(priority=k % N_DMA_QUEUES)
            return carry
        lax.fori_loop(0, tm, body, 0, unroll=2)

    @pl.when(i == 0)
    def _():
        dest_copy(0, 0).start()
        dest_copy(0, 0).wait()
        issue_gathers(0, 0)

        @pl.when(n > 1)
        def _():
            dest_copy(1, 1).start()

    @pl.when(i + 1 < n)
    def _():
        dest_copy(i + 1, (i + 1) % 3).wait()
        issue_gathers((i + 1) % 3, (i + 1) % 2)

        @pl.when(i + 2 < n)
        def _():
            dest_copy(i + 2, (i + 2) % 3).start()

    slot = i % 2

    def drain(t, carry):
        for k in range(TOP_K):
            row_copy(0, slot, k, t).wait()
        return carry
    lax.fori_loop(0, tm, drain, 0)

    m = mod_ref[0]
    w = w_ref[...]
    r_lo = r_hi = None
    for kk in range(TOP_K):
        y_lo, y_hi = _unpack_rows(gbuf[slot, kk])
        wk = w[:, kk:kk + 1]
        r_lo = y_lo * wk if r_lo is None else r_lo + y_lo * wk
        r_hi = y_hi * wk if r_hi is None else r_hi + y_hi * wk
    routed = jnp.concatenate([r_lo, r_hi], axis=1)
    h_lo, h_hi = _unpack_rows(h2_ref[...])
    hb = jnp.concatenate([h_lo.astype(BF16), h_hi.astype(BF16)], axis=1)
    hm = (_silu(_dg(hb, sg_ref[...], NN)) * _dg(hb, su_ref[...], NN)).astype(BF16)
    shared = _dg(hm, sd_ref[...], NN)
    x2 = x1_ref[...] + m[:, 5 * D_MODEL:6 * D_MODEL] * (routed + shared)
    o_ref[...] = _rms(x2) * fn_ref[...]


def _final(x1, h2, wts, mod3, sg, su, sd, fnw, dest, ys, n_prompt_tok, sample_len, tm):
    n_tok = x1.shape[0]
    cond = functools.partial(_cond_row, tiles_prompt=n_prompt_tok // tm, tiles_per_seq=sample_len // tm)
    tok = lambda i: (i, 0)
    const = lambda i: (0, 0)
    return pl.pallas_call(
        _final_kernel,
        grid=(n_tok // tm,),
        in_specs=[pl.BlockSpec((tm, D_MODEL), tok),
                  pl.BlockSpec((tm, D_PACK), tok),
                  pl.BlockSpec((tm, TOP_K), tok),
                  pl.BlockSpec((1, 1, 6 * D_MODEL), lambda i: (cond(i), 0, 0)),
                  pl.BlockSpec((D_MODEL, D_EXPERT), const),
                  pl.BlockSpec((D_MODEL, D_EXPERT), const),
                  pl.BlockSpec((D_EXPERT, D_MODEL), const),
                  pl.BlockSpec((1, D_MODEL), const),
                  pl.BlockSpec(memory_space=pl.ANY),
                  pl.BlockSpec(memory_space=pl.ANY)],
        out_specs=pl.BlockSpec((tm, D_MODEL), tok),
        out_shape=jax.ShapeDtypeStruct((n_tok, D_MODEL), F32),
        scratch_shapes=[pltpu.SMEM((3, TOP_K, tm), jnp.int32),
                        pltpu.VMEM((2, TOP_K, tm, D_PACK), jnp.uint32),
                        pltpu.SemaphoreType.DMA((3,)),
                        pltpu.SemaphoreType.DMA((2,))],
        compiler_params=_cparams(("arbitrary",)),
        name="final",
    )(x1, h2, wts, mod3, sg, su, sd, fnw, dest, ys)


def kernel(x_prompt, x_sample, state_gla, state_gdn, c, c_ctx, w_ada, b_ada, norm1_w, w_in, conv_w, gla_lr_w, gla_lr_b, gdn_a_log, gdn_dt_bias, gla_norm_w, gdn_norm_w, w_o, norm2_w, router_w, router_bias, exp_w_gate, exp_w_up, exp_w_down, sh_w_gate, sh_w_up, sh_w_down, final_norm_w):
    bp, lp, d = x_prompt.shape
    bs, ls, _ = x_sample.shape
    assert d == D_MODEL and lp == UNIT and ls % UNIT == 0 and w_ada.shape[0] == 1
    n_prompt_tok = bp * lp
    n_tok = n_prompt_tok + bs * ls
    n_prompt_units = n_prompt_tok // UNIT
    units_per_seq = ls // UNIT
    grid_w = 64
    layer = 0
    tm_in = 512
    tm = 256
    assert n_prompt_tok % tm_in == 0 and ls % tm_in == 0 and n_tok % ROUTE_TILE == 0

    xp = x_prompt.reshape(n_prompt_tok, d)
    xs_in = x_sample.reshape(bs * ls, d)
    cond = jnp.concatenate([c_ctx[None, :], c, jnp.zeros((SUBLANES - 1 - bs, d), F32)], axis=0)
    mod3 = _ada(cond, w_ada[layer], b_ada[layer][None, :]).reshape(SUBLANES, 1, 6 * d)

    wi = w_in[layer]
    o_lr = 2 * QA + 2 * VA
    o_qkvb = o_lr + N_DIR * GLA_RANK
    o_zb = o_qkvb + 3 * QB
    o_ab = o_zb + VB
    w_main = jnp.concatenate([wi[:, 0:o_lr], wi[:, o_qkvb:o_ab]], axis=1).astype(BF16)
    w_small = jnp.concatenate([wi[:, o_lr:o_qkvb], wi[:, o_ab:], jnp.zeros((d, LANES - 48), F32)], axis=1)
    cw = jnp.concatenate([conv_w[layer], jnp.zeros((SUBLANES - CONV_K, 3 * QB), F32)], axis=0)
    proj = _inproj(xp, xs_in, mod3, norm1_w[layer][None, :], w_main, w_small, cw, ls, lp, grid_w, tm_in)

    wlr = jnp.zeros((N_DIR, LANES, QA), F32)
    for dd in range(N_DIR):
        wlr = wlr.at[dd, dd * GLA_RANK:(dd + 1) * GLA_RANK, :].set(gla_lr_w[layer, dd])
    blr = gla_lr_b[layer][:, None, :]
    init_gla = jnp.concatenate([jnp.zeros((1,) + state_gla.shape[2:], F32), state_gla[:, layer].astype(F32)], axis=0)
    o_af, o_ab_, s_gla = _gla(proj, wlr, blr, init_gla, n_prompt_units, units_per_seq)

    gcoef = jnp.zeros((1, LANES), F32).at[0, SM_AB:SM_AB + N_DIR * H_B].set(-jnp.exp(gdn_a_log[layer].reshape(-1)))
    gdt = jnp.zeros((1, LANES), F32).at[0, SM_AB:SM_AB + N_DIR * H_B].set(gdn_dt_bias[layer].reshape(-1))
    init_gdn = jnp.concatenate([jnp.zeros((1, N_DIR, H_B * DK_B, DV_B), F32),
                                state_gdn[:, layer].astype(F32).reshape(bs, N_DIR, H_B * DK_B, DV_B)], axis=0)
    o_df, o_db, s_gdn = _gdn(proj, gcoef, gdt, init_gdn, n_prompt_units, units_per_seq)

    nwa = jnp.tile(gla_norm_w[layer], H_A)[None, :]
    nwb = jnp.tile(gdn_norm_w[layer], H_B)[None, :]
    x1, h2, scores_t = _post(xp, xs_in, proj, o_af, o_ab_, o_df, o_db, mod3, w_o[layer].astype(BF16), nwa, nwb,
                             norm2_w[layer][None, :], router_w[layer].T, ls, tm)

    bias_col = jnp.broadcast_to(router_bias[layer].astype(F32)[:, None], (N_EXPERTS, LANES))
    e8, r8, w8, cnt = _route(scores_t, bias_col)
    start, item_blk, item_exp, item_nxt, item_lo, item_hi = _items(cnt[:, 0].astype(jnp.int32), n_tok * TOP_K)
    dest = _dest(start, e8, r8, tt=min(2048, n_tok))

    xs = _dispatch(h2, dest, tm)
    ys = _experts(item_blk, item_exp, item_nxt, item_lo, item_hi, xs, exp_w_gate[layer], exp_w_up[layer],
                  exp_w_down[layer])
    y_all = _final(x1, h2, w8.T, mod3, sh_w_gate[layer].astype(BF16), sh_w_up[layer].astype(BF16),
                   sh_w_down[layer].astype(BF16), final_norm_w[None, :], dest, ys, n_prompt_tok, ls, tm)

    y_prompt = y_all[:n_prompt_tok].reshape(bp, lp, d)
    y_sample = y_all[n_prompt_tok:].reshape(bs, ls, d)
    new_state_gla = s_gla[:n_prompt_units].reshape(bp, 1, N_DIR, H_A, DK_A, DV_A).astype(x_prompt.dtype)
    new_state_gdn = s_gdn[:n_prompt_units].reshape(bp, 1, N_DIR, H_B, DK_B, DV_B).astype(x_prompt.dtype)
    return (y_prompt, y_sample, new_state_gla, new_state_gdn)
```

```python
import functools

import jax
import jax.numpy as jnp
from jax import lax
from jax.experimental import pallas as pl
from jax.experimental.pallas import tpu as pltpu

F32 = jnp.float32
BF16 = jnp.bfloat16

D_MODEL = 1024
N_DIR = 2
H_A, DK_A, DV_A = 4, 64, 128
GLA_RANK = 16
GLA_NORMALIZER = 16.0
H_B, DK_B, DV_B = 4, 128, 128
CONV_K = 5
CHUNK = 64
QA, VA = H_A * DK_A, H_A * DV_A
QB, VB = H_B * DK_B, H_B * DV_B
N_EXPERTS = 256
TOP_K = 8
N_GROUP = 8
TOPK_GROUP = 4
GROUP_SIZE = N_EXPERTS // N_GROUP
D_EXPERT = 256
ROUTED_SCALE = 2.5
EPS = 1e-6
NEG_INF = float("-inf")

UNIT = 256
CHUNKS_PER_UNIT = UNIT // CHUNK
HC = H_B * CHUNK
GDN_GROUP = 2
HG = GDN_GROUP * CHUNK
INV_BLOCK = 8
D_PACK = D_MODEL // 2
ROUTE_TILE = 256
ROW_BLOCK = 512
N_DMA_QUEUES = 2
LANES = 128
SUBLANES = 8
VMEM_LIMIT = 56 * 1024 * 1024

C_QKVA = 0
C_GA = 1024
C_QKVB = 1536
C_ZB = 3072
C_SMALL = 3584
D_PROJ = 3712
SM_AB = 32
SM_BETA = 40


def _cparams(sem):
    return pltpu.CompilerParams(dimension_semantics=sem, vmem_limit_bytes=VMEM_LIMIT)


def _split(a):
    hi = a.astype(BF16)
    lo = (a - hi.astype(F32)).astype(BF16)
    return hi, lo


def _dg(a, b, dims):
    return lax.dot_general(a, b, (dims, ((), ())), preferred_element_type=F32)


NN = ((1,), (0,))
NT = ((1,), (1,))
TN = ((0,), (0,))


def _mm(a, b, dims=NN):
    return _dg(a.astype(BF16), b.astype(BF16), dims)


def _mm3(a, b, dims=NN):
    ah, al = _split(a)
    bh, bl = _split(b)
    return _dg(ah, bh, dims) + (_dg(ah, bl, dims) + _dg(al, bh, dims))


def _mm_exact_lhs(a_bf16, b, dims=NN):
    bh, bl = _split(b)
    return _dg(a_bf16, bh, dims) + _dg(a_bf16, bl, dims)


def _silu(x):
    return x * (1.0 / (1.0 + jnp.exp(-x)))


def _sigmoid(x):
    return 1.0 / (1.0 + jnp.exp(-x))


def _softplus(x):
    return jnp.maximum(x, 0.0) + jnp.log1p(jnp.exp(-jnp.abs(x)))


def _log_sigmoid(x):
    return -_softplus(-x)


def _iota2(shape, dim):
    return lax.broadcasted_iota(jnp.int32, shape, dim)


def _rms(x):
    return x * lax.rsqrt(jnp.mean(x * x, axis=-1, keepdims=True) + EPS)


def _pack_rows(x):
    lo = lax.bitcast_convert_type(x[:, 0:D_PACK].astype(BF16).astype(F32), jnp.uint32)
    hi = lax.bitcast_convert_type(x[:, D_PACK:D_MODEL].astype(BF16).astype(F32), jnp.uint32)
    return (lo >> 16) | (hi & jnp.uint32(0xFFFF0000))


def _unpack_rows(p):
    lo = lax.bitcast_convert_type(p << 16, F32)
    hi = lax.bitcast_convert_type(p & jnp.uint32(0xFFFF0000), F32)
    return lo, hi


def _ada_kernel(c_ref, w_ref, b_ref, o_ref):
    o_ref[...] = _mm3(_silu(c_ref[...]), w_ref[...]) + b_ref[...]


def _ada(cond, w, b):
    n = w.shape[1]
    tn = 1536
    return pl.pallas_call(
        _ada_kernel,
        grid=(n // tn,),
        in_specs=[pl.BlockSpec((SUBLANES, D_MODEL), lambda i: (0, 0)),
                  pl.BlockSpec((D_MODEL, tn), lambda i: (0, i)),
                  pl.BlockSpec((1, tn), lambda i: (0, i))],
        out_specs=pl.BlockSpec((SUBLANES, tn), lambda i: (0, i)),
        out_shape=jax.ShapeDtypeStruct((SUBLANES, n), F32),
        compiler_params=_cparams(("parallel",)),
        name="ada",
    )(cond, w, b)


def _two_streams(n_prompt_tok, tm):
    tp = n_prompt_tok // tm
    return [pl.BlockSpec((tm, D_MODEL), lambda i: (jnp.minimum(i, tp - 1), 0)),
            pl.BlockSpec((tm, D_MODEL), lambda i: (jnp.maximum(i - tp, 0), 0))]


def _inproj_kernel(xp_ref, xs_ref, mod_ref, nw_ref, w_ref, ws_ref, cw_ref, o_ref, *, tiles_prompt, prompt_row,
                   sample_row):
    m = mod_ref[0]
    x = jnp.where(pl.program_id(0) < tiles_prompt, xp_ref[...], xs_ref[...])
    h = (_rms(x) * nw_ref[...]) * (1.0 + m[:, D_MODEL:2 * D_MODEL]) + m[:, 0:D_MODEL]
    hb = h.astype(BF16)
    for c0 in range(0, C_SMALL, 512):
        o_ref[:, c0:c0 + 512] = _dg(hb, w_ref[:, c0:c0 + 512], NN)
    o_ref[:, C_SMALL:D_PROJ] = _mm3(h, ws_ref[...])
    tm = xp_ref.shape[0]
    row_len = jnp.where(pl.program_id(0) < tiles_prompt, prompt_row, sample_row)
    pos = _iota2((tm, LANES), 0) & (row_len - 1)
    inside = [jnp.logical_and(pos + (jj - CONV_K // 2) >= 0, pos + (jj - CONV_K // 2) < row_len).astype(F32)
              for jj in range(CONV_K)]
    for part in range(3 * H_B):
        cs = slice(C_QKVB + part * LANES, C_QKVB + (part + 1) * LANES)
        x = o_ref[:, cs]
        acc = x * cw_ref[CONV_K // 2:CONV_K // 2 + 1, part * LANES:(part + 1) * LANES]
        for jj in range(CONV_K):
            off = jj - CONV_K // 2
            if off == 0:
                continue
            xs = pltpu.roll(x, (-off) % tm, 0) * inside[jj]
            acc = acc + xs * cw_ref[jj:jj + 1, part * LANES:(part + 1) * LANES]
        y = _silu(acc)
        if part < 2 * H_B:
            y = y * lax.rsqrt(jnp.sum(y * y, axis=-1, keepdims=True) + EPS)
            if part < H_B:
                y = y * (DK_B ** -0.5)
        o_ref[:, cs] = y


def _cond_row(i, tiles_prompt, tiles_per_seq):
    return jnp.where(i < tiles_prompt, 0, 1 + (i - tiles_prompt) // tiles_per_seq)


def _inproj(xp, xs, mod3, norm_w, w_main, w_small, conv_w, sample_len, prompt_row, sample_row, tm):
    n_prompt_tok = xp.shape[0]
    n_tok = n_prompt_tok + xs.shape[0]
    assert tm % prompt_row == 0 and tm % sample_row == 0
    cond = functools.partial(_cond_row, tiles_prompt=n_prompt_tok // tm, tiles_per_seq=sample_len // tm)
    kern = functools.partial(_inproj_kernel, tiles_prompt=n_prompt_tok // tm, prompt_row=prompt_row,
                             sample_row=sample_row)
    return pl.pallas_call(
        kern,
        grid=(n_tok // tm,),
        in_specs=_two_streams(n_prompt_tok, tm) + [
                  pl.BlockSpec((1, 1, 6 * D_MODEL), lambda i: (cond(i), 0, 0)),
                  pl.BlockSpec((1, D_MODEL), lambda i: (0, 0)),
                  pl.BlockSpec((D_MODEL, C_SMALL), lambda i: (0, 0)),
                  pl.BlockSpec((D_MODEL, LANES), lambda i: (0, 0)),
                  pl.BlockSpec((SUBLANES, 3 * QB), lambda i: (0, 0))],
        out_specs=pl.BlockSpec((tm, D_PROJ), lambda i: (i, 0)),
        out_shape=jax.ShapeDtypeStruct((n_tok, D_PROJ), F32),
        compiler_params=_cparams(("parallel",)),
        name="inproj",
    )(xp, xs, mod3, norm_w, w_main, w_small, conv_w)


def _unit_ids(j, n_prompt_units, units_per_seq):
    jj = j - n_prompt_units
    b = jj // units_per_seq
    r = jj % units_per_seq
    is_prompt = j < n_prompt_units
    uf = j
    ub = jnp.where(is_prompt, j, n_prompt_units + b * units_per_seq + (units_per_seq - 1 - r))
    init_row = jnp.where(is_prompt, 0, 1 + b)
    first = jnp.logical_or(is_prompt, r == 0)
    return uf, ub, init_row, first


def _tri(rev):
    t = _iota2((CHUNK, CHUNK), 0)
    s = _iota2((CHUNK, CHUNK), 1)
    return (t <= s) if rev else (t >= s)


def _mm_exact_lhs_tn(a, ones_bf16):
    ah, al = _split(a)
    return _dg(ah, ones_bf16, TN) + _dg(al, ones_bf16, TN)


def _gla_unit(chunks, order, states, rev):
    tri = _tri(rev).astype(BF16)
    mid, last = (CHUNK // 2 - 1, 0) if rev else (CHUNK // 2, CHUNK - 1)
    scale = DK_A ** -0.5
    ones = jnp.ones((CHUNK, LANES), BF16)
    lane = _iota2((CHUNK, LANES), 1)
    row = _iota2((CHUNK, LANES), 0)
    s_in = lane % DK_A
    causal = (row <= s_in) if rev else (row >= s_in)
    zeros_v = jnp.zeros((CHUNK, DV_A), F32)
    pairs = range(H_A // 2)
    att, qs_l, u_l, dec_l = {}, {}, {}, {}
    for c, (q, k, v, la) in enumerate(chunks):
        b = _mm_exact_lhs(tri, la)
        bref = b[mid:mid + 1, :]
        blast = b[last:last + 1, :]
        qg = q * jnp.exp(b - bref) * scale
        kg = k * jnp.exp(bref - b)
        kd = k * jnp.exp(blast - b)
        qs_l[c] = q * jnp.exp(b) * scale
        for p in pairs:
            ls = slice(p * LANES, (p + 1) * LANES)
            kg_p = kg[:, ls]
            rhs_att = jnp.concatenate([jnp.where(lane < DK_A, kg_p, 0.0), jnp.where(lane >= DK_A, kg_p, 0.0)], axis=0)
            att[c, p] = jnp.where(causal, _mm(qg[:, ls], rhs_att, NT), 0.0)
            u_l[c, p] = _mm(kd[:, ls], v[:, 2 * p * DV_A:(2 * p + 2) * DV_A], TN)
            dec_l[c, p] = jnp.exp(_mm_exact_lhs_tn(la[:, ls], ones))
    start = {}
    for c in order:
        start[c] = list(states)
        nxt = []
        for p in pairs:
            u, dec = u_l[c, p], dec_l[c, p]
            nxt.append(dec[0:DK_A] * states[2 * p] + u[0:DK_A, 0:DV_A])
            nxt.append(dec[DK_A:2 * DK_A] * states[2 * p + 1] + u[DK_A:2 * DK_A, DV_A:2 * DV_A])
        states = nxt
    outs = []
    for c, (q, k, v, la) in enumerate(chunks):
        o = []
        for p in pairs:
            ls = slice(p * LANES, (p + 1) * LANES)
            vs0 = v[:, (2 * p) * DV_A:(2 * p + 1) * DV_A]
            vs1 = v[:, (2 * p + 1) * DV_A:(2 * p + 2) * DV_A]
            s0, s1 = start[c][2 * p], start[c][2 * p + 1]
            rhs_o = jnp.concatenate([jnp.concatenate([vs0, zeros_v], axis=1),
                                     jnp.concatenate([zeros_v, vs1], axis=1),
                                     jnp.concatenate([s0, zeros_v], axis=1),
                                     jnp.concatenate([zeros_v, s1], axis=1)], axis=0)
            lhs_o = jnp.concatenate([att[c, p], qs_l[c][:, ls]], axis=1)
            o.append(_mm(lhs_o, rhs_o))
        outs.append(jnp.concatenate(o, axis=1))
    return outs, states


def _gla_kernel(qf_ref, qb_ref, sf_ref, sb_ref, wlr_ref, blr_ref, init_ref, of_ref, ob_ref, so_ref, s_ref,
                *, n_prompt_units, units_per_seq):
    j = pl.program_id(0)
    _, _, _, first = _unit_ids(j, n_prompt_units, units_per_seq)

    @pl.when(first)
    def _():
        s_ref[...] = init_ref[0]

    for d in range(N_DIR):
        x_ref, sm_ref, o_ref = (qf_ref, sf_ref, of_ref) if d == 0 else (qb_ref, sb_ref, ob_ref)
        la_unit = _log_sigmoid(_mm3(sm_ref[...], wlr_ref[d]) + blr_ref[d]) * (1.0 / GLA_NORMALIZER)
        states = [s_ref[d, h] for h in range(H_A)]
        order = range(CHUNKS_PER_UNIT - 1, -1, -1) if d == 1 else range(CHUNKS_PER_UNIT)
        chunks = []
        for c in range(CHUNKS_PER_UNIT):
            rows = slice(c * CHUNK, (c + 1) * CHUNK)
            chunks.append((x_ref[rows, 0:QA], x_ref[rows, QA:2 * QA], x_ref[rows, 2 * QA:2 * QA + VA], la_unit[rows]))
        outs, states = _gla_unit(chunks, order, states, rev=(d == 1))
        for c in range(CHUNKS_PER_UNIT):
            o_ref[c * CHUNK:(c + 1) * CHUNK, :] = outs[c]
        for h in range(H_A):
            s_ref[d, h] = states[h]
    so_ref[0] = s_ref[...]


def _gla(proj, wlr, blr, init, n_prompt_units, units_per_seq):
    n_tok = proj.shape[0]
    n_units = n_tok // UNIT
    ids = functools.partial(_unit_ids, n_prompt_units=n_prompt_units, units_per_seq=units_per_seq)
    small_blk = C_SMALL // LANES
    st_blk = (1, N_DIR, H_A, DK_A, DV_A)
    kern = functools.partial(_gla_kernel, n_prompt_units=n_prompt_units, units_per_seq=units_per_seq)
    return pl.pallas_call(
        kern,
        grid=(n_units,),
        in_specs=[pl.BlockSpec((UNIT, 1024), lambda j: (ids(j)[0], 0)),
                  pl.BlockSpec((UNIT, 1024), lambda j: (ids(j)[1], 0)),
                  pl.BlockSpec((UNIT, LANES), lambda j: (ids(j)[0], small_blk)),
                  pl.BlockSpec((UNIT, LANES), lambda j: (ids(j)[1], small_blk)),
                  pl.BlockSpec((N_DIR, LANES, QA), lambda j: (0, 0, 0)),
                  pl.BlockSpec((N_DIR, 1, QA), lambda j: (0, 0, 0)),
                  pl.BlockSpec(st_blk, lambda j: (ids(j)[2], 0, 0, 0, 0))],
        out_specs=[pl.BlockSpec((UNIT, VA), lambda j: (ids(j)[0], 0)),
                   pl.BlockSpec((UNIT, VA), lambda j: (ids(j)[1], 0)),
                   pl.BlockSpec(st_blk, lambda j: (jnp.minimum(j, n_prompt_units), 0, 0, 0, 0))],
        out_shape=[jax.ShapeDtypeStruct((n_tok, VA), F32),
                   jax.ShapeDtypeStruct((n_tok, VA), F32),
                   jax.ShapeDtypeStruct((n_prompt_units + 1, N_DIR, H_A, DK_A, DV_A), F32)],
        scratch_shapes=[pltpu.VMEM((N_DIR, H_A, DK_A, DV_A), F32)],
        compiler_params=_cparams(("arbitrary",)),
        name="gla",
    )(proj, proj, proj, proj, wlr, blr, init)


def _stack_masks(rev):
    r = _iota2((HG, HG), 0)
    c = _iota2((HG, HG), 1)
    same = (r // CHUNK) == (c // CHUNK)
    tr, tc = r % CHUNK, c % CHUNK
    incl = jnp.logical_and(same, (tr <= tc) if rev else (tr >= tc))
    strict = jnp.logical_and(same, (tr < tc) if rev else (tr > tc))
    return incl, strict


def _spread(x):
    z = jnp.zeros((CHUNK, LANES), x.dtype)
    rows = []
    for h in range(H_B):
        xh = x[h * CHUNK:(h + 1) * CHUNK]
        rows.append(jnp.concatenate([xh if g == h else z for g in range(H_B)], axis=1))
    return jnp.concatenate(rows, axis=0)


def _gdn_prepare(chunks):
    masks = {rev: _stack_masks(rev) for rev in {ch[6] for ch in chunks}}
    r = _iota2((HG, HG), 0)
    c = _iota2((HG, HG), 1)
    a_l, att_l, kb_l = [], [], []
    for q, k, v, gcb, beta, glast, rev in chunks:
        incl, strict = masks[rev]
        grow = gcb.T[0:1, :]
        diff = gcb[:, 0:1] - grow
        decay = jnp.where(incl, jnp.exp(jnp.where(incl, diff, 0.0)), 0.0)
        kb = k * beta
        m1 = _mm(jnp.concatenate([kb, q], axis=0), k, NT)
        a_l.append(jnp.where(strict, m1[0:HG] * decay, 0.0))
        att_l.append(m1[HG:2 * HG] * decay)
        kb_l.append(kb)
    diag = (r // INV_BLOCK) == (c // INV_BLOCK)
    pw_l = [jnp.where(diag, a, 0.0) for a in a_l]
    qm_l = [-p for p in pw_l]
    n = 2
    while n < INV_BLOCK:
        pw_l = [_mm(p, p) for p in pw_l]
        qm_l = [qm + p + _mm(qm, p) for qm, p in zip(qm_l, pw_l)]
        n *= 2
    b = INV_BLOCK
    while b < CHUNK:
        off = jnp.logical_and((r // (2 * b)) == (c // (2 * b)), (r // b) != (c // b))
        al_l = [jnp.where(off, a, 0.0) for a in a_l]
        t1_l = [al + _mm(qm, al) for qm, al in zip(qm_l, al_l)]
        qm_l = [qm - (t1 + _mm(t1, qm)) for qm, t1 in zip(qm_l, t1_l)]
        b *= 2
    out = []
    for (q, k, v, gcb, beta, glast, _), qm, kb, att in zip(chunks, qm_l, kb_l, att_l):
        egc = jnp.exp(gcb)
        rhs = jnp.concatenate([v * beta, kb * egc], axis=1)
        sol = rhs + _mm(qm, rhs)
        out.append((sol[:, 0:DV_B], sol[:, DV_B:2 * DV_B], q * egc, k * jnp.exp(glast - gcb), att))
    return out


def _gdn_scan_step(value, k_cum, q_dec, k_dec, att, gl_rows, s):
    kq = _mm(jnp.concatenate([_spread(k_cum), _spread(q_dec)], axis=0), s)
    v_new = value - kq[0:HC]
    o = kq[HC:2 * HC] + _mm(att, v_new)
    s_new = s * gl_rows + _mm(_spread(k_dec), v_new, TN)
    return o, s_new


def _gdn_kernel(xf_ref, xb_ref, sf_ref, sb_ref, gco_ref, gdt_ref, init_ref, of_ref, ob_ref, so_ref, s_ref,
                *, n_prompt_units, units_per_seq):
    j = pl.program_id(0)
    _, _, _, first = _unit_ids(j, n_prompt_units, units_per_seq)

    @pl.when(first)
    def _():
        s_ref[...] = init_ref[0]

    chunks, gl_rows = [], []
    for d in range(N_DIR):
        x_ref, sm_ref = (xf_ref, sf_ref) if d == 0 else (xb_ref, sb_ref)
        rev = d == 1
        sm = sm_ref[...]
        g_all = gco_ref[...] * _softplus(sm + gdt_ref[...])
        beta_all = _sigmoid(sm)
        tri = _tri(rev).astype(BF16)
        last = 0 if rev else CHUNK - 1
        for c in range(CHUNKS_PER_UNIT):
            rows = slice(c * CHUNK, (c + 1) * CHUNK)
            gc_all = _mm_exact_lhs(tri, g_all[rows])
            gcb, beta, glast = [], [], []
            for h in range(H_B):
                col = SM_AB + d * H_B + h
                colb = SM_BETA + d * H_B + h
                gh = jnp.broadcast_to(gc_all[:, col:col + 1], (CHUNK, LANES))
                gcb.append(gh)
                glast.append(jnp.broadcast_to(gh[last:last + 1, :], (CHUNK, LANES)))
                beta.append(jnp.broadcast_to(beta_all[rows, colb:colb + 1], (CHUNK, LANES)))
            for g0 in range(0, H_B, GDN_GROUP):
                hs = range(g0, g0 + GDN_GROUP)
                stack = lambda base: jnp.concatenate(
                    [x_ref[rows, base + h * LANES:base + (h + 1) * LANES] for h in hs], axis=0)
                cat = lambda parts: jnp.concatenate([parts[h] for h in hs], axis=0)
                chunks.append((stack(0), stack(QB), stack(2 * QB), cat(gcb), cat(beta), cat(glast), rev))
            gl_rows.append(jnp.concatenate([jnp.broadcast_to(jnp.exp(g[0:1, :]), (DK_B, DV_B)) for g in glast],
                                           axis=0))
    n_grp = H_B // GDN_GROUP
    zero_att = jnp.zeros((HG, HG), F32)
    prepared = []
    grouped = _gdn_prepare(chunks)
    for i in range(0, len(grouped), n_grp):
        parts = grouped[i:i + n_grp]
        rows_cat = [jnp.concatenate([p[f] for p in parts], axis=0) for f in range(4)]
        att = jnp.concatenate([jnp.concatenate([parts[g][4] if gg == g else zero_att for gg in range(n_grp)], axis=1)
                               for g in range(n_grp)], axis=0)
        prepared.append(tuple(rows_cat) + (att,))
    s = [s_ref[d] for d in range(N_DIR)]
    for step in range(CHUNKS_PER_UNIT):
        for d, o_ref in ((0, of_ref), (1, ob_ref)):
            c = step if d == 0 else CHUNKS_PER_UNIT - 1 - step
            o, s[d] = _gdn_scan_step(*prepared[d * CHUNKS_PER_UNIT + c], gl_rows[d * CHUNKS_PER_UNIT + c], s[d])
            for h in range(H_B):
                o_ref[c * CHUNK:(c + 1) * CHUNK, h * DV_B:(h + 1) * DV_B] = o[h * CHUNK:(h + 1) * CHUNK]
    for d in range(N_DIR):
        s_ref[d] = s[d]
    so_ref[0] = s_ref[...]


def _gdn(proj, gcoef, gdt, init, n_prompt_units, units_per_seq):
    n_tok = proj.shape[0]
    n_units = n_tok // UNIT
    ids = functools.partial(_unit_ids, n_prompt_units=n_prompt_units, units_per_seq=units_per_seq)
    small_blk = C_SMALL // LANES
    qkv_blk = C_QKVB // (3 * QB)
    st_blk = (1, N_DIR, H_B * DK_B, DV_B)
    kern = functools.partial(_gdn_kernel, n_prompt_units=n_prompt_units, units_per_seq=units_per_seq)
    return pl.pallas_call(
        kern,
        grid=(n_units,),
        in_specs=[pl.BlockSpec((UNIT, 3 * QB), lambda j: (ids(j)[0], qkv_blk)),
                  pl.BlockSpec((UNIT, 3 * QB), lambda j: (ids(j)[1], qkv_blk)),
                  pl.BlockSpec((UNIT, LANES), lambda j: (ids(j)[0], small_blk)),
                  pl.BlockSpec((UNIT, LANES), lambda j: (ids(j)[1], small_blk)),
                  pl.BlockSpec((1, LANES), lambda j: (0, 0)),
                  pl.BlockSpec((1, LANES), lambda j: (0, 0)),
                  pl.BlockSpec(st_blk, lambda j: (ids(j)[2], 0, 0, 0))],
        out_specs=[pl.BlockSpec((UNIT, VB), lambda j: (ids(j)[0], 0)),
                   pl.BlockSpec((UNIT, VB), lambda j: (ids(j)[1], 0)),
                   pl.BlockSpec(st_blk, lambda j: (jnp.minimum(j, n_prompt_units), 0, 0, 0))],
        out_shape=[jax.ShapeDtypeStruct((n_tok, VB), F32),
                   jax.ShapeDtypeStruct((n_tok, VB), F32),
                   jax.ShapeDtypeStruct((n_prompt_units + 1, N_DIR, H_B * DK_B, DV_B), F32)],
        scratch_shapes=[pltpu.VMEM((N_DIR, H_B * DK_B, DV_B), F32)],
        compiler_params=_cparams(("arbitrary",)),
        name="gdn",
    )(proj, proj, proj, proj, gcoef, gdt, init)


def _head_rms(o, w):
    parts = []
    for h in range(o.shape[1] // LANES):
        parts.append(_rms(o[:, h * LANES:(h + 1) * LANES]))
    return jnp.concatenate(parts, axis=1) * w


def _post_kernel(xp_ref, xs_ref, ga_ref, zb_ref, af_ref, ab_ref, df_ref, db_ref, mod_ref, wo_ref, nwa_ref, nwb_ref,
                 n2_ref, rw_ref, x1_ref, h2_ref, sc_ref, *, tiles_prompt):
    m = mod_ref[0]
    x = jnp.where(pl.program_id(0) < tiles_prompt, xp_ref[...], xs_ref[...])
    gla = _head_rms(af_ref[...] + ab_ref[...], nwa_ref[...]) * _silu(ga_ref[...])
    gdn = _head_rms(df_ref[...] + db_ref[...], nwb_ref[...]) * _silu(zb_ref[...])
    y = _dg(gla.astype(BF16), wo_ref[0:VA, :], NN) + _dg(gdn.astype(BF16), wo_ref[VA:VA + VB, :], NN)
    x1 = x + m[:, 2 * D_MODEL:3 * D_MODEL] * y
    x1_ref[...] = x1
    h2 = (_rms(x1) * n2_ref[...]) * (1.0 + m[:, 4 * D_MODEL:5 * D_MODEL]) + m[:, 3 * D_MODEL:4 * D_MODEL]
    h2_ref[...] = _pack_rows(h2)
    sc_ref[...] = _sigmoid(_mm3(rw_ref[...], h2, NT))


def _post(xp, xs, proj, o_af, o_ab, o_df, o_db, mod3, w_o, nwa, nwb, n2w, router_wt, sample_len, tm):
    n_prompt_tok = xp.shape[0]
    n_tok = n_prompt_tok + xs.shape[0]
    cond = functools.partial(_cond_row, tiles_prompt=n_prompt_tok // tm, tiles_per_seq=sample_len // tm)
    tok = lambda i: (i, 0)
    const = lambda i: (0, 0)
    return pl.pallas_call(
        functools.partial(_post_kernel, tiles_prompt=n_prompt_tok // tm),
        grid=(n_tok // tm,),
        in_specs=_two_streams(n_prompt_tok, tm) + [
                  pl.BlockSpec((tm, VA), lambda i: (i, C_GA // VA)),
                  pl.BlockSpec((tm, VB), lambda i: (i, C_ZB // VB)),
                  pl.BlockSpec((tm, VA), tok), pl.BlockSpec((tm, VA), tok),
                  pl.BlockSpec((tm, VB), tok), pl.BlockSpec((tm, VB), tok),
                  pl.BlockSpec((1, 1, 6 * D_MODEL), lambda i: (cond(i), 0, 0)),
                  pl.BlockSpec((VA + VB, D_MODEL), const),
                  pl.BlockSpec((1, VA), const), pl.BlockSpec((1, VB), const), pl.BlockSpec((1, D_MODEL), const),
                  pl.BlockSpec((N_EXPERTS, D_MODEL), const)],
        out_specs=[pl.BlockSpec((tm, D_MODEL), tok),
                   pl.BlockSpec((tm, D_PACK), tok),
                   pl.BlockSpec((N_EXPERTS, tm), lambda i: (0, i))],
        out_shape=[jax.ShapeDtypeStruct((n_tok, D_MODEL), F32),
                   jax.ShapeDtypeStruct((n_tok, D_PACK), jnp.uint32),
                   jax.ShapeDtypeStruct((N_EXPERTS, n_tok), F32)],
        compiler_params=_cparams(("parallel",)),
        name="post",
    )(xp, xs, proj, proj, o_af, o_ab, o_df, o_db, mod3, w_o, nwa, nwb, n2w, router_wt)


def _route_kernel(sc_ref, bias_ref, e_ref, r_ref, w_ref, cnt_ref, carry_ref):
    i = pl.program_id(0)
    t = sc_ref.shape[1]

    @pl.when(i == 0)
    def _():
        carry_ref[...] = jnp.zeros(carry_ref.shape, F32)

    s = sc_ref[...]
    biased = s + bias_ref[:, 0:1]
    gs = []
    for g in range(N_GROUP):
        blk = biased[g * GROUP_SIZE:(g + 1) * GROUP_SIZE]
        m1 = jnp.max(blk, axis=0, keepdims=True)
        n1 = jnp.sum((blk == m1).astype(F32), axis=0, keepdims=True)
        m2 = jnp.max(jnp.where(blk < m1, blk, NEG_INF), axis=0, keepdims=True)
        gs.append(m1 + jnp.where(n1 >= 2.0, m1, m2))
    gsc = jnp.concatenate(gs, axis=0)
    gid = _iota2((N_GROUP, t), 0)
    beaten = jnp.zeros((N_GROUP, t), F32)
    for g in range(N_GROUP):
        other = gsc[g:g + 1, :]
        wins = jnp.logical_or(other > gsc, jnp.logical_and(other == gsc, g < gid))
        beaten = beaten + wins.astype(F32)
    masked = jnp.concatenate(
        [jnp.where(beaten[g:g + 1, :] < float(TOPK_GROUP), biased[g * GROUP_SIZE:(g + 1) * GROUP_SIZE], NEG_INF)
         for g in range(N_GROUP)], axis=0)
    eid = _iota2((N_EXPERTS, t), 0).astype(F32)
    sel = jnp.zeros((N_EXPERTS, t), F32)
    picks, scores = [], []
    for _ in range(TOP_K):
        m = jnp.max(masked, axis=0, keepdims=True)
        first = jnp.min(jnp.where(masked == m, eid, float(N_EXPERTS)), axis=0, keepdims=True)
        hit = eid == first
        scores.append(jnp.sum(jnp.where(hit, s, 0.0), axis=0, keepdims=True))
        masked = jnp.where(hit, NEG_INF, masked)
        sel = sel + hit.astype(F32)
        picks.append(first)
    upper = (_iota2((t, t), 0) < _iota2((t, t), 1)).astype(BF16)
    carry = carry_ref[...]
    prefix = _dg(sel.astype(BF16), upper, NN) + jnp.concatenate([carry] * (t // LANES), axis=1)
    ranks = [jnp.sum(jnp.where(eid == p, prefix, 0.0), axis=0, keepdims=True) for p in picks]
    carry = carry + _dg(sel.astype(BF16), jnp.ones((t, LANES), BF16), NN)
    carry_ref[...] = carry
    cnt_ref[...] = carry
    sc8 = jnp.concatenate(scores, axis=0)
    e_ref[...] = jnp.concatenate(picks, axis=0).astype(jnp.int32)
    r_ref[...] = jnp.concatenate(ranks, axis=0).astype(jnp.int32)
    w_ref[...] = sc8 / jnp.sum(sc8, axis=0, keepdims=True) * ROUTED_SCALE


def _route(scores_t, bias_col):
    n_tok = scores_t.shape[1]
    t = ROUTE_TILE
    slot = lambda i: (0, i)
    return pl.pallas_call(
        _route_kernel,
        grid=(n_tok // t,),
        in_specs=[pl.BlockSpec((N_EXPERTS, t), slot),
                  pl.BlockSpec((N_EXPERTS, LANES), lambda i: (0, 0))],
        out_specs=[pl.BlockSpec((TOP_K, t), slot), pl.BlockSpec((TOP_K, t), slot), pl.BlockSpec((TOP_K, t), slot),
                   pl.BlockSpec((N_EXPERTS, LANES), lambda i: (0, 0))],
        out_shape=[jax.ShapeDtypeStruct((TOP_K, n_tok), jnp.int32),
                   jax.ShapeDtypeStruct((TOP_K, n_tok), jnp.int32),
                   jax.ShapeDtypeStruct((TOP_K, n_tok), F32),
                   jax.ShapeDtypeStruct((N_EXPERTS, LANES), F32)],
        scratch_shapes=[pltpu.VMEM((N_EXPERTS, LANES), F32)],
        compiler_params=_cparams(("arbitrary",)),
        name="route",
    )(scores_t, bias_col)


def _dest_kernel(start_ref, e_ref, r_ref, d_ref):
    e = e_ref[...]

    def body(x, acc):
        return jnp.where(e == x, start_ref[x], acc)

    d_ref[...] = r_ref[...] + lax.fori_loop(0, N_EXPERTS, body, jnp.zeros(e.shape, jnp.int32))


def _dest(start, e8, r8, tt):
    n_tok = e8.shape[1]
    slot = lambda i, st: (0, i)
    return pl.pallas_call(
        _dest_kernel,
        grid_spec=pltpu.PrefetchScalarGridSpec(
            num_scalar_prefetch=1, grid=(n_tok // tt,),
            in_specs=[pl.BlockSpec((TOP_K, tt), slot), pl.BlockSpec((TOP_K, tt), slot)],
            out_specs=pl.BlockSpec((TOP_K, tt), slot)),
        out_shape=jax.ShapeDtypeStruct((TOP_K, n_tok), jnp.int32),
        compiler_params=_cparams(("parallel",)),
        name="dest",
    )(start, e8, r8)


def _items(counts, n_rows):
    n_blocks = n_rows // ROW_BLOCK
    max_items = n_blocks + N_EXPERTS - 1
    end = jnp.cumsum(counts)
    start = end - counts
    first_blk = start // ROW_BLOCK
    n_it = jnp.where(counts > 0, (end - 1) // ROW_BLOCK - first_blk + 1, 0)
    it_end = jnp.cumsum(n_it)
    it_start = it_end - n_it
    i = jnp.arange(max_items, dtype=jnp.int32)
    valid = i < it_end[-1]
    ex = jnp.minimum(jnp.sum((it_end[None, :] <= i[:, None]).astype(jnp.int32), axis=1), N_EXPERTS - 1)
    onehot = ex[:, None] == jnp.arange(N_EXPERTS, dtype=jnp.int32)[None, :]
    pick = lambda tab: jnp.sum(jnp.where(onehot, tab[None, :], 0), axis=1)
    blk = pick(first_blk) + (i - pick(it_start))
    lo = jnp.maximum(pick(start), blk * ROW_BLOCK) - blk * ROW_BLOCK
    hi = jnp.minimum(pick(end), (blk + 1) * ROW_BLOCK) - blk * ROW_BLOCK
    blk = jnp.where(valid, blk, n_blocks - 1).astype(jnp.int32)
    lo = jnp.where(valid, lo, 0).astype(jnp.int32)
    hi = jnp.where(valid, hi, 0).astype(jnp.int32)
    eids = jnp.arange(N_EXPERTS, dtype=jnp.int32)
    later = jnp.logical_and(eids[None, :] > eids[:, None], (counts > 0)[None, :])
    nxt_e = jnp.min(jnp.where(later, eids[None, :], N_EXPERTS), axis=1)
    nxt = pick(jnp.where(nxt_e < N_EXPERTS, nxt_e, -1)).astype(jnp.int32)
    return start.astype(jnp.int32), blk, ex.astype(jnp.int32), nxt, lo, hi


def _dispatch_kernel(h2_ref, dest_hbm, xs_hbm, dsm, sem_d, sem_s):
    i = pl.program_id(0)
    n = pl.num_programs(0)
    tm = h2_ref.shape[0]

    def dest_copy(step, slot):
        return pltpu.make_async_copy(dest_hbm.at[:, pl.ds(step * tm, tm)], dsm.at[slot], sem_d.at[slot])

    def row_copy(t, dst):
        return pltpu.make_async_copy(h2_ref.at[pl.ds(t, 1), :], xs_hbm.at[pl.ds(dst, 1), :], sem_s.at[0])

    @pl.when(i == 0)
    def _():
        dest_copy(0, 0).start()

    slot = i % 2
    dest_copy(i, slot).wait()

    @pl.when(i + 1 < n)
    def _():
        dest_copy(i + 1, 1 - slot).start()

    def issue(t, carry):
        for k in range(TOP_K):
            row_copy(t, dsm[slot, k, t]).start(priority=k % N_DMA_QUEUES)
        return carry
    lax.fori_loop(0, tm, issue, 0, unroll=2)

    def drain(t, carry):
        for k in range(TOP_K):
            row_copy(t, 0).wait()
        return carry
    lax.fori_loop(0, tm, drain, 0)


def _dispatch(h2, dest, tm):
    n_tok = h2.shape[0]
    return pl.pallas_call(
        _dispatch_kernel,
        grid=(n_tok // tm,),
        in_specs=[pl.BlockSpec((tm, D_PACK), lambda i: (i, 0)),
                  pl.BlockSpec(memory_space=pl.ANY)],
        out_specs=pl.BlockSpec(memory_space=pl.ANY),
        out_shape=jax.ShapeDtypeStruct((n_tok * TOP_K, D_PACK), jnp.uint32),
        scratch_shapes=[pltpu.SMEM((2, TOP_K, tm), jnp.int32),
                        pltpu.SemaphoreType.DMA((2,)),
                        pltpu.SemaphoreType.DMA((1,))],
        compiler_params=_cparams(("arbitrary",)),
        name="dispatch",
    )(h2, dest)


def _experts_kernel(blk_ref, exp_ref, nxt_ref, lo_ref, hi_ref, x_ref, wg_hbm, wu_hbm, wd_hbm, y_ref,
                    wgf, wuf, wdf, wgb, wub, wdb, nchg, sem_w):
    i = pl.program_id(0)
    lo, hi = lo_ref[i], hi_ref[i]

    def weight_copies(e, slot):
        return (pltpu.make_async_copy(wg_hbm.at[e], wgf.at[slot], sem_w.at[slot]),
                pltpu.make_async_copy(wu_hbm.at[e], wuf.at[slot], sem_w.at[slot]),
                pltpu.make_async_copy(wd_hbm.at[e], wdf.at[slot], sem_w.at[slot]))

    @pl.when(i == 0)
    def _():
        nchg[0] = 0
        for cp in weight_copies(exp_ref[0], 0):
            cp.start()

    @pl.when(hi > lo)
    def _():
        @pl.when(jnp.logical_or(i == 0, exp_ref[i] != exp_ref[jnp.maximum(i - 1, 0)]))
        def _():
            slot = nchg[0] % 2
            for cp in weight_copies(exp_ref[i], slot):
                cp.wait()

            @pl.when(nxt_ref[i] >= 0)
            def _():
                for cp in weight_copies(nxt_ref[i], 1 - slot):
                    cp.start()

            wgb[...] = wgf[slot].astype(BF16)
            wub[...] = wuf[slot].astype(BF16)
            wdb[...] = wdf[slot].astype(BF16)
            nchg[0] = nchg[0] + 1

        x_lo, x_hi = _unpack_rows(x_ref[...])
        x = jnp.concatenate([x_lo.astype(BF16), x_hi.astype(BF16)], axis=1)
        g = _dg(x, wgb[...], NN)
        u = _dg(x, wub[...], NN)
        hmid = (_silu(g) * u).astype(BF16)
        y = _pack_rows(_dg(hmid, wdb[...], NN))
        row = _iota2((ROW_BLOCK, D_PACK), 0)
        mine = jnp.logical_and(row >= lo, row < hi)

        @pl.when(lo == 0)
        def _():
            y_ref[...] = jnp.where(mine, y, jnp.uint32(0))

        @pl.when(lo > 0)
        def _():
            y_ref[...] = jnp.where(mine, y, y_ref[...])


def _experts(item_blk, item_exp, item_nxt, item_lo, item_hi, xs, w_gate, w_up, w_down):
    n_items = item_blk.shape[0]
    xmap = lambda i, blk, ex, nxt, lo, hi: (blk[i], 0)
    return pl.pallas_call(
        _experts_kernel,
        grid_spec=pltpu.PrefetchScalarGridSpec(
            num_scalar_prefetch=5, grid=(n_items,),
            in_specs=[pl.BlockSpec((ROW_BLOCK, D_PACK), xmap),
                      pl.BlockSpec(memory_space=pl.ANY),
                      pl.BlockSpec(memory_space=pl.ANY),
                      pl.BlockSpec(memory_space=pl.ANY)],
            out_specs=pl.BlockSpec((ROW_BLOCK, D_PACK), xmap),
            scratch_shapes=[pltpu.VMEM((2, D_MODEL, D_EXPERT), F32), pltpu.VMEM((2, D_MODEL, D_EXPERT), F32),
                            pltpu.VMEM((2, D_EXPERT, D_MODEL), F32),
                            pltpu.VMEM((D_MODEL, D_EXPERT), BF16), pltpu.VMEM((D_MODEL, D_EXPERT), BF16),
                            pltpu.VMEM((D_EXPERT, D_MODEL), BF16),
                            pltpu.SMEM((1,), jnp.int32),
                            pltpu.SemaphoreType.DMA((2,))]),
        out_shape=jax.ShapeDtypeStruct(xs.shape, jnp.uint32),
        compiler_params=_cparams(("arbitrary",)),
        name="experts",
    )(item_blk, item_exp, item_nxt, item_lo, item_hi, xs, w_gate, w_up, w_down)


def _final_kernel(x1_ref, h2_ref, w_ref, mod_ref, sg_ref, su_ref, sd_ref, fn_ref, dest_hbm, ys_hbm, o_ref,
                  dsm, gbuf, sem_d, sem_g):
    i = pl.program_id(0)
    n = pl.num_programs(0)
    tm = x1_ref.shape[0]

    def dest_copy(step, slot):
        return pltpu.make_async_copy(dest_hbm.at[:, pl.ds(step * tm, tm)], dsm.at[slot], sem_d.at[slot])

    def row_copy(src, slot, k, t):
        return pltpu.make_async_copy(ys_hbm.at[pl.ds(src, 1), :], gbuf.at[slot, k, pl.ds(t, 1), :], sem_g.at[slot])

    def issue_gathers(slot3, slot2):
        def body(t, carry):
            for k in range(TOP_K):
                row_copy(dsm[slot3, k, t], slot2, k, t).start(priority=k % N_DMA_QUEUES)
            return carry
        lax.fori_loop(0, tm, body, 0, unroll=2)

    @pl.when(i == 0)
    def _():
        dest_copy(0, 0).start()
        dest_copy(0, 0).wait()
        issue_gathers(0, 0)

        @pl.when(n > 1)
        def _():
            dest_copy(1, 1).start()

    @pl.when(i + 1 < n)
    def _():
        dest_copy(i + 1, (i + 1) % 3).wait()
        issue_gathers((i + 1) % 3, (i + 1) % 2)

        @pl.when(i + 2 < n)
        def _():
            dest_copy(i + 2, (i + 2) % 3).start()

    slot = i % 2

    def drain(t, carry):
        for k in range(TOP_K):
            row_copy(0, slot, k, t).wait()
        return carry
    lax.fori_loop(0, tm, drain, 0)

    m = mod_ref[0]
    w = w_ref[...]
    r_lo = r_hi = None
    for kk in range(TOP_K):
        y_lo, y_hi = _unpack_rows(gbuf[slot, kk])
        wk = w[:, kk:kk + 1]
        r_lo = y_lo * wk if r_lo is None else r_lo + y_lo * wk
        r_hi = y_hi * wk if r_hi is None else r_hi + y_hi * wk
    routed = jnp.concatenate([r_lo, r_hi], axis=1)
    h_lo, h_hi = _unpack_rows(h2_ref[...])
    hb = jnp.concatenate([h_lo.astype(BF16), h_hi.astype(BF16)], axis=1)
    hm = (_silu(_dg(hb, sg_ref[...], NN)) * _dg(hb, su_ref[...], NN)).astype(BF16)
    shared = _dg(hm, sd_ref[...], NN)
    x2 = x1_ref[...] + m[:, 5 * D_MODEL:6 * D_MODEL] * (routed + shared)
    o_ref[...] = _rms(x2) * fn_ref[...]


def _final(x1, h2, wts, mod3, sg, su, sd, fnw, dest, ys, n_prompt_tok, sample_len, tm):
    n_tok = x1.shape[0]
    cond = functools.partial(_cond_row, tiles_prompt=n_prompt_tok // tm, tiles_per_seq=sample_len // tm)
    tok = lambda i: (i, 0)
    const = lambda i: (0, 0)
    return pl.pallas_call(
        _final_kernel,
        grid=(n_tok // tm,),
        in_specs=[pl.BlockSpec((tm, D_MODEL), tok),
                  pl.BlockSpec((tm, D_PACK), tok),
                  pl.BlockSpec((tm, TOP_K), tok),
                  pl.BlockSpec((1, 1, 6 * D_MODEL), lambda i: (cond(i), 0, 0)),
                  pl.BlockSpec((D_MODEL, D_EXPERT), const),
                  pl.BlockSpec((D_MODEL, D_EXPERT), const),
                  pl.BlockSpec((D_EXPERT, D_MODEL), const),
                  pl.BlockSpec((1, D_MODEL), const),
                  pl.BlockSpec(memory_space=pl.ANY),
                  pl.BlockSpec(memory_space=pl.ANY)],
        out_specs=pl.BlockSpec((tm, D_MODEL), tok),
        out_shape=jax.ShapeDtypeStruct((n_tok, D_MODEL), F32),
        scratch_shapes=[pltpu.SMEM((3, TOP_K, tm), jnp.int32),
                        pltpu.VMEM((2, TOP_K, tm, D_PACK), jnp.uint32),
                        pltpu.SemaphoreType.DMA((3,)),
                        pltpu.SemaphoreType.DMA((2,))],
        compiler_params=_cparams(("arbitrary",)),
        name="final",
    )(x1, h2, wts, mod3, sg, su, sd, fnw, dest, ys)


def kernel(x_prompt, x_sample, state_gla, state_gdn, c, c_ctx, w_ada, b_ada, norm1_w, w_in, conv_w, gla_lr_w, gla_lr_b, gdn_a_log, gdn_dt_bias, gla_norm_w, gdn_norm_w, w_o, norm2_w, router_w, router_bias, exp_w_gate, exp_w_up, exp_w_down, sh_w_gate, sh_w_up, sh_w_down, final_norm_w):
    bp, lp, d = x_prompt.shape
    bs, ls, _ = x_sample.shape
    assert d == D_MODEL and lp == UNIT and ls % UNIT == 0 and w_ada.shape[0] == 1
    n_prompt_tok = bp * lp
    n_tok = n_prompt_tok + bs * ls
    n_prompt_units = n_prompt_tok // UNIT
    units_per_seq = ls // UNIT
    grid_w = 64
    layer = 0
    tm_in = 512
    tm = 256
    tm_moe = 512
    assert n_prompt_tok % tm_in == 0 and ls % tm_in == 0 and n_tok % ROUTE_TILE == 0

    xp = x_prompt.reshape(n_prompt_tok, d)
    xs_in = x_sample.reshape(bs * ls, d)
    cond = jnp.concatenate([c_ctx[None, :], c, jnp.zeros((SUBLANES - 1 - bs, d), F32)], axis=0)
    mod3 = _ada(cond, w_ada[layer], b_ada[layer][None, :]).reshape(SUBLANES, 1, 6 * d)

    wi = w_in[layer]
    o_lr = 2 * QA + 2 * VA
    o_qkvb = o_lr + N_DIR * GLA_RANK
    o_zb = o_qkvb + 3 * QB
    o_ab = o_zb + VB
    w_main = jnp.concatenate([wi[:, 0:o_lr], wi[:, o_qkvb:o_ab]], axis=1).astype(BF16)
    w_small = jnp.concatenate([wi[:, o_lr:o_qkvb], wi[:, o_ab:], jnp.zeros((d, LANES - 48), F32)], axis=1)
    cw = jnp.concatenate([conv_w[layer], jnp.zeros((SUBLANES - CONV_K, 3 * QB), F32)], axis=0)
    proj = _inproj(xp, xs_in, mod3, norm1_w[layer][None, :], w_main, w_small, cw, ls, lp, grid_w, tm_in)

    wlr = jnp.zeros((N_DIR, LANES, QA), F32)
    for dd in range(N_DIR):
        wlr = wlr.at[dd, dd * GLA_RANK:(dd + 1) * GLA_RANK, :].set(gla_lr_w[layer, dd])
    blr = gla_lr_b[layer][:, None, :]
    init_gla = jnp.concatenate([jnp.zeros((1,) + state_gla.shape[2:], F32), state_gla[:, layer].astype(F32)], axis=0)
    o_af, o_ab_, s_gla = _gla(proj, wlr, blr, init_gla, n_prompt_units, units_per_seq)

    gcoef = jnp.zeros((1, LANES), F32).at[0, SM_AB:SM_AB + N_DIR * H_B].set(-jnp.exp(gdn_a_log[layer].reshape(-1)))
    gdt = jnp.zeros((1, LANES), F32).at[0, SM_AB:SM_AB + N_DIR * H_B].set(gdn_dt_bias[layer].reshape(-1))
    init_gdn = jnp.concatenate([jnp.zeros((1, N_DIR, H_B * DK_B, DV_B), F32),
                                state_gdn[:, layer].astype(F32).reshape(bs, N_DIR, H_B * DK_B, DV_B)], axis=0)
    o_df, o_db, s_gdn = _gdn(proj, gcoef, gdt, init_gdn, n_prompt_units, units_per_seq)

    nwa = jnp.tile(gla_norm_w[layer], H_A)[None, :]
    nwb = jnp.tile(gdn_norm_w[layer], H_B)[None, :]
    x1, h2, scores_t = _post(xp, xs_in, proj, o_af, o_ab_, o_df, o_db, mod3, w_o[layer].astype(BF16), nwa, nwb,
                             norm2_w[layer][None, :], router_w[layer].T, ls, tm)

    bias_col = jnp.broadcast_to(router_bias[layer].astype(F32)[:, None], (N_EXPERTS, LANES))
    e8, r8, w8, cnt = _route(scores_t, bias_col)
    start, item_blk, item_exp, item_nxt, item_lo, item_hi = _items(cnt[:, 0].astype(jnp.int32), n_tok * TOP_K)
    dest = _dest(start, e8, r8, tt=min(2048, n_tok))

    xs = _dispatch(h2, dest, tm_moe)
    ys = _experts(item_blk, item_exp, item_nxt, item_lo, item_hi, xs, exp_w_gate[layer], exp_w_up[layer],
                  exp_w_down[layer])
    y_all = _final(x1, h2, w8.T, mod3, sh_w_gate[layer].astype(BF16), sh_w_up[layer].astype(BF16),
                   sh_w_down[layer].astype(BF16), final_norm_w[None, :], dest, ys, n_prompt_tok, ls, tm_moe)

    y_prompt = y_all[:n_prompt_tok].reshape(bp, lp, d)
    y_sample = y_all[n_prompt_tok:].reshape(bs, ls, d)
    new_state_gla = s_gla[:n_prompt_units].reshape(bp, 1, N_DIR, H_A, DK_A, DV_A).astype(x_prompt.dtype)
    new_state_gdn = s_gdn[:n_prompt_units].reshape(bp, 1, N_DIR, H_B, DK_B, DV_B).astype(x_prompt.dtype)
    return (y_prompt, y_sample, new_state_gla, new_state_gdn)
```

```python
import functools

import jax
import jax.numpy as jnp
from jax import lax
from jax.experimental import pallas as pl
from jax.experimental.pallas import tpu as pltpu

F32 = jnp.float32
BF16 = jnp.bfloat16

D_MODEL = 1024
N_DIR = 2
H_A, DK_A, DV_A = 4, 64, 128
GLA_RANK = 16
GLA_NORMALIZER = 16.0
H_B, DK_B, DV_B = 4, 128, 128
CONV_K = 5
CHUNK = 64
QA, VA = H_A * DK_A, H_A * DV_A
QB, VB = H_B * DK_B, H_B * DV_B
N_EXPERTS = 256
TOP_K = 8
N_GROUP = 8
TOPK_GROUP = 4
GROUP_SIZE = N_EXPERTS // N_GROUP
D_EXPERT = 256
ROUTED_SCALE = 2.5
EPS = 1e-6
NEG_INF = float("-inf")

UNIT = 256
CHUNKS_PER_UNIT = UNIT // CHUNK
HC = H_B * CHUNK
GDN_GROUP = 2
HG = GDN_GROUP * CHUNK
INV_BLOCK = 8
D_PACK = D_MODEL // 2
ROUTE_TILE = 256
ROW_BLOCK = 512
N_DMA_QUEUES = 2
LANES = 128
SUBLANES = 8
VMEM_LIMIT = 56 * 1024 * 1024

C_QKVA = 0
C_GA = 1024
C_QKVB = 1536
C_ZB = 3072
C_SMALL = 3584
D_PROJ = 3712
SM_AB = 32
SM_BETA = 40


def _cparams(sem):
    return pltpu.CompilerParams(dimension_semantics=sem, vmem_limit_bytes=VMEM_LIMIT)


def _split(a):
    hi = a.astype(BF16)
    lo = (a - hi.astype(F32)).astype(BF16)
    return hi, lo


def _dg(a, b, dims):
    return lax.dot_general(a, b, (dims, ((), ())), preferred_element_type=F32)


NN = ((1,), (0,))
NT = ((1,), (1,))
TN = ((0,), (0,))


def _mm(a, b, dims=NN):
    return _dg(a.astype(BF16), b.astype(BF16), dims)


def _mm3(a, b, dims=NN):
    ah, al = _split(a)
    bh, bl = _split(b)
    return _dg(ah, bh, dims) + (_dg(ah, bl, dims) + _dg(al, bh, dims))


def _mm_exact_lhs(a_bf16, b, dims=NN):
    bh, bl = _split(b)
    return _dg(a_bf16, bh, dims) + _dg(a_bf16, bl, dims)


def _silu(x):
    return x * (1.0 / (1.0 + jnp.exp(-x)))


def _sigmoid(x):
    return 1.0 / (1.0 + jnp.exp(-x))


def _softplus(x):
    return jnp.maximum(x, 0.0) + jnp.log1p(jnp.exp(-jnp.abs(x)))


def _log_sigmoid(x):
    return -_softplus(-x)


def _iota2(shape, dim):
    return lax.broadcasted_iota(jnp.int32, shape, dim)


def _rms(x):
    return x * lax.rsqrt(jnp.mean(x * x, axis=-1, keepdims=True) + EPS)


def _pack_rows(x):
    lo = lax.bitcast_convert_type(x[:, 0:D_PACK].astype(BF16).astype(F32), jnp.uint32)
    hi = lax.bitcast_convert_type(x[:, D_PACK:D_MODEL].astype(BF16).astype(F32), jnp.uint32)
    return (lo >> 16) | (hi & jnp.uint32(0xFFFF0000))


def _unpack_rows(p):
    lo = lax.bitcast_convert_type(p << 16, F32)
    hi = lax.bitcast_convert_type(p & jnp.uint32(0xFFFF0000), F32)
    return lo, hi


def _ada_kernel(c_ref, w_ref, b_ref, o_ref):
    o_ref[...] = _mm3(_silu(c_ref[...]), w_ref[...]) + b_ref[...]


def _ada(cond, w, b):
    n = w.shape[1]
    tn = 1536
    return pl.pallas_call(
        _ada_kernel,
        grid=(n // tn,),
        in_specs=[pl.BlockSpec((SUBLANES, D_MODEL), lambda i: (0, 0)),
                  pl.BlockSpec((D_MODEL, tn), lambda i: (0, i)),
                  pl.BlockSpec((1, tn), lambda i: (0, i))],
        out_specs=pl.BlockSpec((SUBLANES, tn), lambda i: (0, i)),
        out_shape=jax.ShapeDtypeStruct((SUBLANES, n), F32),
        compiler_params=_cparams(("parallel",)),
        name="ada",
    )(cond, w, b)


def _two_streams(n_prompt_tok, tm):
    tp = n_prompt_tok // tm
    return [pl.BlockSpec((tm, D_MODEL), lambda i: (jnp.minimum(i, tp - 1), 0)),
            pl.BlockSpec((tm, D_MODEL), lambda i: (jnp.maximum(i - tp, 0), 0))]


def _inproj_kernel(xp_ref, xs_ref, mod_ref, nw_ref, w_ref, ws_ref, cw_ref, o_ref, *, tiles_prompt, prompt_row,
                   sample_row):
    m = mod_ref[0]
    x = jnp.where(pl.program_id(0) < tiles_prompt, xp_ref[...], xs_ref[...])
    h = (_rms(x) * nw_ref[...]) * (1.0 + m[:, D_MODEL:2 * D_MODEL]) + m[:, 0:D_MODEL]
    hb = h.astype(BF16)
    for c0 in range(0, C_SMALL, 512):
        o_ref[:, c0:c0 + 512] = _dg(hb, w_ref[:, c0:c0 + 512], NN)
    o_ref[:, C_SMALL:D_PROJ] = _mm3(h, ws_ref[...])
    tm = xp_ref.shape[0]
    row_len = jnp.where(pl.program_id(0) < tiles_prompt, prompt_row, sample_row)
    pos = _iota2((tm, LANES), 0) & (row_len - 1)
    inside = [jnp.logical_and(pos + (jj - CONV_K // 2) >= 0, pos + (jj - CONV_K // 2) < row_len).astype(F32)
              for jj in range(CONV_K)]
    for part in range(3 * H_B):
        cs = slice(C_QKVB + part * LANES, C_QKVB + (part + 1) * LANES)
        x = o_ref[:, cs]
        acc = x * cw_ref[CONV_K // 2:CONV_K // 2 + 1, part * LANES:(part + 1) * LANES]
        for jj in range(CONV_K):
            off = jj - CONV_K // 2
            if off == 0:
                continue
            xs = pltpu.roll(x, (-off) % tm, 0) * inside[jj]
            acc = acc + xs * cw_ref[jj:jj + 1, part * LANES:(part + 1) * LANES]
        y = _silu(acc)
        if part < 2 * H_B:
            y = y * lax.rsqrt(jnp.sum(y * y, axis=-1, keepdims=True) + EPS)
            if part < H_B:
                y = y * (DK_B ** -0.5)
        o_ref[:, cs] = y


def _cond_row(i, tiles_prompt, tiles_per_seq):
    return jnp.where(i < tiles_prompt, 0, 1 + (i - tiles_prompt) // tiles_per_seq)


def _inproj(xp, xs, mod3, norm_w, w_main, w_small, conv_w, sample_len, prompt_row, sample_row, tm):
    n_prompt_tok = xp.shape[0]
    n_tok = n_prompt_tok + xs.shape[0]
    assert tm % prompt_row == 0 and tm % sample_row == 0
    cond = functools.partial(_cond_row, tiles_prompt=n_prompt_tok // tm, tiles_per_seq=sample_len // tm)
    kern = functools.partial(_inproj_kernel, tiles_prompt=n_prompt_tok // tm, prompt_row=prompt_row,
                             sample_row=sample_row)
    return pl.pallas_call(
        kern,
        grid=(n_tok // tm,),
        in_specs=_two_streams(n_prompt_tok, tm) + [
                  pl.BlockSpec((1, 1, 6 * D_MODEL), lambda i: (cond(i), 0, 0)),
                  pl.BlockSpec((1, D_MODEL), lambda i: (0, 0)),
                  pl.BlockSpec((D_MODEL, C_SMALL), lambda i: (0, 0)),
                  pl.BlockSpec((D_MODEL, LANES), lambda i: (0, 0)),
                  pl.BlockSpec((SUBLANES, 3 * QB), lambda i: (0, 0))],
        out_specs=pl.BlockSpec((tm, D_PROJ), lambda i: (i, 0)),
        out_shape=jax.ShapeDtypeStruct((n_tok, D_PROJ), F32),
        compiler_params=_cparams(("parallel",)),
        name="inproj",
    )(xp, xs, mod3, norm_w, w_main, w_small, conv_w)


def _unit_ids(j, n_prompt_units, units_per_seq):
    jj = j - n_prompt_units
    b = jj // units_per_seq
    r = jj % units_per_seq
    is_prompt = j < n_prompt_units
    uf = j
    ub = jnp.where(is_prompt, j, n_prompt_units + b * units_per_seq + (units_per_seq - 1 - r))
    init_row = jnp.where(is_prompt, 0, 1 + b)
    first = jnp.logical_or(is_prompt, r == 0)
    return uf, ub, init_row, first


def _tri(rev):
    t = _iota2((CHUNK, CHUNK), 0)
    s = _iota2((CHUNK, CHUNK), 1)
    return (t <= s) if rev else (t >= s)


def _mm_exact_lhs_tn(a, ones_bf16):
    ah, al = _split(a)
    return _dg(ah, ones_bf16, TN) + _dg(al, ones_bf16, TN)


def _gla_unit(chunks, order, states, rev):
    tri = _tri(rev).astype(BF16)
    mid, last = (CHUNK // 2 - 1, 0) if rev else (CHUNK // 2, CHUNK - 1)
    scale = DK_A ** -0.5
    ones = jnp.ones((CHUNK, LANES), BF16)
    lane = _iota2((CHUNK, LANES), 1)
    row = _iota2((CHUNK, LANES), 0)
    s_in = lane % DK_A
    causal = (row <= s_in) if rev else (row >= s_in)
    zeros_v = jnp.zeros((CHUNK, DV_A), F32)
    pairs = range(H_A // 2)
    att, qs_l, u_l, dec_l = {}, {}, {}, {}
    for c, (q, k, v, la) in enumerate(chunks):
        b = _mm_exact_lhs(tri, la)
        bref = b[mid:mid + 1, :]
        blast = b[last:last + 1, :]
        qg = q * jnp.exp(b - bref) * scale
        kg = k * jnp.exp(bref - b)
        kd = k * jnp.exp(blast - b)
        qs_l[c] = q * jnp.exp(b) * scale
        for p in pairs:
            ls = slice(p * LANES, (p + 1) * LANES)
            kg_p = kg[:, ls]
            rhs_att = jnp.concatenate([jnp.where(lane < DK_A, kg_p, 0.0), jnp.where(lane >= DK_A, kg_p, 0.0)], axis=0)
            att[c, p] = jnp.where(causal, _mm(qg[:, ls], rhs_att, NT), 0.0)
            u_l[c, p] = _mm(kd[:, ls], v[:, 2 * p * DV_A:(2 * p + 2) * DV_A], TN)
            dec_l[c, p] = jnp.exp(_mm_exact_lhs_tn(la[:, ls], ones))
    start = {}
    for c in order:
        start[c] = list(states)
        nxt = []
        for p in pairs:
            u, dec = u_l[c, p], dec_l[c, p]
            nxt.append(dec[0:DK_A] * states[2 * p] + u[0:DK_A, 0:DV_A])
            nxt.append(dec[DK_A:2 * DK_A] * states[2 * p + 1] + u[DK_A:2 * DK_A, DV_A:2 * DV_A])
        states = nxt
    outs = []
    for c, (q, k, v, la) in enumerate(chunks):
        o = []
        for p in pairs:
            ls = slice(p * LANES, (p + 1) * LANES)
            vs0 = v[:, (2 * p) * DV_A:(2 * p + 1) * DV_A]
            vs1 = v[:, (2 * p + 1) * DV_A:(2 * p + 2) * DV_A]
            s0, s1 = start[c][2 * p], start[c][2 * p + 1]
            rhs_o = jnp.concatenate([jnp.concatenate([vs0, zeros_v], axis=1),
                                     jnp.concatenate([zeros_v, vs1], axis=1),
                                     jnp.concatenate([s0, zeros_v], axis=1),
                                     jnp.concatenate([zeros_v, s1], axis=1)], axis=0)
            lhs_o = jnp.concatenate([att[c, p], qs_l[c][:, ls]], axis=1)
            o.append(_mm(lhs_o, rhs_o))
        outs.append(jnp.concatenate(o, axis=1))
    return outs, states


def _gla_kernel(qf_ref, qb_ref, sf_ref, sb_ref, wlr_ref, blr_ref, init_ref, of_ref, ob_ref, so_ref, s_ref,
                *, n_prompt_units, units_per_seq):
    j = pl.program_id(0)
    _, _, _, first = _unit_ids(j, n_prompt_units, units_per_seq)

    @pl.when(first)
    def _():
        s_ref[...] = init_ref[0]

    for d in range(N_DIR):
        x_ref, sm_ref, o_ref = (qf_ref, sf_ref, of_ref) if d == 0 else (qb_ref, sb_ref, ob_ref)
        la_unit = _log_sigmoid(_mm3(sm_ref[...], wlr_ref[d]) + blr_ref[d]) * (1.0 / GLA_NORMALIZER)
        states = [s_ref[d, h] for h in range(H_A)]
        order = range(CHUNKS_PER_UNIT - 1, -1, -1) if d == 1 else range(CHUNKS_PER_UNIT)
        chunks = []
        for c in range(CHUNKS_PER_UNIT):
            rows = slice(c * CHUNK, (c + 1) * CHUNK)
            chunks.append((x_ref[rows, 0:QA], x_ref[rows, QA:2 * QA], x_ref[rows, 2 * QA:2 * QA + VA], la_unit[rows]))
        outs, states = _gla_unit(chunks, order, states, rev=(d == 1))
        for c in range(CHUNKS_PER_UNIT):
            o_ref[c * CHUNK:(c + 1) * CHUNK, :] = outs[c]
        for h in range(H_A):
            s_ref[d, h] = states[h]
    so_ref[0] = s_ref[...]


def _gla(proj, wlr, blr, init, n_prompt_units, units_per_seq):
    n_tok = proj.shape[0]
    n_units = n_tok // UNIT
    ids = functools.partial(_unit_ids, n_prompt_units=n_prompt_units, units_per_seq=units_per_seq)
    small_blk = C_SMALL // LANES
    st_blk = (1, N_DIR, H_A, DK_A, DV_A)
    kern = functools.partial(_gla_kernel, n_prompt_units=n_prompt_units, units_per_seq=units_per_seq)
    return pl.pallas_call(
        kern,
        grid=(n_units,),
        in_specs=[pl.BlockSpec((UNIT, 1024), lambda j: (ids(j)[0], 0)),
                  pl.BlockSpec((UNIT, 1024), lambda j: (ids(j)[1], 0)),
                  pl.BlockSpec((UNIT, LANES), lambda j: (ids(j)[0], small_blk)),
                  pl.BlockSpec((UNIT, LANES), lambda j: (ids(j)[1], small_blk)),
                  pl.BlockSpec((N_DIR, LANES, QA), lambda j: (0, 0, 0)),
                  pl.BlockSpec((N_DIR, 1, QA), lambda j: (0, 0, 0)),
                  pl.BlockSpec(st_blk, lambda j: (ids(j)[2], 0, 0, 0, 0))],
        out_specs=[pl.BlockSpec((UNIT, VA), lambda j: (ids(j)[0], 0)),
                   pl.BlockSpec((UNIT, VA), lambda j: (ids(j)[1], 0)),
                   pl.BlockSpec(st_blk, lambda j: (jnp.minimum(j, n_prompt_units), 0, 0, 0, 0))],
        out_shape=[jax.ShapeDtypeStruct((n_tok, VA), F32),
                   jax.ShapeDtypeStruct((n_tok, VA), F32),
                   jax.ShapeDtypeStruct((n_prompt_units + 1, N_DIR, H_A, DK_A, DV_A), F32)],
        scratch_shapes=[pltpu.VMEM((N_DIR, H_A, DK_A, DV_A), F32)],
        compiler_params=_cparams(("arbitrary",)),
        name="gla",
    )(proj, proj, proj, proj, wlr, blr, init)


def _stack_masks(rev):
    r = _iota2((HG, HG), 0)
    c = _iota2((HG, HG), 1)
    same = (r // CHUNK) == (c // CHUNK)
    tr, tc = r % CHUNK, c % CHUNK
    incl = jnp.logical_and(same, (tr <= tc) if rev else (tr >= tc))
    strict = jnp.logical_and(same, (tr < tc) if rev else (tr > tc))
    return incl, strict


def _spread(x):
    z = jnp.zeros((CHUNK, LANES), x.dtype)
    rows = []
    for h in range(H_B):
        xh = x[h * CHUNK:(h + 1) * CHUNK]
        rows.append(jnp.concatenate([xh if g == h else z for g in range(H_B)], axis=1))
    return jnp.concatenate(rows, axis=0)


def _gdn_prepare(chunks):
    masks = {rev: _stack_masks(rev) for rev in {ch[6] for ch in chunks}}
    r = _iota2((HG, HG), 0)
    c = _iota2((HG, HG), 1)
    a_l, att_l, kb_l = [], [], []
    for q, k, v, gcb, beta, glast, rev in chunks:
        incl, strict = masks[rev]
        grow = gcb.T[0:1, :]
        diff = gcb[:, 0:1] - grow
        decay = jnp.where(incl, jnp.exp(jnp.where(incl, diff, 0.0)), 0.0)
        kb = k * beta
        m1 = _mm(jnp.concatenate([kb, q], axis=0), k, NT)
        a_l.append(jnp.where(strict, m1[0:HG] * decay, 0.0))
        att_l.append(m1[HG:2 * HG] * decay)
        kb_l.append(kb)
    diag = (r // INV_BLOCK) == (c // INV_BLOCK)
    pw_l = [jnp.where(diag, a, 0.0) for a in a_l]
    qm_l = [-p for p in pw_l]
    n = 2
    while n < INV_BLOCK:
        pw_l = [_mm(p, p) for p in pw_l]
        qm_l = [qm + p + _mm(qm, p) for qm, p in zip(qm_l, pw_l)]
        n *= 2
    b = INV_BLOCK
    while b < CHUNK:
        off = jnp.logical_and((r // (2 * b)) == (c // (2 * b)), (r // b) != (c // b))
        al_l = [jnp.where(off, a, 0.0) for a in a_l]
        t1_l = [al + _mm(qm, al) for qm, al in zip(qm_l, al_l)]
        qm_l = [qm - (t1 + _mm(t1, qm)) for qm, t1 in zip(qm_l, t1_l)]
        b *= 2
    out = []
    for (q, k, v, gcb, beta, glast, _), qm, kb, att in zip(chunks, qm_l, kb_l, att_l):
        egc = jnp.exp(gcb)
        rhs = jnp.concatenate([v * beta, kb * egc], axis=1)
        sol = rhs + _mm(qm, rhs)
        out.append((sol[:, 0:DV_B], sol[:, DV_B:2 * DV_B], q * egc, k * jnp.exp(glast - gcb), att))
    return out


def _gdn_scan_step(value, k_cum, q_dec, k_dec, att, gl_rows, s):
    kq = _mm(jnp.concatenate([_spread(k_cum), _spread(q_dec)], axis=0), s)
    v_new = value - kq[0:HC]
    o = kq[HC:2 * HC] + _mm(att, v_new)
    s_new = s * gl_rows + _mm(_spread(k_dec), v_new, TN)
    return o, s_new


def _gdn_kernel(xf_ref, xb_ref, sf_ref, sb_ref, gco_ref, gdt_ref, init_ref, of_ref, ob_ref, so_ref, s_ref,
                *, n_prompt_units, units_per_seq):
    j = pl.program_id(0)
    _, _, _, first = _unit_ids(j, n_prompt_units, units_per_seq)

    @pl.when(first)
    def _():
        s_ref[...] = init_ref[0]

    chunks, gl_rows = [], []
    for d in range(N_DIR):
        x_ref, sm_ref = (xf_ref, sf_ref) if d == 0 else (xb_ref, sb_ref)
        rev = d == 1
        sm = sm_ref[...]
        g_all = gco_ref[...] * _softplus(sm + gdt_ref[...])
        beta_all = _sigmoid(sm)
        tri = _tri(rev).astype(BF16)
        last = 0 if rev else CHUNK - 1
        for c in range(CHUNKS_PER_UNIT):
            rows = slice(c * CHUNK, (c + 1) * CHUNK)
            gc_all = _mm_exact_lhs(tri, g_all[rows])
            gcb, beta, glast = [], [], []
            for h in range(H_B):
                col = SM_AB + d * H_B + h
                colb = SM_BETA + d * H_B + h
                gh = jnp.broadcast_to(gc_all[:, col:col + 1], (CHUNK, LANES))
                gcb.append(gh)
                glast.append(jnp.broadcast_to(gh[last:last + 1, :], (CHUNK, LANES)))
                beta.append(jnp.broadcast_to(beta_all[rows, colb:colb + 1], (CHUNK, LANES)))
            for g0 in range(0, H_B, GDN_GROUP):
                hs = range(g0, g0 + GDN_GROUP)
                stack = lambda base: jnp.concatenate(
                    [x_ref[rows, base + h * LANES:base + (h + 1) * LANES] for h in hs], axis=0)
                cat = lambda parts: jnp.concatenate([parts[h] for h in hs], axis=0)
                chunks.append((stack(0), stack(QB), stack(2 * QB), cat(gcb), cat(beta), cat(glast), rev))
            gl_rows.append(jnp.concatenate([jnp.broadcast_to(jnp.exp(g[0:1, :]), (DK_B, DV_B)) for g in glast],
                                           axis=0))
    n_grp = H_B // GDN_GROUP
    zero_att = jnp.zeros((HG, HG), F32)
    prepared = []
    grouped = _gdn_prepare(chunks)
    for i in range(0, len(grouped), n_grp):
        parts = grouped[i:i + n_grp]
        rows_cat = [jnp.concatenate([p[f] for p in parts], axis=0) for f in range(4)]
        att = jnp.concatenate([jnp.concatenate([parts[g][4] if gg == g else zero_att for gg in range(n_grp)], axis=1)
                               for g in range(n_grp)], axis=0)
        prepared.append(tuple(rows_cat) + (att,))
    s = [s_ref[d] for d in range(N_DIR)]
    for step in range(CHUNKS_PER_UNIT):
        for d, o_ref in ((0, of_ref), (1, ob_ref)):
            c = step if d == 0 else CHUNKS_PER_UNIT - 1 - step
            o, s[d] = _gdn_scan_step(*prepared[d * CHUNKS_PER_UNIT + c], gl_rows[d * CHUNKS_PER_UNIT + c], s[d])
            for h in range(H_B):
                o_ref[c * CHUNK:(c + 1) * CHUNK, h * DV_B:(h + 1) * DV_B] = o[h * CHUNK:(h + 1) * CHUNK]
    for d in range(N_DIR):
        s_ref[d] = s[d]
    so_ref[0] = s_ref[...]


def _gdn(proj, gcoef, gdt, init, n_prompt_units, units_per_seq):
    n_tok = proj.shape[0]
    n_units = n_tok // UNIT
    ids = functools.partial(_unit_ids, n_prompt_units=n_prompt_units, units_per_seq=units_per_seq)
    small_blk = C_SMALL // LANES
    qkv_blk = C_QKVB // (3 * QB)
    st_blk = (1, N_DIR, H_B * DK_B, DV_B)
    kern = functools.partial(_gdn_kernel, n_prompt_units=n_prompt_units, units_per_seq=units_per_seq)
    return pl.pallas_call(
        kern,
        grid=(n_units,),
        in_specs=[pl.BlockSpec((UNIT, 3 * QB), lambda j: (ids(j)[0], qkv_blk)),
                  pl.BlockSpec((UNIT, 3 * QB), lambda j: (ids(j)[1], qkv_blk)),
                  pl.BlockSpec((UNIT, LANES), lambda j: (ids(j)[0], small_blk)),
                  pl.BlockSpec((UNIT, LANES), lambda j: (ids(j)[1], small_blk)),
                  pl.BlockSpec((1, LANES), lambda j: (0, 0)),
                  pl.BlockSpec((1, LANES), lambda j: (0, 0)),
                  pl.BlockSpec(st_blk, lambda j: (ids(j)[2], 0, 0, 0))],
        out_specs=[pl.BlockSpec((UNIT, VB), lambda j: (ids(j)[0], 0)),
                   pl.BlockSpec((UNIT, VB), lambda j: (ids(j)[1], 0)),
                   pl.BlockSpec(st_blk, lambda j: (jnp.minimum(j, n_prompt_units), 0, 0, 0))],
        out_shape=[jax.ShapeDtypeStruct((n_tok, VB), F32),
                   jax.ShapeDtypeStruct((n_tok, VB), F32),
                   jax.ShapeDtypeStruct((n_prompt_units + 1, N_DIR, H_B * DK_B, DV_B), F32)],
        scratch_shapes=[pltpu.VMEM((N_DIR, H_B * DK_B, DV_B), F32)],
        compiler_params=_cparams(("arbitrary",)),
        name="gdn",
    )(proj, proj, proj, proj, gcoef, gdt, init)


def _head_rms(o, w):
    parts = []
    for h in range(o.shape[1] // LANES):
        parts.append(_rms(o[:, h * LANES:(h + 1) * LANES]))
    return jnp.concatenate(parts, axis=1) * w


def _post_kernel(xp_ref, xs_ref, ga_ref, zb_ref, af_ref, ab_ref, df_ref, db_ref, mod_ref, wo_ref, nwa_ref, nwb_ref,
                 n2_ref, rw_ref, x1_ref, h2_ref, sc_ref, *, tiles_prompt):
    m = mod_ref[0]
    x = jnp.where(pl.program_id(0) < tiles_prompt, xp_ref[...], xs_ref[...])
    gla = _head_rms(af_ref[...] + ab_ref[...], nwa_ref[...]) * _silu(ga_ref[...])
    gdn = _head_rms(df_ref[...] + db_ref[...], nwb_ref[...]) * _silu(zb_ref[...])
    y = _dg(gla.astype(BF16), wo_ref[0:VA, :], NN) + _dg(gdn.astype(BF16), wo_ref[VA:VA + VB, :], NN)
    x1 = x + m[:, 2 * D_MODEL:3 * D_MODEL] * y
    x1_ref[...] = x1
    h2 = (_rms(x1) * n2_ref[...]) * (1.0 + m[:, 4 * D_MODEL:5 * D_MODEL]) + m[:, 3 * D_MODEL:4 * D_MODEL]
    h2_ref[...] = _pack_rows(h2)
    sc_ref[...] = _sigmoid(_mm3(rw_ref[...], h2, NT))


def _post(xp, xs, proj, o_af, o_ab, o_df, o_db, mod3, w_o, nwa, nwb, n2w, router_wt, sample_len, tm):
    n_prompt_tok = xp.shape[0]
    n_tok = n_prompt_tok + xs.shape[0]
    cond = functools.partial(_cond_row, tiles_prompt=n_prompt_tok // tm, tiles_per_seq=sample_len // tm)
    tok = lambda i: (i, 0)
    const = lambda i: (0, 0)
    return pl.pallas_call(
        functools.partial(_post_kernel, tiles_prompt=n_prompt_tok // tm),
        grid=(n_tok // tm,),
        in_specs=_two_streams(n_prompt_tok, tm) + [
                  pl.BlockSpec((tm, VA), lambda i: (i, C_GA // VA)),
                  pl.BlockSpec((tm, VB), lambda i: (i, C_ZB // VB)),
                  pl.BlockSpec((tm, VA), tok), pl.BlockSpec((tm, VA), tok),
                  pl.BlockSpec((tm, VB), tok), pl.BlockSpec((tm, VB), tok),
                  pl.BlockSpec((1, 1, 6 * D_MODEL), lambda i: (cond(i), 0, 0)),
                  pl.BlockSpec((VA + VB, D_MODEL), const),
                  pl.BlockSpec((1, VA), const), pl.BlockSpec((1, VB), const), pl.BlockSpec((1, D_MODEL), const),
                  pl.BlockSpec((N_EXPERTS, D_MODEL), const)],
        out_specs=[pl.BlockSpec((tm, D_MODEL), tok),
                   pl.BlockSpec((tm, D_PACK), tok),
                   pl.BlockSpec((N_EXPERTS, tm), lambda i: (0, i))],
        out_shape=[jax.ShapeDtypeStruct((n_tok, D_MODEL), F32),
                   jax.ShapeDtypeStruct((n_tok, D_PACK), jnp.uint32),
                   jax.ShapeDtypeStruct((N_EXPERTS, n_tok), F32)],
        compiler_params=_cparams(("parallel",)),
        name="post",
    )(xp, xs, proj, proj, o_af, o_ab, o_df, o_db, mod3, w_o, nwa, nwb, n2w, router_wt)


def _route_kernel(sc_ref, bias_ref, e_ref, r_ref, w_ref, cnt_ref, carry_ref):
    i = pl.program_id(0)
    t = sc_ref.shape[1]

    @pl.when(i == 0)
    def _():
        carry_ref[...] = jnp.zeros(carry_ref.shape, F32)

    s = sc_ref[...]
    biased = s + bias_ref[:, 0:1]
    gs = []
    for g in range(N_GROUP):
        blk = biased[g * GROUP_SIZE:(g + 1) * GROUP_SIZE]
        m1 = jnp.max(blk, axis=0, keepdims=True)
        n1 = jnp.sum((blk == m1).astype(F32), axis=0, keepdims=True)
        m2 = jnp.max(jnp.where(blk < m1, blk, NEG_INF), axis=0, keepdims=True)
        gs.append(m1 + jnp.where(n1 >= 2.0, m1, m2))
    gsc = jnp.concatenate(gs, axis=0)
    gid = _iota2((N_GROUP, t), 0)
    beaten = jnp.zeros((N_GROUP, t), F32)
    for g in range(N_GROUP):
        other = gsc[g:g + 1, :]
        wins = jnp.logical_or(other > gsc, jnp.logical_and(other == gsc, g < gid))
        beaten = beaten + wins.astype(F32)
    masked = jnp.concatenate(
        [jnp.where(beaten[g:g + 1, :] < float(TOPK_GROUP), biased[g * GROUP_SIZE:(g + 1) * GROUP_SIZE], NEG_INF)
         for g in range(N_GROUP)], axis=0)
    eid = _iota2((N_EXPERTS, t), 0).astype(F32)
    sel = jnp.zeros((N_EXPERTS, t), F32)
    picks, scores = [], []
    for _ in range(TOP_K):
        m = jnp.max(masked, axis=0, keepdims=True)
        first = jnp.min(jnp.where(masked == m, eid, float(N_EXPERTS)), axis=0, keepdims=True)
        hit = eid == first
        scores.append(jnp.sum(jnp.where(hit, s, 0.0), axis=0, keepdims=True))
        masked = jnp.where(hit, NEG_INF, masked)
        sel = sel + hit.astype(F32)
        picks.append(first)
    upper = (_iota2((t, t), 0) < _iota2((t, t), 1)).astype(BF16)
    carry = carry_ref[...]
    prefix = _dg(sel.astype(BF16), upper, NN) + jnp.concatenate([carry] * (t // LANES), axis=1)
    ranks = [jnp.sum(jnp.where(eid == p, prefix, 0.0), axis=0, keepdims=True) for p in picks]
    carry = carry + _dg(sel.astype(BF16), jnp.ones((t, LANES), BF16), NN)
    carry_ref[...] = carry
    cnt_ref[...] = carry
    sc8 = jnp.concatenate(scores, axis=0)
    e_ref[...] = jnp.concatenate(picks, axis=0).astype(jnp.int32)
    r_ref[...] = jnp.concatenate(ranks, axis=0).astype(jnp.int32)
    w_ref[...] = sc8 / jnp.sum(sc8, axis=0, keepdims=True) * ROUTED_SCALE


def _route(scores_t, bias_col):
    n_tok = scores_t.shape[1]
    t = ROUTE_TILE
    slot = lambda i: (0, i)
    return pl.pallas_call(
        _route_kernel,
        grid=(n_tok // t,),
        in_specs=[pl.BlockSpec((N_EXPERTS, t), slot),
                  pl.BlockSpec((N_EXPERTS, LANES), lambda i: (0, 0))],
        out_specs=[pl.BlockSpec((TOP_K, t), slot), pl.BlockSpec((TOP_K, t), slot), pl.BlockSpec((TOP_K, t), slot),
                   pl.BlockSpec((N_EXPERTS, LANES), lambda i: (0, 0))],
        out_shape=[jax.ShapeDtypeStruct((TOP_K, n_tok), jnp.int32),
                   jax.ShapeDtypeStruct((TOP_K, n_tok), jnp.int32),
                   jax.ShapeDtypeStruct((TOP_K, n_tok), F32),
                   jax.ShapeDtypeStruct((N_EXPERTS, LANES), F32)],
        scratch_shapes=[pltpu.VMEM((N_EXPERTS, LANES), F32)],
        compiler_params=_cparams(("arbitrary",)),
        name="route",
    )(scores_t, bias_col)


def _dest_kernel(start_ref, e_ref, r_ref, d_ref):
    e = e_ref[...]

    def body(x, acc):
        return jnp.where(e == x, start_ref[x], acc)

    d_ref[...] = r_ref[...] + lax.fori_loop(0, N_EXPERTS, body, jnp.zeros(e.shape, jnp.int32))


def _dest(start, e8, r8, tt):
    n_tok = e8.shape[1]
    slot = lambda i, st: (0, i)
    return pl.pallas_call(
        _dest_kernel,
        grid_spec=pltpu.PrefetchScalarGridSpec(
            num_scalar_prefetch=1, grid=(n_tok // tt,),
            in_specs=[pl.BlockSpec((TOP_K, tt), slot), pl.BlockSpec((TOP_K, tt), slot)],
            out_specs=pl.BlockSpec((TOP_K, tt), slot)),
        out_shape=jax.ShapeDtypeStruct((TOP_K, n_tok), jnp.int32),
        compiler_params=_cparams(("parallel",)),
        name="dest",
    )(start, e8, r8)


def _items(counts, n_rows):
    n_blocks = n_rows // ROW_BLOCK
    max_items = n_blocks + N_EXPERTS - 1
    end = jnp.cumsum(counts)
    start = end - counts
    first_blk = start // ROW_BLOCK
    n_it = jnp.where(counts > 0, (end - 1) // ROW_BLOCK - first_blk + 1, 0)
    it_end = jnp.cumsum(n_it)
    it_start = it_end - n_it
    i = jnp.arange(max_items, dtype=jnp.int32)
    valid = i < it_end[-1]
    ex = jnp.minimum(jnp.sum((it_end[None, :] <= i[:, None]).astype(jnp.int32), axis=1), N_EXPERTS - 1)
    onehot = ex[:, None] == jnp.arange(N_EXPERTS, dtype=jnp.int32)[None, :]
    pick = lambda tab: jnp.sum(jnp.where(onehot, tab[None, :], 0), axis=1)
    blk = pick(first_blk) + (i - pick(it_start))
    lo = jnp.maximum(pick(start), blk * ROW_BLOCK) - blk * ROW_BLOCK
    hi = jnp.minimum(pick(end), (blk + 1) * ROW_BLOCK) - blk * ROW_BLOCK
    blk = jnp.where(valid, blk, n_blocks - 1).astype(jnp.int32)
    lo = jnp.where(valid, lo, 0).astype(jnp.int32)
    hi = jnp.where(valid, hi, 0).astype(jnp.int32)
    eids = jnp.arange(N_EXPERTS, dtype=jnp.int32)
    later = jnp.logical_and(eids[None, :] > eids[:, None], (counts > 0)[None, :])
    nxt_e = jnp.min(jnp.where(later, eids[None, :], N_EXPERTS), axis=1)
    nxt = pick(jnp.where(nxt_e < N_EXPERTS, nxt_e, -1)).astype(jnp.int32)
    return start.astype(jnp.int32), blk, ex.astype(jnp.int32), nxt, lo, hi


def _dispatch_kernel(h2_ref, dest_hbm, xs_hbm, dsm, sem_d, sem_s):
    i = pl.program_id(0)
    n = pl.num_programs(0)
    tm = h2_ref.shape[0]

    def dest_copy(step, slot):
        return pltpu.make_async_copy(dest_hbm.at[:, pl.ds(step * tm, tm)], dsm.at[slot], sem_d.at[slot])

    def row_copy(t, dst):
        return pltpu.make_async_copy(h2_ref.at[pl.ds(t, 1), :], xs_hbm.at[pl.ds(dst, 1), :], sem_s.at[0])

    @pl.when(i == 0)
    def _():
        dest_copy(0, 0).start()

    slot = i % 2
    dest_copy(i, slot).wait()

    @pl.when(i + 1 < n)
    def _():
        dest_copy(i + 1, 1 - slot).start()

    def issue(t, carry):
        for k in range(TOP_K):
            row_copy(t, dsm[slot, k, t]).start(priority=k % N_DMA_QUEUES)
        return carry
    lax.fori_loop(0, tm, issue, 0, unroll=2)

    def drain(t, carry):
        for k in range(TOP_K):
            row_copy(t, 0).wait()
        return carry
    lax.fori_loop(0, tm, drain, 0)


def _dispatch(h2, dest, tm):
    n_tok = h2.shape[0]
    return pl.pallas_call(
        _dispatch_kernel,
        grid=(n_tok // tm,),
        in_specs=[pl.BlockSpec((tm, D_PACK), lambda i: (i, 0)),
                  pl.BlockSpec(memory_space=pl.ANY)],
        out_specs=pl.BlockSpec(memory_space=pl.ANY),
        out_shape=jax.ShapeDtypeStruct((n_tok * TOP_K, D_PACK), jnp.uint32),
        scratch_shapes=[pltpu.SMEM((2, TOP_K, tm), jnp.int32),
                        pltpu.SemaphoreType.DMA((2,)),
                        pltpu.SemaphoreType.DMA((1,))],
        compiler_params=_cparams(("arbitrary",)),
        name="dispatch",
    )(h2, dest)


def _experts_kernel(blk_ref, exp_ref, nxt_ref, lo_ref, hi_ref, x_ref, wg_hbm, wu_hbm, wd_hbm, y_ref,
                    wgf, wuf, wdf, wgb, wub, wdb, nchg, sem_w):
    i = pl.program_id(0)
    lo, hi = lo_ref[i], hi_ref[i]

    def weight_copies(e, slot):
        return (pltpu.make_async_copy(wg_hbm.at[e], wgf.at[slot], sem_w.at[slot]),
                pltpu.make_async_copy(wu_hbm.at[e], wuf.at[slot], sem_w.at[slot]),
                pltpu.make_async_copy(wd_hbm.at[e], wdf.at[slot], sem_w.at[slot]))

    @pl.when(i == 0)
    def _():
        nchg[0] = 0
        for cp in weight_copies(exp_ref[0], 0):
            cp.start()

    @pl.when(hi > lo)
    def _():
        @pl.when(jnp.logical_or(i == 0, exp_ref[i] != exp_ref[jnp.maximum(i - 1, 0)]))
        def _():
            slot = nchg[0] % 2
            for cp in weight_copies(exp_ref[i], slot):
                cp.wait()

            @pl.when(nxt_ref[i] >= 0)
            def _():
                for cp in weight_copies(nxt_ref[i], 1 - slot):
                    cp.start()

            wgb[...] = wgf[slot].astype(BF16)
            wub[...] = wuf[slot].astype(BF16)
            wdb[...] = wdf[slot].astype(BF16)
            nchg[0] = nchg[0] + 1

        x_lo, x_hi = _unpack_rows(x_ref[...])
        x = jnp.concatenate([x_lo.astype(BF16), x_hi.astype(BF16)], axis=1)
        g = _dg(x, wgb[...], NN)
        u = _dg(x, wub[...], NN)
        hmid = (_silu(g) * u).astype(BF16)
        y = _pack_rows(_dg(hmid, wdb[...], NN))
        row = _iota2((ROW_BLOCK, D_PACK), 0)
        mine = jnp.logical_and(row >= lo, row < hi)

        @pl.when(lo == 0)
        def _():
            y_ref[...] = jnp.where(mine, y, jnp.uint32(0))

        @pl.when(lo > 0)
        def _():
            y_ref[...] = jnp.where(mine, y, y_ref[...])


def _experts(item_blk, item_exp, item_nxt, item_lo, item_hi, xs, w_gate, w_up, w_down):
    n_items = item_blk.shape[0]
    xmap = lambda i, blk, ex, nxt, lo, hi: (blk[i], 0)
    return pl.pallas_call(
        _experts_kernel,
        grid_spec=pltpu.PrefetchScalarGridSpec(
            num_scalar_prefetch=5, grid=(n_items,),
            in_specs=[pl.BlockSpec((ROW_BLOCK, D_PACK), xmap),
                      pl.BlockSpec(memory_space=pl.ANY),
                      pl.BlockSpec(memory_space=pl.ANY),
                      pl.BlockSpec(memory_space=pl.ANY)],
            out_specs=pl.BlockSpec((ROW_BLOCK, D_PACK), xmap),
            scratch_shapes=[pltpu.VMEM((2, D_MODEL, D_EXPERT), F32), pltpu.VMEM((2, D_MODEL, D_EXPERT), F32),
                            pltpu.VMEM((2, D_EXPERT, D_MODEL), F32),
                            pltpu.VMEM((D_MODEL, D_EXPERT), BF16), pltpu.VMEM((D_MODEL, D_EXPERT), BF16),
                            pltpu.VMEM((D_EXPERT, D_MODEL), BF16),
                            pltpu.SMEM((1,), jnp.int32),
                            pltpu.SemaphoreType.DMA((2,))]),
        out_shape=jax.ShapeDtypeStruct(xs.shape, jnp.uint32),
        compiler_params=_cparams(("arbitrary",)),
        name="experts",
    )(item_blk, item_exp, item_nxt, item_lo, item_hi, xs, w_gate, w_up, w_down)


def _final_kernel(x1_ref, h2_ref, w_ref, mod_ref, sg_ref, su_ref, sd_ref, fn_ref, dest_hbm, ys_hbm, op_ref, os_ref,
                  dsm, gbuf, sem_d, sem_g, *, tiles_prompt):
    i = pl.program_id(0)
    n = pl.num_programs(0)
    tm = x1_ref.shape[0]

    def dest_copy(step, slot):
        return pltpu.make_async_copy(dest_hbm.at[:, pl.ds(step * tm, tm)], dsm.at[slot], sem_d.at[slot])

    def row_copy(src, slot, k, t):
        return pltpu.make_async_copy(ys_hbm.at[pl.ds(src, 1), :], gbuf.at[slot, k, pl.ds(t, 1), :], sem_g.at[slot])

    def issue_gathers(slot3, slot2):
        def body(t, carry):
            for k in range(TOP_K):
                row_copy(dsm[slot3, k, t], slot2, k, t).start(priority=k % N_DMA_QUEUES)
            return carry
        lax.fori_loop(0, tm, body, 0, unroll=2)

    @pl.when(i == 0)
    def _():
        dest_copy(0, 0).start()
        dest_copy(0, 0).wait()
        issue_gathers(0, 0)

        @pl.when(n > 1)
        def _():
            dest_copy(1, 1).start()

    @pl.when(i + 1 < n)
    def _():
        dest_copy(i + 1, (i + 1) % 3).wait()
        issue_gathers((i + 1) % 3, (i + 1) % 2)

        @pl.when(i + 2 < n)
        def _():
            dest_copy(i + 2, (i + 2) % 3).start()

    slot = i % 2

    def drain(t, carry):
        for k in range(TOP_K):
            row_copy(0, slot, k, t).wait()
        return carry
    lax.fori_loop(0, tm, drain, 0)

    m = mod_ref[0]
    w = w_ref[...]
    r_lo = r_hi = None
    for kk in range(TOP_K):
        y_lo, y_hi = _unpack_rows(gbuf[slot, kk])
        wk = w[:, kk:kk + 1]
        r_lo = y_lo * wk if r_lo is None else r_lo + y_lo * wk
        r_hi = y_hi * wk if r_hi is None else r_hi + y_hi * wk
    routed = jnp.concatenate([r_lo, r_hi], axis=1)
    h_lo, h_hi = _unpack_rows(h2_ref[...])
    hb = jnp.concatenate([h_lo.astype(BF16), h_hi.astype(BF16)], axis=1)
    hm = (_silu(_dg(hb, sg_ref[...], NN)) * _dg(hb, su_ref[...], NN)).astype(BF16)
    shared = _dg(hm, sd_ref[...], NN)
    x2 = x1_ref[...] + m[:, 5 * D_MODEL:6 * D_MODEL] * (routed + shared)
    out = _rms(x2) * fn_ref[...]

    @pl.when(i < tiles_prompt)
    def _():
        op_ref[...] = out

    @pl.when(i >= tiles_prompt)
    def _():
        os_ref[...] = out


def _final(x1, h2, wts, mod3, sg, su, sd, fnw, dest, ys, n_prompt_tok, sample_len, tm):
    n_tok = x1.shape[0]
    cond = functools.partial(_cond_row, tiles_prompt=n_prompt_tok // tm, tiles_per_seq=sample_len // tm)
    tok = lambda i: (i, 0)
    const = lambda i: (0, 0)
    tp = n_prompt_tok // tm
    return pl.pallas_call(
        functools.partial(_final_kernel, tiles_prompt=tp),
        grid=(n_tok // tm,),
        in_specs=[pl.BlockSpec((tm, D_MODEL), tok),
                  pl.BlockSpec((tm, D_PACK), tok),
                  pl.BlockSpec((tm, TOP_K), tok),
                  pl.BlockSpec((1, 1, 6 * D_MODEL), lambda i: (cond(i), 0, 0)),
                  pl.BlockSpec((D_MODEL, D_EXPERT), const),
                  pl.BlockSpec((D_MODEL, D_EXPERT), const),
                  pl.BlockSpec((D_EXPERT, D_MODEL), const),
                  pl.BlockSpec((1, D_MODEL), const),
                  pl.BlockSpec(memory_space=pl.ANY),
                  pl.BlockSpec(memory_space=pl.ANY)],
        out_specs=[pl.BlockSpec((tm, D_MODEL), lambda i: (jnp.minimum(i, tp - 1), 0)),
                   pl.BlockSpec((tm, D_MODEL), lambda i: (jnp.maximum(i - tp, 0), 0))],
        out_shape=[jax.ShapeDtypeStruct((n_prompt_tok, D_MODEL), F32),
                   jax.ShapeDtypeStruct((n_tok - n_prompt_tok, D_MODEL), F32)],
        scratch_shapes=[pltpu.SMEM((3, TOP_K, tm), jnp.int32),
                        pltpu.VMEM((2, TOP_K, tm, D_PACK), jnp.uint32),
                        pltpu.SemaphoreType.DMA((3,)),
                        pltpu.SemaphoreType.DMA((2,))],
        compiler_params=_cparams(("arbitrary",)),
        name="final",
    )(x1, h2, wts, mod3, sg, su, sd, fnw, dest, ys)


def kernel(x_prompt, x_sample, state_gla, state_gdn, c, c_ctx, w_ada, b_ada, norm1_w, w_in, conv_w, gla_lr_w, gla_lr_b, gdn_a_log, gdn_dt_bias, gla_norm_w, gdn_norm_w, w_o, norm2_w, router_w, router_bias, exp_w_gate, exp_w_up, exp_w_down, sh_w_gate, sh_w_up, sh_w_down, final_norm_w):
    bp, lp, d = x_prompt.shape
    bs, ls, _ = x_sample.shape
    assert d == D_MODEL and lp == UNIT and ls % UNIT == 0 and w_ada.shape[0] == 1
    n_prompt_tok = bp * lp
    n_tok = n_prompt_tok + bs * ls
    n_prompt_units = n_prompt_tok // UNIT
    units_per_seq = ls // UNIT
    grid_w = 64
    layer = 0
    tm_in = 512
    tm = 256
    assert n_prompt_tok % tm_in == 0 and ls % tm_in == 0 and n_tok % ROUTE_TILE == 0

    xp = x_prompt.reshape(n_prompt_tok, d)
    xs_in = x_sample.reshape(bs * ls, d)
    cond = jnp.concatenate([c_ctx[None, :], c, jnp.zeros((SUBLANES - 1 - bs, d), F32)], axis=0)
    mod3 = _ada(cond, w_ada[layer], b_ada[layer][None, :]).reshape(SUBLANES, 1, 6 * d)

    wi = w_in[layer]
    o_lr = 2 * QA + 2 * VA
    o_qkvb = o_lr + N_DIR * GLA_RANK
    o_zb = o_qkvb + 3 * QB
    o_ab = o_zb + VB
    w_main = jnp.concatenate([wi[:, 0:o_lr], wi[:, o_qkvb:o_ab]], axis=1).astype(BF16)
    w_small = jnp.concatenate([wi[:, o_lr:o_qkvb], wi[:, o_ab:], jnp.zeros((d, LANES - 48), F32)], axis=1)
    cw = jnp.concatenate([conv_w[layer], jnp.zeros((SUBLANES - CONV_K, 3 * QB), F32)], axis=0)
    proj = _inproj(xp, xs_in, mod3, norm1_w[layer][None, :], w_main, w_small, cw, ls, lp, grid_w, tm_in)

    wlr = jnp.zeros((N_DIR, LANES, QA), F32)
    for dd in range(N_DIR):
        wlr = wlr.at[dd, dd * GLA_RANK:(dd + 1) * GLA_RANK, :].set(gla_lr_w[layer, dd])
    blr = gla_lr_b[layer][:, None, :]
    init_gla = jnp.concatenate([jnp.zeros((1,) + state_gla.shape[2:], F32), state_gla[:, layer].astype(F32)], axis=0)
    o_af, o_ab_, s_gla = _gla(proj, wlr, blr, init_gla, n_prompt_units, units_per_seq)

    gcoef = jnp.zeros((1, LANES), F32).at[0, SM_AB:SM_AB + N_DIR * H_B].set(-jnp.exp(gdn_a_log[layer].reshape(-1)))
    gdt = jnp.zeros((1, LANES), F32).at[0, SM_AB:SM_AB + N_DIR * H_B].set(gdn_dt_bias[layer].reshape(-1))
    init_gdn = jnp.concatenate([jnp.zeros((1, N_DIR, H_B * DK_B, DV_B), F32),
                                state_gdn[:, layer].astype(F32).reshape(bs, N_DIR, H_B * DK_B, DV_B)], axis=0)
    o_df, o_db, s_gdn = _gdn(proj, gcoef, gdt, init_gdn, n_prompt_units, units_per_seq)

    nwa = jnp.tile(gla_norm_w[layer], H_A)[None, :]
    nwb = jnp.tile(gdn_norm_w[layer], H_B)[None, :]
    x1, h2, scores_t = _post(xp, xs_in, proj, o_af, o_ab_, o_df, o_db, mod3, w_o[layer].astype(BF16), nwa, nwb,
                             norm2_w[layer][None, :], router_w[layer].T, ls, tm)

    bias_col = jnp.broadcast_to(router_bias[layer].astype(F32)[:, None], (N_EXPERTS, LANES))
    e8, r8, w8, cnt = _route(scores_t, bias_col)
    start, item_blk, item_exp, item_nxt, item_lo, item_hi = _items(cnt[:, 0].astype(jnp.int32), n_tok * TOP_K)
    dest = _dest(start, e8, r8, tt=min(2048, n_tok))

    xs = _dispatch(h2, dest, tm)
    ys = _experts(item_blk, item_exp, item_nxt, item_lo, item_hi, xs, exp_w_gate[layer], exp_w_up[layer],
                  exp_w_down[layer])
    y_p, y_s = _final(x1, h2, w8.T, mod3, sh_w_gate[layer].astype(BF16), sh_w_up[layer].astype(BF16),
                      sh_w_down[layer].astype(BF16), final_norm_w[None, :], dest, ys, n_prompt_tok, ls, tm)

    y_prompt = y_p.reshape(bp, lp, d)
    y_sample = y_s.reshape(bs, ls, d)
    new_state_gla = s_gla[:n_prompt_units].reshape(bp, 1, N_DIR, H_A, DK_A, DV_A).astype(x_prompt.dtype)
    new_state_gdn = s_gdn[:n_prompt_units].reshape(bp, 1, N_DIR, H_B, DK_B, DV_B).astype(x_prompt.dtype)
    return (y_prompt, y_sample, new_state_gla, new_state_gdn)
```

```python
import functools

import jax
import jax.numpy as jnp
from jax import lax
from jax.experimental import pallas as pl
from jax.experimental.pallas import tpu as pltpu

F32 = jnp.float32
BF16 = jnp.bfloat16

D_MODEL = 1024
N_DIR = 2
H_A, DK_A, DV_A = 4, 64, 128
GLA_RANK = 16
GLA_NORMALIZER = 16.0
H_B, DK_B, DV_B = 4, 128, 128
CONV_K = 5
CHUNK = 64
QA, VA = H_A * DK_A, H_A * DV_A
QB, VB = H_B * DK_B, H_B * DV_B
N_EXPERTS = 256
TOP_K = 8
N_GROUP = 8
TOPK_GROUP = 4
GROUP_SIZE = N_EXPERTS // N_GROUP
D_EXPERT = 256
ROUTED_SCALE = 2.5
EPS = 1e-6
NEG_INF = float("-inf")

UNIT = 256
CHUNKS_PER_UNIT = UNIT // CHUNK
HC = H_B * CHUNK
GDN_GROUP = 2
HG = GDN_GROUP * CHUNK
INV_BLOCK = 8
D_PACK = D_MODEL // 2
ROUTE_TILE = 256
ROW_BLOCK = 512
N_DMA_QUEUES = 2
LANES = 128
SUBLANES = 8
VMEM_LIMIT = 56 * 1024 * 1024

C_QKVA = 0
C_GA = 1024
C_QKVB = 1536
C_ZB = 3072
C_SMALL = 3584
D_PROJ = 3712
SM_AB = 32
SM_BETA = 40


def _cparams(sem):
    return pltpu.CompilerParams(dimension_semantics=sem, vmem_limit_bytes=VMEM_LIMIT)


def _split(a):
    hi = a.astype(BF16)
    lo = (a - hi.astype(F32)).astype(BF16)
    return hi, lo


def _dg(a, b, dims):
    return lax.dot_general(a, b, (dims, ((), ())), preferred_element_type=F32)


NN = ((1,), (0,))
NT = ((1,), (1,))
TN = ((0,), (0,))


def _mm(a, b, dims=NN):
    return _dg(a.astype(BF16), b.astype(BF16), dims)


def _mm3(a, b, dims=NN):
    ah, al = _split(a)
    bh, bl = _split(b)
    return _dg(ah, bh, dims) + (_dg(ah, bl, dims) + _dg(al, bh, dims))


def _mm_exact_lhs(a_bf16, b, dims=NN):
    bh, bl = _split(b)
    return _dg(a_bf16, bh, dims) + _dg(a_bf16, bl, dims)


def _silu(x):
    return x * (1.0 / (1.0 + jnp.exp(-x)))


def _sigmoid(x):
    return 1.0 / (1.0 + jnp.exp(-x))


def _softplus(x):
    return jnp.maximum(x, 0.0) + jnp.log1p(jnp.exp(-jnp.abs(x)))


def _log_sigmoid(x):
    return -_softplus(-x)


def _iota2(shape, dim):
    return lax.broadcasted_iota(jnp.int32, shape, dim)


def _rms(x):
    return x * lax.rsqrt(jnp.mean(x * x, axis=-1, keepdims=True) + EPS)


def _pack_rows(x):
    lo = lax.bitcast_convert_type(x[:, 0:D_PACK].astype(BF16).astype(F32), jnp.uint32)
    hi = lax.bitcast_convert_type(x[:, D_PACK:D_MODEL].astype(BF16).astype(F32), jnp.uint32)
    return (lo >> 16) | (hi & jnp.uint32(0xFFFF0000))


def _unpack_rows(p):
    lo = lax.bitcast_convert_type(p << 16, F32)
    hi = lax.bitcast_convert_type(p & jnp.uint32(0xFFFF0000), F32)
    return lo, hi


def _ada_kernel(c_ref, w_ref, b_ref, o_ref):
    o_ref[...] = _mm3(_silu(c_ref[...]), w_ref[...]) + b_ref[...]


def _ada(cond, w, b):
    n = w.shape[1]
    tn = 1536
    return pl.pallas_call(
        _ada_kernel,
        grid=(n // tn,),
        in_specs=[pl.BlockSpec((SUBLANES, D_MODEL), lambda i: (0, 0)),
                  pl.BlockSpec((D_MODEL, tn), lambda i: (0, i)),
                  pl.BlockSpec((1, tn), lambda i: (0, i))],
        out_specs=pl.BlockSpec((SUBLANES, tn), lambda i: (0, i)),
        out_shape=jax.ShapeDtypeStruct((SUBLANES, n), F32),
        compiler_params=_cparams(("parallel",)),
        name="ada",
    )(cond, w, b)


def _two_streams(n_prompt_tok, tm):
    tp = n_prompt_tok // tm
    return [pl.BlockSpec((tm, D_MODEL), lambda i: (jnp.minimum(i, tp - 1), 0)),
            pl.BlockSpec((tm, D_MODEL), lambda i: (jnp.maximum(i - tp, 0), 0))]


def _inproj_kernel(xp_ref, xs_ref, mod_ref, nw_ref, w_ref, ws_ref, cw_ref, o_ref, *, tiles_prompt, prompt_row,
                   sample_row):
    m = mod_ref[0]
    x = jnp.where(pl.program_id(0) < tiles_prompt, xp_ref[...], xs_ref[...])
    h = (_rms(x) * nw_ref[...]) * (1.0 + m[:, D_MODEL:2 * D_MODEL]) + m[:, 0:D_MODEL]
    hb = h.astype(BF16)
    for c0 in range(0, C_SMALL, 512):
        o_ref[:, c0:c0 + 512] = _dg(hb, w_ref[:, c0:c0 + 512], NN)
    o_ref[:, C_SMALL:D_PROJ] = _mm3(h, ws_ref[...])
    tm = xp_ref.shape[0]
    row_len = jnp.where(pl.program_id(0) < tiles_prompt, prompt_row, sample_row)
    pos = _iota2((tm, LANES), 0) & (row_len - 1)
    inside = [jnp.logical_and(pos + (jj - CONV_K // 2) >= 0, pos + (jj - CONV_K // 2) < row_len).astype(F32)
              for jj in range(CONV_K)]
    for part in range(3 * H_B):
        cs = slice(C_QKVB + part * LANES, C_QKVB + (part + 1) * LANES)
        x = o_ref[:, cs]
        acc = x * cw_ref[CONV_K // 2:CONV_K // 2 + 1, part * LANES:(part + 1) * LANES]
        for jj in range(CONV_K):
            off = jj - CONV_K // 2
            if off == 0:
                continue
            xs = pltpu.roll(x, (-off) % tm, 0) * inside[jj]
            acc = acc + xs * cw_ref[jj:jj + 1, part * LANES:(part + 1) * LANES]
        y = _silu(acc)
        if part < 2 * H_B:
            y = y * lax.rsqrt(jnp.sum(y * y, axis=-1, keepdims=True) + EPS)
            if part < H_B:
                y = y * (DK_B ** -0.5)
        o_ref[:, cs] = y


def _cond_row(i, tiles_prompt, tiles_per_seq):
    return jnp.where(i < tiles_prompt, 0, 1 + (i - tiles_prompt) // tiles_per_seq)


def _inproj(xp, xs, mod3, norm_w, w_main, w_small, conv_w, sample_len, prompt_row, sample_row, tm):
    n_prompt_tok = xp.shape[0]
    n_tok = n_prompt_tok + xs.shape[0]
    assert tm % prompt_row == 0 and tm % sample_row == 0
    cond = functools.partial(_cond_row, tiles_prompt=n_prompt_tok // tm, tiles_per_seq=sample_len // tm)
    kern = functools.partial(_inproj_kernel, tiles_prompt=n_prompt_tok // tm, prompt_row=prompt_row,
                             sample_row=sample_row)
    return pl.pallas_call(
        kern,
        grid=(n_tok // tm,),
        in_specs=_two_streams(n_prompt_tok, tm) + [
                  pl.BlockSpec((1, 1, 6 * D_MODEL), lambda i: (cond(i), 0, 0)),
                  pl.BlockSpec((1, D_MODEL), lambda i: (0, 0)),
                  pl.BlockSpec((D_MODEL, C_SMALL), lambda i: (0, 0)),
                  pl.BlockSpec((D_MODEL, LANES), lambda i: (0, 0)),
                  pl.BlockSpec((SUBLANES, 3 * QB), lambda i: (0, 0))],
        out_specs=pl.BlockSpec((tm, D_PROJ), lambda i: (i, 0)),
        out_shape=jax.ShapeDtypeStruct((n_tok, D_PROJ), F32),
        compiler_params=_cparams(("parallel",)),
        name="inproj",
    )(xp, xs, mod3, norm_w, w_main, w_small, conv_w)


def _unit_ids(j, n_prompt_units, units_per_seq):
    jj = j - n_prompt_units
    b = jj // units_per_seq
    r = jj % units_per_seq
    is_prompt = j < n_prompt_units
    uf = j
    ub = jnp.where(is_prompt, j, n_prompt_units + b * units_per_seq + (units_per_seq - 1 - r))
    init_row = jnp.where(is_prompt, 0, 1 + b)
    first = jnp.logical_or(is_prompt, r == 0)
    return uf, ub, init_row, first


def _tri(rev):
    t = _iota2((CHUNK, CHUNK), 0)
    s = _iota2((CHUNK, CHUNK), 1)
    return (t <= s) if rev else (t >= s)


def _mm_exact_lhs_tn(a, ones_bf16):
    ah, al = _split(a)
    return _dg(ah, ones_bf16, TN) + _dg(al, ones_bf16, TN)


def _gla_unit(chunks, order, states, rev):
    tri = _tri(rev).astype(BF16)
    mid, last = (CHUNK // 2 - 1, 0) if rev else (CHUNK // 2, CHUNK - 1)
    scale = DK_A ** -0.5
    ones = jnp.ones((CHUNK, LANES), BF16)
    lane = _iota2((CHUNK, LANES), 1)
    row = _iota2((CHUNK, LANES), 0)
    s_in = lane % DK_A
    causal = (row <= s_in) if rev else (row >= s_in)
    zeros_v = jnp.zeros((CHUNK, DV_A), F32)
    pairs = range(H_A // 2)
    att, qs_l, u_l, dec_l = {}, {}, {}, {}
    for c, (q, k, v, la) in enumerate(chunks):
        b = _mm_exact_lhs(tri, la)
        bref = b[mid:mid + 1, :]
        blast = b[last:last + 1, :]
        qg = q * jnp.exp(b - bref) * scale
        kg = k * jnp.exp(bref - b)
        kd = k * jnp.exp(blast - b)
        qs_l[c] = q * jnp.exp(b) * scale
        for p in pairs:
            ls = slice(p * LANES, (p + 1) * LANES)
            kg_p = kg[:, ls]
            rhs_att = jnp.concatenate([jnp.where(lane < DK_A, kg_p, 0.0), jnp.where(lane >= DK_A, kg_p, 0.0)], axis=0)
            att[c, p] = jnp.where(causal, _mm(qg[:, ls], rhs_att, NT), 0.0)
            u_l[c, p] = _mm(kd[:, ls], v[:, 2 * p * DV_A:(2 * p + 2) * DV_A], TN)
            dec_l[c, p] = jnp.exp(_mm_exact_lhs_tn(la[:, ls], ones))
    start = {}
    for c in order:
        start[c] = list(states)
        nxt = []
        for p in pairs:
            u, dec = u_l[c, p], dec_l[c, p]
            nxt.append(dec[0:DK_A] * states[2 * p] + u[0:DK_A, 0:DV_A])
            nxt.append(dec[DK_A:2 * DK_A] * states[2 * p + 1] + u[DK_A:2 * DK_A, DV_A:2 * DV_A])
        states = nxt
    outs = []
    for c, (q, k, v, la) in enumerate(chunks):
        o = []
        for p in pairs:
            ls = slice(p * LANES, (p + 1) * LANES)
            vs0 = v[:, (2 * p) * DV_A:(2 * p + 1) * DV_A]
            vs1 = v[:, (2 * p + 1) * DV_A:(2 * p + 2) * DV_A]
            s0, s1 = start[c][2 * p], start[c][2 * p + 1]
            rhs_o = jnp.concatenate([jnp.concatenate([vs0, zeros_v], axis=1),
                                     jnp.concatenate([zeros_v, vs1], axis=1),
                                     jnp.concatenate([s0, zeros_v], axis=1),
                                     jnp.concatenate([zeros_v, s1], axis=1)], axis=0)
            lhs_o = jnp.concatenate([att[c, p], qs_l[c][:, ls]], axis=1)
            o.append(_mm(lhs_o, rhs_o))
        outs.append(jnp.concatenate(o, axis=1))
    return outs, states


def _gla_kernel(qf_ref, qb_ref, sf_ref, sb_ref, wlr_ref, blr_ref, init_ref, of_ref, ob_ref, so_ref, s_ref,
                *, n_prompt_units, units_per_seq):
    j = pl.program_id(0)
    _, _, _, first = _unit_ids(j, n_prompt_units, units_per_seq)

    @pl.when(first)
    def _():
        s_ref[...] = init_ref[0]

    for d in range(N_DIR):
        x_ref, sm_ref, o_ref = (qf_ref, sf_ref, of_ref) if d == 0 else (qb_ref, sb_ref, ob_ref)
        la_unit = _log_sigmoid(_mm3(sm_ref[...], wlr_ref[d]) + blr_ref[d]) * (1.0 / GLA_NORMALIZER)
        states = [s_ref[d, h] for h in range(H_A)]
        order = range(CHUNKS_PER_UNIT - 1, -1, -1) if d == 1 else range(CHUNKS_PER_UNIT)
        chunks = []
        for c in range(CHUNKS_PER_UNIT):
            rows = slice(c * CHUNK, (c + 1) * CHUNK)
            chunks.append((x_ref[rows, 0:QA], x_ref[rows, QA:2 * QA], x_ref[rows, 2 * QA:2 * QA + VA], la_unit[rows]))
        outs, states = _gla_unit(chunks, order, states, rev=(d == 1))
        for c in range(CHUNKS_PER_UNIT):
            o_ref[c * CHUNK:(c + 1) * CHUNK, :] = outs[c]
        for h in range(H_A):
            s_ref[d, h] = states[h]

    @pl.when(j < n_prompt_units)
    def _():
        so_ref[0] = s_ref[...]


def _gla(proj, wlr, blr, init, n_prompt_units, units_per_seq):
    n_tok = proj.shape[0]
    n_units = n_tok // UNIT
    ids = functools.partial(_unit_ids, n_prompt_units=n_prompt_units, units_per_seq=units_per_seq)
    small_blk = C_SMALL // LANES
    st_blk = (1, N_DIR, H_A, DK_A, DV_A)
    kern = functools.partial(_gla_kernel, n_prompt_units=n_prompt_units, units_per_seq=units_per_seq)
    return pl.pallas_call(
        kern,
        grid=(n_units,),
        in_specs=[pl.BlockSpec((UNIT, 1024), lambda j: (ids(j)[0], 0)),
                  pl.BlockSpec((UNIT, 1024), lambda j: (ids(j)[1], 0)),
                  pl.BlockSpec((UNIT, LANES), lambda j: (ids(j)[0], small_blk)),
                  pl.BlockSpec((UNIT, LANES), lambda j: (ids(j)[1], small_blk)),
                  pl.BlockSpec((N_DIR, LANES, QA), lambda j: (0, 0, 0)),
                  pl.BlockSpec((N_DIR, 1, QA), lambda j: (0, 0, 0)),
                  pl.BlockSpec(st_blk, lambda j: (ids(j)[2], 0, 0, 0, 0))],
        out_specs=[pl.BlockSpec((UNIT, VA), lambda j: (ids(j)[0], 0)),
                   pl.BlockSpec((UNIT, VA), lambda j: (ids(j)[1], 0)),
                   pl.BlockSpec(st_blk, lambda j: (jnp.minimum(j, n_prompt_units - 1), 0, 0, 0, 0))],
        out_shape=[jax.ShapeDtypeStruct((n_tok, VA), F32),
                   jax.ShapeDtypeStruct((n_tok, VA), F32),
                   jax.ShapeDtypeStruct((n_prompt_units, N_DIR, H_A, DK_A, DV_A), F32)],
        scratch_shapes=[pltpu.VMEM((N_DIR, H_A, DK_A, DV_A), F32)],
        compiler_params=_cparams(("arbitrary",)),
        name="gla",
    )(proj, proj, proj, proj, wlr, blr, init)


def _stack_masks(rev):
    r = _iota2((HG, HG), 0)
    c = _iota2((HG, HG), 1)
    same = (r // CHUNK) == (c // CHUNK)
    tr, tc = r % CHUNK, c % CHUNK
    incl = jnp.logical_and(same, (tr <= tc) if rev else (tr >= tc))
    strict = jnp.logical_and(same, (tr < tc) if rev else (tr > tc))
    return incl, strict


def _spread(x):
    z = jnp.zeros((CHUNK, LANES), x.dtype)
    rows = []
    for h in range(H_B):
        xh = x[h * CHUNK:(h + 1) * CHUNK]
        rows.append(jnp.concatenate([xh if g == h else z for g in range(H_B)], axis=1))
    return jnp.concatenate(rows, axis=0)


def _gdn_prepare(chunks):
    masks = {rev: _stack_masks(rev) for rev in {ch[6] for ch in chunks}}
    r = _iota2((HG, HG), 0)
    c = _iota2((HG, HG), 1)
    a_l, att_l, kb_l = [], [], []
    for q, k, v, gcb, beta, glast, rev in chunks:
        incl, strict = masks[rev]
        grow = gcb.T[0:1, :]
        diff = gcb[:, 0:1] - grow
        decay = jnp.where(incl, jnp.exp(jnp.where(incl, diff, 0.0)), 0.0)
        kb = k * beta
        m1 = _mm(jnp.concatenate([kb, q], axis=0), k, NT)
        a_l.append(jnp.where(strict, m1[0:HG] * decay, 0.0))
        att_l.append(m1[HG:2 * HG] * decay)
        kb_l.append(kb)
    diag = (r // INV_BLOCK) == (c // INV_BLOCK)
    pw_l = [jnp.where(diag, a, 0.0) for a in a_l]
    qm_l = [-p for p in pw_l]
    n = 2
    while n < INV_BLOCK:
        pw_l = [_mm(p, p) for p in pw_l]
        qm_l = [qm + p + _mm(qm, p) for qm, p in zip(qm_l, pw_l)]
        n *= 2
    b = INV_BLOCK
    while b < CHUNK:
        off = jnp.logical_and((r // (2 * b)) == (c // (2 * b)), (r // b) != (c // b))
        al_l = [jnp.where(off, a, 0.0) for a in a_l]
        t1_l = [al + _mm(qm, al) for qm, al in zip(qm_l, al_l)]
        qm_l = [qm - (t1 + _mm(t1, qm)) for qm, t1 in zip(qm_l, t1_l)]
        b *= 2
    out = []
    for (q, k, v, gcb, beta, glast, _), qm, kb, att in zip(chunks, qm_l, kb_l, att_l):
        egc = jnp.exp(gcb)
        rhs = jnp.concatenate([v * beta, kb * egc], axis=1)
        sol = rhs + _mm(qm, rhs)
        out.append((sol[:, 0:DV_B], sol[:, DV_B:2 * DV_B], q * egc, k * jnp.exp(glast - gcb), att))
    return out


def _gdn_scan_step(value, k_cum, q_dec, k_dec, att, gl_rows, s):
    kq = _mm(jnp.concatenate([_spread(k_cum), _spread(q_dec)], axis=0), s)
    v_new = value - kq[0:HC]
    o = kq[HC:2 * HC] + _mm(att, v_new)
    s_new = s * gl_rows + _mm(_spread(k_dec), v_new, TN)
    return o, s_new


def _gdn_kernel(xf_ref, xb_ref, sf_ref, sb_ref, gco_ref, gdt_ref, init_ref, of_ref, ob_ref, so_ref, s_ref,
                *, n_prompt_units, units_per_seq):
    j = pl.program_id(0)
    _, _, _, first = _unit_ids(j, n_prompt_units, units_per_seq)

    @pl.when(first)
    def _():
        s_ref[...] = init_ref[0]

    chunks, gl_rows = [], []
    for d in range(N_DIR):
        x_ref, sm_ref = (xf_ref, sf_ref) if d == 0 else (xb_ref, sb_ref)
        rev = d == 1
        sm = sm_ref[...]
        g_all = gco_ref[...] * _softplus(sm + gdt_ref[...])
        beta_all = _sigmoid(sm)
        tri = _tri(rev).astype(BF16)
        last = 0 if rev else CHUNK - 1
        for c in range(CHUNKS_PER_UNIT):
            rows = slice(c * CHUNK, (c + 1) * CHUNK)
            gc_all = _mm_exact_lhs(tri, g_all[rows])
            gcb, beta, glast = [], [], []
            for h in range(H_B):
                col = SM_AB + d * H_B + h
                colb = SM_BETA + d * H_B + h
                gh = jnp.broadcast_to(gc_all[:, col:col + 1], (CHUNK, LANES))
                gcb.append(gh)
                glast.append(jnp.broadcast_to(gh[last:last + 1, :], (CHUNK, LANES)))
                beta.append(jnp.broadcast_to(beta_all[rows, colb:colb + 1], (CHUNK, LANES)))
            for g0 in range(0, H_B, GDN_GROUP):
                hs = range(g0, g0 + GDN_GROUP)
                stack = lambda base: jnp.concatenate(
                    [x_ref[rows, base + h * LANES:base + (h + 1) * LANES] for h in hs], axis=0)
                cat = lambda parts: jnp.concatenate([parts[h] for h in hs], axis=0)
                chunks.append((stack(0), stack(QB), stack(2 * QB), cat(gcb), cat(beta), cat(glast), rev))
            gl_rows.append(jnp.concatenate([jnp.broadcast_to(jnp.exp(g[0:1, :]), (DK_B, DV_B)) for g in glast],
                                           axis=0))
    n_grp = H_B // GDN_GROUP
    zero_att = jnp.zeros((HG, HG), F32)
    prepared = []
    grouped = _gdn_prepare(chunks)
    for i in range(0, len(grouped), n_grp):
        parts = grouped[i:i + n_grp]
        rows_cat = [jnp.concatenate([p[f] for p in parts], axis=0) for f in range(4)]
        att = jnp.concatenate([jnp.concatenate([parts[g][4] if gg == g else zero_att for gg in range(n_grp)], axis=1)
                               for g in range(n_grp)], axis=0)
        prepared.append(tuple(rows_cat) + (att,))
    s = [s_ref[d] for d in range(N_DIR)]
    for step in range(CHUNKS_PER_UNIT):
        for d, o_ref in ((0, of_ref), (1, ob_ref)):
            c = step if d == 0 else CHUNKS_PER_UNIT - 1 - step
            o, s[d] = _gdn_scan_step(*prepared[d * CHUNKS_PER_UNIT + c], gl_rows[d * CHUNKS_PER_UNIT + c], s[d])
            for h in range(H_B):
                o_ref[c * CHUNK:(c + 1) * CHUNK, h * DV_B:(h + 1) * DV_B] = o[h * CHUNK:(h + 1) * CHUNK]
    for d in range(N_DIR):
        s_ref[d] = s[d]

    @pl.when(j < n_prompt_units)
    def _():
        so_ref[0] = s_ref[...]


def _gdn(proj, gcoef, gdt, init, n_prompt_units, units_per_seq):
    n_tok = proj.shape[0]
    n_units = n_tok // UNIT
    ids = functools.partial(_unit_ids, n_prompt_units=n_prompt_units, units_per_seq=units_per_seq)
    small_blk = C_SMALL // LANES
    qkv_blk = C_QKVB // (3 * QB)
    st_blk = (1, N_DIR, H_B * DK_B, DV_B)
    kern = functools.partial(_gdn_kernel, n_prompt_units=n_prompt_units, units_per_seq=units_per_seq)
    return pl.pallas_call(
        kern,
        grid=(n_units,),
        in_specs=[pl.BlockSpec((UNIT, 3 * QB), lambda j: (ids(j)[0], qkv_blk)),
                  pl.BlockSpec((UNIT, 3 * QB), lambda j: (ids(j)[1], qkv_blk)),
                  pl.BlockSpec((UNIT, LANES), lambda j: (ids(j)[0], small_blk)),
                  pl.BlockSpec((UNIT, LANES), lambda j: (ids(j)[1], small_blk)),
                  pl.BlockSpec((1, LANES), lambda j: (0, 0)),
                  pl.BlockSpec((1, LANES), lambda j: (0, 0)),
                  pl.BlockSpec(st_blk, lambda j: (ids(j)[2], 0, 0, 0))],
        out_specs=[pl.BlockSpec((UNIT, VB), lambda j: (ids(j)[0], 0)),
                   pl.BlockSpec((UNIT, VB), lambda j: (ids(j)[1], 0)),
                   pl.BlockSpec(st_blk, lambda j: (jnp.minimum(j, n_prompt_units - 1), 0, 0, 0))],
        out_shape=[jax.ShapeDtypeStruct((n_tok, VB), F32),
                   jax.ShapeDtypeStruct((n_tok, VB), F32),
                   jax.ShapeDtypeStruct((n_prompt_units, N_DIR, H_B * DK_B, DV_B), F32)],
        scratch_shapes=[pltpu.VMEM((N_DIR, H_B * DK_B, DV_B), F32)],
        compiler_params=_cparams(("arbitrary",)),
        name="gdn",
    )(proj, proj, proj, proj, gcoef, gdt, init)


def _head_rms(o, w):
    parts = []
    for h in range(o.shape[1] // LANES):
        parts.append(_rms(o[:, h * LANES:(h + 1) * LANES]))
    return jnp.concatenate(parts, axis=1) * w


def _post_kernel(xp_ref, xs_ref, ga_ref, zb_ref, af_ref, ab_ref, df_ref, db_ref, mod_ref, wo_ref, nwa_ref, nwb_ref,
                 n2_ref, rw_ref, x1_ref, h2_ref, sc_ref, *, tiles_prompt):
    m = mod_ref[0]
    x = jnp.where(pl.program_id(0) < tiles_prompt, xp_ref[...], xs_ref[...])
    gla = _head_rms(af_ref[...] + ab_ref[...], nwa_ref[...]) * _silu(ga_ref[...])
    gdn = _head_rms(df_ref[...] + db_ref[...], nwb_ref[...]) * _silu(zb_ref[...])
    y = _dg(gla.astype(BF16), wo_ref[0:VA, :], NN) + _dg(gdn.astype(BF16), wo_ref[VA:VA + VB, :], NN)
    x1 = x + m[:, 2 * D_MODEL:3 * D_MODEL] * y
    x1_ref[...] = x1
    h2 = (_rms(x1) * n2_ref[...]) * (1.0 + m[:, 4 * D_MODEL:5 * D_MODEL]) + m[:, 3 * D_MODEL:4 * D_MODEL]
    h2_ref[...] = _pack_rows(h2)
    sc_ref[...] = _sigmoid(_mm3(rw_ref[...], h2, NT))


def _post(xp, xs, proj, o_af, o_ab, o_df, o_db, mod3, w_o, nwa, nwb, n2w, router_wt, sample_len, tm):
    n_prompt_tok = xp.shape[0]
    n_tok = n_prompt_tok + xs.shape[0]
    cond = functools.partial(_cond_row, tiles_prompt=n_prompt_tok // tm, tiles_per_seq=sample_len // tm)
    tok = lambda i: (i, 0)
    const = lambda i: (0, 0)
    return pl.pallas_call(
        functools.partial(_post_kernel, tiles_prompt=n_prompt_tok // tm),
        grid=(n_tok // tm,),
        in_specs=_two_streams(n_prompt_tok, tm) + [
                  pl.BlockSpec((tm, VA), lambda i: (i, C_GA // VA)),
                  pl.BlockSpec((tm, VB), lambda i: (i, C_ZB // VB)),
                  pl.BlockSpec((tm, VA), tok), pl.BlockSpec((tm, VA), tok),
                  pl.BlockSpec((tm, VB), tok), pl.BlockSpec((tm, VB), tok),
                  pl.BlockSpec((1, 1, 6 * D_MODEL), lambda i: (cond(i), 0, 0)),
                  pl.BlockSpec((VA + VB, D_MODEL), const),
                  pl.BlockSpec((1, VA), const), pl.BlockSpec((1, VB), const), pl.BlockSpec((1, D_MODEL), const),
                  pl.BlockSpec((N_EXPERTS, D_MODEL), const)],
        out_specs=[pl.BlockSpec((tm, D_MODEL), tok),
                   pl.BlockSpec((tm, D_PACK), tok),
                   pl.BlockSpec((N_EXPERTS, tm), lambda i: (0, i))],
        out_shape=[jax.ShapeDtypeStruct((n_tok, D_MODEL), F32),
                   jax.ShapeDtypeStruct((n_tok, D_PACK), jnp.uint32),
                   jax.ShapeDtypeStruct((N_EXPERTS, n_tok), F32)],
        compiler_params=_cparams(("parallel",)),
        name="post",
    )(xp, xs, proj, proj, o_af, o_ab, o_df, o_db, mod3, w_o, nwa, nwb, n2w, router_wt)


def _route_kernel(sc_ref, bias_ref, e_ref, r_ref, w_ref, cnt_ref, carry_ref):
    i = pl.program_id(0)
    t = sc_ref.shape[1]

    @pl.when(i == 0)
    def _():
        carry_ref[...] = jnp.zeros(carry_ref.shape, F32)

    s = sc_ref[...]
    biased = s + bias_ref[:, 0:1]
    gs = []
    for g in range(N_GROUP):
        blk = biased[g * GROUP_SIZE:(g + 1) * GROUP_SIZE]
        m1 = jnp.max(blk, axis=0, keepdims=True)
        n1 = jnp.sum((blk == m1).astype(F32), axis=0, keepdims=True)
        m2 = jnp.max(jnp.where(blk < m1, blk, NEG_INF), axis=0, keepdims=True)
        gs.append(m1 + jnp.where(n1 >= 2.0, m1, m2))
    gsc = jnp.concatenate(gs, axis=0)
    gid = _iota2((N_GROUP, t), 0)
    beaten = jnp.zeros((N_GROUP, t), F32)
    for g in range(N_GROUP):
        other = gsc[g:g + 1, :]
        wins = jnp.logical_or(other > gsc, jnp.logical_and(other == gsc, g < gid))
        beaten = beaten + wins.astype(F32)
    masked = jnp.concatenate(
        [jnp.where(beaten[g:g + 1, :] < float(TOPK_GROUP), biased[g * GROUP_SIZE:(g + 1) * GROUP_SIZE], NEG_INF)
         for g in range(N_GROUP)], axis=0)
    eid = _iota2((N_EXPERTS, t), 0).astype(F32)
    sel = jnp.zeros((N_EXPERTS, t), F32)
    picks, scores = [], []
    for _ in range(TOP_K):
        m = jnp.max(masked, axis=0, keepdims=True)
        first = jnp.min(jnp.where(masked == m, eid, float(N_EXPERTS)), axis=0, keepdims=True)
        hit = eid == first
        scores.append(jnp.sum(jnp.where(hit, s, 0.0), axis=0, keepdims=True))
        masked = jnp.where(hit, NEG_INF, masked)
        sel = sel + hit.astype(F32)
        picks.append(first)
    upper = (_iota2((t, t), 0) < _iota2((t, t), 1)).astype(BF16)
    carry = carry_ref[...]
    prefix = _dg(sel.astype(BF16), upper, NN) + jnp.concatenate([carry] * (t // LANES), axis=1)
    ranks = [jnp.sum(jnp.where(eid == p, prefix, 0.0), axis=0, keepdims=True) for p in picks]
    carry = carry + _dg(sel.astype(BF16), jnp.ones((t, LANES), BF16), NN)
    carry_ref[...] = carry
    cnt_ref[...] = carry
    sc8 = jnp.concatenate(scores, axis=0)
    e_ref[...] = jnp.concatenate(picks, axis=0).astype(jnp.int32)
    r_ref[...] = jnp.concatenate(ranks, axis=0).astype(jnp.int32)
    w_ref[...] = sc8 / jnp.sum(sc8, axis=0, keepdims=True) * ROUTED_SCALE


def _route(scores_t, bias_col):
    n_tok = scores_t.shape[1]
    t = ROUTE_TILE
    slot = lambda i: (0, i)
    return pl.pallas_call(
        _route_kernel,
        grid=(n_tok // t,),
        in_specs=[pl.BlockSpec((N_EXPERTS, t), slot),
                  pl.BlockSpec((N_EXPERTS, LANES), lambda i: (0, 0))],
        out_specs=[pl.BlockSpec((TOP_K, t), slot), pl.BlockSpec((TOP_K, t), slot), pl.BlockSpec((TOP_K, t), slot),
                   pl.BlockSpec((N_EXPERTS, LANES), lambda i: (0, 0))],
        out_shape=[jax.ShapeDtypeStruct((TOP_K, n_tok), jnp.int32),
                   jax.ShapeDtypeStruct((TOP_K, n_tok), jnp.int32),
                   jax.ShapeDtypeStruct((TOP_K, n_tok), F32),
                   jax.ShapeDtypeStruct((N_EXPERTS, LANES), F32)],
        scratch_shapes=[pltpu.VMEM((N_EXPERTS, LANES), F32)],
        compiler_params=_cparams(("arbitrary",)),
        name="route",
    )(scores_t, bias_col)


def _dest_kernel(start_ref, e_ref, r_ref, d_ref):
    e = e_ref[...]

    def body(x, acc):
        return jnp.where(e == x, start_ref[x], acc)

    d_ref[...] = r_ref[...] + lax.fori_loop(0, N_EXPERTS, body, jnp.zeros(e.shape, jnp.int32))


def _dest(start, e8, r8, tt):
    n_tok = e8.shape[1]
    slot = lambda i, st: (0, i)
    return pl.pallas_call(
        _dest_kernel,
        grid_spec=pltpu.PrefetchScalarGridSpec(
            num_scalar_prefetch=1, grid=(n_tok // tt,),
            in_specs=[pl.BlockSpec((TOP_K, tt), slot), pl.BlockSpec((TOP_K, tt), slot)],
            out_specs=pl.BlockSpec((TOP_K, tt), slot)),
        out_shape=jax.ShapeDtypeStruct((TOP_K, n_tok), jnp.int32),
        compiler_params=_cparams(("parallel",)),
        name="dest",
    )(start, e8, r8)


def _items(counts, n_rows):
    n_blocks = n_rows // ROW_BLOCK
    max_items = n_blocks + N_EXPERTS - 1
    end = jnp.cumsum(counts)
    start = end - counts
    first_blk = start // ROW_BLOCK
    n_it = jnp.where(counts > 0, (end - 1) // ROW_BLOCK - first_blk + 1, 0)
    it_end = jnp.cumsum(n_it)
    it_start = it_end - n_it
    i = jnp.arange(max_items, dtype=jnp.int32)
    valid = i < it_end[-1]
    ex = jnp.minimum(jnp.sum((it_end[None, :] <= i[:, None]).astype(jnp.int32), axis=1), N_EXPERTS - 1)
    onehot = ex[:, None] == jnp.arange(N_EXPERTS, dtype=jnp.int32)[None, :]
    pick = lambda tab: jnp.sum(jnp.where(onehot, tab[None, :], 0), axis=1)
    blk = pick(first_blk) + (i - pick(it_start))
    lo = jnp.maximum(pick(start), blk * ROW_BLOCK) - blk * ROW_BLOCK
    hi = jnp.minimum(pick(end), (blk + 1) * ROW_BLOCK) - blk * ROW_BLOCK
    blk = jnp.where(valid, blk, n_blocks - 1).astype(jnp.int32)
    lo = jnp.where(valid, lo, 0).astype(jnp.int32)
    hi = jnp.where(valid, hi, 0).astype(jnp.int32)
    eids = jnp.arange(N_EXPERTS, dtype=jnp.int32)
    later = jnp.logical_and(eids[None, :] > eids[:, None], (counts > 0)[None, :])
    nxt_e = jnp.min(jnp.where(later, eids[None, :], N_EXPERTS), axis=1)
    nxt = pick(jnp.where(nxt_e < N_EXPERTS, nxt_e, -1)).astype(jnp.int32)
    return start.astype(jnp.int32), blk, ex.astype(jnp.int32), nxt, lo, hi


def _dispatch_kernel(h2_ref, dest_hbm, xs_hbm, dsm, sem_d, sem_s):
    i = pl.program_id(0)
    n = pl.num_programs(0)
    tm = h2_ref.shape[0]

    def dest_copy(step, slot):
        return pltpu.make_async_copy(dest_hbm.at[:, pl.ds(step * tm, tm)], dsm.at[slot], sem_d.at[slot])

    def row_copy(t, dst):
        return pltpu.make_async_copy(h2_ref.at[pl.ds(t, 1), :], xs_hbm.at[pl.ds(dst, 1), :], sem_s.at[0])

    @pl.when(i == 0)
    def _():
        dest_copy(0, 0).start()

    slot = i % 2
    dest_copy(i, slot).wait()

    @pl.when(i + 1 < n)
    def _():
        dest_copy(i + 1, 1 - slot).start()

    def issue(t, carry):
        for k in range(TOP_K):
            row_copy(t, dsm[slot, k, t]).start(priority=k % N_DMA_QUEUES)
        return carry
    lax.fori_loop(0, tm, issue, 0, unroll=2)

    def drain(t, carry):
        for k in range(TOP_K):
            row_copy(t, 0).wait()
        return carry
    lax.fori_loop(0, tm, drain, 0)


def _dispatch(h2, dest, tm):
    n_tok = h2.shape[0]
    return pl.pallas_call(
        _dispatch_kernel,
        grid=(n_tok // tm,),
        in_specs=[pl.BlockSpec((tm, D_PACK), lambda i: (i, 0)),
                  pl.BlockSpec(memory_space=pl.ANY)],
        out_specs=pl.BlockSpec(memory_space=pl.ANY),
        out_shape=jax.ShapeDtypeStruct((n_tok * TOP_K, D_PACK), jnp.uint32),
        scratch_shapes=[pltpu.SMEM((2, TOP_K, tm), jnp.int32),
                        pltpu.SemaphoreType.DMA((2,)),
                        pltpu.SemaphoreType.DMA((1,))],
        compiler_params=_cparams(("arbitrary",)),
        name="dispatch",
    )(h2, dest)


def _experts_kernel(blk_ref, exp_ref, nxt_ref, lo_ref, hi_ref, x_ref, wg_hbm, wu_hbm, wd_hbm, y_ref,
                    wgf, wuf, wdf, wgb, wub, wdb, nchg, sem_w):
    i = pl.program_id(0)
    lo, hi = lo_ref[i], hi_ref[i]

    def weight_copies(e, slot):
        return (pltpu.make_async_copy(wg_hbm.at[e], wgf.at[slot], sem_w.at[slot]),
                pltpu.make_async_copy(wu_hbm.at[e], wuf.at[slot], sem_w.at[slot]),
                pltpu.make_async_copy(wd_hbm.at[e], wdf.at[slot], sem_w.at[slot]))

    @pl.when(i == 0)
    def _():
        nchg[0] = 0
        for cp in weight_copies(exp_ref[0], 0):
            cp.start()

    @pl.when(hi > lo)
    def _():
        @pl.when(jnp.logical_or(i == 0, exp_ref[i] != exp_ref[jnp.maximum(i - 1, 0)]))
        def _():
            slot = nchg[0] % 2
            for cp in weight_copies(exp_ref[i], slot):
                cp.wait()

            @pl.when(nxt_ref[i] >= 0)
            def _():
                for cp in weight_copies(nxt_ref[i], 1 - slot):
                    cp.start()

            wgb[...] = wgf[slot].astype(BF16)
            wub[...] = wuf[slot].astype(BF16)
            wdb[...] = wdf[slot].astype(BF16)
            nchg[0] = nchg[0] + 1

        x_lo, x_hi = _unpack_rows(x_ref[...])
        x = jnp.concatenate([x_lo.astype(BF16), x_hi.astype(BF16)], axis=1)
        g = _dg(x, wgb[...], NN)
        u = _dg(x, wub[...], NN)
        hmid = (_silu(g) * u).astype(BF16)
        y = _pack_rows(_dg(hmid, wdb[...], NN))
        row = _iota2((ROW_BLOCK, D_PACK), 0)
        mine = jnp.logical_and(row >= lo, row < hi)

        @pl.when(lo == 0)
        def _():
            y_ref[...] = jnp.where(mine, y, jnp.uint32(0))

        @pl.when(lo > 0)
        def _():
            y_ref[...] = jnp.where(mine, y, y_ref[...])


def _experts(item_blk, item_exp, item_nxt, item_lo, item_hi, xs, w_gate, w_up, w_down):
    n_items = item_blk.shape[0]
    xmap = lambda i, blk, ex, nxt, lo, hi: (blk[i], 0)
    return pl.pallas_call(
        _experts_kernel,
        grid_spec=pltpu.PrefetchScalarGridSpec(
            num_scalar_prefetch=5, grid=(n_items,),
            in_specs=[pl.BlockSpec((ROW_BLOCK, D_PACK), xmap),
                      pl.BlockSpec(memory_space=pl.ANY),
                      pl.BlockSpec(memory_space=pl.ANY),
                      pl.BlockSpec(memory_space=pl.ANY)],
            out_specs=pl.BlockSpec((ROW_BLOCK, D_PACK), xmap),
            scratch_shapes=[pltpu.VMEM((2, D_MODEL, D_EXPERT), F32), pltpu.VMEM((2, D_MODEL, D_EXPERT), F32),
                            pltpu.VMEM((2, D_EXPERT, D_MODEL), F32),
                            pltpu.VMEM((D_MODEL, D_EXPERT), BF16), pltpu.VMEM((D_MODEL, D_EXPERT), BF16),
                            pltpu.VMEM((D_EXPERT, D_MODEL), BF16),
                            pltpu.SMEM((1,), jnp.int32),
                            pltpu.SemaphoreType.DMA((2,))]),
        out_shape=jax.ShapeDtypeStruct(xs.shape, jnp.uint32),
        compiler_params=_cparams(("arbitrary",)),
        name="experts",
    )(item_blk, item_exp, item_nxt, item_lo, item_hi, xs, w_gate, w_up, w_down)


def _final_kernel(x1_ref, h2_ref, w_ref, mod_ref, sg_ref, su_ref, sd_ref, fn_ref, dest_hbm, ys_hbm, op_ref, os_ref,
                  dsm, gbuf, sem_d, sem_g, *, tiles_prompt):
    i = pl.program_id(0)
    n = pl.num_programs(0)
    tm = x1_ref.shape[0]

    def dest_copy(step, slot):
        return pltpu.make_async_copy(dest_hbm.at[:, pl.ds(step * tm, tm)], dsm.at[slot], sem_d.at[slot])

    def row_copy(src, slot, k, t):
        return pltpu.make_async_copy(ys_hbm.at[pl.ds(src, 1), :], gbuf.at[slot, k, pl.ds(t, 1), :], sem_g.at[slot])

    def issue_gathers(slot3, slot2):
        def body(t, carry):
            for k in range(TOP_K):
                row_copy(dsm[slot3, k, t], slot2, k, t).start(priority=k % N_DMA_QUEUES)
            return carry
        lax.fori_loop(0, tm, body, 0, unroll=2)

    @pl.when(i == 0)
    def _():
        dest_copy(0, 0).start()
        dest_copy(0, 0).wait()
        issue_gathers(0, 0)

        @pl.when(n > 1)
        def _():
            dest_copy(1, 1).start()

    @pl.when(i + 1 < n)
    def _():
        dest_copy(i + 1, (i + 1) % 3).wait()
        issue_gathers((i + 1) % 3, (i + 1) % 2)

        @pl.when(i + 2 < n)
        def _():
            dest_copy(i + 2, (i + 2) % 3).start()

    slot = i % 2

    def drain(t, carry):
        for k in range(TOP_K):
            row_copy(0, slot, k, t).wait()
        return carry
    lax.fori_loop(0, tm, drain, 0)

    m = mod_ref[0]
    w = w_ref[...]
    r_lo = r_hi = None
    for kk in range(TOP_K):
        y_lo, y_hi = _unpack_rows(gbuf[slot, kk])
        wk = w[:, kk:kk + 1]
        r_lo = y_lo * wk if r_lo is None else r_lo + y_lo * wk
        r_hi = y_hi * wk if r_hi is None else r_hi + y_hi * wk
    routed = jnp.concatenate([r_lo, r_hi], axis=1)
    h_lo, h_hi = _unpack_rows(h2_ref[...])
    hb = jnp.concatenate([h_lo.astype(BF16), h_hi.astype(BF16)], axis=1)
    hm = (_silu(_dg(hb, sg_ref[...], NN)) * _dg(hb, su_ref[...], NN)).astype(BF16)
    shared = _dg(hm, sd_ref[...], NN)
    x2 = x1_ref[...] + m[:, 5 * D_MODEL:6 * D_MODEL] * (routed + shared)
    out = _rms(x2) * fn_ref[...]

    @pl.when(i < tiles_prompt)
    def _():
        op_ref[...] = out

    @pl.when(i >= tiles_prompt)
    def _():
        os_ref[...] = out


def _final(x1, h2, wts, mod3, sg, su, sd, fnw, dest, ys, n_prompt_tok, sample_len, tm):
    n_tok = x1.shape[0]
    cond = functools.partial(_cond_row, tiles_prompt=n_prompt_tok // tm, tiles_per_seq=sample_len // tm)
    tok = lambda i: (i, 0)
    const = lambda i: (0, 0)
    tp = n_prompt_tok // tm
    return pl.pallas_call(
        functools.partial(_final_kernel, tiles_prompt=tp),
        grid=(n_tok // tm,),
        in_specs=[pl.BlockSpec((tm, D_MODEL), tok),
                  pl.BlockSpec((tm, D_PACK), tok),
                  pl.BlockSpec((tm, TOP_K), tok),
                  pl.BlockSpec((1, 1, 6 * D_MODEL), lambda i: (cond(i), 0, 0)),
                  pl.BlockSpec((D_MODEL, D_EXPERT), const),
                  pl.BlockSpec((D_MODEL, D_EXPERT), const),
                  pl.BlockSpec((D_EXPERT, D_MODEL), const),
                  pl.BlockSpec((1, D_MODEL), const),
                  pl.BlockSpec(memory_space=pl.ANY),
                  pl.BlockSpec(memory_space=pl.ANY)],
        out_specs=[pl.BlockSpec((tm, D_MODEL), lambda i: (jnp.minimum(i, tp - 1), 0)),
                   pl.BlockSpec((tm, D_MODEL), lambda i: (jnp.maximum(i - tp, 0), 0))],
        out_shape=[jax.ShapeDtypeStruct((n_prompt_tok, D_MODEL), F32),
                   jax.ShapeDtypeStruct((n_tok - n_prompt_tok, D_MODEL), F32)],
        scratch_shapes=[pltpu.SMEM((3, TOP_K, tm), jnp.int32),
                        pltpu.VMEM((2, TOP_K, tm, D_PACK), jnp.uint32),
                        pltpu.SemaphoreType.DMA((3,)),
                        pltpu.SemaphoreType.DMA((2,))],
        compiler_params=_cparams(("arbitrary",)),
        name="final",
    )(x1, h2, wts, mod3, sg, su, sd, fnw, dest, ys)


def kernel(x_prompt, x_sample, state_gla, state_gdn, c, c_ctx, w_ada, b_ada, norm1_w, w_in, conv_w, gla_lr_w, gla_lr_b, gdn_a_log, gdn_dt_bias, gla_norm_w, gdn_norm_w, w_o, norm2_w, router_w, router_bias, exp_w_gate, exp_w_up, exp_w_down, sh_w_gate, sh_w_up, sh_w_down, final_norm_w):
    bp, lp, d = x_prompt.shape
    bs, ls, _ = x_sample.shape
    assert d == D_MODEL and lp == UNIT and ls % UNIT == 0 and w_ada.shape[0] == 1
    n_prompt_tok = bp * lp
    n_tok = n_prompt_tok + bs * ls
    n_prompt_units = n_prompt_tok // UNIT
    units_per_seq = ls // UNIT
    grid_w = 64
    layer = 0
    tm_in = 512
    tm = 256
    assert n_prompt_tok % tm_in == 0 and ls % tm_in == 0 and n_tok % ROUTE_TILE == 0

    xp = x_prompt.reshape(n_prompt_tok, d)
    xs_in = x_sample.reshape(bs * ls, d)
    cond = jnp.concatenate([c_ctx[None, :], c, jnp.zeros((SUBLANES - 1 - bs, d), F32)], axis=0)
    mod3 = _ada(cond, w_ada[layer], b_ada[layer][None, :]).reshape(SUBLANES, 1, 6 * d)

    wi = w_in[layer]
    o_lr = 2 * QA + 2 * VA
    o_qkvb = o_lr + N_DIR * GLA_RANK
    o_zb = o_qkvb + 3 * QB
    o_ab = o_zb + VB
    w_main = jnp.concatenate([wi[:, 0:o_lr], wi[:, o_qkvb:o_ab]], axis=1).astype(BF16)
    w_small = jnp.concatenate([wi[:, o_lr:o_qkvb], wi[:, o_ab:], jnp.zeros((d, LANES - 48), F32)], axis=1)
    cw = jnp.concatenate([conv_w[layer], jnp.zeros((SUBLANES - CONV_K, 3 * QB), F32)], axis=0)
    proj = _inproj(xp, xs_in, mod3, norm1_w[layer][None, :], w_main, w_small, cw, ls, lp, grid_w, tm_in)

    wlr = jnp.zeros((N_DIR, LANES, QA), F32)
    for dd in range(N_DIR):
        wlr = wlr.at[dd, dd * GLA_RANK:(dd + 1) * GLA_RANK, :].set(gla_lr_w[layer, dd])
    blr = gla_lr_b[layer][:, None, :]
    init_gla = jnp.concatenate([jnp.zeros((1,) + state_gla.shape[2:], F32), state_gla[:, layer].astype(F32)], axis=0)
    o_af, o_ab_, s_gla = _gla(proj, wlr, blr, init_gla, n_prompt_units, units_per_seq)

    gcoef = jnp.zeros((1, LANES), F32).at[0, SM_AB:SM_AB + N_DIR * H_B].set(-jnp.exp(gdn_a_log[layer].reshape(-1)))
    gdt = jnp.zeros((1, LANES), F32).at[0, SM_AB:SM_AB + N_DIR * H_B].set(gdn_dt_bias[layer].reshape(-1))
    init_gdn = jnp.concatenate([jnp.zeros((1, N_DIR, H_B * DK_B, DV_B), F32),
                                state_gdn[:, layer].astype(F32).reshape(bs, N_DIR, H_B * DK_B, DV_B)], axis=0)
    o_df, o_db, s_gdn = _gdn(proj, gcoef, gdt, init_gdn, n_prompt_units, units_per_seq)

    nwa = jnp.tile(gla_norm_w[layer], H_A)[None, :]
    nwb = jnp.tile(gdn_norm_w[layer], H_B)[None, :]
    x1, h2, scores_t = _post(xp, xs_in, proj, o_af, o_ab_, o_df, o_db, mod3, w_o[layer].astype(BF16), nwa, nwb,
                             norm2_w[layer][None, :], router_w[layer].T, ls, tm)

    bias_col = jnp.broadcast_to(router_bias[layer].astype(F32)[:, None], (N_EXPERTS, LANES))
    e8, r8, w8, cnt = _route(scores_t, bias_col)
    start, item_blk, item_exp, item_nxt, item_lo, item_hi = _items(cnt[:, 0].astype(jnp.int32), n_tok * TOP_K)
    dest = _dest(start, e8, r8, tt=min(2048, n_tok))

    xs = _dispatch(h2, dest, tm)
    ys = _experts(item_blk, item_exp, item_nxt, item_lo, item_hi, xs, exp_w_gate[layer], exp_w_up[layer],
                  exp_w_down[layer])
    y_p, y_s = _final(x1, h2, w8.T, mod3, sh_w_gate[layer].astype(BF16), sh_w_up[layer].astype(BF16),
                      sh_w_down[layer].astype(BF16), final_norm_w[None, :], dest, ys, n_prompt_tok, ls, tm)

    y_prompt = y_p.reshape(bp, lp, d)
    y_sample = y_s.reshape(bs, ls, d)
    new_state_gla = s_gla.reshape(bp, 1, N_DIR, H_A, DK_A, DV_A).astype(x_prompt.dtype)
    new_state_gdn = s_gdn.reshape(bp, 1, N_DIR, H_B, DK_B, DV_B).astype(x_prompt.dtype)
    return (y_prompt, y_sample, new_state_gla, new_state_gdn)
```

```python
import functools

import jax
import jax.numpy as jnp
from jax import lax
from jax.experimental import pallas as pl
from jax.experimental.pallas import tpu as pltpu

F32 = jnp.float32
BF16 = jnp.bfloat16

D_MODEL = 1024
N_DIR = 2
H_A, DK_A, DV_A = 4, 64, 128
GLA_RANK = 16
GLA_NORMALIZER = 16.0
H_B, DK_B, DV_B = 4, 128, 128
CONV_K = 5
CHUNK = 64
QA, VA = H_A * DK_A, H_A * DV_A
QB, VB = H_B * DK_B, H_B * DV_B
N_EXPERTS = 256
TOP_K = 8
N_GROUP = 8
TOPK_GROUP = 4
GROUP_SIZE = N_EXPERTS // N_GROUP
D_EXPERT = 256
ROUTED_SCALE = 2.5
EPS = 1e-6
NEG_INF = float("-inf")

UNIT = 256
CHUNKS_PER_UNIT = UNIT // CHUNK
HC = H_B * CHUNK
GDN_GROUP = 2
HG = GDN_GROUP * CHUNK
INV_BLOCK = 8
D_PACK = D_MODEL // 2
ROUTE_TILE = 256
ROW_BLOCK = 512
LANES = 128
SUBLANES = 8
VMEM_LIMIT = 56 * 1024 * 1024

C_QKVA = 0
C_GA = 1024
C_QKVB = 1536
C_ZB = 3072
C_SMALL = 3584
D_PROJ = 3712
SM_AB = 32
SM_BETA = 40


def _cparams(sem):
    return pltpu.CompilerParams(dimension_semantics=sem, vmem_limit_bytes=VMEM_LIMIT)


def _split(a):
    hi = a.astype(BF16)
    lo = (a - hi.astype(F32)).astype(BF16)
    return hi, lo


def _dg(a, b, dims):
    return lax.dot_general(a, b, (dims, ((), ())), preferred_element_type=F32)


NN = ((1,), (0,))
NT = ((1,), (1,))
TN = ((0,), (0,))


def _mm(a, b, dims=NN):
    return _dg(a.astype(BF16), b.astype(BF16), dims)


def _mm3(a, b, dims=NN):
    ah, al = _split(a)
    bh, bl = _split(b)
    return _dg(ah, bh, dims) + (_dg(ah, bl, dims) + _dg(al, bh, dims))


def _mm_exact_lhs(a_bf16, b, dims=NN):
    bh, bl = _split(b)
    return _dg(a_bf16, bh, dims) + _dg(a_bf16, bl, dims)


def _silu(x):
    return x * (1.0 / (1.0 + jnp.exp(-x)))


def _sigmoid(x):
    return 1.0 / (1.0 + jnp.exp(-x))


def _softplus(x):
    return jnp.maximum(x, 0.0) + jnp.log1p(jnp.exp(-jnp.abs(x)))


def _log_sigmoid(x):
    return -_softplus(-x)


def _iota2(shape, dim):
    return lax.broadcasted_iota(jnp.int32, shape, dim)


def _rms(x):
    return x * lax.rsqrt(jnp.mean(x * x, axis=-1, keepdims=True) + EPS)


def _pack_rows(x):
    lo = lax.bitcast_convert_type(x[:, 0:D_PACK].astype(BF16).astype(F32), jnp.uint32)
    hi = lax.bitcast_convert_type(x[:, D_PACK:D_MODEL].astype(BF16).astype(F32), jnp.uint32)
    return (lo >> 16) | (hi & jnp.uint32(0xFFFF0000))


def _unpack_rows(p):
    lo = lax.bitcast_convert_type(p << 16, F32)
    hi = lax.bitcast_convert_type(p & jnp.uint32(0xFFFF0000), F32)
    return lo, hi


def _ada_kernel(c_ref, w_ref, b_ref, o_ref):
    o_ref[...] = _mm3(_silu(c_ref[...]), w_ref[...]) + b_ref[...]


def _ada(cond, w, b):
    n = w.shape[1]
    tn = 1536
    return pl.pallas_call(
        _ada_kernel,
        grid=(n // tn,),
        in_specs=[pl.BlockSpec((SUBLANES, D_MODEL), lambda i: (0, 0)),
                  pl.BlockSpec((D_MODEL, tn), lambda i: (0, i)),
                  pl.BlockSpec((1, tn), lambda i: (0, i))],
        out_specs=pl.BlockSpec((SUBLANES, tn), lambda i: (0, i)),
        out_shape=jax.ShapeDtypeStruct((SUBLANES, n), F32),
        compiler_params=_cparams(("parallel",)),
        name="ada",
    )(cond, w, b)


def _two_streams(n_prompt_tok, tm):
    tp = n_prompt_tok // tm
    return [pl.BlockSpec((tm, D_MODEL), lambda i: (jnp.minimum(i, tp - 1), 0)),
            pl.BlockSpec((tm, D_MODEL), lambda i: (jnp.maximum(i - tp, 0), 0))]


def _inproj_kernel(xp_ref, xs_ref, mod_ref, nw_ref, w_ref, ws_ref, cw_ref, o_ref, *, tiles_prompt, prompt_row,
                   sample_row):
    m = mod_ref[0]
    x = jnp.where(pl.program_id(0) < tiles_prompt, xp_ref[...], xs_ref[...])
    h = (_rms(x) * nw_ref[...]) * (1.0 + m[:, D_MODEL:2 * D_MODEL]) + m[:, 0:D_MODEL]
    hb = h.astype(BF16)
    for c0 in range(0, C_SMALL, 512):
        o_ref[:, c0:c0 + 512] = _dg(hb, w_ref[:, c0:c0 + 512], NN)
    o_ref[:, C_SMALL:D_PROJ] = _mm3(h, ws_ref[...])
    tm = xp_ref.shape[0]
    row_len = jnp.where(pl.program_id(0) < tiles_prompt, prompt_row, sample_row)
    pos = _iota2((tm, LANES), 0) & (row_len - 1)
    inside = [jnp.logical_and(pos + (jj - CONV_K // 2) >= 0, pos + (jj - CONV_K // 2) < row_len).astype(F32)
              for jj in range(CONV_K)]
    for part in range(3 * H_B):
        cs = slice(C_QKVB + part * LANES, C_QKVB + (part + 1) * LANES)
        x = o_ref[:, cs]
        acc = x * cw_ref[CONV_K // 2:CONV_K // 2 + 1, part * LANES:(part + 1) * LANES]
        for jj in range(CONV_K):
            off = jj - CONV_K // 2
            if off == 0:
                continue
            xs = pltpu.roll(x, (-off) % tm, 0) * inside[jj]
            acc = acc + xs * cw_ref[jj:jj + 1, part * LANES:(part + 1) * LANES]
        y = _silu(acc)
        if part < 2 * H_B:
            y = y * lax.rsqrt(jnp.sum(y * y, axis=-1, keepdims=True) + EPS)
            if part < H_B:
                y = y * (DK_B ** -0.5)
        o_ref[:, cs] = y


def _cond_row(i, tiles_prompt, tiles_per_seq):
    return jnp.where(i < tiles_prompt, 0, 1 + (i - tiles_prompt) // tiles_per_seq)


def _inproj(xp, xs, mod3, norm_w, w_main, w_small, conv_w, sample_len, prompt_row, sample_row, tm):
    n_prompt_tok = xp.shape[0]
    n_tok = n_prompt_tok + xs.shape[0]
    assert tm % prompt_row == 0 and tm % sample_row == 0
    cond = functools.partial(_cond_row, tiles_prompt=n_prompt_tok // tm, tiles_per_seq=sample_len // tm)
    kern = functools.partial(_inproj_kernel, tiles_prompt=n_prompt_tok // tm, prompt_row=prompt_row,
                             sample_row=sample_row)
    return pl.pallas_call(
        kern,
        grid=(n_tok // tm,),
        in_specs=_two_streams(n_prompt_tok, tm) + [
                  pl.BlockSpec((1, 1, 6 * D_MODEL), lambda i: (cond(i), 0, 0)),
                  pl.BlockSpec((1, D_MODEL), lambda i: (0, 0)),
                  pl.BlockSpec((D_MODEL, C_SMALL), lambda i: (0, 0)),
                  pl.BlockSpec((D_MODEL, LANES), lambda i: (0, 0)),
                  pl.BlockSpec((SUBLANES, 3 * QB), lambda i: (0, 0))],
        out_specs=pl.BlockSpec((tm, D_PROJ), lambda i: (i, 0)),
        out_shape=jax.ShapeDtypeStruct((n_tok, D_PROJ), F32),
        compiler_params=_cparams(("parallel",)),
        name="inproj",
    )(xp, xs, mod3, norm_w, w_main, w_small, conv_w)


def _unit_ids(j, n_prompt_units, units_per_seq):
    jj = j - n_prompt_units
    b = jj // units_per_seq
    r = jj % units_per_seq
    is_prompt = j < n_prompt_units
    uf = j
    ub = jnp.where(is_prompt, j, n_prompt_units + b * units_per_seq + (units_per_seq - 1 - r))
    init_row = jnp.where(is_prompt, 0, 1 + b)
    first = jnp.logical_or(is_prompt, r == 0)
    return uf, ub, init_row, first


def _tri(rev):
    t = _iota2((CHUNK, CHUNK), 0)
    s = _iota2((CHUNK, CHUNK), 1)
    return (t <= s) if rev else (t >= s)


def _mm_exact_lhs_tn(a, ones_bf16):
    ah, al = _split(a)
    return _dg(ah, ones_bf16, TN) + _dg(al, ones_bf16, TN)


def _gla_unit(chunks, order, states, rev):
    tri = _tri(rev).astype(BF16)
    mid, last = (CHUNK // 2 - 1, 0) if rev else (CHUNK // 2, CHUNK - 1)
    scale = DK_A ** -0.5
    ones = jnp.ones((CHUNK, LANES), BF16)
    lane = _iota2((CHUNK, LANES), 1)
    row = _iota2((CHUNK, LANES), 0)
    s_in = lane % DK_A
    causal = (row <= s_in) if rev else (row >= s_in)
    zeros_v = jnp.zeros((CHUNK, DV_A), F32)
    pairs = range(H_A // 2)
    att, qs_l, u_l, dec_l = {}, {}, {}, {}
    for c, (q, k, v, la) in enumerate(chunks):
        b = _mm_exact_lhs(tri, la)
        bref = b[mid:mid + 1, :]
        blast = b[last:last + 1, :]
        qg = q * jnp.exp(b - bref) * scale
        kg = k * jnp.exp(bref - b)
        kd = k * jnp.exp(blast - b)
        qs_l[c] = q * jnp.exp(b) * scale
        for p in pairs:
            ls = slice(p * LANES, (p + 1) * LANES)
            kg_p = kg[:, ls]
            rhs_att = jnp.concatenate([jnp.where(lane < DK_A, kg_p, 0.0), jnp.where(lane >= DK_A, kg_p, 0.0)], axis=0)
            att[c, p] = jnp.where(causal, _mm(qg[:, ls], rhs_att, NT), 0.0)
            u_l[c, p] = _mm(kd[:, ls], v[:, 2 * p * DV_A:(2 * p + 2) * DV_A], TN)
            dec_l[c, p] = jnp.exp(_mm_exact_lhs_tn(la[:, ls], ones))
    start = {}
    for c in order:
        start[c] = list(states)
        nxt = []
        for p in pairs:
            u, dec = u_l[c, p], dec_l[c, p]
            nxt.append(dec[0:DK_A] * states[2 * p] + u[0:DK_A, 0:DV_A])
            nxt.append(dec[DK_A:2 * DK_A] * states[2 * p + 1] + u[DK_A:2 * DK_A, DV_A:2 * DV_A])
        states = nxt
    outs = []
    for c, (q, k, v, la) in enumerate(chunks):
        o = []
        for p in pairs:
            ls = slice(p * LANES, (p + 1) * LANES)
            vs0 = v[:, (2 * p) * DV_A:(2 * p + 1) * DV_A]
            vs1 = v[:, (2 * p + 1) * DV_A:(2 * p + 2) * DV_A]
            s0, s1 = start[c][2 * p], start[c][2 * p + 1]
            rhs_o = jnp.concatenate([jnp.concatenate([vs0, zeros_v], axis=1),
                                     jnp.concatenate([zeros_v, vs1], axis=1),
                                     jnp.concatenate([s0, zeros_v], axis=1),
                                     jnp.concatenate([zeros_v, s1], axis=1)], axis=0)
            lhs_o = jnp.concatenate([att[c, p], qs_l[c][:, ls]], axis=1)
            o.append(_mm(lhs_o, rhs_o))
        outs.append(jnp.concatenate(o, axis=1))
    return outs, states


def _gla_kernel(qf_ref, qb_ref, sf_ref, sb_ref, wlr_ref, blr_ref, init_ref, of_ref, ob_ref, so_ref, s_ref,
                *, n_prompt_units, units_per_seq):
    j = pl.program_id(0)
    _, _, _, first = _unit_ids(j, n_prompt_units, units_per_seq)

    @pl.when(first)
    def _():
        s_ref[...] = init_ref[0]

    for d in range(N_DIR):
        x_ref, sm_ref, o_ref = (qf_ref, sf_ref, of_ref) if d == 0 else (qb_ref, sb_ref, ob_ref)
        la_unit = _log_sigmoid(_mm3(sm_ref[...], wlr_ref[d]) + blr_ref[d]) * (1.0 / GLA_NORMALIZER)
        states = [s_ref[d, h] for h in range(H_A)]
        order = range(CHUNKS_PER_UNIT - 1, -1, -1) if d == 1 else range(CHUNKS_PER_UNIT)
        chunks = []
        for c in range(CHUNKS_PER_UNIT):
            rows = slice(c * CHUNK, (c + 1) * CHUNK)
            chunks.append((x_ref[rows, 0:QA], x_ref[rows, QA:2 * QA], x_ref[rows, 2 * QA:2 * QA + VA], la_unit[rows]))
        outs, states = _gla_unit(chunks, order, states, rev=(d == 1))
        for c in range(CHUNKS_PER_UNIT):
            o_ref[c * CHUNK:(c + 1) * CHUNK, :] = outs[c]
        for h in range(H_A):
            s_ref[d, h] = states[h]

    @pl.when(j < n_prompt_units)
    def _():
        so_ref[0] = s_ref[...]


def _gla(proj, wlr, blr, init, n_prompt_units, units_per_seq):
    n_tok = proj.shape[0]
    n_units = n_tok // UNIT
    ids = functools.partial(_unit_ids, n_prompt_units=n_prompt_units, units_per_seq=units_per_seq)
    small_blk = C_SMALL // LANES
    st_blk = (1, N_DIR, H_A, DK_A, DV_A)
    kern = functools.partial(_gla_kernel, n_prompt_units=n_prompt_units, units_per_seq=units_per_seq)
    return pl.pallas_call(
        kern,
        grid=(n_units,),
        in_specs=[pl.BlockSpec((UNIT, 2 * QA + VA), lambda j: (ids(j)[0], C_QKVA // (2 * QA + VA))),
                  pl.BlockSpec((UNIT, 2 * QA + VA), lambda j: (ids(j)[1], C_QKVA // (2 * QA + VA))),
                  pl.BlockSpec((UNIT, LANES), lambda j: (ids(j)[0], small_blk)),
                  pl.BlockSpec((UNIT, LANES), lambda j: (ids(j)[1], small_blk)),
                  pl.BlockSpec((N_DIR, LANES, QA), lambda j: (0, 0, 0)),
                  pl.BlockSpec((N_DIR, 1, QA), lambda j: (0, 0, 0)),
                  pl.BlockSpec(st_blk, lambda j: (ids(j)[2], 0, 0, 0, 0))],
        out_specs=[pl.BlockSpec((UNIT, VA), lambda j: (ids(j)[0], 0)),
                   pl.BlockSpec((UNIT, VA), lambda j: (ids(j)[1], 0)),
                   pl.BlockSpec(st_blk, lambda j: (jnp.minimum(j, n_prompt_units - 1), 0, 0, 0, 0))],
        out_shape=[jax.ShapeDtypeStruct((n_tok, VA), F32),
                   jax.ShapeDtypeStruct((n_tok, VA), F32),
                   jax.ShapeDtypeStruct((n_prompt_units, N_DIR, H_A, DK_A, DV_A), F32)],
        scratch_shapes=[pltpu.VMEM((N_DIR, H_A, DK_A, DV_A), F32)],
        compiler_params=_cparams(("arbitrary",)),
        name="gla",
    )(proj, proj, proj, proj, wlr, blr, init)


def _stack_masks(rev):
    r = _iota2((HG, HG), 0)
    c = _iota2((HG, HG), 1)
    same = (r // CHUNK) == (c // CHUNK)
    tr, tc = r % CHUNK, c % CHUNK
    incl = jnp.logical_and(same, (tr <= tc) if rev else (tr >= tc))
    strict = jnp.logical_and(same, (tr < tc) if rev else (tr > tc))
    return incl, strict


def _spread(x):
    z = jnp.zeros((CHUNK, LANES), x.dtype)
    rows = []
    for h in range(H_B):
        xh = x[h * CHUNK:(h + 1) * CHUNK]
        rows.append(jnp.concatenate([xh if g == h else z for g in range(H_B)], axis=1))
    return jnp.concatenate(rows, axis=0)


def _gdn_prepare(chunks):
    masks = {rev: _stack_masks(rev) for rev in {ch[6] for ch in chunks}}
    r = _iota2((HG, HG), 0)
    c = _iota2((HG, HG), 1)
    a_l, att_l, kb_l = [], [], []
    for q, k, v, gcb, beta, glast, rev in chunks:
        incl, strict = masks[rev]
        grow = gcb.T[0:1, :]
        diff = gcb[:, 0:1] - grow
        decay = jnp.where(incl, jnp.exp(jnp.where(incl, diff, 0.0)), 0.0)
        kb = k * beta
        m1 = _mm(jnp.concatenate([kb, q], axis=0), k, NT)
        a_l.append(jnp.where(strict, m1[0:HG] * decay, 0.0))
        att_l.append(m1[HG:2 * HG] * decay)
        kb_l.append(kb)
    diag = (r // INV_BLOCK) == (c // INV_BLOCK)
    pw_l = [jnp.where(diag, a, 0.0) for a in a_l]
    qm_l = [-p for p in pw_l]
    n = 2
    while n < INV_BLOCK:
        pw_l = [_mm(p, p) for p in pw_l]
        qm_l = [qm + p + _mm(qm, p) for qm, p in zip(qm_l, pw_l)]
        n *= 2
    b = INV_BLOCK
    while b < CHUNK:
        off = jnp.logical_and((r // (2 * b)) == (c // (2 * b)), (r // b) != (c // b))
        al_l = [jnp.where(off, a, 0.0) for a in a_l]
        t1_l = [al + _mm(qm, al) for qm, al in zip(qm_l, al_l)]
        qm_l = [qm - (t1 + _mm(t1, qm)) for qm, t1 in zip(qm_l, t1_l)]
        b *= 2
    out = []
    for (q, k, v, gcb, beta, glast, _), qm, kb, att in zip(chunks, qm_l, kb_l, att_l):
        egc = jnp.exp(gcb)
        rhs = jnp.concatenate([v * beta, kb * egc], axis=1)
        sol = rhs + _mm(qm, rhs)
        out.append((sol[:, 0:DV_B], sol[:, DV_B:2 * DV_B], q * egc, k * jnp.exp(glast - gcb), att))
    return out


def _gdn_scan_step(value, k_cum, q_dec, k_dec, att, gl_rows, s):
    kq = _mm(jnp.concatenate([_spread(k_cum), _spread(q_dec)], axis=0), s)
    v_new = value - kq[0:HC]
    o = kq[HC:2 * HC] + _mm(att, v_new)
    s_new = s * gl_rows + _mm(_spread(k_dec), v_new, TN)
    return o, s_new


def _gdn_kernel(xf_ref, xb_ref, sf_ref, sb_ref, gco_ref, gdt_ref, init_ref, of_ref, ob_ref, so_ref, s_ref,
                *, n_prompt_units, units_per_seq):
    j = pl.program_id(0)
    _, _, _, first = _unit_ids(j, n_prompt_units, units_per_seq)

    @pl.when(first)
    def _():
        s_ref[...] = init_ref[0]

    chunks, gl_rows = [], []
    for d in range(N_DIR):
        x_ref, sm_ref = (xf_ref, sf_ref) if d == 0 else (xb_ref, sb_ref)
        rev = d == 1
        sm = sm_ref[...]
        g_all = gco_ref[...] * _softplus(sm + gdt_ref[...])
        beta_all = _sigmoid(sm)
        tri = _tri(rev).astype(BF16)
        last = 0 if rev else CHUNK - 1
        for c in range(CHUNKS_PER_UNIT):
            rows = slice(c * CHUNK, (c + 1) * CHUNK)
            gc_all = _mm_exact_lhs(tri, g_all[rows])
            gcb, beta, glast = [], [], []
            for h in range(H_B):
                col = SM_AB + d * H_B + h
                colb = SM_BETA + d * H_B + h
                gh = jnp.broadcast_to(gc_all[:, col:col + 1], (CHUNK, LANES))
                gcb.append(gh)
                glast.append(jnp.broadcast_to(gh[last:last + 1, :], (CHUNK, LANES)))
                beta.append(jnp.broadcast_to(beta_all[rows, colb:colb + 1], (CHUNK, LANES)))
            for g0 in range(0, H_B, GDN_GROUP):
                hs = range(g0, g0 + GDN_GROUP)
                stack = lambda base: jnp.concatenate(
                    [x_ref[rows, base + h * LANES:base + (h + 1) * LANES] for h in hs], axis=0)
                cat = lambda parts: jnp.concatenate([parts[h] for h in hs], axis=0)
                chunks.append((stack(0), stack(QB), stack(2 * QB), cat(gcb), cat(beta), cat(glast), rev))
            gl_rows.append(jnp.concatenate([jnp.broadcast_to(jnp.exp(g[0:1, :]), (DK_B, DV_B)) for g in glast],
                                           axis=0))
    n_grp = H_B // GDN_GROUP
    zero_att = jnp.zeros((HG, HG), F32)
    prepared = []
    grouped = _gdn_prepare(chunks)
    for i in range(0, len(grouped), n_grp):
        parts = grouped[i:i + n_grp]
        rows_cat = [jnp.concatenate([p[f] for p in parts], axis=0) for f in range(4)]
        att = jnp.concatenate([jnp.concatenate([parts[g][4] if gg == g else zero_att for gg in range(n_grp)], axis=1)
                               for g in range(n_grp)], axis=0)
        prepared.append(tuple(rows_cat) + (att,))
    s = [s_ref[d] for d in range(N_DIR)]
    for step in range(CHUNKS_PER_UNIT):
        for d, o_ref in ((0, of_ref), (1, ob_ref)):
            c = step if d == 0 else CHUNKS_PER_UNIT - 1 - step
            o, s[d] = _gdn_scan_step(*prepared[d * CHUNKS_PER_UNIT + c], gl_rows[d * CHUNKS_PER_UNIT + c], s[d])
            for h in range(H_B):
                o_ref[c * CHUNK:(c + 1) * CHUNK, h * DV_B:(h + 1) * DV_B] = o[h * CHUNK:(h + 1) * CHUNK]
    for d in range(N_DIR):
        s_ref[d] = s[d]

    @pl.when(j < n_prompt_units)
    def _():
        so_ref[0] = s_ref[...]


def _gdn(proj, gcoef, gdt, init, n_prompt_units, units_per_seq):
    n_tok = proj.shape[0]
    n_units = n_tok // UNIT
    ids = functools.partial(_unit_ids, n_prompt_units=n_prompt_units, units_per_seq=units_per_seq)
    small_blk = C_SMALL // LANES
    qkv_blk = C_QKVB // (3 * QB)
    st_blk = (1, N_DIR, H_B * DK_B, DV_B)
    kern = functools.partial(_gdn_kernel, n_prompt_units=n_prompt_units, units_per_seq=units_per_seq)
    return pl.pallas_call(
        kern,
        grid=(n_units,),
        in_specs=[pl.BlockSpec((UNIT, 3 * QB), lambda j: (ids(j)[0], qkv_blk)),
                  pl.BlockSpec((UNIT, 3 * QB), lambda j: (ids(j)[1], qkv_blk)),
                  pl.BlockSpec((UNIT, LANES), lambda j: (ids(j)[0], small_blk)),
                  pl.BlockSpec((UNIT, LANES), lambda j: (ids(j)[1], small_blk)),
                  pl.BlockSpec((1, LANES), lambda j: (0, 0)),
                  pl.BlockSpec((1, LANES), lambda j: (0, 0)),
                  pl.BlockSpec(st_blk, lambda j: (ids(j)[2], 0, 0, 0))],
        out_specs=[pl.BlockSpec((UNIT, VB), lambda j: (ids(j)[0], 0)),
                   pl.BlockSpec((UNIT, VB), lambda j: (ids(j)[1], 0)),
                   pl.BlockSpec(st_blk, lambda j: (jnp.minimum(j, n_prompt_units - 1), 0, 0, 0))],
        out_shape=[jax.ShapeDtypeStruct((n_tok, VB), F32),
                   jax.ShapeDtypeStruct((n_tok, VB), F32),
                   jax.ShapeDtypeStruct((n_prompt_units, N_DIR, H_B * DK_B, DV_B), F32)],
        scratch_shapes=[pltpu.VMEM((N_DIR, H_B * DK_B, DV_B), F32)],
        compiler_params=_cparams(("arbitrary",)),
        name="gdn",
    )(proj, proj, proj, proj, gcoef, gdt, init)


def _head_rms(o, w):
    parts = []
    for h in range(o.shape[1] // LANES):
        parts.append(_rms(o[:, h * LANES:(h + 1) * LANES]))
    return jnp.concatenate(parts, axis=1) * w


def _post_kernel(xp_ref, xs_ref, ga_ref, zb_ref, af_ref, ab_ref, df_ref, db_ref, mod_ref, wo_ref, nwa_ref, nwb_ref,
                 n2_ref, rw_ref, x1_ref, h2_ref, sc_ref, *, tiles_prompt):
    m = mod_ref[0]
    x = jnp.where(pl.program_id(0) < tiles_prompt, xp_ref[...], xs_ref[...])
    gla = _head_rms(af_ref[...] + ab_ref[...], nwa_ref[...]) * _silu(ga_ref[...])
    gdn = _head_rms(df_ref[...] + db_ref[...], nwb_ref[...]) * _silu(zb_ref[...])
    y = _dg(gla.astype(BF16), wo_ref[0:VA, :], NN) + _dg(gdn.astype(BF16), wo_ref[VA:VA + VB, :], NN)
    x1 = x + m[:, 2 * D_MODEL:3 * D_MODEL] * y
    x1_ref[...] = x1
    h2 = (_rms(x1) * n2_ref[...]) * (1.0 + m[:, 4 * D_MODEL:5 * D_MODEL]) + m[:, 3 * D_MODEL:4 * D_MODEL]
    h2_ref[...] = _pack_rows(h2)
    sc_ref[...] = _sigmoid(_mm3(rw_ref[...], h2, NT))


def _post(xp, xs, proj, o_af, o_ab, o_df, o_db, mod3, w_o, nwa, nwb, n2w, router_wt, sample_len, tm):
    n_prompt_tok = xp.shape[0]
    n_tok = n_prompt_tok + xs.shape[0]
    cond = functools.partial(_cond_row, tiles_prompt=n_prompt_tok // tm, tiles_per_seq=sample_len // tm)
    tok = lambda i: (i, 0)
    const = lambda i: (0, 0)
    return pl.pallas_call(
        functools.partial(_post_kernel, tiles_prompt=n_prompt_tok // tm),
        grid=(n_tok // tm,),
        in_specs=_two_streams(n_prompt_tok, tm) + [
                  pl.BlockSpec((tm, VA), lambda i: (i, C_GA // VA)),
                  pl.BlockSpec((tm, VB), lambda i: (i, C_ZB // VB)),
                  pl.BlockSpec((tm, VA), tok), pl.BlockSpec((tm, VA), tok),
                  pl.BlockSpec((tm, VB), tok), pl.BlockSpec((tm, VB), tok),
                  pl.BlockSpec((1, 1, 6 * D_MODEL), lambda i: (cond(i), 0, 0)),
                  pl.BlockSpec((VA + VB, D_MODEL), const),
                  pl.BlockSpec((1, VA), const), pl.BlockSpec((1, VB), const), pl.BlockSpec((1, D_MODEL), const),
                  pl.BlockSpec((N_EXPERTS, D_MODEL), const)],
        out_specs=[pl.BlockSpec((tm, D_MODEL), tok),
                   pl.BlockSpec((tm, D_PACK), tok),
                   pl.BlockSpec((N_EXPERTS, tm), lambda i: (0, i))],
        out_shape=[jax.ShapeDtypeStruct((n_tok, D_MODEL), F32),
                   jax.ShapeDtypeStruct((n_tok, D_PACK), jnp.uint32),
                   jax.ShapeDtypeStruct((N_EXPERTS, n_tok), F32)],
        compiler_params=_cparams(("parallel",)),
        name="post",
    )(xp, xs, proj, proj, o_af, o_ab, o_df, o_db, mod3, w_o, nwa, nwb, n2w, router_wt)


def _route_kernel(sc_ref, bias_ref, e_ref, r_ref, w_ref, cnt_ref, carry_ref):
    i = pl.program_id(0)
    t = sc_ref.shape[1]

    @pl.when(i == 0)
    def _():
        carry_ref[...] = jnp.zeros(carry_ref.shape, F32)

    s = sc_ref[...]
    biased = s + bias_ref[:, 0:1]
    gs = []
    for g in range(N_GROUP):
        blk = biased[g * GROUP_SIZE:(g + 1) * GROUP_SIZE]
        m1 = jnp.max(blk, axis=0, keepdims=True)
        n1 = jnp.sum((blk == m1).astype(F32), axis=0, keepdims=True)
        m2 = jnp.max(jnp.where(blk < m1, blk, NEG_INF), axis=0, keepdims=True)
        gs.append(m1 + jnp.where(n1 >= 2.0, m1, m2))
    gsc = jnp.concatenate(gs, axis=0)
    gid = _iota2((N_GROUP, t), 0)
    beaten = jnp.zeros((N_GROUP, t), F32)
    for g in range(N_GROUP):
        other = gsc[g:g + 1, :]
        wins = jnp.logical_or(other > gsc, jnp.logical_and(other == gsc, g < gid))
        beaten = beaten + wins.astype(F32)
    masked = jnp.concatenate(
        [jnp.where(beaten[g:g + 1, :] < float(TOPK_GROUP), biased[g * GROUP_SIZE:(g + 1) * GROUP_SIZE], NEG_INF)
         for g in range(N_GROUP)], axis=0)
    eid = _iota2((N_EXPERTS, t), 0).astype(F32)
    sel = jnp.zeros((N_EXPERTS, t), F32)
    picks, scores = [], []
    for _ in range(TOP_K):
        m = jnp.max(masked, axis=0, keepdims=True)
        first = jnp.min(jnp.where(masked == m, eid, float(N_EXPERTS)), axis=0, keepdims=True)
        hit = eid == first
        scores.append(jnp.sum(jnp.where(hit, s, 0.0), axis=0, keepdims=True))
        masked = jnp.where(hit, NEG_INF, masked)
        sel = sel + hit.astype(F32)
        picks.append(first)
    upper = (_iota2((t, t), 0) < _iota2((t, t), 1)).astype(BF16)
    carry = carry_ref[...]
    prefix = _dg(sel.astype(BF16), upper, NN) + jnp.concatenate([carry] * (t // LANES), axis=1)
    ranks = [jnp.sum(jnp.where(eid == p, prefix, 0.0), axis=0, keepdims=True) for p in picks]
    carry = carry + _dg(sel.astype(BF16), jnp.ones((t, LANES), BF16), NN)
    carry_ref[...] = carry
    cnt_ref[...] = carry
    sc8 = jnp.concatenate(scores, axis=0)
    e_ref[...] = jnp.concatenate(picks, axis=0).astype(jnp.int32)
    r_ref[...] = jnp.concatenate(ranks, axis=0).astype(jnp.int32)
    w_ref[...] = sc8 / jnp.sum(sc8, axis=0, keepdims=True) * ROUTED_SCALE


def _route(scores_t, bias_col):
    n_tok = scores_t.shape[1]
    t = ROUTE_TILE
    slot = lambda i: (0, i)
    return pl.pallas_call(
        _route_kernel,
        grid=(n_tok // t,),
        in_specs=[pl.BlockSpec((N_EXPERTS, t), slot),
                  pl.BlockSpec((N_EXPERTS, LANES), lambda i: (0, 0))],
        out_specs=[pl.BlockSpec((TOP_K, t), slot), pl.BlockSpec((TOP_K, t), slot), pl.BlockSpec((TOP_K, t), slot),
                   pl.BlockSpec((N_EXPERTS, LANES), lambda i: (0, 0))],
        out_shape=[jax.ShapeDtypeStruct((TOP_K, n_tok), jnp.int32),
                   jax.ShapeDtypeStruct((TOP_K, n_tok), jnp.int32),
                   jax.ShapeDtypeStruct((TOP_K, n_tok), F32),
                   jax.ShapeDtypeStruct((N_EXPERTS, LANES), F32)],
        scratch_shapes=[pltpu.VMEM((N_EXPERTS, LANES), F32)],
        compiler_params=_cparams(("arbitrary",)),
        name="route",
    )(scores_t, bias_col)


def _dest_kernel(start_ref, e_ref, r_ref, d_ref):
    e = e_ref[...]

    def body(x, acc):
        return jnp.where(e == x, start_ref[x], acc)

    d_ref[...] = r_ref[...] + lax.fori_loop(0, N_EXPERTS, body, jnp.zeros(e.shape, jnp.int32))


def _dest(start, e8, r8, tt):
    n_tok = e8.shape[1]
    slot = lambda i, st: (0, i)
    return pl.pallas_call(
        _dest_kernel,
        grid_spec=pltpu.PrefetchScalarGridSpec(
            num_scalar_prefetch=1, grid=(n_tok // tt,),
            in_specs=[pl.BlockSpec((TOP_K, tt), slot), pl.BlockSpec((TOP_K, tt), slot)],
            out_specs=pl.BlockSpec((TOP_K, tt), slot)),
        out_shape=jax.ShapeDtypeStruct((TOP_K, n_tok), jnp.int32),
        compiler_params=_cparams(("parallel",)),
        name="dest",
    )(start, e8, r8)


def _items(counts, n_rows):
    n_blocks = n_rows // ROW_BLOCK
    max_items = n_blocks + N_EXPERTS - 1
    end = jnp.cumsum(counts)
    start = end - counts
    first_blk = start // ROW_BLOCK
    n_it = jnp.where(counts > 0, (end - 1) // ROW_BLOCK - first_blk + 1, 0)
    it_end = jnp.cumsum(n_it)
    it_start = it_end - n_it
    i = jnp.arange(max_items, dtype=jnp.int32)
    valid = i < it_end[-1]
    ex = jnp.minimum(jnp.sum((it_end[None, :] <= i[:, None]).astype(jnp.int32), axis=1), N_EXPERTS - 1)
    onehot = ex[:, None] == jnp.arange(N_EXPERTS, dtype=jnp.int32)[None, :]
    pick = lambda tab: jnp.sum(jnp.where(onehot, tab[None, :], 0), axis=1)
    blk = pick(first_blk) + (i - pick(it_start))
    lo = jnp.maximum(pick(start), blk * ROW_BLOCK) - blk * ROW_BLOCK
    hi = jnp.minimum(pick(end), (blk + 1) * ROW_BLOCK) - blk * ROW_BLOCK
    blk = jnp.where(valid, blk, n_blocks - 1).astype(jnp.int32)
    lo = jnp.where(valid, lo, 0).astype(jnp.int32)
    hi = jnp.where(valid, hi, 0).astype(jnp.int32)
    eids = jnp.arange(N_EXPERTS, dtype=jnp.int32)
    later = jnp.logical_and(eids[None, :] > eids[:, None], (counts > 0)[None, :])
    nxt_e = jnp.min(jnp.where(later, eids[None, :], N_EXPERTS), axis=1)
    nxt = pick(jnp.where(nxt_e < N_EXPERTS, nxt_e, -1)).astype(jnp.int32)
    return start.astype(jnp.int32), blk, ex.astype(jnp.int32), nxt, lo, hi


def _dispatch_kernel(h2_ref, dest_hbm, xs_hbm, dsm, sem_d, sem_s):
    i = pl.program_id(0)
    n = pl.num_programs(0)
    tm = h2_ref.shape[0]

    def dest_copy(step, slot):
        return pltpu.make_async_copy(dest_hbm.at[:, pl.ds(step * tm, tm)], dsm.at[slot], sem_d.at[slot])

    def row_copy(t, dst):
        return pltpu.make_async_copy(h2_ref.at[pl.ds(t, 1), :], xs_hbm.at[pl.ds(dst, 1), :], sem_s.at[0])

    @pl.when(i == 0)
    def _():
        dest_copy(0, 0).start()

    slot = i % 2
    dest_copy(i, slot).wait()

    @pl.when(i + 1 < n)
    def _():
        dest_copy(i + 1, 1 - slot).start()

    def issue(t, carry):
        for k in range(TOP_K):
            row_copy(t, dsm[slot, k, t]).start()
        return carry
    lax.fori_loop(0, tm, issue, 0, unroll=2)

    def drain(t, carry):
        for k in range(TOP_K):
            row_copy(t, 0).wait()
        return carry
    lax.fori_loop(0, tm, drain, 0)


def _dispatch(h2, dest, tm):
    n_tok = h2.shape[0]
    return pl.pallas_call(
        _dispatch_kernel,
        grid=(n_tok // tm,),
        in_specs=[pl.BlockSpec((tm, D_PACK), lambda i: (i, 0)),
                  pl.BlockSpec(memory_space=pl.ANY)],
        out_specs=pl.BlockSpec(memory_space=pl.ANY),
        out_shape=jax.ShapeDtypeStruct((n_tok * TOP_K, D_PACK), jnp.uint32),
        scratch_shapes=[pltpu.SMEM((2, TOP_K, tm), jnp.int32),
                        pltpu.SemaphoreType.DMA((2,)),
                        pltpu.SemaphoreType.DMA((1,))],
        compiler_params=_cparams(("arbitrary",)),
        name="dispatch",
    )(h2, dest)


def _experts_kernel(blk_ref, exp_ref, nxt_ref, lo_ref, hi_ref, x_ref, wg_hbm, wu_hbm, wd_hbm, y_ref,
                    wgf, wuf, wdf, wgb, wub, wdb, nchg, sem_w):
    i = pl.program_id(0)
    lo, hi = lo_ref[i], hi_ref[i]

    def weight_copies(e, slot):
        return (pltpu.make_async_copy(wg_hbm.at[e], wgf.at[slot], sem_w.at[slot]),
                pltpu.make_async_copy(wu_hbm.at[e], wuf.at[slot], sem_w.at[slot]),
                pltpu.make_async_copy(wd_hbm.at[e], wdf.at[slot], sem_w.at[slot]))

    @pl.when(i == 0)
    def _():
        nchg[0] = 0
        for cp in weight_copies(exp_ref[0], 0):
            cp.start()

    @pl.when(hi > lo)
    def _():
        @pl.when(jnp.logical_or(i == 0, exp_ref[i] != exp_ref[jnp.maximum(i - 1, 0)]))
        def _():
            slot = nchg[0] % 2
            for cp in weight_copies(exp_ref[i], slot):
                cp.wait()

            @pl.when(nxt_ref[i] >= 0)
            def _():
                for cp in weight_copies(nxt_ref[i], 1 - slot):
                    cp.start()

            wgb[...] = wgf[slot].astype(BF16)
            wub[...] = wuf[slot].astype(BF16)
            wdb[...] = wdf[slot].astype(BF16)
            nchg[0] = nchg[0] + 1

        x_lo, x_hi = _unpack_rows(x_ref[...])
        x = jnp.concatenate([x_lo.astype(BF16), x_hi.astype(BF16)], axis=1)
        g = _dg(x, wgb[...], NN)
        u = _dg(x, wub[...], NN)
        hmid = (_silu(g) * u).astype(BF16)
        y = _pack_rows(_dg(hmid, wdb[...], NN))
        row = _iota2((ROW_BLOCK, D_PACK), 0)
        mine = jnp.logical_and(row >= lo, row < hi)

        @pl.when(lo == 0)
        def _():
            y_ref[...] = jnp.where(mine, y, jnp.uint32(0))

        @pl.when(lo > 0)
        def _():
            y_ref[...] = jnp.where(mine, y, y_ref[...])


def _experts(item_blk, item_exp, item_nxt, item_lo, item_hi, xs, w_gate, w_up, w_down):
    n_items = item_blk.shape[0]
    xmap = lambda i, blk, ex, nxt, lo, hi: (blk[i], 0)
    return pl.pallas_call(
        _experts_kernel,
        grid_spec=pltpu.PrefetchScalarGridSpec(
            num_scalar_prefetch=5, grid=(n_items,),
            in_specs=[pl.BlockSpec((ROW_BLOCK, D_PACK), xmap),
                      pl.BlockSpec(memory_space=pl.ANY),
                      pl.BlockSpec(memory_space=pl.ANY),
                      pl.BlockSpec(memory_space=pl.ANY)],
            out_specs=pl.BlockSpec((ROW_BLOCK, D_PACK), xmap),
            scratch_shapes=[pltpu.VMEM((2, D_MODEL, D_EXPERT), F32), pltpu.VMEM((2, D_MODEL, D_EXPERT), F32),
                            pltpu.VMEM((2, D_EXPERT, D_MODEL), F32),
                            pltpu.VMEM((D_MODEL, D_EXPERT), BF16), pltpu.VMEM((D_MODEL, D_EXPERT), BF16),
                            pltpu.VMEM((D_EXPERT, D_MODEL), BF16),
                            pltpu.SMEM((1,), jnp.int32),
                            pltpu.SemaphoreType.DMA((2,))]),
        out_shape=jax.ShapeDtypeStruct(xs.shape, jnp.uint32),
        compiler_params=_cparams(("arbitrary",)),
        name="experts",
    )(item_blk, item_exp, item_nxt, item_lo, item_hi, xs, w_gate, w_up, w_down)


def _final_kernel(x1_ref, h2_ref, w_ref, mod_ref, sg_ref, su_ref, sd_ref, fn_ref, dest_hbm, ys_hbm, op_ref, os_ref,
                  dsm, gbuf, sem_d, sem_g, *, tiles_prompt):
    i = pl.program_id(0)
    n = pl.num_programs(0)
    tm = x1_ref.shape[0]

    def dest_copy(step, slot):
        return pltpu.make_async_copy(dest_hbm.at[:, pl.ds(step * tm, tm)], dsm.at[slot], sem_d.at[slot])

    def row_copy(src, slot, k, t):
        return pltpu.make_async_copy(ys_hbm.at[pl.ds(src, 1), :], gbuf.at[slot, k, pl.ds(t, 1), :], sem_g.at[slot])

    def issue_gathers(slot3, slot2):
        def body(t, carry):
            for k in range(TOP_K):
                row_copy(dsm[slot3, k, t], slot2, k, t).start()
            return carry
        lax.fori_loop(0, tm, body, 0, unroll=2)

    @pl.when(i == 0)
    def _():
        dest_copy(0, 0).start()
        dest_copy(0, 0).wait()
        issue_gathers(0, 0)

        @pl.when(n > 1)
        def _():
            dest_copy(1, 1).start()

    @pl.when(i + 1 < n)
    def _():
        dest_copy(i + 1, (i + 1) % 3).wait()
        issue_gathers((i + 1) % 3, (i + 1) % 2)

        @pl.when(i + 2 < n)
        def _():
            dest_copy(i + 2, (i + 2) % 3).start()

    slot = i % 2

    def drain(t, carry):
        for k in range(TOP_K):
            row_copy(0, slot, k, t).wait()
        return carry
    lax.fori_loop(0, tm, drain, 0)

    m = mod_ref[0]
    w = w_ref[...]
    r_lo = r_hi = None
    for kk in range(TOP_K):
        y_lo, y_hi = _unpack_rows(gbuf[slot, kk])
        wk = w[:, kk:kk + 1]
        r_lo = y_lo * wk if r_lo is None else r_lo + y_lo * wk
        r_hi = y_hi * wk if r_hi is None else r_hi + y_hi * wk
    routed = jnp.concatenate([r_lo, r_hi], axis=1)
    h_lo, h_hi = _unpack_rows(h2_ref[...])
    hb = jnp.concatenate([h_lo.astype(BF16), h_hi.astype(BF16)], axis=1)
    hm = (_silu(_dg(hb, sg_ref[...], NN)) * _dg(hb, su_ref[...], NN)).astype(BF16)
    shared = _dg(hm, sd_ref[...], NN)
    x2 = x1_ref[...] + m[:, 5 * D_MODEL:6 * D_MODEL] * (routed + shared)
    out = _rms(x2) * fn_ref[...]

    @pl.when(i < tiles_prompt)
    def _():
        op_ref[...] = out

    @pl.when(i >= tiles_prompt)
    def _():
        os_ref[...] = out


def _final(x1, h2, wts, mod3, sg, su, sd, fnw, dest, ys, n_prompt_tok, sample_len, tm):
    n_tok = x1.shape[0]
    cond = functools.partial(_cond_row, tiles_prompt=n_prompt_tok // tm, tiles_per_seq=sample_len // tm)
    tok = lambda i: (i, 0)
    const = lambda i: (0, 0)
    tp = n_prompt_tok // tm
    return pl.pallas_call(
        functools.partial(_final_kernel, tiles_prompt=tp),
        grid=(n_tok // tm,),
        in_specs=[pl.BlockSpec((tm, D_MODEL), tok),
                  pl.BlockSpec((tm, D_PACK), tok),
                  pl.BlockSpec((tm, TOP_K), tok),
                  pl.BlockSpec((1, 1, 6 * D_MODEL), lambda i: (cond(i), 0, 0)),
                  pl.BlockSpec((D_MODEL, D_EXPERT), const),
                  pl.BlockSpec((D_MODEL, D_EXPERT), const),
                  pl.BlockSpec((D_EXPERT, D_MODEL), const),
                  pl.BlockSpec((1, D_MODEL), const),
                  pl.BlockSpec(memory_space=pl.ANY),
                  pl.BlockSpec(memory_space=pl.ANY)],
        out_specs=[pl.BlockSpec((tm, D_MODEL), lambda i: (jnp.minimum(i, tp - 1), 0)),
                   pl.BlockSpec((tm, D_MODEL), lambda i: (jnp.maximum(i - tp, 0), 0))],
        out_shape=[jax.ShapeDtypeStruct((n_prompt_tok, D_MODEL), F32),
                   jax.ShapeDtypeStruct((n_tok - n_prompt_tok, D_MODEL), F32)],
        scratch_shapes=[pltpu.SMEM((3, TOP_K, tm), jnp.int32),
                        pltpu.VMEM((2, TOP_K, tm, D_PACK), jnp.uint32),
                        pltpu.SemaphoreType.DMA((3,)),
                        pltpu.SemaphoreType.DMA((2,))],
        compiler_params=_cparams(("arbitrary",)),
        name="final",
    )(x1, h2, wts, mod3, sg, su, sd, fnw, dest, ys)


def kernel(x_prompt, x_sample, state_gla, state_gdn, c, c_ctx, w_ada, b_ada, norm1_w, w_in, conv_w, gla_lr_w, gla_lr_b, gdn_a_log, gdn_dt_bias, gla_norm_w, gdn_norm_w, w_o, norm2_w, router_w, router_bias, exp_w_gate, exp_w_up, exp_w_down, sh_w_gate, sh_w_up, sh_w_down, final_norm_w):
    bp, lp, d = x_prompt.shape
    bs, ls, _ = x_sample.shape
    assert d == D_MODEL and lp == UNIT and ls % UNIT == 0 and w_ada.shape[0] == 1
    n_prompt_tok = bp * lp
    n_tok = n_prompt_tok + bs * ls
    n_prompt_units = n_prompt_tok // UNIT
    units_per_seq = ls // UNIT
    grid_w = 64
    layer = 0
    tm_in = 512
    tm = 256
    assert n_prompt_tok % tm_in == 0 and ls % tm_in == 0 and n_tok % ROUTE_TILE == 0

    xp = x_prompt.reshape(n_prompt_tok, d)
    xs_in = x_sample.reshape(bs * ls, d)
    cond = jnp.concatenate([c_ctx[None, :], c, jnp.zeros((SUBLANES - 1 - bs, d), F32)], axis=0)
    mod3 = _ada(cond, w_ada[layer], b_ada[layer][None, :]).reshape(SUBLANES, 1, 6 * d)

    wi = w_in[layer]
    o_lr = 2 * QA + 2 * VA
    o_qkvb = o_lr + N_DIR * GLA_RANK
    o_zb = o_qkvb + 3 * QB
    o_ab = o_zb + VB
    w_main = jnp.concatenate([wi[:, 0:o_lr], wi[:, o_qkvb:o_ab]], axis=1).astype(BF16)
    w_small = jnp.concatenate([wi[:, o_lr:o_qkvb], wi[:, o_ab:], jnp.zeros((d, LANES - 48), F32)], axis=1)
    cw = jnp.concatenate([conv_w[layer], jnp.zeros((SUBLANES - CONV_K, 3 * QB), F32)], axis=0)
    proj = _inproj(xp, xs_in, mod3, norm1_w[layer][None, :], w_main, w_small, cw, ls, lp, grid_w, tm_in)

    wlr = jnp.zeros((N_DIR, LANES, QA), F32)
    for dd in range(N_DIR):
        wlr = wlr.at[dd, dd * GLA_RANK:(dd + 1) * GLA_RANK, :].set(gla_lr_w[layer, dd])
    blr = gla_lr_b[layer][:, None, :]
    init_gla = jnp.concatenate([jnp.zeros((1,) + state_gla.shape[2:], F32), state_gla[:, layer].astype(F32)], axis=0)
    o_af, o_ab_, s_gla = _gla(proj, wlr, blr, init_gla, n_prompt_units, units_per_seq)

    gcoef = jnp.zeros((1, LANES), F32).at[0, SM_AB:SM_AB + N_DIR * H_B].set(-jnp.exp(gdn_a_log[layer].reshape(-1)))
    gdt = jnp.zeros((1, LANES), F32).at[0, SM_AB:SM_AB + N_DIR * H_B].set(gdn_dt_bias[layer].reshape(-1))
    init_gdn = jnp.concatenate([jnp.zeros((1, N_DIR, H_B * DK_B, DV_B), F32),
                                state_gdn[:, layer].astype(F32).reshape(bs, N_DIR, H_B * DK_B, DV_B)], axis=0)
    o_df, o_db, s_gdn = _gdn(proj, gcoef, gdt, init_gdn, n_prompt_units, units_per_seq)

    nwa = jnp.tile(gla_norm_w[layer], H_A)[None, :]
    nwb = jnp.tile(gdn_norm_w[layer], H_B)[None, :]
    x1, h2, scores_t = _post(xp, xs_in, proj, o_af, o_ab_, o_df, o_db, mod3, w_o[layer].astype(BF16), nwa, nwb,
                             norm2_w[layer][None, :], router_w[layer].T, ls, tm)

    bias_col = jnp.broadcast_to(router_bias[layer].astype(F32)[:, None], (N_EXPERTS, LANES))
    e8, r8, w8, cnt = _route(scores_t, bias_col)
    start, item_blk, item_exp, item_nxt, item_lo, item_hi = _items(cnt[:, 0].astype(jnp.int32), n_tok * TOP_K)
    dest = _dest(start, e8, r8, tt=min(2048, n_tok))

    xs = _dispatch(h2, dest, tm)
    ys = _experts(item_blk, item_exp, item_nxt, item_lo, item_hi, xs, exp_w_gate[layer], exp_w_up[layer],
                  exp_w_down[layer])
    y_p, y_s = _final(x1, h2, w8.T, mod3, sh_w_gate[layer].astype(BF16), sh_w_up[layer].astype(BF16),
                      sh_w_down[layer].astype(BF16), final_norm_w[None, :], dest, ys, n_prompt_tok, ls, tm)

    y_prompt = y_p.reshape(bp, lp, d)
    y_sample = y_s.reshape(bs, ls, d)
    new_state_gla = s_gla.reshape(bp, 1, N_DIR, H_A, DK_A, DV_A).astype(x_prompt.dtype)
    new_state_gdn = s_gdn.reshape(bp, 1, N_DIR, H_B, DK_B, DV_B).astype(x_prompt.dtype)
    return (y_prompt, y_sample, new_state_gla, new_state_gdn)
```

```python
import functools

import jax
import jax.numpy as jnp
from jax import lax
from jax.experimental import pallas as pl
from jax.experimental.pallas import tpu as pltpu

F32 = jnp.float32
BF16 = jnp.bfloat16

D_MODEL = 1024
N_DIR = 2
H_A, DK_A, DV_A = 4, 64, 128
GLA_RANK = 16
GLA_NORMALIZER = 16.0
H_B, DK_B, DV_B = 4, 128, 128
CONV_K = 5
CHUNK = 64
QA, VA = H_A * DK_A, H_A * DV_A
QB, VB = H_B * DK_B, H_B * DV_B
N_EXPERTS = 256
TOP_K = 8
N_GROUP = 8
TOPK_GROUP = 4
GROUP_SIZE = N_EXPERTS // N_GROUP
D_EXPERT = 256
ROUTED_SCALE = 2.5
EPS = 1e-6
NEG_INF = float("-inf")

UNIT = 256
CHUNKS_PER_UNIT = UNIT // CHUNK
HC = H_B * CHUNK
GDN_GROUP = 2
HG = GDN_GROUP * CHUNK
INV_BLOCK = 8
D_PACK = D_MODEL // 2
ROUTE_TILE = 256
ROW_BLOCK = 512
LANES = 128
SUBLANES = 8
VMEM_LIMIT = 56 * 1024 * 1024

C_QKVA = 0
C_GA = 1024
C_QKVB = 1536
C_ZB = 3072
C_SMALL = 3584
D_PROJ = 3712
SM_AB = 32
SM_BETA = 40


def _cparams(sem):
    return pltpu.CompilerParams(dimension_semantics=sem, vmem_limit_bytes=VMEM_LIMIT)


def _split(a):
    hi = a.astype(BF16)
    lo = (a - hi.astype(F32)).astype(BF16)
    return hi, lo


def _dg(a, b, dims):
    return lax.dot_general(a, b, (dims, ((), ())), preferred_element_type=F32)


NN = ((1,), (0,))
NT = ((1,), (1,))
TN = ((0,), (0,))


def _mm(a, b, dims=NN):
    return _dg(a.astype(BF16), b.astype(BF16), dims)


def _mm3(a, b, dims=NN):
    ah, al = _split(a)
    bh, bl = _split(b)
    return _dg(ah, bh, dims) + (_dg(ah, bl, dims) + _dg(al, bh, dims))


def _mm_exact_lhs(a_bf16, b, dims=NN):
    bh, bl = _split(b)
    return _dg(a_bf16, bh, dims) + _dg(a_bf16, bl, dims)


def _silu(x):
    return x * (1.0 / (1.0 + jnp.exp(-x)))


def _sigmoid(x):
    return 1.0 / (1.0 + jnp.exp(-x))


def _softplus(x):
    return jnp.maximum(x, 0.0) + jnp.log1p(jnp.exp(-jnp.abs(x)))


def _log_sigmoid(x):
    return -_softplus(-x)


def _iota2(shape, dim):
    return lax.broadcasted_iota(jnp.int32, shape, dim)


def _rms(x):
    return x * lax.rsqrt(jnp.mean(x * x, axis=-1, keepdims=True) + EPS)


def _pack_rows(x):
    lo = lax.bitcast_convert_type(x[:, 0:D_PACK].astype(BF16).astype(F32), jnp.uint32)
    hi = lax.bitcast_convert_type(x[:, D_PACK:D_MODEL].astype(BF16).astype(F32), jnp.uint32)
    return (lo >> 16) | (hi & jnp.uint32(0xFFFF0000))


def _unpack_rows(p):
    lo = lax.bitcast_convert_type(p << 16, F32)
    hi = lax.bitcast_convert_type(p & jnp.uint32(0xFFFF0000), F32)
    return lo, hi


def _ada_kernel(c_ref, w_ref, b_ref, o_ref):
    o_ref[...] = _mm3(_silu(c_ref[...]), w_ref[...]) + b_ref[...]


def _ada(cond, w, b):
    n = w.shape[1]
    tn = 1536
    return pl.pallas_call(
        _ada_kernel,
        grid=(n // tn,),
        in_specs=[pl.BlockSpec((SUBLANES, D_MODEL), lambda i: (0, 0)),
                  pl.BlockSpec((D_MODEL, tn), lambda i: (0, i)),
                  pl.BlockSpec((1, tn), lambda i: (0, i))],
        out_specs=pl.BlockSpec((SUBLANES, tn), lambda i: (0, i)),
        out_shape=jax.ShapeDtypeStruct((SUBLANES, n), F32),
        compiler_params=_cparams(("parallel",)),
        name="ada",
    )(cond, w, b)


def _two_streams(n_prompt_tok, tm):
    tp = n_prompt_tok // tm
    return [pl.BlockSpec((tm, D_MODEL), lambda i: (jnp.minimum(i, tp - 1), 0)),
            pl.BlockSpec((tm, D_MODEL), lambda i: (jnp.maximum(i - tp, 0), 0))]


def _inproj_kernel(xp_ref, xs_ref, mod_ref, nw_ref, w_ref, ws_ref, cw_ref, o_ref, *, tiles_prompt, prompt_row,
                   sample_row):
    m = mod_ref[0]
    x = jnp.where(pl.program_id(0) < tiles_prompt, xp_ref[...], xs_ref[...])
    h = (_rms(x) * nw_ref[...]) * (1.0 + m[:, D_MODEL:2 * D_MODEL]) + m[:, 0:D_MODEL]
    hb = h.astype(BF16)
    for c0 in range(0, C_SMALL, 512):
        o_ref[:, c0:c0 + 512] = _dg(hb, w_ref[:, c0:c0 + 512], NN)
    o_ref[:, C_SMALL:D_PROJ] = _mm3(h, ws_ref[...])
    tm = xp_ref.shape[0]
    row_len = jnp.where(pl.program_id(0) < tiles_prompt, prompt_row, sample_row)
    pos = _iota2((tm, LANES), 0) & (row_len - 1)
    inside = [jnp.logical_and(pos + (jj - CONV_K // 2) >= 0, pos + (jj - CONV_K // 2) < row_len).astype(F32)
              for jj in range(CONV_K)]
    for part in range(3 * H_B):
        cs = slice(C_QKVB + part * LANES, C_QKVB + (part + 1) * LANES)
        x = o_ref[:, cs]
        acc = x * cw_ref[CONV_K // 2:CONV_K // 2 + 1, part * LANES:(part + 1) * LANES]
        for jj in range(CONV_K):
            off = jj - CONV_K // 2
            if off == 0:
                continue
            xs = pltpu.roll(x, (-off) % tm, 0) * inside[jj]
            acc = acc + xs * cw_ref[jj:jj + 1, part * LANES:(part + 1) * LANES]
        y = _silu(acc)
        if part < 2 * H_B:
            y = y * lax.rsqrt(jnp.sum(y * y, axis=-1, keepdims=True) + EPS)
            if part < H_B:
                y = y * (DK_B ** -0.5)
        o_ref[:, cs] = y


def _cond_row(i, tiles_prompt, tiles_per_seq):
    return jnp.where(i < tiles_prompt, 0, 1 + (i - tiles_prompt) // tiles_per_seq)


def _inproj(xp, xs, mod3, norm_w, w_main, w_small, conv_w, sample_len, prompt_row, sample_row, tm):
    n_prompt_tok = xp.shape[0]
    n_tok = n_prompt_tok + xs.shape[0]
    assert tm % prompt_row == 0 and tm % sample_row == 0
    cond = functools.partial(_cond_row, tiles_prompt=n_prompt_tok // tm, tiles_per_seq=sample_len // tm)
    kern = functools.partial(_inproj_kernel, tiles_prompt=n_prompt_tok // tm, prompt_row=prompt_row,
                             sample_row=sample_row)
    return pl.pallas_call(
        kern,
        grid=(n_tok // tm,),
        in_specs=_two_streams(n_prompt_tok, tm) + [
                  pl.BlockSpec((1, 1, 6 * D_MODEL), lambda i: (cond(i), 0, 0)),
                  pl.BlockSpec((1, D_MODEL), lambda i: (0, 0)),
                  pl.BlockSpec((D_MODEL, C_SMALL), lambda i: (0, 0)),
                  pl.BlockSpec((D_MODEL, LANES), lambda i: (0, 0)),
                  pl.BlockSpec((SUBLANES, 3 * QB), lambda i: (0, 0))],
        out_specs=pl.BlockSpec((tm, D_PROJ), lambda i: (i, 0)),
        out_shape=jax.ShapeDtypeStruct((n_tok, D_PROJ), F32),
        compiler_params=_cparams(("parallel",)),
        name="inproj",
    )(xp, xs, mod3, norm_w, w_main, w_small, conv_w)


def _unit_ids(j, n_prompt_units, units_per_seq):
    jj = j - n_prompt_units
    b = jj // units_per_seq
    r = jj % units_per_seq
    is_prompt = j < n_prompt_units
    uf = j
    ub = jnp.where(is_prompt, j, n_prompt_units + b * units_per_seq + (units_per_seq - 1 - r))
    init_row = jnp.where(is_prompt, 0, 1 + b)
    first = jnp.logical_or(is_prompt, r == 0)
    return uf, ub, init_row, first


def _tri(rev):
    t = _iota2((CHUNK, CHUNK), 0)
    s = _iota2((CHUNK, CHUNK), 1)
    return (t <= s) if rev else (t >= s)


def _mm_exact_lhs_tn(a, ones_bf16):
    ah, al = _split(a)
    return _dg(ah, ones_bf16, TN) + _dg(al, ones_bf16, TN)


def _gla_unit(chunks, order, states, rev):
    tri = _tri(rev).astype(BF16)
    mid, last = (CHUNK // 2 - 1, 0) if rev else (CHUNK // 2, CHUNK - 1)
    scale = DK_A ** -0.5
    ones = jnp.ones((CHUNK, LANES), BF16)
    lane = _iota2((CHUNK, LANES), 1)
    row = _iota2((CHUNK, LANES), 0)
    s_in = lane % DK_A
    causal = (row <= s_in) if rev else (row >= s_in)
    zeros_v = jnp.zeros((CHUNK, DV_A), F32)
    pairs = range(H_A // 2)
    att, qs_l, u_l, dec_l = {}, {}, {}, {}
    for c, (q, k, v, la) in enumerate(chunks):
        b = _mm_exact_lhs(tri, la)
        bref = b[mid:mid + 1, :]
        blast = b[last:last + 1, :]
        qg = q * jnp.exp(b - bref) * scale
        kg = k * jnp.exp(bref - b)
        kd = k * jnp.exp(blast - b)
        qs_l[c] = q * jnp.exp(b) * scale
        for p in pairs:
            ls = slice(p * LANES, (p + 1) * LANES)
            kg_p = kg[:, ls]
            rhs_att = jnp.concatenate([jnp.where(lane < DK_A, kg_p, 0.0), jnp.where(lane >= DK_A, kg_p, 0.0)], axis=0)
            att[c, p] = jnp.where(causal, _mm(qg[:, ls], rhs_att, NT), 0.0)
            u_l[c, p] = _mm(kd[:, ls], v[:, 2 * p * DV_A:(2 * p + 2) * DV_A], TN)
            dec_l[c, p] = jnp.exp(_mm_exact_lhs_tn(la[:, ls], ones))
    yield
    start = {}
    for c in order:
        start[c] = list(states)
        nxt = []
        for p in pairs:
            u, dec = u_l[c, p], dec_l[c, p]
            nxt.append(dec[0:DK_A] * states[2 * p] + u[0:DK_A, 0:DV_A])
            nxt.append(dec[DK_A:2 * DK_A] * states[2 * p + 1] + u[DK_A:2 * DK_A, DV_A:2 * DV_A])
        states = nxt
    yield
    outs = []
    for c, (q, k, v, la) in enumerate(chunks):
        o = []
        for p in pairs:
            ls = slice(p * LANES, (p + 1) * LANES)
            vs0 = v[:, (2 * p) * DV_A:(2 * p + 1) * DV_A]
            vs1 = v[:, (2 * p + 1) * DV_A:(2 * p + 2) * DV_A]
            s0, s1 = start[c][2 * p], start[c][2 * p + 1]
            rhs_o = jnp.concatenate([jnp.concatenate([vs0, zeros_v], axis=1),
                                     jnp.concatenate([zeros_v, vs1], axis=1),
                                     jnp.concatenate([s0, zeros_v], axis=1),
                                     jnp.concatenate([zeros_v, s1], axis=1)], axis=0)
            lhs_o = jnp.concatenate([att[c, p], qs_l[c][:, ls]], axis=1)
            o.append(_mm(lhs_o, rhs_o))
        outs.append(jnp.concatenate(o, axis=1))
    yield outs, states


def _gla_kernel(qf_ref, qb_ref, sf_ref, sb_ref, wlr_ref, blr_ref, init_ref, of_ref, ob_ref, so_ref, s_ref,
                *, n_prompt_units, units_per_seq):
    j = pl.program_id(0)
    _, _, _, first = _unit_ids(j, n_prompt_units, units_per_seq)

    @pl.when(first)
    def _():
        s_ref[...] = init_ref[0]

    stages = []
    for d in range(N_DIR):
        x_ref, sm_ref = (qf_ref, sf_ref) if d == 0 else (qb_ref, sb_ref)
        la_unit = _log_sigmoid(_mm3(sm_ref[...], wlr_ref[d]) + blr_ref[d]) * (1.0 / GLA_NORMALIZER)
        states = [s_ref[d, h] for h in range(H_A)]
        order = range(CHUNKS_PER_UNIT - 1, -1, -1) if d == 1 else range(CHUNKS_PER_UNIT)
        chunks = []
        for c in range(CHUNKS_PER_UNIT):
            rows = slice(c * CHUNK, (c + 1) * CHUNK)
            chunks.append((x_ref[rows, 0:QA], x_ref[rows, QA:2 * QA], x_ref[rows, 2 * QA:2 * QA + VA], la_unit[rows]))
        stages.append(_gla_unit(chunks, order, states, rev=(d == 1)))
    for _ in range(2):
        for g in stages:
            next(g)
    for d, (g, o_ref) in enumerate(zip(stages, (of_ref, ob_ref))):
        outs, states = next(g)
        for c in range(CHUNKS_PER_UNIT):
            o_ref[c * CHUNK:(c + 1) * CHUNK, :] = outs[c]
        for h in range(H_A):
            s_ref[d, h] = states[h]

    @pl.when(j < n_prompt_units)
    def _():
        so_ref[0] = s_ref[...]


def _gla(proj, wlr, blr, init, n_prompt_units, units_per_seq):
    n_tok = proj.shape[0]
    n_units = n_tok // UNIT
    ids = functools.partial(_unit_ids, n_prompt_units=n_prompt_units, units_per_seq=units_per_seq)
    small_blk = C_SMALL // LANES
    st_blk = (1, N_DIR, H_A, DK_A, DV_A)
    kern = functools.partial(_gla_kernel, n_prompt_units=n_prompt_units, units_per_seq=units_per_seq)
    return pl.pallas_call(
        kern,
        grid=(n_units,),
        in_specs=[pl.BlockSpec((UNIT, 2 * QA + VA), lambda j: (ids(j)[0], C_QKVA // (2 * QA + VA))),
                  pl.BlockSpec((UNIT, 2 * QA + VA), lambda j: (ids(j)[1], C_QKVA // (2 * QA + VA))),
                  pl.BlockSpec((UNIT, LANES), lambda j: (ids(j)[0], small_blk)),
                  pl.BlockSpec((UNIT, LANES), lambda j: (ids(j)[1], small_blk)),
                  pl.BlockSpec((N_DIR, LANES, QA), lambda j: (0, 0, 0)),
                  pl.BlockSpec((N_DIR, 1, QA), lambda j: (0, 0, 0)),
                  pl.BlockSpec(st_blk, lambda j: (ids(j)[2], 0, 0, 0, 0))],
        out_specs=[pl.BlockSpec((UNIT, VA), lambda j: (ids(j)[0], 0)),
                   pl.BlockSpec((UNIT, VA), lambda j: (ids(j)[1], 0)),
                   pl.BlockSpec(st_blk, lambda j: (jnp.minimum(j, n_prompt_units - 1), 0, 0, 0, 0))],
        out_shape=[jax.ShapeDtypeStruct((n_tok, VA), F32),
                   jax.ShapeDtypeStruct((n_tok, VA), F32),
                   jax.ShapeDtypeStruct((n_prompt_units, N_DIR, H_A, DK_A, DV_A), F32)],
        scratch_shapes=[pltpu.VMEM((N_DIR, H_A, DK_A, DV_A), F32)],
        compiler_params=_cparams(("arbitrary",)),
        name="gla",
    )(proj, proj, proj, proj, wlr, blr, init)


def _stack_masks(rev):
    r = _iota2((HG, HG), 0)
    c = _iota2((HG, HG), 1)
    same = (r // CHUNK) == (c // CHUNK)
    tr, tc = r % CHUNK, c % CHUNK
    incl = jnp.logical_and(same, (tr <= tc) if rev else (tr >= tc))
    strict = jnp.logical_and(same, (tr < tc) if rev else (tr > tc))
    return incl, strict


def _spread(x):
    z = jnp.zeros((CHUNK, LANES), x.dtype)
    rows = []
    for h in range(H_B):
        xh = x[h * CHUNK:(h + 1) * CHUNK]
        rows.append(jnp.concatenate([xh if g == h else z for g in range(H_B)], axis=1))
    return jnp.concatenate(rows, axis=0)


def _gdn_prepare(chunks):
    masks = {rev: _stack_masks(rev) for rev in {ch[6] for ch in chunks}}
    r = _iota2((HG, HG), 0)
    c = _iota2((HG, HG), 1)
    a_l, att_l, kb_l = [], [], []
    for q, k, v, gcb, beta, glast, rev in chunks:
        incl, strict = masks[rev]
        grow = gcb.T[0:1, :]
        diff = gcb[:, 0:1] - grow
        decay = jnp.where(incl, jnp.exp(jnp.where(incl, diff, 0.0)), 0.0)
        kb = k * beta
        m1 = _mm(jnp.concatenate([kb, q], axis=0), k, NT)
        a_l.append(jnp.where(strict, m1[0:HG] * decay, 0.0))
        att_l.append(m1[HG:2 * HG] * decay)
        kb_l.append(kb)
    diag = (r // INV_BLOCK) == (c // INV_BLOCK)
    pw_l = [jnp.where(diag, a, 0.0) for a in a_l]
    qm_l = [-p for p in pw_l]
    n = 2
    while n < INV_BLOCK:
        pw_l = [_mm(p, p) for p in pw_l]
        qm_l = [qm + p + _mm(qm, p) for qm, p in zip(qm_l, pw_l)]
        n *= 2
    b = INV_BLOCK
    while b < CHUNK:
        off = jnp.logical_and((r // (2 * b)) == (c // (2 * b)), (r // b) != (c // b))
        al_l = [jnp.where(off, a, 0.0) for a in a_l]
        t1_l = [al + _mm(qm, al) for qm, al in zip(qm_l, al_l)]
        qm_l = [qm - (t1 + _mm(t1, qm)) for qm, t1 in zip(qm_l, t1_l)]
        b *= 2
    out = []
    for (q, k, v, gcb, beta, glast, _), qm, kb, att in zip(chunks, qm_l, kb_l, att_l):
        egc = jnp.exp(gcb)
        rhs = jnp.concatenate([v * beta, kb * egc], axis=1)
        sol = rhs + _mm(qm, rhs)
        out.append((sol[:, 0:DV_B], sol[:, DV_B:2 * DV_B], q * egc, k * jnp.exp(glast - gcb), att))
    return out


def _gdn_scan_step(value, k_cum, q_dec, k_dec, att, gl_rows, s):
    kq = _mm(jnp.concatenate([_spread(k_cum), _spread(q_dec)], axis=0), s)
    v_new = value - kq[0:HC]
    o = kq[HC:2 * HC] + _mm(att, v_new)
    s_new = s * gl_rows + _mm(_spread(k_dec), v_new, TN)
    return o, s_new


def _gdn_kernel(xf_ref, xb_ref, sf_ref, sb_ref, gco_ref, gdt_ref, init_ref, of_ref, ob_ref, so_ref, s_ref,
                *, n_prompt_units, units_per_seq):
    j = pl.program_id(0)
    _, _, _, first = _unit_ids(j, n_prompt_units, units_per_seq)

    @pl.when(first)
    def _():
        s_ref[...] = init_ref[0]

    chunks, gl_rows = [], []
    for d in range(N_DIR):
        x_ref, sm_ref = (xf_ref, sf_ref) if d == 0 else (xb_ref, sb_ref)
        rev = d == 1
        sm = sm_ref[...]
        g_all = gco_ref[...] * _softplus(sm + gdt_ref[...])
        beta_all = _sigmoid(sm)
        tri = _tri(rev).astype(BF16)
        last = 0 if rev else CHUNK - 1
        for c in range(CHUNKS_PER_UNIT):
            rows = slice(c * CHUNK, (c + 1) * CHUNK)
            gc_all = _mm_exact_lhs(tri, g_all[rows])
            gcb, beta, glast = [], [], []
            for h in range(H_B):
                col = SM_AB + d * H_B + h
                colb = SM_BETA + d * H_B + h
                gh = jnp.broadcast_to(gc_all[:, col:col + 1], (CHUNK, LANES))
                gcb.append(gh)
                glast.append(jnp.broadcast_to(gh[last:last + 1, :], (CHUNK, LANES)))
                beta.append(jnp.broadcast_to(beta_all[rows, colb:colb + 1], (CHUNK, LANES)))
            for g0 in range(0, H_B, GDN_GROUP):
                hs = range(g0, g0 + GDN_GROUP)
                stack = lambda base: jnp.concatenate(
                    [x_ref[rows, base + h * LANES:base + (h + 1) * LANES] for h in hs], axis=0)
                cat = lambda parts: jnp.concatenate([parts[h] for h in hs], axis=0)
                chunks.append((stack(0), stack(QB), stack(2 * QB), cat(gcb), cat(beta), cat(glast), rev))
            gl_rows.append(jnp.concatenate([jnp.broadcast_to(jnp.exp(g[0:1, :]), (DK_B, DV_B)) for g in glast],
                                           axis=0))
    n_grp = H_B // GDN_GROUP
    zero_att = jnp.zeros((HG, HG), F32)
    prepared = []
    grouped = _gdn_prepare(chunks)
    for i in range(0, len(grouped), n_grp):
        parts = grouped[i:i + n_grp]
        rows_cat = [jnp.concatenate([p[f] for p in parts], axis=0) for f in range(4)]
        att = jnp.concatenate([jnp.concatenate([parts[g][4] if gg == g else zero_att for gg in range(n_grp)], axis=1)
                               for g in range(n_grp)], axis=0)
        prepared.append(tuple(rows_cat) + (att,))
    s = [s_ref[d] for d in range(N_DIR)]
    for step in range(CHUNKS_PER_UNIT):
        for d, o_ref in ((0, of_ref), (1, ob_ref)):
            c = step if d == 0 else CHUNKS_PER_UNIT - 1 - step
            o, s[d] = _gdn_scan_step(*prepared[d * CHUNKS_PER_UNIT + c], gl_rows[d * CHUNKS_PER_UNIT + c], s[d])
            for h in range(H_B):
                o_ref[c * CHUNK:(c + 1) * CHUNK, h * DV_B:(h + 1) * DV_B] = o[h * CHUNK:(h + 1) * CHUNK]
    for d in range(N_DIR):
        s_ref[d] = s[d]

    @pl.when(j < n_prompt_units)
    def _():
        so_ref[0] = s_ref[...]


def _gdn(proj, gcoef, gdt, init, n_prompt_units, units_per_seq):
    n_tok = proj.shape[0]
    n_units = n_tok // UNIT
    ids = functools.partial(_unit_ids, n_prompt_units=n_prompt_units, units_per_seq=units_per_seq)
    small_blk = C_SMALL // LANES
    qkv_blk = C_QKVB // (3 * QB)
    st_blk = (1, N_DIR, H_B * DK_B, DV_B)
    kern = functools.partial(_gdn_kernel, n_prompt_units=n_prompt_units, units_per_seq=units_per_seq)
    return pl.pallas_call(
        kern,
        grid=(n_units,),
        in_specs=[pl.BlockSpec((UNIT, 3 * QB), lambda j: (ids(j)[0], qkv_blk)),
                  pl.BlockSpec((UNIT, 3 * QB), lambda j: (ids(j)[1], qkv_blk)),
                  pl.BlockSpec((UNIT, LANES), lambda j: (ids(j)[0], small_blk)),
                  pl.BlockSpec((UNIT, LANES), lambda j: (ids(j)[1], small_blk)),
                  pl.BlockSpec((1, LANES), lambda j: (0, 0)),
                  pl.BlockSpec((1, LANES), lambda j: (0, 0)),
                  pl.BlockSpec(st_blk, lambda j: (ids(j)[2], 0, 0, 0))],
        out_specs=[pl.BlockSpec((UNIT, VB), lambda j: (ids(j)[0], 0)),
                   pl.BlockSpec((UNIT, VB), lambda j: (ids(j)[1], 0)),
                   pl.BlockSpec(st_blk, lambda j: (jnp.minimum(j, n_prompt_units - 1), 0, 0, 0))],
        out_shape=[jax.ShapeDtypeStruct((n_tok, VB), F32),
                   jax.ShapeDtypeStruct((n_tok, VB), F32),
                   jax.ShapeDtypeStruct((n_prompt_units, N_DIR, H_B * DK_B, DV_B), F32)],
        scratch_shapes=[pltpu.VMEM((N_DIR, H_B * DK_B, DV_B), F32)],
        compiler_params=_cparams(("arbitrary",)),
        name="gdn",
    )(proj, proj, proj, proj, gcoef, gdt, init)


def _head_rms(o, w):
    parts = []
    for h in range(o.shape[1] // LANES):
        parts.append(_rms(o[:, h * LANES:(h + 1) * LANES]))
    return jnp.concatenate(parts, axis=1) * w


def _post_kernel(xp_ref, xs_ref, ga_ref, zb_ref, af_ref, ab_ref, df_ref, db_ref, mod_ref, wo_ref, nwa_ref, nwb_ref,
                 n2_ref, rw_ref, x1_ref, h2_ref, sc_ref, *, tiles_prompt):
    m = mod_ref[0]
    x = jnp.where(pl.program_id(0) < tiles_prompt, xp_ref[...], xs_ref[...])
    gla = _head_rms(af_ref[...] + ab_ref[...], nwa_ref[...]) * _silu(ga_ref[...])
    gdn = _head_rms(df_ref[...] + db_ref[...], nwb_ref[...]) * _silu(zb_ref[...])
    y = _dg(gla.astype(BF16), wo_ref[0:VA, :], NN) + _dg(gdn.astype(BF16), wo_ref[VA:VA + VB, :], NN)
    x1 = x + m[:, 2 * D_MODEL:3 * D_MODEL] * y
    x1_ref[...] = x1
    h2 = (_rms(x1) * n2_ref[...]) * (1.0 + m[:, 4 * D_MODEL:5 * D_MODEL]) + m[:, 3 * D_MODEL:4 * D_MODEL]
    h2_ref[...] = _pack_rows(h2)
    sc_ref[...] = _sigmoid(_mm3(rw_ref[...], h2, NT))


def _post(xp, xs, proj, o_af, o_ab, o_df, o_db, mod3, w_o, nwa, nwb, n2w, router_wt, sample_len, tm):
    n_prompt_tok = xp.shape[0]
    n_tok = n_prompt_tok + xs.shape[0]
    cond = functools.partial(_cond_row, tiles_prompt=n_prompt_tok // tm, tiles_per_seq=sample_len // tm)
    tok = lambda i: (i, 0)
    const = lambda i: (0, 0)
    return pl.pallas_call(
        functools.partial(_post_kernel, tiles_prompt=n_prompt_tok // tm),
        grid=(n_tok // tm,),
        in_specs=_two_streams(n_prompt_tok, tm) + [
                  pl.BlockSpec((tm, VA), lambda i: (i, C_GA // VA)),
                  pl.BlockSpec((tm, VB), lambda i: (i, C_ZB // VB)),
                  pl.BlockSpec((tm, VA), tok), pl.BlockSpec((tm, VA), tok),
                  pl.BlockSpec((tm, VB), tok), pl.BlockSpec((tm, VB), tok),
                  pl.BlockSpec((1, 1, 6 * D_MODEL), lambda i: (cond(i), 0, 0)),
                  pl.BlockSpec((VA + VB, D_MODEL), const),
                  pl.BlockSpec((1, VA), const), pl.BlockSpec((1, VB), const), pl.BlockSpec((1, D_MODEL), const),
                  pl.BlockSpec((N_EXPERTS, D_MODEL), const)],
        out_specs=[pl.BlockSpec((tm, D_MODEL), tok),
                   pl.BlockSpec((tm, D_PACK), tok),
                   pl.BlockSpec((N_EXPERTS, tm), lambda i: (0, i))],
        out_shape=[jax.ShapeDtypeStruct((n_tok, D_MODEL), F32),
                   jax.ShapeDtypeStruct((n_tok, D_PACK), jnp.uint32),
                   jax.ShapeDtypeStruct((N_EXPERTS, n_tok), F32)],
        compiler_params=_cparams(("parallel",)),
        name="post",
    )(xp, xs, proj, proj, o_af, o_ab, o_df, o_db, mod3, w_o, nwa, nwb, n2w, router_wt)


def _route_kernel(sc_ref, bias_ref, e_ref, r_ref, w_ref, cnt_ref, carry_ref):
    i = pl.program_id(0)
    t = sc_ref.shape[1]

    @pl.when(i == 0)
    def _():
        carry_ref[...] = jnp.zeros(carry_ref.shape, F32)

    s = sc_ref[...]
    biased = s + bias_ref[:, 0:1]
    gs = []
    for g in range(N_GROUP):
        blk = biased[g * GROUP_SIZE:(g + 1) * GROUP_SIZE]
        m1 = jnp.max(blk, axis=0, keepdims=True)
        n1 = jnp.sum((blk == m1).astype(F32), axis=0, keepdims=True)
        m2 = jnp.max(jnp.where(blk < m1, blk, NEG_INF), axis=0, keepdims=True)
        gs.append(m1 + jnp.where(n1 >= 2.0, m1, m2))
    gsc = jnp.concatenate(gs, axis=0)
    gid = _iota2((N_GROUP, t), 0)
    beaten = jnp.zeros((N_GROUP, t), F32)
    for g in range(N_GROUP):
        other = gsc[g:g + 1, :]
        wins = jnp.logical_or(other > gsc, jnp.logical_and(other == gsc, g < gid))
        beaten = beaten + wins.astype(F32)
    masked = jnp.concatenate(
        [jnp.where(beaten[g:g + 1, :] < float(TOPK_GROUP), biased[g * GROUP_SIZE:(g + 1) * GROUP_SIZE], NEG_INF)
         for g in range(N_GROUP)], axis=0)
    eid = _iota2((N_EXPERTS, t), 0).astype(F32)
    sel = jnp.zeros((N_EXPERTS, t), F32)
    picks, scores = [], []
    for _ in range(TOP_K):
        m = jnp.max(masked, axis=0, keepdims=True)
        first = jnp.min(jnp.where(masked == m, eid, float(N_EXPERTS)), axis=0, keepdims=True)
        hit = eid == first
        scores.append(jnp.sum(jnp.where(hit, s, 0.0), axis=0, keepdims=True))
        masked = jnp.where(hit, NEG_INF, masked)
        sel = sel + hit.astype(F32)
        picks.append(first)
    upper = (_iota2((t, t), 0) < _iota2((t, t), 1)).astype(BF16)
    carry = carry_ref[...]
    prefix = _dg(sel.astype(BF16), upper, NN) + jnp.concatenate([carry] * (t // LANES), axis=1)
    ranks = [jnp.sum(jnp.where(eid == p, prefix, 0.0), axis=0, keepdims=True) for p in picks]
    carry = carry + _dg(sel.astype(BF16), jnp.ones((t, LANES), BF16), NN)
    carry_ref[...] = carry
    cnt_ref[...] = carry
    sc8 = jnp.concatenate(scores, axis=0)
    e_ref[...] = jnp.concatenate(picks, axis=0).astype(jnp.int32)
    r_ref[...] = jnp.concatenate(ranks, axis=0).astype(jnp.int32)
    w_ref[...] = sc8 / jnp.sum(sc8, axis=0, keepdims=True) * ROUTED_SCALE


def _route(scores_t, bias_col):
    n_tok = scores_t.shape[1]
    t = ROUTE_TILE
    slot = lambda i: (0, i)
    return pl.pallas_call(
        _route_kernel,
        grid=(n_tok // t,),
        in_specs=[pl.BlockSpec((N_EXPERTS, t), slot),
                  pl.BlockSpec((N_EXPERTS, LANES), lambda i: (0, 0))],
        out_specs=[pl.BlockSpec((TOP_K, t), slot), pl.BlockSpec((TOP_K, t), slot), pl.BlockSpec((TOP_K, t), slot),
                   pl.BlockSpec((N_EXPERTS, LANES), lambda i: (0, 0))],
        out_shape=[jax.ShapeDtypeStruct((TOP_K, n_tok), jnp.int32),
                   jax.ShapeDtypeStruct((TOP_K, n_tok), jnp.int32),
                   jax.ShapeDtypeStruct((TOP_K, n_tok), F32),
                   jax.ShapeDtypeStruct((N_EXPERTS, LANES), F32)],
        scratch_shapes=[pltpu.VMEM((N_EXPERTS, LANES), F32)],
        compiler_params=_cparams(("arbitrary",)),
        name="route",
    )(scores_t, bias_col)


def _dest_kernel(start_ref, e_ref, r_ref, d_ref):
    e = e_ref[...]

    def body(x, acc):
        return jnp.where(e == x, start_ref[x], acc)

    d_ref[...] = r_ref[...] + lax.fori_loop(0, N_EXPERTS, body, jnp.zeros(e.shape, jnp.int32))


def _dest(start, e8, r8, tt):
    n_tok = e8.shape[1]
    slot = lambda i, st: (0, i)
    return pl.pallas_call(
        _dest_kernel,
        grid_spec=pltpu.PrefetchScalarGridSpec(
            num_scalar_prefetch=1, grid=(n_tok // tt,),
            in_specs=[pl.BlockSpec((TOP_K, tt), slot), pl.BlockSpec((TOP_K, tt), slot)],
            out_specs=pl.BlockSpec((TOP_K, tt), slot)),
        out_shape=jax.ShapeDtypeStruct((TOP_K, n_tok), jnp.int32),
        compiler_params=_cparams(("parallel",)),
        name="dest",
    )(start, e8, r8)


def _items(counts, n_rows):
    n_blocks = n_rows // ROW_BLOCK
    max_items = n_blocks + N_EXPERTS - 1
    end = jnp.cumsum(counts)
    start = end - counts
    first_blk = start // ROW_BLOCK
    n_it = jnp.where(counts > 0, (end - 1) // ROW_BLOCK - first_blk + 1, 0)
    it_end = jnp.cumsum(n_it)
    it_start = it_end - n_it
    i = jnp.arange(max_items, dtype=jnp.int32)
    valid = i < it_end[-1]
    ex = jnp.minimum(jnp.sum((it_end[None, :] <= i[:, None]).astype(jnp.int32), axis=1), N_EXPERTS - 1)
    onehot = ex[:, None] == jnp.arange(N_EXPERTS, dtype=jnp.int32)[None, :]
    pick = lambda tab: jnp.sum(jnp.where(onehot, tab[None, :], 0), axis=1)
    blk = pick(first_blk) + (i - pick(it_start))
    lo = jnp.maximum(pick(start), blk * ROW_BLOCK) - blk * ROW_BLOCK
    hi = jnp.minimum(pick(end), (blk + 1) * ROW_BLOCK) - blk * ROW_BLOCK
    blk = jnp.where(valid, blk, n_blocks - 1).astype(jnp.int32)
    lo = jnp.where(valid, lo, 0).astype(jnp.int32)
    hi = jnp.where(valid, hi, 0).astype(jnp.int32)
    eids = jnp.arange(N_EXPERTS, dtype=jnp.int32)
    later = jnp.logical_and(eids[None, :] > eids[:, None], (counts > 0)[None, :])
    nxt_e = jnp.min(jnp.where(later, eids[None, :], N_EXPERTS), axis=1)
    nxt = pick(jnp.where(nxt_e < N_EXPERTS, nxt_e, -1)).astype(jnp.int32)
    return start.astype(jnp.int32), blk, ex.astype(jnp.int32), nxt, lo, hi


def _dispatch_kernel(h2_ref, dest_hbm, xs_hbm, dsm, sem_d, sem_s):
    i = pl.program_id(0)
    n = pl.num_programs(0)
    tm = h2_ref.shape[0]

    def dest_copy(step, slot):
        return pltpu.make_async_copy(dest_hbm.at[:, pl.ds(step * tm, tm)], dsm.at[slot], sem_d.at[slot])

    def row_copy(t, dst):
        return pltpu.make_async_copy(h2_ref.at[pl.ds(t, 1), :], xs_hbm.at[pl.ds(dst, 1), :], sem_s.at[0])

    @pl.when(i == 0)
    def _():
        dest_copy(0, 0).start()

    slot = i % 2
    dest_copy(i, slot).wait()

    @pl.when(i + 1 < n)
    def _():
        dest_copy(i + 1, 1 - slot).start()

    def issue(t, carry):
        for k in range(TOP_K):
            row_copy(t, dsm[slot, k, t]).start()
        return carry
    lax.fori_loop(0, tm, issue, 0, unroll=2)

    def drain(t, carry):
        for k in range(TOP_K):
            row_copy(t, 0).wait()
        return carry
    lax.fori_loop(0, tm, drain, 0)


def _dispatch(h2, dest, tm):
    n_tok = h2.shape[0]
    return pl.pallas_call(
        _dispatch_kernel,
        grid=(n_tok // tm,),
        in_specs=[pl.BlockSpec((tm, D_PACK), lambda i: (i, 0)),
                  pl.BlockSpec(memory_space=pl.ANY)],
        out_specs=pl.BlockSpec(memory_space=pl.ANY),
        out_shape=jax.ShapeDtypeStruct((n_tok * TOP_K, D_PACK), jnp.uint32),
        scratch_shapes=[pltpu.SMEM((2, TOP_K, tm), jnp.int32),
                        pltpu.SemaphoreType.DMA((2,)),
                        pltpu.SemaphoreType.DMA((1,))],
        compiler_params=_cparams(("arbitrary",)),
        name="dispatch",
    )(h2, dest)


def _experts_kernel(blk_ref, exp_ref, nxt_ref, lo_ref, hi_ref, x_ref, wg_hbm, wu_hbm, wd_hbm, y_ref,
                    wgf, wuf, wdf, wgb, wub, wdb, nchg, sem_w):
    i = pl.program_id(0)
    lo, hi = lo_ref[i], hi_ref[i]

    def weight_copies(e, slot):
        return (pltpu.make_async_copy(wg_hbm.at[e], wgf.at[slot], sem_w.at[slot]),
                pltpu.make_async_copy(wu_hbm.at[e], wuf.at[slot], sem_w.at[slot]),
                pltpu.make_async_copy(wd_hbm.at[e], wdf.at[slot], sem_w.at[slot]))

    @pl.when(i == 0)
    def _():
        nchg[0] = 0
        for cp in weight_copies(exp_ref[0], 0):
            cp.start()

    @pl.when(hi > lo)
    def _():
        @pl.when(jnp.logical_or(i == 0, exp_ref[i] != exp_ref[jnp.maximum(i - 1, 0)]))
        def _():
            slot = nchg[0] % 2
            for cp in weight_copies(exp_ref[i], slot):
                cp.wait()

            @pl.when(nxt_ref[i] >= 0)
            def _():
                for cp in weight_copies(nxt_ref[i], 1 - slot):
                    cp.start()

            wgb[...] = wgf[slot].astype(BF16)
            wub[...] = wuf[slot].astype(BF16)
            wdb[...] = wdf[slot].astype(BF16)
            nchg[0] = nchg[0] + 1

        x_lo, x_hi = _unpack_rows(x_ref[...])
        x = jnp.concatenate([x_lo.astype(BF16), x_hi.astype(BF16)], axis=1)
        g = _dg(x, wgb[...], NN)
        u = _dg(x, wub[...], NN)
        hmid = (_silu(g) * u).astype(BF16)
        y = _pack_rows(_dg(hmid, wdb[...], NN))
        row = _iota2((ROW_BLOCK, D_PACK), 0)
        mine = jnp.logical_and(row >= lo, row < hi)

        @pl.when(lo == 0)
        def _():
            y_ref[...] = jnp.where(mine, y, jnp.uint32(0))

        @pl.when(lo > 0)
        def _():
            y_ref[...] = jnp.where(mine, y, y_ref[...])


def _experts(item_blk, item_exp, item_nxt, item_lo, item_hi, xs, w_gate, w_up, w_down):
    n_items = item_blk.shape[0]
    xmap = lambda i, blk, ex, nxt, lo, hi: (blk[i], 0)
    return pl.pallas_call(
        _experts_kernel,
        grid_spec=pltpu.PrefetchScalarGridSpec(
            num_scalar_prefetch=5, grid=(n_items,),
            in_specs=[pl.BlockSpec((ROW_BLOCK, D_PACK), xmap),
                      pl.BlockSpec(memory_space=pl.ANY),
                      pl.BlockSpec(memory_space=pl.ANY),
                      pl.BlockSpec(memory_space=pl.ANY)],
            out_specs=pl.BlockSpec((ROW_BLOCK, D_PACK), xmap),
            scratch_shapes=[pltpu.VMEM((2, D_MODEL, D_EXPERT), F32), pltpu.VMEM((2, D_MODEL, D_EXPERT), F32),
                            pltpu.VMEM((2, D_EXPERT, D_MODEL), F32),
                            pltpu.VMEM((D_MODEL, D_EXPERT), BF16), pltpu.VMEM((D_MODEL, D_EXPERT), BF16),
                            pltpu.VMEM((D_EXPERT, D_MODEL), BF16),
                            pltpu.SMEM((1,), jnp.int32),
                            pltpu.SemaphoreType.DMA((2,))]),
        out_shape=jax.ShapeDtypeStruct(xs.shape, jnp.uint32),
        compiler_params=_cparams(("arbitrary",)),
        name="experts",
    )(item_blk, item_exp, item_nxt, item_lo, item_hi, xs, w_gate, w_up, w_down)


def _final_kernel(x1_ref, h2_ref, w_ref, mod_ref, sg_ref, su_ref, sd_ref, fn_ref, dest_hbm, ys_hbm, op_ref, os_ref,
                  dsm, gbuf, sem_d, sem_g, *, tiles_prompt):
    i = pl.program_id(0)
    n = pl.num_programs(0)
    tm = x1_ref.shape[0]

    def dest_copy(step, slot):
        return pltpu.make_async_copy(dest_hbm.at[:, pl.ds(step * tm, tm)], dsm.at[slot], sem_d.at[slot])

    def row_copy(src, slot, k, t):
        return pltpu.make_async_copy(ys_hbm.at[pl.ds(src, 1), :], gbuf.at[slot, k, pl.ds(t, 1), :], sem_g.at[slot])

    def issue_gathers(slot3, slot2):
        def body(t, carry):
            for k in range(TOP_K):
                row_copy(dsm[slot3, k, t], slot2, k, t).start()
            return carry
        lax.fori_loop(0, tm, body, 0, unroll=2)

    @pl.when(i == 0)
    def _():
        dest_copy(0, 0).start()
        dest_copy(0, 0).wait()
        issue_gathers(0, 0)

        @pl.when(n > 1)
        def _():
            dest_copy(1, 1).start()

    @pl.when(i + 1 < n)
    def _():
        dest_copy(i + 1, (i + 1) % 3).wait()
        issue_gathers((i + 1) % 3, (i + 1) % 2)

        @pl.when(i + 2 < n)
        def _():
            dest_copy(i + 2, (i + 2) % 3).start()

    slot = i % 2

    def drain(t, carry):
        for k in range(TOP_K):
            row_copy(0, slot, k, t).wait()
        return carry
    lax.fori_loop(0, tm, drain, 0)

    m = mod_ref[0]
    w = w_ref[...]
    r_lo = r_hi = None
    for kk in range(TOP_K):
        y_lo, y_hi = _unpack_rows(gbuf[slot, kk])
        wk = w[:, kk:kk + 1]
        r_lo = y_lo * wk if r_lo is None else r_lo + y_lo * wk
        r_hi = y_hi * wk if r_hi is None else r_hi + y_hi * wk
    routed = jnp.concatenate([r_lo, r_hi], axis=1)
    h_lo, h_hi = _unpack_rows(h2_ref[...])
    hb = jnp.concatenate([h_lo.astype(BF16), h_hi.astype(BF16)], axis=1)
    hm = (_silu(_dg(hb, sg_ref[...], NN)) * _dg(hb, su_ref[...], NN)).astype(BF16)
    shared = _dg(hm, sd_ref[...], NN)
    x2 = x1_ref[...] + m[:, 5 * D_MODEL:6 * D_MODEL] * (routed + shared)
    out = _rms(x2) * fn_ref[...]

    @pl.when(i < tiles_prompt)
    def _():
        op_ref[...] = out

    @pl.when(i >= tiles_prompt)
    def _():
        os_ref[...] = out


def _final(x1, h2, wts, mod3, sg, su, sd, fnw, dest, ys, n_prompt_tok, sample_len, tm):
    n_tok = x1.shape[0]
    cond = functools.partial(_cond_row, tiles_prompt=n_prompt_tok // tm, tiles_per_seq=sample_len // tm)
    tok = lambda i: (i, 0)
    const = lambda i: (0, 0)
    tp = n_prompt_tok // tm
    return pl.pallas_call(
        functools.partial(_final_kernel, tiles_prompt=tp),
        grid=(n_tok // tm,),
        in_specs=[pl.BlockSpec((tm, D_MODEL), tok),
                  pl.BlockSpec((tm, D_PACK), tok),
                  pl.BlockSpec((tm, TOP_K), tok),
                  pl.BlockSpec((1, 1, 6 * D_MODEL), lambda i: (cond(i), 0, 0)),
                  pl.BlockSpec((D_MODEL, D_EXPERT), const),
                  pl.BlockSpec((D_MODEL, D_EXPERT), const),
                  pl.BlockSpec((D_EXPERT, D_MODEL), const),
                  pl.BlockSpec((1, D_MODEL), const),
                  pl.BlockSpec(memory_space=pl.ANY),
                  pl.BlockSpec(memory_space=pl.ANY)],
        out_specs=[pl.BlockSpec((tm, D_MODEL), lambda i: (jnp.minimum(i, tp - 1), 0)),
                   pl.BlockSpec((tm, D_MODEL), lambda i: (jnp.maximum(i - tp, 0), 0))],
        out_shape=[jax.ShapeDtypeStruct((n_prompt_tok, D_MODEL), F32),
                   jax.ShapeDtypeStruct((n_tok - n_prompt_tok, D_MODEL), F32)],
        scratch_shapes=[pltpu.SMEM((3, TOP_K, tm), jnp.int32),
                        pltpu.VMEM((2, TOP_K, tm, D_PACK), jnp.uint32),
                        pltpu.SemaphoreType.DMA((3,)),
                        pltpu.SemaphoreType.DMA((2,))],
        compiler_params=_cparams(("arbitrary",)),
        name="final",
    )(x1, h2, wts, mod3, sg, su, sd, fnw, dest, ys)


def kernel(x_prompt, x_sample, state_gla, state_gdn, c, c_ctx, w_ada, b_ada, norm1_w, w_in, conv_w, gla_lr_w, gla_lr_b, gdn_a_log, gdn_dt_bias, gla_norm_w, gdn_norm_w, w_o, norm2_w, router_w, router_bias, exp_w_gate, exp_w_up, exp_w_down, sh_w_gate, sh_w_up, sh_w_down, final_norm_w):
    bp, lp, d = x_prompt.shape
    bs, ls, _ = x_sample.shape
    assert d == D_MODEL and lp == UNIT and ls % UNIT == 0 and w_ada.shape[0] == 1
    n_prompt_tok = bp * lp
    n_tok = n_prompt_tok + bs * ls
    n_prompt_units = n_prompt_tok // UNIT
    units_per_seq = ls // UNIT
    grid_w = 64
    layer = 0
    tm_in = 512
    tm = 256
    assert n_prompt_tok % tm_in == 0 and ls % tm_in == 0 and n_tok % ROUTE_TILE == 0

    xp = x_prompt.reshape(n_prompt_tok, d)
    xs_in = x_sample.reshape(bs * ls, d)
    cond = jnp.concatenate([c_ctx[None, :], c, jnp.zeros((SUBLANES - 1 - bs, d), F32)], axis=0)
    mod3 = _ada(cond, w_ada[layer], b_ada[layer][None, :]).reshape(SUBLANES, 1, 6 * d)

    wi = w_in[layer]
    o_lr = 2 * QA + 2 * VA
    o_qkvb = o_lr + N_DIR * GLA_RANK
    o_zb = o_qkvb + 3 * QB
    o_ab = o_zb + VB
    w_main = jnp.concatenate([wi[:, 0:o_lr], wi[:, o_qkvb:o_ab]], axis=1).astype(BF16)
    w_small = jnp.concatenate([wi[:, o_lr:o_qkvb], wi[:, o_ab:], jnp.zeros((d, LANES - 48), F32)], axis=1)
    cw = jnp.concatenate([conv_w[layer], jnp.zeros((SUBLANES - CONV_K, 3 * QB), F32)], axis=0)
    proj = _inproj(xp, xs_in, mod3, norm1_w[layer][None, :], w_main, w_small, cw, ls, lp, grid_w, tm_in)

    wlr = jnp.zeros((N_DIR, LANES, QA), F32)
    for dd in range(N_DIR):
        wlr = wlr.at[dd, dd * GLA_RANK:(dd + 1) * GLA_RANK, :].set(gla_lr_w[layer, dd])
    blr = gla_lr_b[layer][:, None, :]
    init_gla = jnp.concatenate([jnp.zeros((1,) + state_gla.shape[2:], F32), state_gla[:, layer].astype(F32)], axis=0)
    o_af, o_ab_, s_gla = _gla(proj, wlr, blr, init_gla, n_prompt_units, units_per_seq)

    gcoef = jnp.zeros((1, LANES), F32).at[0, SM_AB:SM_AB + N_DIR * H_B].set(-jnp.exp(gdn_a_log[layer].reshape(-1)))
    gdt = jnp.zeros((1, LANES), F32).at[0, SM_AB:SM_AB + N_DIR * H_B].set(gdn_dt_bias[layer].reshape(-1))
    init_gdn = jnp.concatenate([jnp.zeros((1, N_DIR, H_B * DK_B, DV_B), F32),
                                state_gdn[:, layer].astype(F32).reshape(bs, N_DIR, H_B * DK_B, DV_B)], axis=0)
    o_df, o_db, s_gdn = _gdn(proj, gcoef, gdt, init_gdn, n_prompt_units, units_per_seq)

    nwa = jnp.tile(gla_norm_w[layer], H_A)[None, :]
    nwb = jnp.tile(gdn_norm_w[layer], H_B)[None, :]
    x1, h2, scores_t = _post(xp, xs_in, proj, o_af, o_ab_, o_df, o_db, mod3, w_o[layer].astype(BF16), nwa, nwb,
                             norm2_w[layer][None, :], router_w[layer].T, ls, tm)

    bias_col = jnp.broadcast_to(router_bias[layer].astype(F32)[:, None], (N_EXPERTS, LANES))
    e8, r8, w8, cnt = _route(scores_t, bias_col)
    start, item_blk, item_exp, item_nxt, item_lo, item_hi = _items(cnt[:, 0].astype(jnp.int32), n_tok * TOP_K)
    dest = _dest(start, e8, r8, tt=min(2048, n_tok))

    xs = _dispatch(h2, dest, tm)
    ys = _experts(item_blk, item_exp, item_nxt, item_lo, item_hi, xs, exp_w_gate[layer], exp_w_up[layer],
                  exp_w_down[layer])
    y_p, y_s = _final(x1, h2, w8.T, mod3, sh_w_gate[layer].astype(BF16), sh_w_up[layer].astype(BF16),
                      sh_w_down[layer].astype(BF16), final_norm_w[None, :], dest, ys, n_prompt_tok, ls, tm)

    y_prompt = y_p.reshape(bp, lp, d)
    y_sample = y_s.reshape(bs, ls, d)
    new_state_gla = s_gla.reshape(bp, 1, N_DIR, H_A, DK_A, DV_A).astype(x_prompt.dtype)
    new_state_gdn = s_gdn.reshape(bp, 1, N_DIR, H_B, DK_B, DV_B).astype(x_prompt.dtype)
    return (y_prompt, y_sample, new_state_gla, new_state_gdn)
```
